```python
import math
import jax, jax.numpy as jnp
from jax import lax
import numpy as np

D_MODEL = 1024
BATCH = 32
SEQ = 256
DEPTH = 4
DEC_BATCH = 8
DEC_SEQ = 4096
PAST_LEN = 256

GRID_W = 64
RET_HEADS = 4
RET_DK = 64
RET_DV = 64
RET_CHUNK = 128
DIFF_HEADS = 4
DIFF_DK = 32
DIFF_DV = 2 * DIFF_DK
MLA_HEADS = 8
MLA_NOPE = 64
MLA_ROPE = 32
MLA_DQK = MLA_NOPE + MLA_ROPE
MLA_DV = 64
MLA_Q_RANK = 768
MLA_KV_RANK = 256
RET_W = RET_HEADS * RET_DV
DIFF_W = DIFF_HEADS * DIFF_DV
MLA_W = MLA_HEADS * MLA_DV
MIX_W = RET_W + DIFF_W + MLA_W
IN_SPLIT_SIZES = (RET_HEADS * RET_DK, RET_HEADS * RET_DK, RET_W, RET_W,
                  DIFF_HEADS * 2 * DIFF_DK, DIFF_HEADS * 2 * DIFF_DK, DIFF_W,
                  MLA_Q_RANK, MLA_KV_RANK, MLA_ROPE)
D_IN = sum(IN_SPLIT_SIZES)
D_FF = -(-8 * D_MODEL // (3 * 256)) * 256
ROPE_BASE = 10000.0
QUERY_BLOCK = 128
EPS = 1e-6

kernel_name = "hybrid_retention_diffattn_mla_dit_step"

F32 = jnp.float32


def rms_norm(x, gain):
    x32 = x.astype(F32)
    y = x32 * lax.rsqrt(jnp.mean(x32 * x32, axis=-1, keepdims=True) + EPS)
    return (y * gain.astype(F32)).astype(x.dtype)


def head_group_norm(o, gain):
    mu = jnp.mean(o, axis=-1, keepdims=True)
    var = jnp.mean(jnp.square(o - mu), axis=-1, keepdims=True)
    return (o - mu) * lax.rsqrt(var + EPS) * gain.astype(F32).reshape(o.shape[-2], o.shape[-1])


def axial_rope_tables(n, rot_dim):
    rows = n // GRID_W
    row = jnp.repeat(jnp.arange(rows, dtype=F32), GRID_W)
    col = jnp.tile(jnp.arange(GRID_W, dtype=F32), rows)
    n_freq = rot_dim // 4
    inv = 1.0 / (ROPE_BASE ** (jnp.arange(n_freq, dtype=F32) / n_freq))
    ang = jnp.concatenate([row[:, None] * inv[None], col[:, None] * inv[None]], axis=-1)
    return jnp.cos(ang), jnp.sin(ang)


def apply_rope(x, cos, sin):
    shape = (x.shape[1],) + (1,) * (x.ndim - 3) + (cos.shape[-1],)
    c = cos.reshape(shape)
    s = sin.reshape(shape)
    x32 = x.astype(F32)
    x1, x2 = jnp.split(x32, 2, axis=-1)
    return jnp.concatenate([x1 * c - x2 * s, x2 * c + x1 * s], axis=-1).astype(x.dtype)


def sweep_query_blocks(fn, q):
    b, n = q.shape[0], q.shape[1]
    nb = n // QUERY_BLOCK
    qb = jnp.moveaxis(q.reshape((b, nb, QUERY_BLOCK) + q.shape[2:]), 1, 0)
    out = jnp.moveaxis(lax.map(fn, qb), 0, 1)
    return out.reshape((b, n) + out.shape[3:])


def softmax_attend(q, k, v):
    scale = q.shape[-1] ** -0.5
    k32 = k.astype(F32)
    v32 = v.astype(F32)

    def block(qb):
        s = jnp.einsum('bqhd,bkhd->bhqk', qb.astype(F32), k32) * scale
        p = jax.nn.softmax(s, axis=-1)
        return jnp.einsum('bhqk,bkhe->bqhe', p, v32)

    return sweep_query_blocks(block, q)


def diff_attend(q, k, v, lam):
    scale = q.shape[-1] ** -0.5
    k32 = k.astype(F32)
    v32 = v.astype(F32)

    def block(qb):
        s = jnp.einsum('bqhmd,bkhmd->bhmqk', qb.astype(F32), k32) * scale
        p = jax.nn.softmax(s, axis=-1)
        a = p[:, :, 0] - lam * p[:, :, 1]
        return jnp.einsum('bhqk,bkhe->bqhe', a, v32)

    return sweep_query_blocks(block, q)


def retention_scan(q, k, v, log_gamma, s0):
    b, n, nh, _ = q.shape
    dv = v.shape[-1]
    C = RET_CHUNK
    nc = n // C
    idx = jnp.arange(C, dtype=F32)
    dist = idx[:, None] - idx[None, :]
    causal = dist >= 0
    intra = jnp.where(causal[None], jnp.exp(log_gamma[:, None, None] * jnp.where(causal, dist, 0.0)[None]), 0.0)
    q_decay = jnp.exp(log_gamma[None, :] * (idx[:, None] + 1.0))
    k_decay = jnp.exp(log_gamma[None, :] * (C - 1.0 - idx[:, None]))
    chunk_decay = jnp.exp(log_gamma * C)

    def to_chunks(a):
        return jnp.moveaxis(a.reshape(b, nc, C, nh, a.shape[-1]), 1, 0)

    def step(s, blk):
        qc, kc, vc = blk
        att = jnp.einsum('bnhd,bmhd->bhnm', qc, kc) * intra[None]
        o = (jnp.einsum('bhnm,bmhe->bnhe', att, vc)
             + jnp.einsum('bnhd,bhde->bnhe', qc, s) * q_decay[None, :, :, None])
        s = (s * chunk_decay[None, :, None, None]
             + jnp.einsum('bmhd,bmhe->bhde', kc * k_decay[None, :, :, None], vc))
        return s, o

    s_final, o = lax.scan(step, s0, (to_chunks(q), to_chunks(k), to_chunks(v)))
    return jnp.moveaxis(o, 0, 1).reshape(b, n, nh, dv), s_final


def bi_retention(q, k, v, log_gamma, s0):
    o_f, s_f = retention_scan(q, k, v, log_gamma[0], s0[:, 0])
    o_b, s_b = retention_scan(q[:, ::-1], k[:, ::-1], v[:, ::-1], log_gamma[1], s0[:, 1])
    return o_f + o_b[:, ::-1], jnp.stack([s_f, s_b], axis=1)


def mla_keys_values(ckv_n, kr, w_ukv, k_gain):
    b, n, _ = ckv_n.shape
    kv = (ckv_n @ w_ukv).reshape(b, n, MLA_HEADS, MLA_NOPE + MLA_DV)
    k = jnp.concatenate([kv[..., :MLA_NOPE],
                         jnp.broadcast_to(kr[:, :, None, :], (b, n, MLA_HEADS, MLA_ROPE)).astype(kv.dtype)], axis=-1)
    return rms_norm(k, k_gain), kv[..., MLA_NOPE:]


def token_mixers(h, lp, lam_init, ctx):
    b, n, _ = h.shape
    offs = np.cumsum(IN_SPLIT_SIZES)[:-1].tolist()
    rq, rk, rv, rg, dq, dk, dv, cq, ckv, kr = jnp.split(h @ lp["w_in"], offs, axis=-1)
    rq = rq.reshape(b, n, RET_HEADS, RET_DK)
    rk = rk.reshape(b, n, RET_HEADS, RET_DK)
    rv = rv.reshape(b, n, RET_HEADS, RET_DV)
    dq = rms_norm(dq.reshape(b, n, DIFF_HEADS, 2, DIFF_DK), lp["diff_qk_gain"][0])
    dk = rms_norm(dk.reshape(b, n, DIFF_HEADS, 2, DIFF_DK), lp["diff_qk_gain"][1])
    dv = dv.reshape(b, n, DIFF_HEADS, DIFF_DV)
    mq = rms_norm((rms_norm(cq, lp["mla_q_norm"]) @ lp["w_uq"]).reshape(b, n, MLA_HEADS, MLA_DQK),
                  lp["mla_qk_gain"][0])
    ckv_n = rms_norm(ckv, lp["mla_kv_norm"])
    mk, mv = mla_keys_values(ckv_n, kr, lp["w_ukv"], lp["mla_qk_gain"][1])

    if ctx is None:
        state0 = jnp.zeros((b, 2, RET_HEADS, RET_DK, RET_DV), F32)
        dk_all, dv_all, mk_all, mv_all = dk, dv, mk, mv
    else:
        state0, ctx_dk, ctx_dv, ctx_ckv, ctx_kr = ctx
        cos_r, sin_r = axial_rope_tables(n, RET_DK)
        cos_s, sin_s = axial_rope_tables(n, DIFF_DK)
        rq = apply_rope(rq, cos_r, sin_r)
        rk = apply_rope(rk, cos_r, sin_r)
        dq = apply_rope(dq, cos_s, sin_s)
        dk = apply_rope(dk, cos_s, sin_s)
        mq = jnp.concatenate([mq[..., :MLA_NOPE], apply_rope(mq[..., MLA_NOPE:], cos_s, sin_s)], axis=-1)
        mk = jnp.concatenate([mk[..., :MLA_NOPE], apply_rope(mk[..., MLA_NOPE:], cos_s, sin_s)], axis=-1)
        ctx_mk, ctx_mv = mla_keys_values(ctx_ckv.astype(h.dtype), ctx_kr.astype(h.dtype), lp["w_ukv"], lp["mla_qk_gain"][1])
        dk_all = jnp.concatenate([dk, ctx_dk.astype(dk.dtype)], axis=1)
        dv_all = jnp.concatenate([dv, ctx_dv.astype(dv.dtype)], axis=1)
        mk_all = jnp.concatenate([mk, ctx_mk.astype(mk.dtype)], axis=1)
        mv_all = jnp.concatenate([mv, ctx_mv.astype(mv.dtype)], axis=1)

    log_gamma = jax.nn.log_sigmoid(lp["ret_decay"].astype(F32))
    o_ret, states = bi_retention(rq.astype(F32), rk.astype(F32) * (RET_DK ** -0.5), rv.astype(F32),
                                 log_gamma, state0.astype(F32))
    ret = head_group_norm(o_ret, lp["ret_gn_gain"]).reshape(b, n, RET_W) * jax.nn.silu(rg.astype(F32))

    dl = lp["diff_lambda"].astype(F32)
    lam = jnp.exp(jnp.sum(dl[0] * dl[1])) - jnp.exp(jnp.sum(dl[2] * dl[3])) + lam_init
    o_diff = diff_attend(dq, dk_all, dv_all, lam)
    diff = (rms_norm(o_diff, lp["diff_subln_gain"]) * (1.0 - lam_init)).reshape(b, n, DIFF_W)

    mla = softmax_attend(mq, mk_all, mv_all).reshape(b, n, MLA_W)

    mixed = jnp.concatenate([ret.astype(h.dtype), diff.astype(h.dtype), mla.astype(h.dtype)], axis=-1)
    ctx_out = (states, dk, dv, ckv_n, kr) if ctx is None else None
    return mixed, ctx_out


def trunk_layer(x, cond, lp, lam_init, ctx):
    mod = (jax.nn.silu(cond) @ lp["w_mod"] + lp["b_mod"])[:, None, :]
    shift1, scale1, gate1, shift2, scale2, gate2 = jnp.split(mod, 6, axis=-1)
    h = rms_norm(x, lp["norm1"]) * (1.0 + scale1) + shift1
    mixed, ctx_out = token_mixers(h, lp, lam_init, ctx)
    x = x + gate1 * (mixed @ lp["w_out"])
    h = rms_norm(x, lp["norm2"]) * (1.0 + scale2) + shift2
    g, u = jnp.split(h @ lp["w_gu"], 2, axis=-1)
    x = x + gate2 * ((jax.nn.silu(g) * u) @ lp["w_down"])
    return x, ctx_out


def setup_inputs(seed: int = 0) -> dict:
    key = jax.random.key(seed)
    ks = iter(jax.random.split(key, 40))

    def nrm(shape, scale):
        return jax.random.normal(next(ks), shape, F32) * scale

    def gain(shape):
        return 1.0 + nrm(shape, 0.02)

    gamma0 = 1.0 - 2.0 ** (-5.0 - jnp.arange(RET_HEADS, dtype=F32))
    decay_logit0 = jnp.log(gamma0) - jnp.log1p(-gamma0)
    return {
        "x_prompt": nrm((BATCH, SEQ, D_MODEL), 1.0),
        "x_sample": nrm((DEC_BATCH, DEC_SEQ, D_MODEL), 1.0),
        "c": nrm((DEC_BATCH, D_MODEL), 1.0),
        "state_ret": nrm((DEC_BATCH, DEPTH, 2, RET_HEADS, RET_DK, RET_DV), 1.0),
        "cache_diff_k": nrm((DEC_BATCH, DEPTH, PAST_LEN, DIFF_HEADS, 2, DIFF_DK), 1.0),
        "cache_diff_v": nrm((DEC_BATCH, DEPTH, PAST_LEN, DIFF_HEADS, DIFF_DV), 1.0),
        "cache_mla_ckv": nrm((DEC_BATCH, DEPTH, PAST_LEN, MLA_KV_RANK), 1.0),
        "cache_mla_kr": nrm((DEC_BATCH, DEPTH, PAST_LEN, MLA_ROPE), 1.0),
        "c_ctx": nrm((D_MODEL,), 1.0),
        "w_mod": nrm((DEPTH, D_MODEL, 6 * D_MODEL), 0.5 * D_MODEL ** -0.5),
        "b_mod": nrm((DEPTH, 6 * D_MODEL), 0.01),
        "norm1": gain((DEPTH, D_MODEL)),
        "norm2": gain((DEPTH, D_MODEL)),
        "w_in": nrm((DEPTH, D_MODEL, D_IN), D_MODEL ** -0.5),
        "ret_decay": decay_logit0[None, None, :] + nrm((DEPTH, 2, RET_HEADS), 0.1),
        "ret_gn_gain": gain((DEPTH, RET_W)),
        "diff_qk_gain": gain((DEPTH, 2, DIFF_DK)),
        "diff_lambda": nrm((DEPTH, 4, DIFF_DK), 0.1),
        "diff_subln_gain": gain((DEPTH, DIFF_DV)),
        "mla_q_norm": gain((DEPTH, MLA_Q_RANK)),
        "mla_kv_norm": gain((DEPTH, MLA_KV_RANK)),
        "w_uq": nrm((DEPTH, MLA_Q_RANK, MLA_HEADS * MLA_DQK), MLA_Q_RANK ** -0.5),
        "w_ukv": nrm((DEPTH, MLA_KV_RANK, MLA_HEADS * (MLA_NOPE + MLA_DV)), MLA_KV_RANK ** -0.5),
        "mla_qk_gain": gain((DEPTH, 2, MLA_DQK)),
        "w_out": nrm((DEPTH, MIX_W, D_MODEL), MIX_W ** -0.5),
        "w_gu": nrm((DEPTH, D_MODEL, 2 * D_FF), D_MODEL ** -0.5),
        "w_down": nrm((DEPTH, D_FF, D_MODEL), D_FF ** -0.5),
    }


def reference(x_prompt, x_sample, c, state_ret, cache_diff_k, cache_diff_v, cache_mla_ckv, cache_mla_kr,
              c_ctx, w_mod, b_mod, norm1, norm2, w_in, ret_decay, ret_gn_gain, diff_qk_gain, diff_lambda,
              diff_subln_gain, mla_q_norm, mla_kv_norm, w_uq, w_ukv, mla_qk_gain, w_out, w_gu, w_down):
    y_p = x_prompt
    y_s = x_sample
    new_ret, new_dk, new_dv, new_ckv, new_kr = [], [], [], [], []
    for l in range(DEPTH):
        lp = {
            "w_mod": w_mod[l], "b_mod": b_mod[l], "norm1": norm1[l], "norm2": norm2[l],
            "w_in": w_in[l], "ret_decay": ret_decay[l], "ret_gn_gain": ret_gn_gain[l],
            "diff_qk_gain": diff_qk_gain[l], "diff_lambda": diff_lambda[l],
            "diff_subln_gain": diff_subln_gain[l], "mla_q_norm": mla_q_norm[l],
            "mla_kv_norm": mla_kv_norm[l], "w_uq": w_uq[l], "w_ukv": w_ukv[l],
            "mla_qk_gain": mla_qk_gain[l], "w_out": w_out[l], "w_gu": w_gu[l], "w_down": w_down[l],
        }
        lam_init = 0.8 - 0.6 * math.exp(-0.3 * l)
        y_p, (s_l, dk_l, dv_l, ckv_l, kr_l) = trunk_layer(y_p, c_ctx[None, :], lp, lam_init, None)
        new_ret.append(s_l)
        new_dk.append(dk_l)
        new_dv.append(dv_l)
        new_ckv.append(ckv_l)
        new_kr.append(kr_l)
        cache_l = (state_ret[:, l], cache_diff_k[:, l], cache_diff_v[:, l], cache_mla_ckv[:, l], cache_mla_kr[:, l])
        y_s, _ = trunk_layer(y_s, c, lp, lam_init, cache_l)
    new_state_ret = jnp.stack(new_ret, axis=1)
    new_diff_k = jnp.stack(new_dk, axis=1)
    new_diff_v = jnp.stack(new_dv, axis=1)
    new_mla_ckv = jnp.stack(new_ckv, axis=1)
    new_mla_kr = jnp.stack(new_kr, axis=1)
    return (y_p, y_s, new_state_ret, new_diff_k, new_diff_v, new_mla_ckv, new_mla_kr)
```

```python
import functools
import math

import jax
import jax.numpy as jnp
from jax import lax
from jax.experimental import pallas as pl
from jax.experimental.pallas import tpu as pltpu

F32 = jnp.float32
BF16 = jnp.bfloat16
EPS = 1e-6

GRID_W = 64
RET_HEADS = 4
RET_DK = 64
RET_DV = 64
RET_CHUNK = 128
DIFF_HEADS = 4
DIFF_DK = 32
DIFF_DV = 64
MLA_HEADS = 8
MLA_NOPE = 64
MLA_ROPE = 32
MLA_DQK = MLA_NOPE + MLA_ROPE
MLA_DV = 64
MLA_Q_RANK = 768
MLA_KV_RANK = 256
ROPE_BASE = 10000.0

RET_W = RET_HEADS * RET_DV
DIFF_W = DIFF_HEADS * DIFF_DV
MLA_W = MLA_HEADS * MLA_DV
DIFF_QK_W = DIFF_HEADS * 2 * DIFF_DK
MLA_PAD = 128
MLA_QK_PAD_W = MLA_HEADS * MLA_PAD

_SPLITS = (RET_HEADS * RET_DK, RET_HEADS * RET_DK, RET_W, RET_W,
           DIFF_QK_W, DIFF_QK_W, DIFF_W, MLA_Q_RANK, MLA_KV_RANK, MLA_ROPE)
_OFFS = tuple(int(sum(_SPLITS[:i])) for i in range(len(_SPLITS) + 1))
O_RQ, O_RK, O_RV, O_RG, O_DQ, O_DK, O_DV, O_CQ, O_CKV, O_KR, O_END = _OFFS

LANE = 128
VMEM_LIMIT = 56 * 1024 * 1024


def _cparams(n_grid):
    return pltpu.CompilerParams(dimension_semantics=("arbitrary",) * n_grid,
                                vmem_limit_bytes=VMEM_LIMIT)


def _full(shape):
    nd = len(shape)
    return pl.BlockSpec(shape, lambda *_: (0,) * nd)


def _silu(x):
    return x / (1.0 + jnp.exp(-x))


def _rms_scale(x, n):
    return lax.rsqrt(jnp.sum(x * x, axis=0, keepdims=True) * (1.0 / n) + EPS)


def _rope_pair(x1, x2, c, s):
    return x1 * c - x2 * s, x2 * c + x1 * s


def _mod_kernel(c_ref, w_ref, b_ref, o_ref):
    a = _silu(c_ref[...]).astype(BF16)
    o_ref[...] = jnp.dot(a, w_ref[...].astype(BF16), preferred_element_type=F32) + b_ref[...]


def _modulation(cond, w_mod, b_mod):
    depth, d, d6 = w_mod.shape
    r = cond.shape[0]
    tn = 1536
    return pl.pallas_call(
        _mod_kernel,
        grid=(depth, d6 // tn),
        in_specs=[pl.BlockSpec((r, d), lambda l, j: (0, 0)),
                  pl.BlockSpec((None, d, tn), lambda l, j: (l, 0, j)),
                  pl.BlockSpec((None, 1, tn), lambda l, j: (l, 0, j))],
        out_specs=pl.BlockSpec((None, r, tn), lambda l, j: (l, 0, j)),
        out_shape=jax.ShapeDtypeStruct((depth, r, d6), F32),
        compiler_params=_cparams(2),
        name="adaln_mod",
    )(cond, w_mod, b_mod.reshape(depth, 1, d6))


def _mla_keys_values(ckvn_bf, kr, w_uk_ref, w_uv_ref, g_mk_ref, rope):
    t = kr.shape[1]
    kn = jnp.dot(w_uk_ref[...], ckvn_bf, preferred_element_type=F32)
    v = jnp.dot(w_uv_ref[...], ckvn_bf, preferred_element_type=F32)
    kr_ss = jnp.sum(kr * kr, axis=0, keepdims=True)
    zpad = jnp.zeros((MLA_PAD - MLA_DQK, t), F32)
    half = MLA_ROPE // 2
    heads = []
    for hd in range(MLA_HEADS):
        r0 = hd * MLA_PAD
        kh = kn[r0:r0 + MLA_NOPE]
        ss = jnp.sum(kh * kh, axis=0, keepdims=True) + kr_ss
        r = lax.rsqrt(ss * (1.0 / MLA_DQK) + EPS)
        g = g_mk_ref[r0:r0 + MLA_PAD, :]
        y_nope = kh * r * g[:MLA_NOPE]
        y_r = kr * r * g[MLA_NOPE:MLA_DQK]
        if rope is not None:
            c, s = rope
            y1, y2 = _rope_pair(y_r[:half], y_r[half:], c, s)
            heads += [y_nope, y1, y2, zpad]
        else:
            heads += [y_nope, y_r, zpad]
    return jnp.concatenate(heads, axis=0), v


def _ctx_kv_kernel(ckv_ref, kr_ref, w_uk_ref, w_uv_ref, g_mk_ref, mk_ref, mv_ref):
    k, v = _mla_keys_values(ckv_ref[...].astype(BF16), kr_ref[...], w_uk_ref, w_uv_ref, g_mk_ref, None)
    mk_ref[...] = k.T.astype(BF16)
    mv_ref[...] = v.astype(BF16)


def _ctx_mla_kv(ckv_t, kr_t, w_uk, w_uv, g_mk):
    b, _, l = ckv_t.shape
    return pl.pallas_call(
        _ctx_kv_kernel,
        grid=(b,),
        in_specs=[pl.BlockSpec((None, MLA_KV_RANK, l), lambda i: (i, 0, 0)),
                  pl.BlockSpec((None, MLA_ROPE, l), lambda i: (i, 0, 0)),
                  _full(w_uk.shape), _full(w_uv.shape), _full(g_mk.shape)],
        out_specs=[pl.BlockSpec((None, l, MLA_QK_PAD_W), lambda i: (i, 0, 0)),
                   pl.BlockSpec((None, MLA_W, l), lambda i: (i, 0, 0))],
        out_shape=[jax.ShapeDtypeStruct((b, l, MLA_QK_PAD_W), BF16),
                   jax.ShapeDtypeStruct((b, MLA_W, l), BF16)],
        compiler_params=_cparams(1),
        name="ctx_mla_kv",
    )(ckv_t, kr_t, w_uk, w_uv, g_mk)


def _premix_kernel(*refs, is_ctx, tm):
    it = iter(refs)
    x_ref, shift_ref, scale_ref, n1_ref, w_in_ref = (next(it) for _ in range(5))
    g_dq_ref, g_dk_ref, g_qn_ref, g_kvn_ref = (next(it) for _ in range(4))
    w_uq_ref, g_mq_ref, w_uk_ref, w_uv_ref, g_mk_ref = (next(it) for _ in range(5))
    if is_ctx:
        rope_r = rope_s = None
    else:
        cr_ref, sr_ref, cs_ref, ss_ref = (next(it) for _ in range(4))
        rope_r = (cr_ref[...], sr_ref[...])
        rope_s = (cs_ref[...], ss_ref[...])
    rq_ref, rk_ref, rv_ref, rg_ref = (next(it) for _ in range(4))
    dq_ref, dk_ref, dv_ref, mq_ref, mk_ref, mv_ref = (next(it) for _ in range(6))
    if is_ctx:
        dkf_ref, dvf_ref, ckvf_ref, krf_ref = (next(it) for _ in range(4))

    x = x_ref[...]
    d = x.shape[0]
    h = x * _rms_scale(x, d) * n1_ref[...]
    h = (h * (1.0 + scale_ref[...]) + shift_ref[...]).astype(BF16)

    def proj(lo, hi):
        return jnp.dot(w_in_ref[lo:hi, :], h, preferred_element_type=F32)

    nch = tm // LANE

    def put_chunks(ref, val):
        for j in range(nch):
            ref[j] = val[:, j * LANE:(j + 1) * LANE].astype(ref.dtype)

    rq = proj(O_RQ, O_RK)
    rk = proj(O_RK, O_RV)
    if rope_r is not None:
        c, s = rope_r
        hk = RET_DK // 2

        def rope_heads(a):
            parts = []
            for hd in range(RET_HEADS):
                r0 = hd * RET_DK
                parts += list(_rope_pair(a[r0:r0 + hk], a[r0 + hk:r0 + RET_DK], c, s))
            return jnp.concatenate(parts, axis=0)

        rq = rope_heads(rq)
        rk = rope_heads(rk)
    put_chunks(rq_ref, rq)
    put_chunks(rk_ref, rk * (RET_DK ** -0.5))
    put_chunks(rv_ref, proj(O_RV, O_RG))
    put_chunks(rg_ref, proj(O_RG, O_DQ))

    def diff_qk(a, g_ref, out_scale):
        parts = []
        hs = DIFF_DK // 2
        for seg in range(2 * DIFF_HEADS):
            r0 = seg * DIFF_DK
            xs = a[r0:r0 + DIFF_DK]
            y = xs * _rms_scale(xs, DIFF_DK) * g_ref[r0:r0 + DIFF_DK, :]
            if out_scale != 1.0:
                y = y * out_scale
            if rope_s is not None:
                parts += list(_rope_pair(y[:hs], y[hs:], rope_s[0], rope_s[1]))
            else:
                parts.append(y)
        return jnp.concatenate(parts, axis=0)

    dq_ref[...] = diff_qk(proj(O_DQ, O_DK), g_dq_ref, DIFF_DK ** -0.5).astype(BF16)
    dk = diff_qk(proj(O_DK, O_DV), g_dk_ref, 1.0)
    dk_t = dk.T
    dk_ref[...] = dk_t.astype(BF16)
    dv = proj(O_DV, O_CQ)
    dv_ref[...] = dv.astype(BF16)
    if is_ctx:
        dkf_ref[...] = dk_t
        dvf_ref[...] = dv.T

    cq = proj(O_CQ, O_CKV)
    cqn = (cq * _rms_scale(cq, MLA_Q_RANK) * g_qn_ref[...]).astype(BF16)
    mq = jnp.dot(w_uq_ref[...], cqn, preferred_element_type=F32)
    half = MLA_ROPE // 2
    parts = []
    for hd in range(MLA_HEADS):
        r0 = hd * MLA_PAD
        xs = mq[r0:r0 + MLA_PAD]
        y = xs * _rms_scale(xs, MLA_DQK) * (g_mq_ref[r0:r0 + MLA_PAD, :] * (MLA_DQK ** -0.5))
        if rope_s is not None:
            y1, y2 = _rope_pair(y[MLA_NOPE:MLA_NOPE + half], y[MLA_NOPE + half:MLA_DQK], rope_s[0], rope_s[1])
            parts += [y[:MLA_NOPE], y1, y2, y[MLA_DQK:]]
        else:
            parts.append(y)
    mq_ref[...] = jnp.concatenate(parts, axis=0).astype(BF16)

    ckv = proj(O_CKV, O_KR)
    ckvn = ckv * _rms_scale(ckv, MLA_KV_RANK) * g_kvn_ref[...]
    kr = proj(O_KR, O_END)
    mk, mv = _mla_keys_values(ckvn.astype(BF16), kr, w_uk_ref, w_uv_ref, g_mk_ref, rope_s)
    mk_ref[...] = mk.T.astype(BF16)
    mv_ref[...] = mv.astype(BF16)
    if is_ctx:
        ckvf_ref[...] = ckvn.T
        krf_ref[...] = jnp.concatenate([kr, jnp.zeros((LANE - MLA_ROPE, tm), F32)], axis=0).T


def _premix(x_t, shift, scale, lw, rope, *, is_ctx, tm, tiles_per_batch):
    d, n = x_t.shape
    nt = n // tm
    nch = tm // LANE
    if is_ctx:
        bidx = lambda j: (0, 0, 0)
    else:
        bidx = lambda j: (j // tiles_per_batch, 0, 0)
    in_arrays = [x_t, shift, scale, lw["n1"], lw["w_in"], lw["g_dq"], lw["g_dk"], lw["g_qn"], lw["g_kvn"],
                 lw["w_uq"], lw["g_mq"], lw["w_uk"], lw["w_uv"], lw["g_mk"]]
    in_specs = [pl.BlockSpec((d, tm), lambda j: (0, j)),
                pl.BlockSpec((None, d, 1), bidx), pl.BlockSpec((None, d, 1), bidx)]
    in_specs += [_full(a.shape) for a in in_arrays[3:]]
    if not is_ctx:
        for tab in rope:
            in_arrays.append(tab)
            in_specs.append(pl.BlockSpec((tab.shape[0], tm), lambda j: (0, j % tiles_per_batch)))

    chunk_spec = pl.BlockSpec((nch, RET_W, LANE), lambda j: (j, 0, 0))
    fm = lambda rows: pl.BlockSpec((rows, tm), lambda j: (0, j))
    tok = lambda cols: pl.BlockSpec((tm, cols), lambda j: (j, 0))
    nck = n // LANE
    out_specs = [chunk_spec] * 4 + [fm(DIFF_QK_W), tok(DIFF_QK_W), fm(DIFF_W),
                                    fm(MLA_QK_PAD_W), tok(MLA_QK_PAD_W), fm(MLA_W)]
    out_shape = [jax.ShapeDtypeStruct((nck, RET_W, LANE), BF16), jax.ShapeDtypeStruct((nck, RET_W, LANE), F32),
                 jax.ShapeDtypeStruct((nck, RET_W, LANE), BF16), jax.ShapeDtypeStruct((nck, RET_W, LANE), F32),
                 jax.ShapeDtypeStruct((DIFF_QK_W, n), BF16), jax.ShapeDtypeStruct((n, DIFF_QK_W), BF16),
                 jax.ShapeDtypeStruct((DIFF_W, n), BF16),
                 jax.ShapeDtypeStruct((MLA_QK_PAD_W, n), BF16), jax.ShapeDtypeStruct((n, MLA_QK_PAD_W), BF16),
                 jax.ShapeDtypeStruct((MLA_W, n), BF16)]
    if is_ctx:
        out_specs += [tok(DIFF_QK_W), tok(DIFF_W), tok(MLA_KV_RANK), tok(LANE)]
        out_shape += [jax.ShapeDtypeStruct((n, DIFF_QK_W), F32), jax.ShapeDtypeStruct((n, DIFF_W), F32),
                      jax.ShapeDtypeStruct((n, MLA_KV_RANK), F32), jax.ShapeDtypeStruct((n, LANE), F32)]
    return pl.pallas_call(
        functools.partial(_premix_kernel, is_ctx=is_ctx, tm=tm),
        grid=(nt,),
        in_specs=in_specs, out_specs=out_specs, out_shape=out_shape,
        compiler_params=_cparams(1),
        name="premix_ctx" if is_ctx else "premix_smp",
    )(*in_arrays)


def _retention_kernel(rq_ref, rk_ref, rv_ref, rg_ref, s0_ref, dec_ref, gn_ref, out_ref, sfin_ref,
                      o_acc, s_acc, intra_ref, vec_ref, *, nc):
    C = RET_CHUNK
    n_idx = lax.broadcasted_iota(jnp.int32, (C, C), 1).astype(F32)
    m_idx = lax.broadcasted_iota(jnp.int32, (C, C), 0).astype(F32)
    lane = lax.broadcasted_iota(jnp.int32, (8, C), 1).astype(F32)
    for d in range(2):
        for hd in range(RET_HEADS):
            i = d * RET_HEADS + hd
            z = dec_ref[i]
            lg8 = jnp.minimum(z, 0.0) - jnp.log1p(jnp.exp(-jnp.abs(z)))
            lg = jnp.broadcast_to(lg8[0:1, :], (C, C))
            dist = (n_idx - m_idx) if d == 0 else (m_idx - n_idx)
            ok = dist >= 0.0
            intra_ref[i] = jnp.where(ok, jnp.exp(lg * jnp.where(ok, dist, 0.0)), 0.0)
            if d == 0:
                qdec = jnp.exp(lg8 * (lane + 1.0))
                kdec = jnp.exp(lg8 * (C - 1.0 - lane))
            else:
                qdec = jnp.exp(lg8 * (C - lane))
                kdec = jnp.exp(lg8 * lane)
            vec_ref[i, 0] = qdec
            vec_ref[i, 1] = kdec
            vec_ref[i, 2] = jnp.exp(lg8 * float(C))
    s_acc[...] = s0_ref[...]
    o_acc[...] = jnp.zeros_like(o_acc)

    row_head = lax.broadcasted_iota(jnp.int32, (RET_HEADS * RET_DK, C), 0) // RET_DK

    def step(c, carry):
        for d in range(2):
            cc = c if d == 0 else nc - 1 - c
            q_all = rq_ref[cc]
            k_all = rk_ref[cc]
            k_tok = k_all.T.astype(BF16)
            for hd in range(RET_HEADS):
                i = d * RET_HEADS + hd
                r0 = hd * RET_DK
                q = q_all[r0:r0 + RET_DK]
                k = k_all[r0:r0 + RET_DK]
                v = rv_ref[cc, r0:r0 + RET_DV, :]
                q_m = jnp.where(row_head == hd, q_all, jnp.zeros_like(q_all))
                att_t = jnp.dot(k_tok, q_m, preferred_element_type=F32) * intra_ref[i]
                s_t = s_acc[i]
                o = jnp.dot(v, att_t.astype(BF16), preferred_element_type=F32)
                o = o + jnp.dot(s_t.astype(BF16), q, preferred_element_type=F32) * vec_ref[i, 0][0:1, :]
                kd = (k * vec_ref[i, 1][0:1, :]).astype(BF16)
                kv = lax.dot_general(v, kd, (((1,), (1,)), ((), ())), preferred_element_type=F32)
                s_acc[i] = s_t * vec_ref[i, 2][0:1, 0:RET_DK] + kv
                o_acc[cc, r0:r0 + RET_DV, :] += o
        return carry

    lax.fori_loop(0, nc, step, 0)
    sfin_ref[...] = s_acc[...]

    def finish(c, carry):
        o = o_acc[c]
        g = rg_ref[c]
        parts = []
        for hd in range(RET_HEADS):
            r0 = hd * RET_DV
            oh = o[r0:r0 + RET_DV]
            mu = jnp.mean(oh, axis=0, keepdims=True)
            var = jnp.mean(jnp.square(oh - mu), axis=0, keepdims=True)
            y = (oh - mu) * lax.rsqrt(var + EPS) * gn_ref[r0:r0 + RET_DV, :]
            parts.append(y * _silu(g[r0:r0 + RET_DV]))
        out_ref[c] = jnp.concatenate(parts, axis=0).astype(BF16)
        return carry

    lax.fori_loop(0, nc, finish, 0)


def _retention(rq, rk, rv, rg, s0_t, dec_b, gn_col, *, batch):
    nck = rq.shape[0]
    nc = nck // batch
    blk = pl.BlockSpec((nc, RET_W, LANE), lambda b: (b, 0, 0))
    nst = 2 * RET_HEADS
    st_spec = pl.BlockSpec((None, nst, RET_DV, RET_DK), lambda b: (b, 0, 0, 0))
    return pl.pallas_call(
        functools.partial(_retention_kernel, nc=nc),
        grid=(batch,),
        in_specs=[blk, blk, blk, blk, st_spec, _full(dec_b.shape), _full(gn_col.shape)],
        out_specs=[blk, st_spec],
        out_shape=[jax.ShapeDtypeStruct((nck, RET_W, LANE), BF16),
                   jax.ShapeDtypeStruct((batch, nst, RET_DV, RET_DK), F32)],
        scratch_shapes=[pltpu.VMEM((nc, RET_W, LANE), F32),
                        pltpu.VMEM((nst, RET_DV, RET_DK), F32),
                        pltpu.VMEM((nst, RET_CHUNK, RET_CHUNK), F32),
                        pltpu.VMEM((nst, 3, 8, LANE), F32)],
        compiler_params=_cparams(1),
        name="retention",
    )(rq, rk, rv, rg, s0_t, dec_b, gn_col)


def _softmax_pv(q_t, k_parts, v_parts, s_ref, p_ref):
    tq = q_t.shape[1]
    m = jnp.full((1, tq), -jnp.inf, F32)
    off = 0
    for k_ref, rows, tk in k_parts:
        def score_blk(i, m, k_ref=k_ref, tk=tk, off=off):
            r0 = pl.multiple_of(i * tk, tk)
            s = jnp.dot(k_ref[pl.ds(r0, tk), :], q_t, preferred_element_type=F32)
            s_ref[pl.ds(off + r0, tk), :] = s
            return jnp.maximum(m, jnp.max(s, axis=0, keepdims=True))
        m = lax.fori_loop(0, rows // tk, score_blk, m)
        off += rows
    total = off
    tk2 = k_parts[-1][2]

    def exp_blk(i, l):
        r0 = pl.multiple_of(i * tk2, tk2)
        p = jnp.exp(s_ref[pl.ds(r0, tk2), :] - m)
        p_ref[pl.ds(r0, tk2), :] = p.astype(BF16)
        return l + jnp.sum(p, axis=0, keepdims=True)
    l = lax.fori_loop(0, total // tk2, exp_blk, jnp.zeros((1, tq), F32))
    o = None
    off = 0
    for (k_ref, rows, tk), v_ref in zip(k_parts, v_parts):
        part = jnp.dot(v_ref[...], p_ref[off:off + rows, :], preferred_element_type=F32)
        o = part if o is None else o + part
        off += rows
    return o / l


def _diff_attn_kernel(*refs, has_ctx, rows_main, rows_ctx, tk, lam_init):
    it = iter(refs)
    q_ref, k_ref, v_ref = (next(it) for _ in range(3))
    if has_ctx:
        kc_ref, vc_ref = (next(it) for _ in range(2))
    dl_ref, g_ref, out_ref, s_ref, p_ref = (next(it) for _ in range(5))
    hd = pl.program_id(1)
    q_full = q_ref[...]
    seg = lax.broadcasted_iota(jnp.int32, q_full.shape, 0) // DIFF_DK
    k_parts = [(k_ref, rows_main, tk)]
    v_parts = [v_ref]
    if has_ctx:
        k_parts.append((kc_ref, rows_ctx, rows_ctx))
        v_parts.append(vc_ref)
    outs = []
    for mm in range(2):
        q_m = jnp.where(seg == 2 * hd + mm, q_full, jnp.zeros_like(q_full))
        outs.append(_softmax_pv(q_m, k_parts, v_parts, s_ref, p_ref))
    dl = dl_ref[...]
    lam = (jnp.exp(jnp.sum(dl[0:1] * dl[1:2], axis=1, keepdims=True))
           - jnp.exp(jnp.sum(dl[2:3] * dl[3:4], axis=1, keepdims=True)) + lam_init)
    o = outs[0] - lam * outs[1]
    y = o * _rms_scale(o, DIFF_DV) * g_ref[...]
    out_ref[...] = (y * (1.0 - lam_init)).astype(BF16)


def _diff_attention(dq_t, dk_tok, dv_t, ctx, dl, g_col, *, batch, tq, tk, lam_init):
    n_all = dq_t.shape[1]
    n = n_all // batch
    qt = n // tq
    has_ctx = ctx is not None
    rows_ctx = ctx[0].shape[1] if has_ctx else 0
    in_arrays = [dq_t, dk_tok, dv_t]
    in_specs = [pl.BlockSpec((DIFF_QK_W, tq), lambda b, h, i: (0, b * qt + i)),
                pl.BlockSpec((n, DIFF_QK_W), lambda b, h, i: (b, 0)),
                pl.BlockSpec((DIFF_DV, n), lambda b, h, i: (h, b))]
    if has_ctx:
        in_arrays += list(ctx)
        in_specs += [pl.BlockSpec((None, rows_ctx, DIFF_QK_W), lambda b, h, i: (b, 0, 0)),
                     pl.BlockSpec((None, DIFF_DV, rows_ctx), lambda b, h, i: (b, h, 0))]
    in_arrays += [dl, g_col]
    in_specs += [_full(dl.shape), _full(g_col.shape)]
    total = n + rows_ctx
    return pl.pallas_call(
        functools.partial(_diff_attn_kernel, has_ctx=has_ctx, rows_main=n, rows_ctx=rows_ctx, tk=tk,
                          lam_init=lam_init),
        grid=(batch, DIFF_HEADS, qt),
        in_specs=in_specs,
        out_specs=pl.BlockSpec((DIFF_DV, tq), lambda b, h, i: (h, b * qt + i)),
        out_shape=jax.ShapeDtypeStruct((DIFF_W, n_all), BF16),
        scratch_shapes=[pltpu.VMEM((total, tq), F32), pltpu.VMEM((total, tq), BF16)],
        compiler_params=_cparams(3),
        name="diff_attn_smp" if has_ctx else "diff_attn_ctx",
    )(*in_arrays)


def _mla_attn_kernel(*refs, has_ctx, rows_main, rows_ctx, tk):
    it = iter(refs)
    q_ref, k_ref, v_ref = (next(it) for _ in range(3))
    if has_ctx:
        kc_ref, vc_ref = (next(it) for _ in range(2))
    out_ref, s_ref, p_ref = (next(it) for _ in range(3))
    k_parts = [(k_ref, rows_main, tk)]
    v_parts = [v_ref]
    if has_ctx:
        k_parts.append((kc_ref, rows_ctx, rows_ctx))
        v_parts.append(vc_ref)
    out_ref[...] = _softmax_pv(q_ref[...], k_parts, v_parts, s_ref, p_ref).astype(BF16)


def _mla_attention(mq_t, mk_tok, mv_t, ctx, *, batch, tq, tk):
    n_all = mq_t.shape[1]
    n = n_all // batch
    qt = n // tq
    has_ctx = ctx is not None
    rows_ctx = ctx[0].shape[1] if has_ctx else 0
    in_arrays = [mq_t, mk_tok, mv_t]
    in_specs = [pl.BlockSpec((MLA_PAD, tq), lambda b, h, i: (h, b * qt + i)),
                pl.BlockSpec((n, MLA_PAD), lambda b, h, i: (b, h)),
                pl.BlockSpec((MLA_DV, n), lambda b, h, i: (h, b))]
    if has_ctx:
        in_arrays += list(ctx)
        in_specs += [pl.BlockSpec((None, rows_ctx, MLA_PAD), lambda b, h, i: (b, 0, h)),
                     pl.BlockSpec((None, MLA_DV, rows_ctx), lambda b, h, i: (b, h, 0))]
    total = n + rows_ctx
    return pl.pallas_call(
        functools.partial(_mla_attn_kernel, has_ctx=has_ctx, rows_main=n, rows_ctx=rows_ctx, tk=tk),
        grid=(batch, MLA_HEADS, qt),
        in_specs=in_specs,
        out_specs=pl.BlockSpec((MLA_DV, tq), lambda b, h, i: (h, b * qt + i)),
        out_shape=jax.ShapeDtypeStruct((MLA_W, n_all), BF16),
        scratch_shapes=[pltpu.VMEM((total, tq), F32), pltpu.VMEM((total, tq), BF16)],
        compiler_params=_cparams(3),
        name="mla_attn_smp" if has_ctx else "mla_attn_ctx",
    )(*in_arrays)


def _post_kernel(x_ref, ret_ref, diff_ref, mla_ref, g1_ref, sh2_ref, sc2_ref, g2_ref, n2_ref,
                 w_out_ref, w_gu_ref, w_down_ref, out_ref, *, tm, d_ff, ff_chunk):
    nch = tm // LANE
    ret = jnp.concatenate([ret_ref[j] for j in range(nch)], axis=1)
    mixed = jnp.concatenate([ret, diff_ref[...], mla_ref[...]], axis=0)
    x = x_ref[...]
    d = x.shape[0]
    x1 = x + g1_ref[...] * jnp.dot(w_out_ref[...], mixed, preferred_element_type=F32)
    h = x1 * _rms_scale(x1, d) * n2_ref[...]
    h = (h * (1.0 + sc2_ref[...]) + sh2_ref[...]).astype(BF16)
    ffn = None
    for lo in range(0, d_ff, ff_chunk):
        g = jnp.dot(w_gu_ref[lo:lo + ff_chunk, :], h, preferred_element_type=F32)
        u = jnp.dot(w_gu_ref[d_ff + lo:d_ff + lo + ff_chunk, :], h, preferred_element_type=F32)
        a = (_silu(g) * u).astype(BF16)
        part = jnp.dot(w_down_ref[:, lo:lo + ff_chunk], a, preferred_element_type=F32)
        ffn = part if ffn is None else ffn + part
    out_ref[...] = x1 + g2_ref[...] * ffn


def _post(x_t, ret, diff_t, mla_t, mods, lw, *, is_ctx, tm, tiles_per_batch):
    d, n = x_t.shape
    nt = n // tm
    nch = tm // LANE
    d_ff = lw["w_down"].shape[1]
    ff_chunk = d_ff // 2
    if is_ctx:
        bidx = lambda j: (0, 0, 0)
    else:
        bidx = lambda j: (j // tiles_per_batch, 0, 0)
    col = pl.BlockSpec((None, d, 1), bidx)
    const = lambda a: pl.BlockSpec(a.shape, lambda j: (0,) * a.ndim, pipeline_mode=pl.Buffered(1))
    g1, sh2, sc2, g2 = mods
    return pl.pallas_call(
        functools.partial(_post_kernel, tm=tm, d_ff=d_ff, ff_chunk=ff_chunk),
        grid=(nt,),
        in_specs=[pl.BlockSpec((d, tm), lambda j: (0, j)),
                  pl.BlockSpec((nch, RET_W, LANE), lambda j: (j, 0, 0)),
                  pl.BlockSpec((DIFF_W, tm), lambda j: (0, j)),
                  pl.BlockSpec((MLA_W, tm), lambda j: (0, j)),
                  col, col, col, col, const(lw["n2"]),
                  const(lw["w_out"]), const(lw["w_gu"]), const(lw["w_down"])],
        out_specs=pl.BlockSpec((d, tm), lambda j: (0, j)),
        out_shape=jax.ShapeDtypeStruct((d, n), F32),
        compiler_params=_cparams(1),
        name="post_ctx" if is_ctx else "post_smp",
    )(x_t, ret, diff_t, mla_t, g1, sh2, sc2, g2, lw["n2"], lw["w_out"], lw["w_gu"], lw["w_down"])


def _rope_tables_t(n, rot_dim):
    rows = n // GRID_W
    row = jnp.repeat(jnp.arange(rows, dtype=F32), GRID_W)
    col = jnp.tile(jnp.arange(GRID_W, dtype=F32), rows)
    n_freq = rot_dim // 4
    inv = 1.0 / (ROPE_BASE ** (jnp.arange(n_freq, dtype=F32) / n_freq))
    ang = jnp.concatenate([inv[:, None] * row[None, :], inv[:, None] * col[None, :]], axis=0)
    return jnp.cos(ang), jnp.sin(ang)


def _col(v):
    return v.astype(F32).reshape(-1, 1)


def _pad_heads_rows(w_t, used, pad):
    hk = w_t.shape[0] // used
    w3 = w_t.reshape(hk, used, w_t.shape[1])
    w3 = jnp.pad(w3, ((0, 0), (0, pad - used), (0, 0)))
    return w3.reshape(hk * pad, w_t.shape[1])


def _layer_weights(l, w_in, norm1, norm2, diff_qk_gain, mla_q_norm, mla_kv_norm, w_uq, w_ukv, mla_qk_gain,
                   w_out, w_gu, w_down):
    w_ukv_t = w_ukv[l].T.reshape(MLA_HEADS, MLA_NOPE + MLA_DV, MLA_KV_RANK)
    w_uk = jnp.pad(w_ukv_t[:, :MLA_NOPE], ((0, 0), (0, MLA_PAD - MLA_NOPE), (0, 0)))
    g_pad = lambda g: jnp.tile(jnp.pad(g.astype(F32), (0, MLA_PAD - MLA_DQK)), MLA_HEADS).reshape(-1, 1)
    return {
        "n1": _col(norm1[l]), "n2": _col(norm2[l]),
        "w_in": w_in[l].T.astype(BF16),
        "g_dq": _col(jnp.tile(diff_qk_gain[l, 0], 2 * DIFF_HEADS)),
        "g_dk": _col(jnp.tile(diff_qk_gain[l, 1], 2 * DIFF_HEADS)),
        "g_qn": _col(mla_q_norm[l]), "g_kvn": _col(mla_kv_norm[l]),
        "w_uq": _pad_heads_rows(w_uq[l].T, MLA_DQK, MLA_PAD).astype(BF16),
        "g_mq": g_pad(mla_qk_gain[l, 0]), "g_mk": g_pad(mla_qk_gain[l, 1]),
        "w_uk": w_uk.reshape(MLA_QK_PAD_W, MLA_KV_RANK).astype(BF16),
        "w_uv": w_ukv_t[:, MLA_NOPE:].reshape(MLA_W, MLA_KV_RANK).astype(BF16),
        "w_out": w_out[l].T.astype(BF16), "w_gu": w_gu[l].T.astype(BF16), "w_down": w_down[l].T.astype(BF16),
    }


def kernel(x_prompt, x_sample, c, state_ret, cache_diff_k, cache_diff_v, cache_mla_ckv, cache_mla_kr, c_ctx,
           w_mod, b_mod, norm1, norm2, w_in, ret_decay, ret_gn_gain, diff_qk_gain, diff_lambda, diff_subln_gain,
           mla_q_norm, mla_kv_norm, w_uq, w_ukv, mla_qk_gain, w_out, w_gu, w_down):
    depth = w_in.shape[0]
    bp, sp, d = x_prompt.shape
    bs, ss, _ = x_sample.shape
    past = cache_diff_k.shape[2]

    tm = 512
    tq_s, tk_s = 512, 512
    tq_p = tk_p = sp

    n_cond = 1 + bs
    r_pad = -(-n_cond // 16) * 16
    cond = jnp.concatenate([c_ctx[None, :], c, jnp.zeros((r_pad - n_cond, d), F32)], axis=0)
    mod = _modulation(cond, w_mod, b_mod)
    mod = mod.reshape(depth, r_pad, 6, d, 1)

    xp = x_prompt.reshape(bp * sp, d).T
    xs = x_sample.reshape(bs * ss, d).T
    rope = _rope_tables_t(ss, RET_DK) + _rope_tables_t(ss, DIFF_DK)
    dec_b = jnp.broadcast_to(ret_decay.astype(F32).reshape(depth, 2 * RET_HEADS, 1, 1),
                             (depth, 2 * RET_HEADS, 8, LANE))
    s0_zero = jnp.zeros((bp, 2 * RET_HEADS, RET_DV, RET_DK), F32)

    new_ret, new_dk, new_dv, new_ckv, new_kr = [], [], [], [], []
    for l in range(depth):
        lw = _layer_weights(l, w_in, norm1, norm2, diff_qk_gain, mla_q_norm, mla_kv_norm, w_uq, w_ukv,
                            mla_qk_gain, w_out, w_gu, w_down)
        lam_init = 0.8 - 0.6 * math.exp(-0.3 * l)
        gn_col = _col(ret_gn_gain[l])
        subln_col = _col(diff_subln_gain[l])
        dl = diff_lambda[l].astype(F32)

        mc = [mod[l, 0:1, i] for i in range(6)]
        (rq, rk, rv, rg, dq_t, dk_tok, dv_t, mq_t, mk_tok, mv_t, dk_f, dv_f, ckv_f, kr_f) = _premix(
            xp, mc[0], mc[1], lw, None, is_ctx=True, tm=tm, tiles_per_batch=1)
        ret, s_fin = _retention(rq, rk, rv, rg, s0_zero, dec_b[l], gn_col, batch=bp)
        diff_t = _diff_attention(dq_t, dk_tok, dv_t, None, dl, subln_col, batch=bp, tq=tq_p, tk=tk_p,
                                 lam_init=lam_init)
        mla_t = _mla_attention(mq_t, mk_tok, mv_t, None, batch=bp, tq=tq_p, tk=tk_p)
        xp = _post(xp, ret, diff_t, mla_t, (mc[2], mc[3], mc[4], mc[5]), lw, is_ctx=True, tm=tm,
                   tiles_per_batch=1)
        new_ret.append(jnp.swapaxes(s_fin.reshape(bp, 2, RET_HEADS, RET_DV, RET_DK), -1, -2))
        new_dk.append(dk_f.reshape(bp, sp, DIFF_HEADS, 2, DIFF_DK))
        new_dv.append(dv_f.reshape(bp, sp, DIFF_HEADS, DIFF_DV))
        new_ckv.append(ckv_f.reshape(bp, sp, MLA_KV_RANK))
        new_kr.append(kr_f[:, :MLA_ROPE].reshape(bp, sp, MLA_ROPE))

        ms = [mod[l, 1:1 + bs, i] for i in range(6)]
        s0_t = jnp.swapaxes(state_ret[:, l].astype(F32), -1, -2).reshape(bs, 2 * RET_HEADS, RET_DV, RET_DK)
        ctx_dk = cache_diff_k[:, l].reshape(bs, past, DIFF_QK_W).astype(BF16)
        ctx_dv_t = jnp.swapaxes(cache_diff_v[:, l].reshape(bs, past, DIFF_W), 1, 2).astype(BF16)
        ctx_mk, ctx_mv_t = _ctx_mla_kv(jnp.swapaxes(cache_mla_ckv[:, l].astype(F32), 1, 2),
                                       jnp.swapaxes(cache_mla_kr[:, l].astype(F32), 1, 2),
                                       lw["w_uk"], lw["w_uv"], lw["g_mk"])
        (rq, rk, rv, rg, dq_t, dk_tok, dv_t, mq_t, mk_tok, mv_t) = _premix(
            xs, ms[0], ms[1], lw, rope, is_ctx=False, tm=tm, tiles_per_batch=ss // tm)
        ret, _ = _retention(rq, rk, rv, rg, s0_t, dec_b[l], gn_col, batch=bs)
        diff_t = _diff_attention(dq_t, dk_tok, dv_t, (ctx_dk, ctx_dv_t), dl, subln_col, batch=bs, tq=tq_s,
                                 tk=tk_s, lam_init=lam_init)
        mla_t = _mla_attention(mq_t, mk_tok, mv_t, (ctx_mk, ctx_mv_t), batch=bs, tq=tq_s, tk=tk_s)
        xs = _post(xs, ret, diff_t, mla_t, (ms[2], ms[3], ms[4], ms[5]), lw, is_ctx=False, tm=tm,
                   tiles_per_batch=ss // tm)

    y_p = xp.T.reshape(bp, sp, d)
    y_s = xs.T.reshape(bs, ss, d)
    return (y_p, y_s, jnp.stack(new_ret, axis=1), jnp.stack(new_dk, axis=1), jnp.stack(new_dv, axis=1),
            jnp.stack(new_ckv, axis=1), jnp.stack(new_kr, axis=1))
```

```python
import functools
import math

import jax
import jax.numpy as jnp
from jax import lax
from jax.experimental import pallas as pl
from jax.experimental.pallas import tpu as pltpu

F32 = jnp.float32
BF16 = jnp.bfloat16
EPS = 1e-6

GRID_W = 64
RET_HEADS = 4
RET_DK = 64
RET_DV = 64
RET_CHUNK = 128
DIFF_HEADS = 4
DIFF_DK = 32
DIFF_DV = 64
MLA_HEADS = 8
MLA_NOPE = 64
MLA_ROPE = 32
MLA_DQK = MLA_NOPE + MLA_ROPE
MLA_DV = 64
MLA_Q_RANK = 768
MLA_KV_RANK = 256
ROPE_BASE = 10000.0

RET_W = RET_HEADS * RET_DV
DIFF_W = DIFF_HEADS * DIFF_DV
MLA_W = MLA_HEADS * MLA_DV
DIFF_QK_W = DIFF_HEADS * 2 * DIFF_DK
MLA_PAD = 128
MLA_QK_PAD_W = MLA_HEADS * MLA_PAD
V_PAD = 16
DV_EXT = MLA_DV + V_PAD

_SPLITS = (RET_HEADS * RET_DK, RET_HEADS * RET_DK, RET_W, RET_W,
           DIFF_QK_W, DIFF_QK_W, DIFF_W, MLA_Q_RANK, MLA_KV_RANK, MLA_ROPE)
_OFFS = tuple(int(sum(_SPLITS[:i])) for i in range(len(_SPLITS) + 1))
O_RQ, O_RK, O_RV, O_RG, O_DQ, O_DK, O_DV, O_CQ, O_CKV, O_KR, O_END = _OFFS

LOG2E = math.log2(math.e)
SCORE_LOOKAHEAD = 3
LANE = 128
VMEM_LIMIT = 56 * 1024 * 1024


def _cparams(n_grid):
    return pltpu.CompilerParams(dimension_semantics=("arbitrary",) * n_grid,
                                vmem_limit_bytes=VMEM_LIMIT)


def _full(shape):
    nd = len(shape)
    return pl.BlockSpec(shape, lambda *_: (0,) * nd)


def _silu(x):
    return x / (1.0 + jnp.exp(-x))


def _rms_scale(x, n):
    return lax.rsqrt(jnp.sum(x * x, axis=0, keepdims=True) * (1.0 / n) + EPS)


def _rope_pair(x1, x2, c, s):
    return x1 * c - x2 * s, x2 * c + x1 * s


def _with_ones_rows(v, heads):
    t = v.shape[1]
    dv = v.shape[0] // heads
    extra = jnp.where(lax.broadcasted_iota(jnp.int32, (V_PAD, t), 0) == 0, 1.0, 0.0).astype(BF16)
    parts = []
    for hd in range(heads):
        parts += [v[hd * dv:(hd + 1) * dv].astype(BF16), extra]
    return jnp.concatenate(parts, axis=0)


def _mod_kernel(c_ref, w_ref, b_ref, o_ref):
    a = _silu(c_ref[...]).astype(BF16)
    o_ref[...] = jnp.dot(a, w_ref[...].astype(BF16), preferred_element_type=F32) + b_ref[...]


def _modulation(cond, w_mod, b_mod):
    depth, d, d6 = w_mod.shape
    r = cond.shape[0]
    tn = 1536
    return pl.pallas_call(
        _mod_kernel,
        grid=(depth, d6 // tn),
        in_specs=[pl.BlockSpec((r, d), lambda l, j: (0, 0)),
                  pl.BlockSpec((None, d, tn), lambda l, j: (l, 0, j)),
                  pl.BlockSpec((None, 1, tn), lambda l, j: (l, 0, j))],
        out_specs=pl.BlockSpec((None, r, tn), lambda l, j: (l, 0, j)),
        out_shape=jax.ShapeDtypeStruct((depth, r, d6), F32),
        compiler_params=_cparams(2),
        name="adaln_mod",
    )(cond, w_mod, b_mod.reshape(depth, 1, d6))


def _mla_keys_values(ckvn_bf, kr, w_uk_ref, w_uv_ref, g_mk_ref, rope):
    t = kr.shape[1]
    kn = jnp.dot(w_uk_ref[...], ckvn_bf, preferred_element_type=F32)
    v = jnp.dot(w_uv_ref[...], ckvn_bf, preferred_element_type=F32)
    kr_ss = jnp.sum(kr * kr, axis=0, keepdims=True)
    zpad = jnp.zeros((MLA_PAD - MLA_DQK, t), F32)
    half = MLA_ROPE // 2
    heads = []
    for hd in range(MLA_HEADS):
        r0 = hd * MLA_PAD
        kh = kn[r0:r0 + MLA_NOPE]
        ss = jnp.sum(kh * kh, axis=0, keepdims=True) + kr_ss
        r = lax.rsqrt(ss * (1.0 / MLA_DQK) + EPS)
        g = g_mk_ref[r0:r0 + MLA_PAD, :]
        y_nope = kh * r * g[:MLA_NOPE]
        y_r = kr * r * g[MLA_NOPE:MLA_DQK]
        if rope is not None:
            c, s = rope
            y1, y2 = _rope_pair(y_r[:half], y_r[half:], c, s)
            heads += [y_nope, y1, y2, zpad]
        else:
            heads += [y_nope, y_r, zpad]
    return jnp.concatenate(heads, axis=0), v


def _ctx_kv_kernel(ckv_ref, kr_ref, w_uk_ref, w_uv_ref, g_mk_ref, mk_ref, mv_ref):
    k, v = _mla_keys_values(ckv_ref[...].astype(BF16), kr_ref[...], w_uk_ref, w_uv_ref, g_mk_ref, None)
    mk_ref[...] = k.T.astype(BF16)
    mv_ref[...] = _with_ones_rows(v, MLA_HEADS)


def _ctx_mla_kv(ckv_t, kr_t, w_uk, w_uv, g_mk):
    b, _, l = ckv_t.shape
    return pl.pallas_call(
        _ctx_kv_kernel,
        grid=(b,),
        in_specs=[pl.BlockSpec((None, MLA_KV_RANK, l), lambda i: (i, 0, 0)),
                  pl.BlockSpec((None, MLA_ROPE, l), lambda i: (i, 0, 0)),
                  _full(w_uk.shape), _full(w_uv.shape), _full(g_mk.shape)],
        out_specs=[pl.BlockSpec((None, l, MLA_QK_PAD_W), lambda i: (i, 0, 0)),
                   pl.BlockSpec((None, MLA_HEADS * DV_EXT, l), lambda i: (i, 0, 0))],
        out_shape=[jax.ShapeDtypeStruct((b, l, MLA_QK_PAD_W), BF16),
                   jax.ShapeDtypeStruct((b, MLA_HEADS * DV_EXT, l), BF16)],
        compiler_params=_cparams(1),
        name="ctx_mla_kv",
    )(ckv_t, kr_t, w_uk, w_uv, g_mk)


def _premix_kernel(*refs, is_ctx, tm):
    it = iter(refs)
    x_ref, shift_ref, scale_ref, n1_ref, w_in_ref = (next(it) for _ in range(5))
    g_dq_ref, g_dk_ref, g_qn_ref, g_kvn_ref = (next(it) for _ in range(4))
    w_uq_ref, g_mq_ref, w_uk_ref, w_uv_ref, g_mk_ref = (next(it) for _ in range(5))
    if is_ctx:
        rope_r = rope_s = None
    else:
        cr_ref, sr_ref, cs_ref, ss_ref = (next(it) for _ in range(4))
        rope_r = (cr_ref[...], sr_ref[...])
        rope_s = (cs_ref[...], ss_ref[...])
    rq_ref, rk_ref, rv_ref, rg_ref = (next(it) for _ in range(4))
    dq_ref, dk_ref, dv_ref, mq_ref, mk_ref, mv_ref = (next(it) for _ in range(6))
    if is_ctx:
        dkf_ref, dvf_ref, ckvf_ref, krf_ref = (next(it) for _ in range(4))

    x = x_ref[...]
    d = x.shape[0]
    h = x * _rms_scale(x, d) * n1_ref[...]
    h = (h * (1.0 + scale_ref[...]) + shift_ref[...]).astype(BF16)

    def proj(lo, hi):
        return jnp.dot(w_in_ref[lo:hi, :], h, preferred_element_type=F32)

    nch = tm // LANE

    def put_chunks(ref, val):
        for j in range(nch):
            ref[j] = val[:, j * LANE:(j + 1) * LANE].astype(ref.dtype)

    rq = proj(O_RQ, O_RK)
    rk = proj(O_RK, O_RV)
    if rope_r is not None:
        c, s = rope_r
        hk = RET_DK // 2

        def rope_heads(a):
            parts = []
            for hd in range(RET_HEADS):
                r0 = hd * RET_DK
                parts += list(_rope_pair(a[r0:r0 + hk], a[r0 + hk:r0 + RET_DK], c, s))
            return jnp.concatenate(parts, axis=0)

        rq = rope_heads(rq)
        rk = rope_heads(rk)
    put_chunks(rq_ref, rq)
    put_chunks(rk_ref, rk * (RET_DK ** -0.5))
    put_chunks(rv_ref, proj(O_RV, O_RG))
    put_chunks(rg_ref, proj(O_RG, O_DQ))

    def diff_qk(a, g_ref, out_scale):
        parts = []
        hs = DIFF_DK // 2
        for seg in range(2 * DIFF_HEADS):
            r0 = seg * DIFF_DK
            xs = a[r0:r0 + DIFF_DK]
            y = xs * _rms_scale(xs, DIFF_DK) * g_ref[r0:r0 + DIFF_DK, :]
            if out_scale != 1.0:
                y = y * out_scale
            if rope_s is not None:
                parts += list(_rope_pair(y[:hs], y[hs:], rope_s[0], rope_s[1]))
            else:
                parts.append(y)
        return jnp.concatenate(parts, axis=0)

    dq_ref[...] = diff_qk(proj(O_DQ, O_DK), g_dq_ref, DIFF_DK ** -0.5 * LOG2E).astype(BF16)
    dk = diff_qk(proj(O_DK, O_DV), g_dk_ref, 1.0)
    dk_t = dk.T
    dk_ref[...] = dk_t.astype(BF16)
    dv = proj(O_DV, O_CQ)
    dv_ref[...] = _with_ones_rows(dv, DIFF_HEADS)
    if is_ctx:
        dkf_ref[...] = dk_t
        dvf_ref[...] = dv.T

    cq = proj(O_CQ, O_CKV)
    cqn = (cq * _rms_scale(cq, MLA_Q_RANK) * g_qn_ref[...]).astype(BF16)
    mq = jnp.dot(w_uq_ref[...], cqn, preferred_element_type=F32)
    half = MLA_ROPE // 2
    parts = []
    for hd in range(MLA_HEADS):
        r0 = hd * MLA_PAD
        xs = mq[r0:r0 + MLA_PAD]
        y = xs * _rms_scale(xs, MLA_DQK) * (g_mq_ref[r0:r0 + MLA_PAD, :] * (MLA_DQK ** -0.5 * LOG2E))
        if rope_s is not None:
            y1, y2 = _rope_pair(y[MLA_NOPE:MLA_NOPE + half], y[MLA_NOPE + half:MLA_DQK], rope_s[0], rope_s[1])
            parts += [y[:MLA_NOPE], y1, y2, y[MLA_DQK:]]
        else:
            parts.append(y)
    mq_ref[...] = jnp.concatenate(parts, axis=0).astype(BF16)

    ckv = proj(O_CKV, O_KR)
    ckvn = ckv * _rms_scale(ckv, MLA_KV_RANK) * g_kvn_ref[...]
    kr = proj(O_KR, O_END)
    mk, mv = _mla_keys_values(ckvn.astype(BF16), kr, w_uk_ref, w_uv_ref, g_mk_ref, rope_s)
    mk_ref[...] = mk.T.astype(BF16)
    mv_ref[...] = _with_ones_rows(mv, MLA_HEADS)
    if is_ctx:
        ckvf_ref[...] = ckvn.T
        krf_ref[...] = jnp.concatenate([kr, jnp.zeros((LANE - MLA_ROPE, tm), F32)], axis=0).T


def _premix(x_t, shift, scale, lw, rope, *, is_ctx, tm, tiles_per_batch):
    d, n = x_t.shape
    nt = n // tm
    nch = tm // LANE
    if is_ctx:
        bidx = lambda j: (0, 0, 0)
    else:
        bidx = lambda j: (j // tiles_per_batch, 0, 0)
    in_arrays = [x_t, shift, scale, lw["n1"], lw["w_in"], lw["g_dq"], lw["g_dk"], lw["g_qn"], lw["g_kvn"],
                 lw["w_uq"], lw["g_mq"], lw["w_uk"], lw["w_uv"], lw["g_mk"]]
    in_specs = [pl.BlockSpec((d, tm), lambda j: (0, j)),
                pl.BlockSpec((None, d, 1), bidx), pl.BlockSpec((None, d, 1), bidx)]
    in_specs += [_full(a.shape) for a in in_arrays[3:]]
    if not is_ctx:
        for tab in rope:
            in_arrays.append(tab)
            in_specs.append(pl.BlockSpec((tab.shape[0], tm), lambda j: (0, j % tiles_per_batch)))

    chunk_spec = pl.BlockSpec((nch, RET_W, LANE), lambda j: (j, 0, 0))
    fm = lambda rows: pl.BlockSpec((rows, tm), lambda j: (0, j))
    tok = lambda cols: pl.BlockSpec((tm, cols), lambda j: (j, 0))
    nck = n // LANE
    out_specs = [chunk_spec] * 4 + [fm(DIFF_QK_W), tok(DIFF_QK_W), fm(DIFF_HEADS * DV_EXT),
                                    fm(MLA_QK_PAD_W), tok(MLA_QK_PAD_W), fm(MLA_HEADS * DV_EXT)]
    out_shape = [jax.ShapeDtypeStruct((nck, RET_W, LANE), BF16), jax.ShapeDtypeStruct((nck, RET_W, LANE), F32),
                 jax.ShapeDtypeStruct((nck, RET_W, LANE), BF16), jax.ShapeDtypeStruct((nck, RET_W, LANE), F32),
                 jax.ShapeDtypeStruct((DIFF_QK_W, n), BF16), jax.ShapeDtypeStruct((n, DIFF_QK_W), BF16),
                 jax.ShapeDtypeStruct((DIFF_HEADS * DV_EXT, n), BF16),
                 jax.ShapeDtypeStruct((MLA_QK_PAD_W, n), BF16), jax.ShapeDtypeStruct((n, MLA_QK_PAD_W), BF16),
                 jax.ShapeDtypeStruct((MLA_HEADS * DV_EXT, n), BF16)]
    if is_ctx:
        out_specs += [tok(DIFF_QK_W), tok(DIFF_W), tok(MLA_KV_RANK), tok(LANE)]
        out_shape += [jax.ShapeDtypeStruct((n, DIFF_QK_W), F32), jax.ShapeDtypeStruct((n, DIFF_W), F32),
                      jax.ShapeDtypeStruct((n, MLA_KV_RANK), F32), jax.ShapeDtypeStruct((n, LANE), F32)]
    return pl.pallas_call(
        functools.partial(_premix_kernel, is_ctx=is_ctx, tm=tm),
        grid=(nt,),
        in_specs=in_specs, out_specs=out_specs, out_shape=out_shape,
        compiler_params=_cparams(1),
        name="premix_ctx" if is_ctx else "premix_smp",
    )(*in_arrays)


def _retention_kernel(rq_ref, rk_ref, rv_ref, rg_ref, s0_ref, dec_ref, gn_ref, out_ref, sfin_ref,
                      o_acc, s_acc, intra_ref, vec_ref, *, nc):
    C = RET_CHUNK
    n_idx = lax.broadcasted_iota(jnp.int32, (C, C), 1).astype(F32)
    m_idx = lax.broadcasted_iota(jnp.int32, (C, C), 0).astype(F32)
    lane = lax.broadcasted_iota(jnp.int32, (8, C), 1).astype(F32)
    for d in range(2):
        for hd in range(RET_HEADS):
            i = d * RET_HEADS + hd
            z = dec_ref[i]
            lg8 = jnp.minimum(z, 0.0) - jnp.log1p(jnp.exp(-jnp.abs(z)))
            lg = jnp.broadcast_to(lg8[0:1, :], (C, C))
            dist = (n_idx - m_idx) if d == 0 else (m_idx - n_idx)
            ok = dist >= 0.0
            intra_ref[i] = jnp.where(ok, jnp.exp(lg * jnp.where(ok, dist, 0.0)), 0.0)
            if d == 0:
                qdec = jnp.exp(lg8 * (lane + 1.0))
                kdec = jnp.exp(lg8 * (C - 1.0 - lane))
            else:
                qdec = jnp.exp(lg8 * (C - lane))
                kdec = jnp.exp(lg8 * lane)
            vec_ref[i, 0] = qdec
            vec_ref[i, 1] = kdec
            vec_ref[i, 2] = jnp.exp(lg8 * float(C))
    s_acc[...] = s0_ref[...]
    o_acc[...] = jnp.zeros_like(o_acc)

    row_head = lax.broadcasted_iota(jnp.int32, (RET_HEADS * RET_DK, C), 0) // RET_DK

    def step(c, carry):
        for d in range(2):
            cc = c if d == 0 else nc - 1 - c
            q_all = rq_ref[cc]
            k_all = rk_ref[cc]
            k_tok = k_all.T.astype(BF16)
            for hd in range(RET_HEADS):
                i = d * RET_HEADS + hd
                r0 = hd * RET_DK
                q = q_all[r0:r0 + RET_DK]
                k = k_all[r0:r0 + RET_DK]
                v = rv_ref[cc, r0:r0 + RET_DV, :]
                q_m = jnp.where(row_head == hd, q_all, jnp.zeros_like(q_all))
                att_t = jnp.dot(k_tok, q_m, preferred_element_type=F32) * intra_ref[i]
                s_t = s_acc[i]
                o = jnp.dot(v, att_t.astype(BF16), preferred_element_type=F32)
                o = o + jnp.dot(s_t.astype(BF16), q, preferred_element_type=F32) * vec_ref[i, 0][0:1, :]
                kd = (k * vec_ref[i, 1][0:1, :]).astype(BF16)
                kv = lax.dot_general(v, kd, (((1,), (1,)), ((), ())), preferred_element_type=F32)
                s_acc[i] = s_t * vec_ref[i, 2][0:1, 0:RET_DK] + kv
                o_acc[cc, r0:r0 + RET_DV, :] += o
        return carry

    lax.fori_loop(0, nc, step, 0)
    sfin_ref[...] = s_acc[...]

    def finish(c, carry):
        o = o_acc[c]
        g = rg_ref[c]
        parts = []
        for hd in range(RET_HEADS):
            r0 = hd * RET_DV
            oh = o[r0:r0 + RET_DV]
            mu = jnp.mean(oh, axis=0, keepdims=True)
            var = jnp.mean(jnp.square(oh - mu), axis=0, keepdims=True)
            y = (oh - mu) * lax.rsqrt(var + EPS) * gn_ref[r0:r0 + RET_DV, :]
            parts.append(y * _silu(g[r0:r0 + RET_DV]))
        out_ref[c] = jnp.concatenate(parts, axis=0).astype(BF16)
        return carry

    lax.fori_loop(0, nc, finish, 0)


def _retention(rq, rk, rv, rg, s0_t, dec_b, gn_col, *, batch):
    nck = rq.shape[0]
    nc = nck // batch
    blk = pl.BlockSpec((nc, RET_W, LANE), lambda b: (b, 0, 0))
    nst = 2 * RET_HEADS
    st_spec = pl.BlockSpec((None, nst, RET_DV, RET_DK), lambda b: (b, 0, 0, 0))
    return pl.pallas_call(
        functools.partial(_retention_kernel, nc=nc),
        grid=(batch,),
        in_specs=[blk, blk, blk, blk, st_spec, _full(dec_b.shape), _full(gn_col.shape)],
        out_specs=[blk, st_spec],
        out_shape=[jax.ShapeDtypeStruct((nck, RET_W, LANE), BF16),
                   jax.ShapeDtypeStruct((batch, nst, RET_DV, RET_DK), F32)],
        scratch_shapes=[pltpu.VMEM((nc, RET_W, LANE), F32),
                        pltpu.VMEM((nst, RET_DV, RET_DK), F32),
                        pltpu.VMEM((nst, RET_CHUNK, RET_CHUNK), F32),
                        pltpu.VMEM((nst, 3, 8, LANE), F32)],
        compiler_params=_cparams(1),
        name="retention",
    )(rq, rk, rv, rg, s0_t, dec_b, gn_col)


def _softmax_pv(q_t, k_parts, v_parts, tk):
    tq = q_t.shape[1]
    dv = DV_EXT - V_PAD
    m = jnp.full((1, tq), -jnp.inf, F32)
    acc = jnp.zeros((DV_EXT, tq), F32)
    blocks = [(k_ref, v_ref, r0) for (k_ref, rows), v_ref in zip(k_parts, v_parts) for r0 in range(0, rows, tk)]

    def scores(blk):
        k_ref, _, r0 = blk
        return jnp.dot(k_ref[r0:r0 + tk, :], q_t, preferred_element_type=F32)

    pending = [scores(b) for b in blocks[:SCORE_LOOKAHEAD]]
    for i, (_, v_ref, r0) in enumerate(blocks):
        if i + SCORE_LOOKAHEAD < len(blocks):
            pending.append(scores(blocks[i + SCORE_LOOKAHEAD]))
        s = pending.pop(0)
        m_new = jnp.maximum(m, jnp.max(s, axis=0, keepdims=True))
        alpha = jnp.exp2(m - m_new)
        p = jnp.exp2(s - m_new)
        acc = acc * alpha + jnp.dot(v_ref[:, r0:r0 + tk], p.astype(BF16), preferred_element_type=F32)
        m = m_new
    return acc[:dv] / acc[dv:dv + 1]


def _diff_attn_kernel(*refs, has_ctx, rows_main, rows_ctx, tk, lam_init):
    it = iter(refs)
    q_ref, k_ref, v_ref = (next(it) for _ in range(3))
    if has_ctx:
        kc_ref, vc_ref = (next(it) for _ in range(2))
    dl_ref, g_ref, out_ref = (next(it) for _ in range(3))
    hd = pl.program_id(1)
    q_full = q_ref[...]
    seg = lax.broadcasted_iota(jnp.int32, q_full.shape, 0) // DIFF_DK
    k_parts = [(k_ref, rows_main)]
    v_parts = [v_ref]
    if has_ctx:
        k_parts.append((kc_ref, rows_ctx))
        v_parts.append(vc_ref)
    outs = []
    for mm in range(2):
        q_m = jnp.where(seg == 2 * hd + mm, q_full, jnp.zeros_like(q_full))
        outs.append(_softmax_pv(q_m, k_parts, v_parts, tk))
    dl = dl_ref[...]
    lam = (jnp.exp(jnp.sum(dl[0:1] * dl[1:2], axis=1, keepdims=True))
           - jnp.exp(jnp.sum(dl[2:3] * dl[3:4], axis=1, keepdims=True)) + lam_init)
    o = outs[0] - lam * outs[1]
    y = o * _rms_scale(o, DIFF_DV) * g_ref[...]
    out_ref[...] = (y * (1.0 - lam_init)).astype(BF16)


def _diff_attention(dq_t, dk_tok, dv_t, ctx, dl, g_col, *, batch, tq, tk, lam_init):
    n_all = dq_t.shape[1]
    n = n_all // batch
    qt = n // tq
    has_ctx = ctx is not None
    rows_ctx = ctx[0].shape[1] if has_ctx else 0
    in_arrays = [dq_t, dk_tok, dv_t]
    in_specs = [pl.BlockSpec((DIFF_QK_W, tq), lambda b, h, i: (0, b * qt + i)),
                pl.BlockSpec((n, DIFF_QK_W), lambda b, h, i: (b, 0)),
                pl.BlockSpec((DV_EXT, n), lambda b, h, i: (h, b))]
    if has_ctx:
        in_arrays += list(ctx)
        in_specs += [pl.BlockSpec((None, rows_ctx, DIFF_QK_W), lambda b, h, i: (b, 0, 0)),
                     pl.BlockSpec((None, DV_EXT, rows_ctx), lambda b, h, i: (b, h, 0))]
    in_arrays += [dl, g_col]
    in_specs += [_full(dl.shape), _full(g_col.shape)]
    return pl.pallas_call(
        functools.partial(_diff_attn_kernel, has_ctx=has_ctx, rows_main=n, rows_ctx=rows_ctx, tk=tk,
                          lam_init=lam_init),
        grid=(batch, DIFF_HEADS, qt),
        in_specs=in_specs,
        out_specs=pl.BlockSpec((DIFF_DV, tq), lambda b, h, i: (h, b * qt + i)),
        out_shape=jax.ShapeDtypeStruct((DIFF_W, n_all), BF16),
        compiler_params=_cparams(3),
        name="diff_attn_smp" if has_ctx else "diff_attn_ctx",
    )(*in_arrays)


def _mla_attn_kernel(*refs, has_ctx, rows_main, rows_ctx, tk):
    it = iter(refs)
    q_ref, k_ref, v_ref = (next(it) for _ in range(3))
    if has_ctx:
        kc_ref, vc_ref = (next(it) for _ in range(2))
    out_ref = next(it)
    k_parts = [(k_ref, rows_main)]
    v_parts = [v_ref]
    if has_ctx:
        k_parts.append((kc_ref, rows_ctx))
        v_parts.append(vc_ref)
    out_ref[...] = _softmax_pv(q_ref[...], k_parts, v_parts, tk).astype(BF16)


def _mla_attention(mq_t, mk_tok, mv_t, ctx, *, batch, tq, tk):
    n_all = mq_t.shape[1]
    n = n_all // batch
    qt = n // tq
    has_ctx = ctx is not None
    rows_ctx = ctx[0].shape[1] if has_ctx else 0
    in_arrays = [mq_t, mk_tok, mv_t]
    in_specs = [pl.BlockSpec((MLA_PAD, tq), lambda b, h, i: (h, b * qt + i)),
                pl.BlockSpec((n, MLA_PAD), lambda b, h, i: (b, h)),
                pl.BlockSpec((DV_EXT, n), lambda b, h, i: (h, b))]
    if has_ctx:
        in_arrays += list(ctx)
        in_specs += [pl.BlockSpec((None, rows_ctx, MLA_PAD), lambda b, h, i: (b, 0, h)),
                     pl.BlockSpec((None, DV_EXT, rows_ctx), lambda b, h, i: (b, h, 0))]
    return pl.pallas_call(
        functools.partial(_mla_attn_kernel, has_ctx=has_ctx, rows_main=n, rows_ctx=rows_ctx, tk=tk),
        grid=(batch, MLA_HEADS, qt),
        in_specs=in_specs,
        out_specs=pl.BlockSpec((MLA_DV, tq), lambda b, h, i: (h, b * qt + i)),
        out_shape=jax.ShapeDtypeStruct((MLA_W, n_all), BF16),
        compiler_params=_cparams(3),
        name="mla_attn_smp" if has_ctx else "mla_attn_ctx",
    )(*in_arrays)


def _post_kernel(x_ref, ret_ref, diff_ref, mla_ref, g1_ref, sh2_ref, sc2_ref, g2_ref, n2_ref,
                 w_out_ref, w_gu_ref, w_down_ref, out_ref, *, tm, d_ff, ff_chunk):
    nch = tm // LANE
    ret = jnp.concatenate([ret_ref[j] for j in range(nch)], axis=1)
    mixed = jnp.concatenate([ret, diff_ref[...], mla_ref[...]], axis=0)
    x = x_ref[...]
    d = x.shape[0]
    x1 = x + g1_ref[...] * jnp.dot(w_out_ref[...], mixed, preferred_element_type=F32)
    h = x1 * _rms_scale(x1, d) * n2_ref[...]
    h = (h * (1.0 + sc2_ref[...]) + sh2_ref[...]).astype(BF16)
    ffn = None
    for lo in range(0, d_ff, ff_chunk):
        g = jnp.dot(w_gu_ref[lo:lo + ff_chunk, :], h, preferred_element_type=F32)
        u = jnp.dot(w_gu_ref[d_ff + lo:d_ff + lo + ff_chunk, :], h, preferred_element_type=F32)
        a = (_silu(g) * u).astype(BF16)
        part = jnp.dot(w_down_ref[:, lo:lo + ff_chunk], a, preferred_element_type=F32)
        ffn = part if ffn is None else ffn + part
    out_ref[...] = x1 + g2_ref[...] * ffn


def _post(x_t, ret, diff_t, mla_t, mods, lw, *, is_ctx, tm, tiles_per_batch):
    d, n = x_t.shape
    nt = n // tm
    nch = tm // LANE
    d_ff = lw["w_down"].shape[1]
    ff_chunk = d_ff // 2
    if is_ctx:
        bidx = lambda j: (0, 0, 0)
    else:
        bidx = lambda j: (j // tiles_per_batch, 0, 0)
    col = pl.BlockSpec((None, d, 1), bidx)
    const = lambda a: pl.BlockSpec(a.shape, lambda j: (0,) * a.ndim, pipeline_mode=pl.Buffered(1))
    g1, sh2, sc2, g2 = mods
    return pl.pallas_call(
        functools.partial(_post_kernel, tm=tm, d_ff=d_ff, ff_chunk=ff_chunk),
        grid=(nt,),
        in_specs=[pl.BlockSpec((d, tm), lambda j: (0, j)),
                  pl.BlockSpec((nch, RET_W, LANE), lambda j: (j, 0, 0)),
                  pl.BlockSpec((DIFF_W, tm), lambda j: (0, j)),
                  pl.BlockSpec((MLA_W, tm), lambda j: (0, j)),
                  col, col, col, col, const(lw["n2"]),
                  const(lw["w_out"]), const(lw["w_gu"]), const(lw["w_down"])],
        out_specs=pl.BlockSpec((d, tm), lambda j: (0, j)),
        out_shape=jax.ShapeDtypeStruct((d, n), F32),
        compiler_params=_cparams(1),
        name="post_ctx" if is_ctx else "post_smp",
    )(x_t, ret, diff_t, mla_t, g1, sh2, sc2, g2, lw["n2"], lw["w_out"], lw["w_gu"], lw["w_down"])


def _rope_tables_t(n, rot_dim):
    rows = n // GRID_W
    row = jnp.repeat(jnp.arange(rows, dtype=F32), GRID_W)
    col = jnp.tile(jnp.arange(GRID_W, dtype=F32), rows)
    n_freq = rot_dim // 4
    inv = 1.0 / (ROPE_BASE ** (jnp.arange(n_freq, dtype=F32) / n_freq))
    ang = jnp.concatenate([inv[:, None] * row[None, :], inv[:, None] * col[None, :]], axis=0)
    return jnp.cos(ang), jnp.sin(ang)


def _col(v):
    return v.astype(F32).reshape(-1, 1)


def _pad_heads_rows(w_t, used, pad):
    hk = w_t.shape[0] // used
    w3 = w_t.reshape(hk, used, w_t.shape[1])
    w3 = jnp.pad(w3, ((0, 0), (0, pad - used), (0, 0)))
    return w3.reshape(hk * pad, w_t.shape[1])


def _layer_weights(l, w_in, norm1, norm2, diff_qk_gain, mla_q_norm, mla_kv_norm, w_uq, w_ukv, mla_qk_gain,
                   w_out, w_gu, w_down):
    w_ukv_t = w_ukv[l].T.reshape(MLA_HEADS, MLA_NOPE + MLA_DV, MLA_KV_RANK)
    w_uk = jnp.pad(w_ukv_t[:, :MLA_NOPE], ((0, 0), (0, MLA_PAD - MLA_NOPE), (0, 0)))
    g_pad = lambda g: jnp.tile(jnp.pad(g.astype(F32), (0, MLA_PAD - MLA_DQK)), MLA_HEADS).reshape(-1, 1)
    return {
        "n1": _col(norm1[l]), "n2": _col(norm2[l]),
        "w_in": w_in[l].T.astype(BF16),
        "g_dq": _col(jnp.tile(diff_qk_gain[l, 0], 2 * DIFF_HEADS)),
        "g_dk": _col(jnp.tile(diff_qk_gain[l, 1], 2 * DIFF_HEADS)),
        "g_qn": _col(mla_q_norm[l]), "g_kvn": _col(mla_kv_norm[l]),
        "w_uq": _pad_heads_rows(w_uq[l].T, MLA_DQK, MLA_PAD).astype(BF16),
        "g_mq": g_pad(mla_qk_gain[l, 0]), "g_mk": g_pad(mla_qk_gain[l, 1]),
        "w_uk": w_uk.reshape(MLA_QK_PAD_W, MLA_KV_RANK).astype(BF16),
        "w_uv": w_ukv_t[:, MLA_NOPE:].reshape(MLA_W, MLA_KV_RANK).astype(BF16),
        "w_out": w_out[l].T.astype(BF16), "w_gu": w_gu[l].T.astype(BF16), "w_down": w_down[l].T.astype(BF16),
    }


def kernel(x_prompt, x_sample, c, state_ret, cache_diff_k, cache_diff_v, cache_mla_ckv, cache_mla_kr, c_ctx,
           w_mod, b_mod, norm1, norm2, w_in, ret_decay, ret_gn_gain, diff_qk_gain, diff_lambda, diff_subln_gain,
           mla_q_norm, mla_kv_norm, w_uq, w_ukv, mla_qk_gain, w_out, w_gu, w_down):
    depth = w_in.shape[0]
    bp, sp, d = x_prompt.shape
    bs, ss, _ = x_sample.shape
    past = cache_diff_k.shape[2]

    tm = 512
    tq_s, tk_s = 512, 256
    tq_p = tk_p = sp

    n_cond = 1 + bs
    r_pad = -(-n_cond // 16) * 16
    cond = jnp.concatenate([c_ctx[None, :], c, jnp.zeros((r_pad - n_cond, d), F32)], axis=0)
    mod = _modulation(cond, w_mod, b_mod)
    mod = mod.reshape(depth, r_pad, 6, d, 1)

    xp = x_prompt.reshape(bp * sp, d).T
    xs = x_sample.reshape(bs * ss, d).T
    rope = _rope_tables_t(ss, RET_DK) + _rope_tables_t(ss, DIFF_DK)
    dec_b = jnp.broadcast_to(ret_decay.astype(F32).reshape(depth, 2 * RET_HEADS, 1, 1),
                             (depth, 2 * RET_HEADS, 8, LANE))
    s0_zero = jnp.zeros((bp, 2 * RET_HEADS, RET_DV, RET_DK), F32)
    ones_rows = jnp.zeros((bs, DIFF_HEADS, V_PAD, past), BF16).at[:, :, 0, :].set(1.0)

    new_ret, new_dk, new_dv, new_ckv, new_kr = [], [], [], [], []
    for l in range(depth):
        lw = _layer_weights(l, w_in, norm1, norm2, diff_qk_gain, mla_q_norm, mla_kv_norm, w_uq, w_ukv,
                            mla_qk_gain, w_out, w_gu, w_down)
        lam_init = 0.8 - 0.6 * math.exp(-0.3 * l)
        gn_col = _col(ret_gn_gain[l])
        subln_col = _col(diff_subln_gain[l])
        dl = diff_lambda[l].astype(F32)

        mc = [mod[l, 0:1, i] for i in range(6)]
        (rq, rk, rv, rg, dq_t, dk_tok, dv_t, mq_t, mk_tok, mv_t, dk_f, dv_f, ckv_f, kr_f) = _premix(
            xp, mc[0], mc[1], lw, None, is_ctx=True, tm=tm, tiles_per_batch=1)
        ret, s_fin = _retention(rq, rk, rv, rg, s0_zero, dec_b[l], gn_col, batch=bp)
        diff_t = _diff_attention(dq_t, dk_tok, dv_t, None, dl, subln_col, batch=bp, tq=tq_p, tk=tk_p,
                                 lam_init=lam_init)
        mla_t = _mla_attention(mq_t, mk_tok, mv_t, None, batch=bp, tq=tq_p, tk=tk_p)
        xp = _post(xp, ret, diff_t, mla_t, (mc[2], mc[3], mc[4], mc[5]), lw, is_ctx=True, tm=tm,
                   tiles_per_batch=1)
        new_ret.append(jnp.swapaxes(s_fin.reshape(bp, 2, RET_HEADS, RET_DV, RET_DK), -1, -2))
        new_dk.append(dk_f.reshape(bp, sp, DIFF_HEADS, 2, DIFF_DK))
        new_dv.append(dv_f.reshape(bp, sp, DIFF_HEADS, DIFF_DV))
        new_ckv.append(ckv_f.reshape(bp, sp, MLA_KV_RANK))
        new_kr.append(kr_f[:, :MLA_ROPE].reshape(bp, sp, MLA_ROPE))

        ms = [mod[l, 1:1 + bs, i] for i in range(6)]
        s0_t = jnp.swapaxes(state_ret[:, l].astype(F32), -1, -2).reshape(bs, 2 * RET_HEADS, RET_DV, RET_DK)
        ctx_dk = cache_diff_k[:, l].reshape(bs, past, DIFF_QK_W).astype(BF16)
        ctx_dv_t = jnp.swapaxes(cache_diff_v[:, l].reshape(bs, past, DIFF_HEADS, DIFF_DV), 1, 3)
        ctx_dv_t = jnp.concatenate([jnp.swapaxes(ctx_dv_t, 1, 2).astype(BF16), ones_rows], axis=2)
        ctx_dv_t = ctx_dv_t.reshape(bs, DIFF_HEADS * DV_EXT, past)
        ctx_mk, ctx_mv_t = _ctx_mla_kv(jnp.swapaxes(cache_mla_ckv[:, l].astype(F32), 1, 2),
                                       jnp.swapaxes(cache_mla_kr[:, l].astype(F32), 1, 2),
                                       lw["w_uk"], lw["w_uv"], lw["g_mk"])
        (rq, rk, rv, rg, dq_t, dk_tok, dv_t, mq_t, mk_tok, mv_t) = _premix(
            xs, ms[0], ms[1], lw, rope, is_ctx=False, tm=tm, tiles_per_batch=ss // tm)
        ret, _ = _retention(rq, rk, rv, rg, s0_t, dec_b[l], gn_col, batch=bs)
        diff_t = _diff_attention(dq_t, dk_tok, dv_t, (ctx_dk, ctx_dv_t), dl, subln_col, batch=bs, tq=tq_s,
                                 tk=tk_s, lam_init=lam_init)
        mla_t = _mla_attention(mq_t, mk_tok, mv_t, (ctx_mk, ctx_mv_t), batch=bs, tq=tq_s, tk=tk_s)
        xs = _post(xs, ret, diff_t, mla_t, (ms[2], ms[3], ms[4], ms[5]), lw, is_ctx=False, tm=tm,
                   tiles_per_batch=ss // tm)

    y_p = xp.T.reshape(bp, sp, d)
    y_s = xs.T.reshape(bs, ss, d)
    return (y_p, y_s, jnp.stack(new_ret, axis=1), jnp.stack(new_dk, axis=1), jnp.stack(new_dv, axis=1),
            jnp.stack(new_ckv, axis=1), jnp.stack(new_kr, axis=1))
```

```python
import functools
import math

import jax
import jax.numpy as jnp
from jax import lax
from jax.experimental import pallas as pl
from jax.experimental.pallas import tpu as pltpu

F32 = jnp.float32
BF16 = jnp.bfloat16
EPS = 1e-6

GRID_W = 64
RET_HEADS = 4
RET_DK = 64
RET_DV = 64
RET_CHUNK = 128
DIFF_HEADS = 4
DIFF_DK = 32
DIFF_DV = 64
MLA_HEADS = 8
MLA_NOPE = 64
MLA_ROPE = 32
MLA_DQK = MLA_NOPE + MLA_ROPE
MLA_DV = 64
MLA_Q_RANK = 768
MLA_KV_RANK = 256
ROPE_BASE = 10000.0

RET_W = RET_HEADS * RET_DV
DIFF_W = DIFF_HEADS * DIFF_DV
MLA_W = MLA_HEADS * MLA_DV
DIFF_QK_W = DIFF_HEADS * 2 * DIFF_DK
MLA_PAD = 128
MLA_QK_PAD_W = MLA_HEADS * MLA_PAD
V_PAD = 16
DV_EXT = MLA_DV + V_PAD

_SPLITS = (RET_HEADS * RET_DK, RET_HEADS * RET_DK, RET_W, RET_W,
           DIFF_QK_W, DIFF_QK_W, DIFF_W, MLA_Q_RANK, MLA_KV_RANK, MLA_ROPE)
_OFFS = tuple(int(sum(_SPLITS[:i])) for i in range(len(_SPLITS) + 1))
O_RQ, O_RK, O_RV, O_RG, O_DQ, O_DK, O_DV, O_CQ, O_CKV, O_KR, O_END = _OFFS

LOG2E = math.log2(math.e)
SCORE_LOOKAHEAD = 3
INTERLEAVE_PROBLEMS = True
SAFE_EXP_RANGE = 100.0
BOUND_SLACK = 1.0 + 2.0 ** -6
LANE = 128
VMEM_LIMIT = 56 * 1024 * 1024


def _cparams(n_grid):
    return pltpu.CompilerParams(dimension_semantics=("arbitrary",) * n_grid,
                                vmem_limit_bytes=VMEM_LIMIT)


def _full(shape):
    nd = len(shape)
    return pl.BlockSpec(shape, lambda *_: (0,) * nd)


def _silu(x):
    return x / (1.0 + jnp.exp(-x))


def _rms_scale(x, n):
    return lax.rsqrt(jnp.sum(x * x, axis=0, keepdims=True) * (1.0 / n) + EPS)


def _rope_pair(x1, x2, c, s):
    return x1 * c - x2 * s, x2 * c + x1 * s


def _with_ones_rows(v, heads):
    t = v.shape[1]
    dv = v.shape[0] // heads
    extra = jnp.where(lax.broadcasted_iota(jnp.int32, (V_PAD, t), 0) == 0, 1.0, 0.0).astype(BF16)
    parts = []
    for hd in range(heads):
        parts += [v[hd * dv:(hd + 1) * dv].astype(BF16), extra]
    return jnp.concatenate(parts, axis=0)


def _mod_kernel(c_ref, w_ref, b_ref, o_ref):
    a = _silu(c_ref[...]).astype(BF16)
    o_ref[...] = jnp.dot(a, w_ref[...].astype(BF16), preferred_element_type=F32) + b_ref[...]


def _modulation(cond, w_mod, b_mod):
    depth, d, d6 = w_mod.shape
    r = cond.shape[0]
    tn = 1536
    return pl.pallas_call(
        _mod_kernel,
        grid=(depth, d6 // tn),
        in_specs=[pl.BlockSpec((r, d), lambda l, j: (0, 0)),
                  pl.BlockSpec((None, d, tn), lambda l, j: (l, 0, j)),
                  pl.BlockSpec((None, 1, tn), lambda l, j: (l, 0, j))],
        out_specs=pl.BlockSpec((None, r, tn), lambda l, j: (l, 0, j)),
        out_shape=jax.ShapeDtypeStruct((depth, r, d6), F32),
        compiler_params=_cparams(2),
        name="adaln_mod",
    )(cond, w_mod, b_mod.reshape(depth, 1, d6))


def _mla_keys_values(ckvn_bf, kr, w_uk_ref, w_uv_ref, g_mk_ref, rope):
    t = kr.shape[1]
    kn = jnp.dot(w_uk_ref[...], ckvn_bf, preferred_element_type=F32)
    v = jnp.dot(w_uv_ref[...], ckvn_bf, preferred_element_type=F32)
    kr_ss = jnp.sum(kr * kr, axis=0, keepdims=True)
    zpad = jnp.zeros((MLA_PAD - MLA_DQK, t), F32)
    half = MLA_ROPE // 2
    heads = []
    for hd in range(MLA_HEADS):
        r0 = hd * MLA_PAD
        kh = kn[r0:r0 + MLA_NOPE]
        ss = jnp.sum(kh * kh, axis=0, keepdims=True) + kr_ss
        r = lax.rsqrt(ss * (1.0 / MLA_DQK) + EPS)
        g = g_mk_ref[r0:r0 + MLA_PAD, :]
        y_nope = kh * r * g[:MLA_NOPE]
        y_r = kr * r * g[MLA_NOPE:MLA_DQK]
        if rope is not None:
            c, s = rope
            y1, y2 = _rope_pair(y_r[:half], y_r[half:], c, s)
            heads += [y_nope, y1, y2, zpad]
        else:
            heads += [y_nope, y_r, zpad]
    return jnp.concatenate(heads, axis=0), v


def _group_sq_norms(a, rows_per_group):
    groups = a.shape[0] // rows_per_group
    return jnp.concatenate([jnp.sum(jnp.square(a[g * rows_per_group:(g + 1) * rows_per_group]), axis=0, keepdims=True)
                            for g in range(groups)], axis=0)


def _lane_tile_max(sq):
    return functools.reduce(jnp.maximum, [sq[:, t * LANE:(t + 1) * LANE] for t in range(sq.shape[1] // LANE)])


def _ctx_kv_kernel(ckv_ref, kr_ref, dk_ref, w_uk_ref, w_uv_ref, g_mk_ref, mk_ref, mv_ref, sq_ref):
    k, v = _mla_keys_values(ckv_ref[...].astype(BF16), kr_ref[...], w_uk_ref, w_uv_ref, g_mk_ref, None)
    mk_ref[...] = k.T.astype(BF16)
    mv_ref[...] = _with_ones_rows(v, MLA_HEADS)
    sq_ref[0] = _lane_tile_max(_group_sq_norms(k, MLA_PAD))
    sq_ref[1] = _lane_tile_max(_group_sq_norms(dk_ref[...], DIFF_DK))


def _ctx_mla_kv(ckv_t, kr_t, dk_t, w_uk, w_uv, g_mk):
    b, _, l = ckv_t.shape
    return pl.pallas_call(
        _ctx_kv_kernel,
        grid=(b,),
        in_specs=[pl.BlockSpec((None, MLA_KV_RANK, l), lambda i: (i, 0, 0)),
                  pl.BlockSpec((None, MLA_ROPE, l), lambda i: (i, 0, 0)),
                  pl.BlockSpec((None, DIFF_QK_W, l), lambda i: (i, 0, 0)),
                  _full(w_uk.shape), _full(w_uv.shape), _full(g_mk.shape)],
        out_specs=[pl.BlockSpec((None, l, MLA_QK_PAD_W), lambda i: (i, 0, 0)),
                   pl.BlockSpec((None, MLA_HEADS * DV_EXT, l), lambda i: (i, 0, 0)),
                   pl.BlockSpec((None, 2, 8, LANE), lambda i: (i, 0, 0, 0))],
        out_shape=[jax.ShapeDtypeStruct((b, l, MLA_QK_PAD_W), BF16),
                   jax.ShapeDtypeStruct((b, MLA_HEADS * DV_EXT, l), BF16),
                   jax.ShapeDtypeStruct((b, 2, 8, LANE), F32)],
        compiler_params=_cparams(1),
        name="ctx_mla_kv",
    )(ckv_t, kr_t, dk_t, w_uk, w_uv, g_mk)


def _premix_kernel(*refs, is_ctx, tm, bpt, tpb):
    it = iter(refs)
    x_ref, shift_ref, scale_ref, n1_ref, w_in_ref = (next(it) for _ in range(5))
    g_dq_ref, g_dk_ref, g_qn_ref, g_kvn_ref = (next(it) for _ in range(4))
    w_uq_ref, g_mq_ref, w_uk_ref, w_uv_ref, g_mk_ref = (next(it) for _ in range(5))
    if is_ctx:
        rope_r = rope_s = None
    else:
        cr_ref, sr_ref, cs_ref, ss_ref = (next(it) for _ in range(4))
        rope_r = (cr_ref[...], sr_ref[...])
        rope_s = (cs_ref[...], ss_ref[...])
    rq_ref, rk_ref, rv_ref, rg_ref = (next(it) for _ in range(4))
    dq_ref, dk_ref, dv_ref, mq_ref, mk_ref, mv_ref, sq_ref = (next(it) for _ in range(7))
    if is_ctx:
        dkf_ref, dvf_ref, ckvf_ref, krf_ref = (next(it) for _ in range(4))

    def put_sq_max(idx, a, rows_per_group):
        sq = _group_sq_norms(a, rows_per_group)
        w = tm // bpt
        for bi in range(bpt):
            part = _lane_tile_max(sq[:, bi * w:(bi + 1) * w])
            if tpb == 1:
                sq_ref[bi, idx] = part
            else:
                first = pl.program_id(0) % tpb == 0

                @pl.when(first)
                def _():
                    sq_ref[bi, idx] = part

                @pl.when(jnp.logical_not(first))
                def _():
                    sq_ref[bi, idx] = jnp.maximum(sq_ref[bi, idx], part)

    x = x_ref[...]
    d = x.shape[0]
    h = x * _rms_scale(x, d) * n1_ref[...]
    h = (h * (1.0 + scale_ref[...]) + shift_ref[...]).astype(BF16)

    def proj(lo, hi):
        return jnp.dot(w_in_ref[lo:hi, :], h, preferred_element_type=F32)

    nch = tm // LANE

    def put_chunks(ref, val):
        for j in range(nch):
            ref[j] = val[:, j * LANE:(j + 1) * LANE].astype(ref.dtype)

    rq = proj(O_RQ, O_RK)
    rk = proj(O_RK, O_RV)
    if rope_r is not None:
        c, s = rope_r
        hk = RET_DK // 2

        def rope_heads(a):
            parts = []
            for hd in range(RET_HEADS):
                r0 = hd * RET_DK
                parts += list(_rope_pair(a[r0:r0 + hk], a[r0 + hk:r0 + RET_DK], c, s))
            return jnp.concatenate(parts, axis=0)

        rq = rope_heads(rq)
        rk = rope_heads(rk)
    put_chunks(rq_ref, rq)
    put_chunks(rk_ref, rk * (RET_DK ** -0.5))
    put_chunks(rv_ref, proj(O_RV, O_RG))
    put_chunks(rg_ref, proj(O_RG, O_DQ))

    def diff_qk(a, g_ref, out_scale):
        parts = []
        hs = DIFF_DK // 2
        for seg in range(2 * DIFF_HEADS):
            r0 = seg * DIFF_DK
            xs = a[r0:r0 + DIFF_DK]
            y = xs * _rms_scale(xs, DIFF_DK) * g_ref[r0:r0 + DIFF_DK, :]
            if out_scale != 1.0:
                y = y * out_scale
            if rope_s is not None:
                parts += list(_rope_pair(y[:hs], y[hs:], rope_s[0], rope_s[1]))
            else:
                parts.append(y)
        return jnp.concatenate(parts, axis=0)

    dq = diff_qk(proj(O_DQ, O_DK), g_dq_ref, DIFF_DK ** -0.5 * LOG2E)
    dq_ref[...] = dq.astype(BF16)
    put_sq_max(0, dq, DIFF_DK)
    dk = diff_qk(proj(O_DK, O_DV), g_dk_ref, 1.0)
    dk_t = dk.T
    dk_ref[...] = dk_t.astype(BF16)
    put_sq_max(1, dk, DIFF_DK)
    dv = proj(O_DV, O_CQ)
    dv_ref[...] = _with_ones_rows(dv, DIFF_HEADS)
    if is_ctx:
        dkf_ref[...] = dk_t
        dvf_ref[...] = dv.T

    cq = proj(O_CQ, O_CKV)
    cqn = (cq * _rms_scale(cq, MLA_Q_RANK) * g_qn_ref[...]).astype(BF16)
    mq = jnp.dot(w_uq_ref[...], cqn, preferred_element_type=F32)
    half = MLA_ROPE // 2
    parts = []
    for hd in range(MLA_HEADS):
        r0 = hd * MLA_PAD
        xs = mq[r0:r0 + MLA_PAD]
        y = xs * _rms_scale(xs, MLA_DQK) * (g_mq_ref[r0:r0 + MLA_PAD, :] * (MLA_DQK ** -0.5 * LOG2E))
        if rope_s is not None:
            y1, y2 = _rope_pair(y[MLA_NOPE:MLA_NOPE + half], y[MLA_NOPE + half:MLA_DQK], rope_s[0], rope_s[1])
            parts += [y[:MLA_NOPE], y1, y2, y[MLA_DQK:]]
        else:
            parts.append(y)
    mq_n = jnp.concatenate(parts, axis=0)
    mq_ref[...] = mq_n.astype(BF16)
    put_sq_max(2, mq_n, MLA_PAD)

    ckv = proj(O_CKV, O_KR)
    ckvn = ckv * _rms_scale(ckv, MLA_KV_RANK) * g_kvn_ref[...]
    kr = proj(O_KR, O_END)
    mk, mv = _mla_keys_values(ckvn.astype(BF16), kr, w_uk_ref, w_uv_ref, g_mk_ref, rope_s)
    mk_ref[...] = mk.T.astype(BF16)
    mv_ref[...] = _with_ones_rows(mv, MLA_HEADS)
    put_sq_max(3, mk, MLA_PAD)
    if is_ctx:
        ckvf_ref[...] = ckvn.T
        krf_ref[...] = jnp.concatenate([kr, jnp.zeros((LANE - MLA_ROPE, tm), F32)], axis=0).T


def _premix(x_t, shift, scale, lw, rope, *, is_ctx, tm, seq):
    d, n = x_t.shape
    nt = n // tm
    nch = tm // LANE
    tiles_per_batch = max(1, seq // tm)
    bpt = max(1, tm // seq)
    if is_ctx:
        bidx = lambda j: (0, 0, 0)
    else:
        bidx = lambda j: (j // tiles_per_batch, 0, 0)
    in_arrays = [x_t, shift, scale, lw["n1"], lw["w_in"], lw["g_dq"], lw["g_dk"], lw["g_qn"], lw["g_kvn"],
                 lw["w_uq"], lw["g_mq"], lw["w_uk"], lw["w_uv"], lw["g_mk"]]
    in_specs = [pl.BlockSpec((d, tm), lambda j: (0, j)),
                pl.BlockSpec((None, d, 1), bidx), pl.BlockSpec((None, d, 1), bidx)]
    in_specs += [_full(a.shape) for a in in_arrays[3:]]
    if not is_ctx:
        for tab in rope:
            in_arrays.append(tab)
            in_specs.append(pl.BlockSpec((tab.shape[0], tm), lambda j: (0, j % tiles_per_batch)))

    chunk_spec = pl.BlockSpec((nch, RET_W, LANE), lambda j: (j, 0, 0))
    fm = lambda rows: pl.BlockSpec((rows, tm), lambda j: (0, j))
    tok = lambda cols: pl.BlockSpec((tm, cols), lambda j: (j, 0))
    nck = n // LANE
    out_specs = [chunk_spec] * 4 + [fm(DIFF_QK_W), tok(DIFF_QK_W), fm(DIFF_HEADS * DV_EXT),
                                    fm(MLA_QK_PAD_W), tok(MLA_QK_PAD_W), fm(MLA_HEADS * DV_EXT),
                                    pl.BlockSpec((bpt, 4, 8, LANE), lambda j: (j // tiles_per_batch, 0, 0, 0))]
    out_shape = [jax.ShapeDtypeStruct((nck, RET_W, LANE), BF16), jax.ShapeDtypeStruct((nck, RET_W, LANE), F32),
                 jax.ShapeDtypeStruct((nck, RET_W, LANE), BF16), jax.ShapeDtypeStruct((nck, RET_W, LANE), F32),
                 jax.ShapeDtypeStruct((DIFF_QK_W, n), BF16), jax.ShapeDtypeStruct((n, DIFF_QK_W), BF16),
                 jax.ShapeDtypeStruct((DIFF_HEADS * DV_EXT, n), BF16),
                 jax.ShapeDtypeStruct((MLA_QK_PAD_W, n), BF16), jax.ShapeDtypeStruct((n, MLA_QK_PAD_W), BF16),
                 jax.ShapeDtypeStruct((MLA_HEADS * DV_EXT, n), BF16),
                 jax.ShapeDtypeStruct((n // seq, 4, 8, LANE), F32)]
    if is_ctx:
        out_specs += [tok(DIFF_QK_W), tok(DIFF_W), tok(MLA_KV_RANK), tok(LANE)]
        out_shape += [jax.ShapeDtypeStruct((n, DIFF_QK_W), F32), jax.ShapeDtypeStruct((n, DIFF_W), F32),
                      jax.ShapeDtypeStruct((n, MLA_KV_RANK), F32), jax.ShapeDtypeStruct((n, LANE), F32)]
    return pl.pallas_call(
        functools.partial(_premix_kernel, is_ctx=is_ctx, tm=tm, bpt=bpt, tpb=tiles_per_batch),
        grid=(nt,),
        in_specs=in_specs, out_specs=out_specs, out_shape=out_shape,
        compiler_params=_cparams(1),
        name="premix_ctx" if is_ctx else "premix_smp",
    )(*in_arrays)


def _retention_kernel(rq_ref, rk_ref, rv_ref, rg_ref, s0_ref, dec_ref, gn_ref, out_ref, sfin_ref,
                      o_acc, s_acc, intra_ref, vec_ref, *, nc):
    C = RET_CHUNK
    n_idx = lax.broadcasted_iota(jnp.int32, (C, C), 1).astype(F32)
    m_idx = lax.broadcasted_iota(jnp.int32, (C, C), 0).astype(F32)
    lane = lax.broadcasted_iota(jnp.int32, (8, C), 1).astype(F32)
    for d in range(2):
        for hd in range(RET_HEADS):
            i = d * RET_HEADS + hd
            z = dec_ref[i]
            lg8 = jnp.minimum(z, 0.0) - jnp.log1p(jnp.exp(-jnp.abs(z)))
            lg = jnp.broadcast_to(lg8[0:1, :], (C, C))
            dist = (n_idx - m_idx) if d == 0 else (m_idx - n_idx)
            ok = dist >= 0.0
            intra_ref[i] = jnp.where(ok, jnp.exp(lg * jnp.where(ok, dist, 0.0)), 0.0)
            if d == 0:
                qdec = jnp.exp(lg8 * (lane + 1.0))
                kdec = jnp.exp(lg8 * (C - 1.0 - lane))
            else:
                qdec = jnp.exp(lg8 * (C - lane))
                kdec = jnp.exp(lg8 * lane)
            vec_ref[i, 0] = qdec
            vec_ref[i, 1] = kdec
            vec_ref[i, 2] = jnp.exp(lg8 * float(C))
    s_acc[...] = s0_ref[...]
    o_acc[...] = jnp.zeros_like(o_acc)

    row_head = lax.broadcasted_iota(jnp.int32, (RET_HEADS * RET_DK, C), 0) // RET_DK

    def step(c, carry):
        first, vs = [], []
        for d in range(2):
            cc = c if d == 0 else nc - 1 - c
            q_all = rq_ref[cc]
            k_all = rk_ref[cc]
            k_tok = k_all.T.astype(BF16)
            for hd in range(RET_HEADS):
                i = d * RET_HEADS + hd
                r0 = hd * RET_DK
                q = q_all[r0:r0 + RET_DK]
                k = k_all[r0:r0 + RET_DK]
                v = rv_ref[cc, r0:r0 + RET_DV, :]
                q_m = jnp.where(row_head == hd, q_all, jnp.zeros_like(q_all))
                att_t = jnp.dot(k_tok, q_m, preferred_element_type=F32)
                s_t = s_acc[i]
                cross = jnp.dot(s_t.astype(BF16), q, preferred_element_type=F32)
                kd = (k * vec_ref[i, 1][0:1, :]).astype(BF16)
                kv = lax.dot_general(v, kd, (((1,), (1,)), ((), ())), preferred_element_type=F32)
                s_acc[i] = s_t * vec_ref[i, 2][0:1, 0:RET_DK] + kv
                first.append((att_t, cross))
                vs.append((i, cc, r0, v))
        masked = [(att_t * intra_ref[i]).astype(BF16) for (att_t, _), (i, _, _, _) in zip(first, vs)]
        for (_, cross), att_m, (i, cc, r0, v) in zip(first, masked, vs):
            o = jnp.dot(v, att_m, preferred_element_type=F32) + cross * vec_ref[i, 0][0:1, :]
            o_acc[cc, r0:r0 + RET_DV, :] += o
        return carry

    lax.fori_loop(0, nc, step, 0)
    sfin_ref[...] = s_acc[...]

    def finish(c, carry):
        o = o_acc[c]
        g = rg_ref[c]
        parts = []
        for hd in range(RET_HEADS):
            r0 = hd * RET_DV
            oh = o[r0:r0 + RET_DV]
            mu = jnp.mean(oh, axis=0, keepdims=True)
            var = jnp.mean(jnp.square(oh - mu), axis=0, keepdims=True)
            y = (oh - mu) * lax.rsqrt(var + EPS) * gn_ref[r0:r0 + RET_DV, :]
            parts.append(y * _silu(g[r0:r0 + RET_DV]))
        out_ref[c] = jnp.concatenate(parts, axis=0).astype(BF16)
        return carry

    lax.fori_loop(0, nc, finish, 0)


def _retention(rq, rk, rv, rg, s0_t, dec_b, gn_col, *, batch):
    nck = rq.shape[0]
    nc = nck // batch
    blk = pl.BlockSpec((nc, RET_W, LANE), lambda b: (b, 0, 0))
    nst = 2 * RET_HEADS
    st_spec = pl.BlockSpec((None, nst, RET_DV, RET_DK), lambda b: (b, 0, 0, 0))
    return pl.pallas_call(
        functools.partial(_retention_kernel, nc=nc),
        grid=(batch,),
        in_specs=[blk, blk, blk, blk, st_spec, _full(dec_b.shape), _full(gn_col.shape)],
        out_specs=[blk, st_spec],
        out_shape=[jax.ShapeDtypeStruct((nck, RET_W, LANE), BF16),
                   jax.ShapeDtypeStruct((batch, nst, RET_DV, RET_DK), F32)],
        scratch_shapes=[pltpu.VMEM((nc, RET_W, LANE), F32),
                        pltpu.VMEM((nst, RET_DV, RET_DK), F32),
                        pltpu.VMEM((nst, RET_CHUNK, RET_CHUNK), F32),
                        pltpu.VMEM((nst, 3, 8, LANE), F32)],
        compiler_params=_cparams(1),
        name="retention",
    )(rq, rk, rv, rg, s0_t, dec_b, gn_col)


def _attend(problems, tk, shifts=None):
    dv = DV_EXT - V_PAD
    flat = [(pi, part, r0) for pi, (_, parts) in enumerate(problems) for part in parts
            for r0 in range(0, part[4], tk)]
    if INTERLEAVE_PROBLEMS:
        per = [[it for it in flat if it[0] == pi] for pi in range(len(problems))]
        flat = [per[pi][b] for b in range(max(len(x) for x in per)) for pi in range(len(per)) if b < len(per[pi])]

    def scores(item):
        pi, (k_ref, c0, _, _, _), r0 = item
        q_t = problems[pi][0]
        return jnp.dot(k_ref[r0:r0 + tk, c0:c0 + q_t.shape[0]], q_t, preferred_element_type=F32)

    state = [(jnp.full((1, q_t.shape[1]), -jnp.inf, F32), jnp.zeros((DV_EXT, q_t.shape[1]), F32))
             for q_t, _ in problems]
    pending = [scores(it) for it in flat[:SCORE_LOOKAHEAD]]
    for i, (pi, (_, _, v_ref, v0, _), r0) in enumerate(flat):
        if i + SCORE_LOOKAHEAD < len(flat):
            pending.append(scores(flat[i + SCORE_LOOKAHEAD]))
        s = pending.pop(0)
        m, acc = state[pi]
        if shifts is None:
            m_new = jnp.maximum(m, jnp.max(s, axis=0, keepdims=True))
            acc = acc * jnp.exp2(m - m_new)
        else:
            m_new = shifts[pi]
        p = jnp.exp2(s - m_new)
        pv = jnp.dot(v_ref[v0:v0 + DV_EXT, r0:r0 + tk], p.astype(BF16), preferred_element_type=F32)
        state[pi] = (m_new, acc + pv)
    return [acc[:dv] / acc[dv:dv + 1] for _, acc in state]


def _attend_guarded(problems, key_sq_max, safe, tk):
    shifts = []
    for (q_t, _), ksq in zip(problems, key_sq_max):
        q32 = q_t.astype(F32)
        qsq = jnp.sum(q32 * q32, axis=0, keepdims=True)
        shifts.append(jnp.sqrt(qsq * ksq) * BOUND_SLACK)
    outs = lax.cond(safe,
                    lambda: tuple(_attend(problems, tk, shifts)),
                    lambda: tuple(_attend(problems, tk, None)))
    return list(outs)


def _row_bcast(a, row, tq):
    rid = lax.broadcasted_iota(jnp.int32, a.shape, 0)
    r = jnp.max(jnp.where(rid == row, a, 0.0), axis=0, keepdims=True)
    return jnp.concatenate([r] * (tq // LANE), axis=1)


def _diff_attn_kernel(*refs, has_ctx, rows_main, rows_ctx, tk, hps, lam_init):
    it = iter(refs)
    safe_ref, q_ref, k_ref, v_ref, ksq_ref = (next(it) for _ in range(5))
    if has_ctx:
        kc_ref, vc_ref = (next(it) for _ in range(2))
    dl_ref, g_ref, out_ref = (next(it) for _ in range(3))
    grp = pl.program_id(1)
    q_full = q_ref[...]
    tq = q_full.shape[1]
    seg = lax.broadcasted_iota(jnp.int32, q_full.shape, 0) // DIFF_DK
    problems, key_sq_max = [], []
    for j in range(hps):
        parts = [(k_ref, 0, v_ref, j * DV_EXT, rows_main)]
        if has_ctx:
            parts.append((kc_ref, 0, vc_ref, j * DV_EXT, rows_ctx))
        for mm in range(2):
            sg = 2 * (grp * hps + j) + mm
            q_m = jnp.where(seg == sg, q_full, jnp.zeros_like(q_full))
            problems.append((q_m, parts))
            key_sq_max.append(_row_bcast(ksq_ref[...], sg, tq))
    dl = dl_ref[...]
    lam = (jnp.exp(jnp.sum(dl[0:1] * dl[1:2], axis=1, keepdims=True))
           - jnp.exp(jnp.sum(dl[2:3] * dl[3:4], axis=1, keepdims=True)) + lam_init)
    outs = _attend_guarded(problems, key_sq_max, safe_ref[pl.program_id(0), grp] != 0, tk)
    for j in range(hps):
        o = outs[2 * j] - lam * outs[2 * j + 1]
        y = o * _rms_scale(o, DIFF_DV) * g_ref[...]
        out_ref[j * DIFF_DV:(j + 1) * DIFF_DV, :] = (y * (1.0 - lam_init)).astype(BF16)


def _safe_flags(q_sq, k_sq, per_group):
    bound = jnp.sqrt(q_sq * k_sq) * (BOUND_SLACK * BOUND_SLACK)
    ok = (2.0 * bound <= SAFE_EXP_RANGE).reshape(q_sq.shape[0], -1, per_group)
    return jnp.all(ok, axis=-1).astype(jnp.int32)


def _diff_attention(dq_t, dk_tok, dv_t, q_sq, k_sq, ctx, dl, g_col, *, batch, tq, tk, hps, lam_init):
    n_all = dq_t.shape[1]
    n = n_all // batch
    qt = n // tq
    nseg = 2 * DIFF_HEADS
    has_ctx = ctx is not None
    rows_ctx = ctx[0].shape[1] if has_ctx else 0
    safe = _safe_flags(q_sq, k_sq, 2 * hps)
    k_sq_b = jnp.broadcast_to(k_sq[:, :, None], (batch, nseg, LANE))
    in_arrays = [safe, dq_t, dk_tok, dv_t, k_sq_b]
    in_specs = [pl.BlockSpec(memory_space=pltpu.SMEM),
                pl.BlockSpec((DIFF_QK_W, tq), lambda b, g, i: (0, b * qt + i)),
                pl.BlockSpec((n, DIFF_QK_W), lambda b, g, i: (b, 0)),
                pl.BlockSpec((hps * DV_EXT, n), lambda b, g, i: (g, b)),
                pl.BlockSpec((None, nseg, LANE), lambda b, g, i: (b, 0, 0))]
    if has_ctx:
        in_arrays += list(ctx)
        in_specs += [pl.BlockSpec((None, rows_ctx, DIFF_QK_W), lambda b, g, i: (b, 0, 0)),
                     pl.BlockSpec((None, hps * DV_EXT, rows_ctx), lambda b, g, i: (b, g, 0))]
    in_arrays += [dl, g_col]
    in_specs += [_full(dl.shape), _full(g_col.shape)]
    return pl.pallas_call(
        functools.partial(_diff_attn_kernel, has_ctx=has_ctx, rows_main=n, rows_ctx=rows_ctx, tk=tk, hps=hps,
                          lam_init=lam_init),
        grid=(batch, DIFF_HEADS // hps, qt),
        in_specs=in_specs,
        out_specs=pl.BlockSpec((hps * DIFF_DV, tq), lambda b, g, i: (g, b * qt + i)),
        out_shape=jax.ShapeDtypeStruct((DIFF_W, n_all), BF16),
        compiler_params=_cparams(3),
        name="diff_attn_smp" if has_ctx else "diff_attn_ctx",
    )(*in_arrays)


def _mla_attn_kernel(*refs, has_ctx, rows_main, rows_ctx, tk, hps):
    it = iter(refs)
    safe_ref, q_ref, k_ref, v_ref, ksq_ref = (next(it) for _ in range(5))
    if has_ctx:
        kc_ref, vc_ref = (next(it) for _ in range(2))
    out_ref = next(it)
    grp = pl.program_id(1)
    tq = q_ref.shape[1]
    problems, key_sq_max = [], []
    for j in range(hps):
        parts = [(k_ref, j * MLA_PAD, v_ref, j * DV_EXT, rows_main)]
        if has_ctx:
            parts.append((kc_ref, j * MLA_PAD, vc_ref, j * DV_EXT, rows_ctx))
        problems.append((q_ref[j * MLA_PAD:(j + 1) * MLA_PAD, :], parts))
        key_sq_max.append(_row_bcast(ksq_ref[...], grp * hps + j, tq))
    outs = _attend_guarded(problems, key_sq_max, safe_ref[pl.program_id(0), grp] != 0, tk)
    for j, o in enumerate(outs):
        out_ref[j * MLA_DV:(j + 1) * MLA_DV, :] = o.astype(BF16)


def _mla_attention(mq_t, mk_tok, mv_t, q_sq, k_sq, ctx, *, batch, tq, tk, hps):
    n_all = mq_t.shape[1]
    n = n_all // batch
    qt = n // tq
    has_ctx = ctx is not None
    rows_ctx = ctx[0].shape[1] if has_ctx else 0
    safe = _safe_flags(q_sq, k_sq, hps)
    k_sq_b = jnp.broadcast_to(k_sq[:, :, None], (batch, MLA_HEADS, LANE))
    in_arrays = [safe, mq_t, mk_tok, mv_t, k_sq_b]
    in_specs = [pl.BlockSpec(memory_space=pltpu.SMEM),
                pl.BlockSpec((hps * MLA_PAD, tq), lambda b, g, i: (g, b * qt + i)),
                pl.BlockSpec((n, hps * MLA_PAD), lambda b, g, i: (b, g)),
                pl.BlockSpec((hps * DV_EXT, n), lambda b, g, i: (g, b)),
                pl.BlockSpec((None, MLA_HEADS, LANE), lambda b, g, i: (b, 0, 0))]
    if has_ctx:
        in_arrays += list(ctx)
        in_specs += [pl.BlockSpec((None, rows_ctx, hps * MLA_PAD), lambda b, g, i: (b, 0, g)),
                     pl.BlockSpec((None, hps * DV_EXT, rows_ctx), lambda b, g, i: (b, g, 0))]
    return pl.pallas_call(
        functools.partial(_mla_attn_kernel, has_ctx=has_ctx, rows_main=n, rows_ctx=rows_ctx, tk=tk, hps=hps),
        grid=(batch, MLA_HEADS // hps, qt),
        in_specs=in_specs,
        out_specs=pl.BlockSpec((hps * MLA_DV, tq), lambda b, g, i: (g, b * qt + i)),
        out_shape=jax.ShapeDtypeStruct((MLA_W, n_all), BF16),
        compiler_params=_cparams(3),
        name="mla_attn_smp" if has_ctx else "mla_attn_ctx",
    )(*in_arrays)


def _post_kernel(x_ref, ret_ref, diff_ref, mla_ref, g1_ref, sh2_ref, sc2_ref, g2_ref, n2_ref,
                 w_out_ref, w_gu_ref, w_down_ref, out_ref, *, tm, d_ff, ff_chunk):
    nch = tm // LANE
    ret = jnp.concatenate([ret_ref[j] for j in range(nch)], axis=1)
    mixed = jnp.concatenate([ret, diff_ref[...], mla_ref[...]], axis=0)
    x = x_ref[...]
    d = x.shape[0]
    x1 = x + g1_ref[...] * jnp.dot(w_out_ref[...], mixed, preferred_element_type=F32)
    h = x1 * _rms_scale(x1, d) * n2_ref[...]
    h = (h * (1.0 + sc2_ref[...]) + sh2_ref[...]).astype(BF16)
    ffn = None
    for lo in range(0, d_ff, ff_chunk):
        g = jnp.dot(w_gu_ref[lo:lo + ff_chunk, :], h, preferred_element_type=F32)
        u = jnp.dot(w_gu_ref[d_ff + lo:d_ff + lo + ff_chunk, :], h, preferred_element_type=F32)
        a = (_silu(g) * u).astype(BF16)
        part = jnp.dot(w_down_ref[:, lo:lo + ff_chunk], a, preferred_element_type=F32)
        ffn = part if ffn is None else ffn + part
    out_ref[...] = x1 + g2_ref[...] * ffn


def _post(x_t, ret, diff_t, mla_t, mods, lw, *, is_ctx, tm, tiles_per_batch):
    d, n = x_t.shape
    nt = n // tm
    nch = tm // LANE
    d_ff = lw["w_down"].shape[1]
    ff_chunk = d_ff // 2
    if is_ctx:
        bidx = lambda j: (0, 0, 0)
    else:
        bidx = lambda j: (j // tiles_per_batch, 0, 0)
    col = pl.BlockSpec((None, d, 1), bidx)
    const = lambda a: pl.BlockSpec(a.shape, lambda j: (0,) * a.ndim, pipeline_mode=pl.Buffered(1))
    g1, sh2, sc2, g2 = mods
    return pl.pallas_call(
        functools.partial(_post_kernel, tm=tm, d_ff=d_ff, ff_chunk=ff_chunk),
        grid=(nt,),
        in_specs=[pl.BlockSpec((d, tm), lambda j: (0, j)),
                  pl.BlockSpec((nch, RET_W, LANE), lambda j: (j, 0, 0)),
                  pl.BlockSpec((DIFF_W, tm), lambda j: (0, j)),
                  pl.BlockSpec((MLA_W, tm), lambda j: (0, j)),
                  col, col, col, col, const(lw["n2"]),
                  const(lw["w_out"]), const(lw["w_gu"]), const(lw["w_down"])],
        out_specs=pl.BlockSpec((d, tm), lambda j: (0, j)),
        out_shape=jax.ShapeDtypeStruct((d, n), F32),
        compiler_params=_cparams(1),
        name="post_ctx" if is_ctx else "post_smp",
    )(x_t, ret, diff_t, mla_t, g1, sh2, sc2, g2, lw["n2"], lw["w_out"], lw["w_gu"], lw["w_down"])


def _rope_tables_t(n, rot_dim):
    rows = n // GRID_W
    row = jnp.repeat(jnp.arange(rows, dtype=F32), GRID_W)
    col = jnp.tile(jnp.arange(GRID_W, dtype=F32), rows)
    n_freq = rot_dim // 4
    inv = 1.0 / (ROPE_BASE ** (jnp.arange(n_freq, dtype=F32) / n_freq))
    ang = jnp.concatenate([inv[:, None] * row[None, :], inv[:, None] * col[None, :]], axis=0)
    return jnp.cos(ang), jnp.sin(ang)


def _col(v):
    return v.astype(F32).reshape(-1, 1)


def _pad_heads_rows(w_t, used, pad):
    hk = w_t.shape[0] // used
    w3 = w_t.reshape(hk, used, w_t.shape[1])
    w3 = jnp.pad(w3, ((0, 0), (0, pad - used), (0, 0)))
    return w3.reshape(hk * pad, w_t.shape[1])


def _layer_weights(l, w_in, norm1, norm2, diff_qk_gain, mla_q_norm, mla_kv_norm, w_uq, w_ukv, mla_qk_gain,
                   w_out, w_gu, w_down):
    w_ukv_t = w_ukv[l].T.reshape(MLA_HEADS, MLA_NOPE + MLA_DV, MLA_KV_RANK)
    w_uk = jnp.pad(w_ukv_t[:, :MLA_NOPE], ((0, 0), (0, MLA_PAD - MLA_NOPE), (0, 0)))
    g_pad = lambda g: jnp.tile(jnp.pad(g.astype(F32), (0, MLA_PAD - MLA_DQK)), MLA_HEADS).reshape(-1, 1)
    return {
        "n1": _col(norm1[l]), "n2": _col(norm2[l]),
        "w_in": w_in[l].T.astype(BF16),
        "g_dq": _col(jnp.tile(diff_qk_gain[l, 0], 2 * DIFF_HEADS)),
        "g_dk": _col(jnp.tile(diff_qk_gain[l, 1], 2 * DIFF_HEADS)),
        "g_qn": _col(mla_q_norm[l]), "g_kvn": _col(mla_kv_norm[l]),
        "w_uq": _pad_heads_rows(w_uq[l].T, MLA_DQK, MLA_PAD).astype(BF16),
        "g_mq": g_pad(mla_qk_gain[l, 0]), "g_mk": g_pad(mla_qk_gain[l, 1]),
        "w_uk": w_uk.reshape(MLA_QK_PAD_W, MLA_KV_RANK).astype(BF16),
        "w_uv": w_ukv_t[:, MLA_NOPE:].reshape(MLA_W, MLA_KV_RANK).astype(BF16),
        "w_out": w_out[l].T.astype(BF16), "w_gu": w_gu[l].T.astype(BF16), "w_down": w_down[l].T.astype(BF16),
    }


def kernel(x_prompt, x_sample, c, state_ret, cache_diff_k, cache_diff_v, cache_mla_ckv, cache_mla_kr, c_ctx,
           w_mod, b_mod, norm1, norm2, w_in, ret_decay, ret_gn_gain, diff_qk_gain, diff_lambda, diff_subln_gain,
           mla_q_norm, mla_kv_norm, w_uq, w_ukv, mla_qk_gain, w_out, w_gu, w_down):
    depth = w_in.shape[0]
    bp, sp, d = x_prompt.shape
    bs, ss, _ = x_sample.shape
    past = cache_diff_k.shape[2]

    tm = 512
    tq_s, tk_s = 512, 256
    tq_p = tk_p = sp

    n_cond = 1 + bs
    r_pad = -(-n_cond // 16) * 16
    cond = jnp.concatenate([c_ctx[None, :], c, jnp.zeros((r_pad - n_cond, d), F32)], axis=0)
    mod = _modulation(cond, w_mod, b_mod)
    mod = mod.reshape(depth, r_pad, 6, d, 1)

    xp = x_prompt.reshape(bp * sp, d).T
    xs = x_sample.reshape(bs * ss, d).T
    rope = _rope_tables_t(ss, RET_DK) + _rope_tables_t(ss, DIFF_DK)
    dec_b = jnp.broadcast_to(ret_decay.astype(F32).reshape(depth, 2 * RET_HEADS, 1, 1),
                             (depth, 2 * RET_HEADS, 8, LANE))
    s0_zero = jnp.zeros((bp, 2 * RET_HEADS, RET_DV, RET_DK), F32)
    ones_rows = jnp.zeros((bs, DIFF_HEADS, V_PAD, past), BF16).at[:, :, 0, :].set(1.0)

    new_ret, new_dk, new_dv, new_ckv, new_kr = [], [], [], [], []
    for l in range(depth):
        lw = _layer_weights(l, w_in, norm1, norm2, diff_qk_gain, mla_q_norm, mla_kv_norm, w_uq, w_ukv,
                            mla_qk_gain, w_out, w_gu, w_down)
        lam_init = 0.8 - 0.6 * math.exp(-0.3 * l)
        gn_col = _col(ret_gn_gain[l])
        subln_col = _col(diff_subln_gain[l])
        dl = diff_lambda[l].astype(F32)

        mc = [mod[l, 0:1, i] for i in range(6)]
        (rq, rk, rv, rg, dq_t, dk_tok, dv_t, mq_t, mk_tok, mv_t, sq, dk_f, dv_f, ckv_f, kr_f) = _premix(
            xp, mc[0], mc[1], lw, None, is_ctx=True, tm=tm, seq=sp)
        sq = jnp.max(sq, axis=-1)
        ret, s_fin = _retention(rq, rk, rv, rg, s0_zero, dec_b[l], gn_col, batch=bp)
        diff_t = _diff_attention(dq_t, dk_tok, dv_t, sq[:, 0], sq[:, 1], None, dl, subln_col, batch=bp, tq=tq_p,
                                 tk=tk_p, hps=DIFF_HEADS, lam_init=lam_init)
        mla_t = _mla_attention(mq_t, mk_tok, mv_t, sq[:, 2], sq[:, 3], None, batch=bp, tq=tq_p, tk=tk_p,
                               hps=MLA_HEADS)
        xp = _post(xp, ret, diff_t, mla_t, (mc[2], mc[3], mc[4], mc[5]), lw, is_ctx=True, tm=tm,
                   tiles_per_batch=1)
        new_ret.append(jnp.swapaxes(s_fin.reshape(bp, 2, RET_HEADS, RET_DV, RET_DK), -1, -2))
        new_dk.append(dk_f.reshape(bp, sp, DIFF_HEADS, 2, DIFF_DK))
        new_dv.append(dv_f.reshape(bp, sp, DIFF_HEADS, DIFF_DV))
        new_ckv.append(ckv_f.reshape(bp, sp, MLA_KV_RANK))
        new_kr.append(kr_f[:, :MLA_ROPE].reshape(bp, sp, MLA_ROPE))

        ms = [mod[l, 1:1 + bs, i] for i in range(6)]
        s0_t = jnp.swapaxes(state_ret[:, l].astype(F32), -1, -2).reshape(bs, 2 * RET_HEADS, RET_DV, RET_DK)
        ctx_dk_f = cache_diff_k[:, l].reshape(bs, past, DIFF_QK_W).astype(F32)
        ctx_dk = ctx_dk_f.astype(BF16)
        ctx_dv_t = jnp.swapaxes(cache_diff_v[:, l].reshape(bs, past, DIFF_HEADS, DIFF_DV), 1, 3)
        ctx_dv_t = jnp.concatenate([jnp.swapaxes(ctx_dv_t, 1, 2).astype(BF16), ones_rows], axis=2)
        ctx_dv_t = ctx_dv_t.reshape(bs, DIFF_HEADS * DV_EXT, past)
        ctx_mk, ctx_mv_t, ctx_sq = _ctx_mla_kv(
            jnp.swapaxes(cache_mla_ckv[:, l].astype(F32), 1, 2), jnp.swapaxes(cache_mla_kr[:, l].astype(F32), 1, 2),
            jnp.swapaxes(ctx_dk_f, 1, 2), lw["w_uk"], lw["w_uv"], lw["g_mk"])
        ctx_sq = jnp.max(ctx_sq, axis=-1)
        (rq, rk, rv, rg, dq_t, dk_tok, dv_t, mq_t, mk_tok, mv_t, sq) = _premix(
            xs, ms[0], ms[1], lw, rope, is_ctx=False, tm=tm, seq=ss)
        sq = jnp.max(sq, axis=-1)
        ret, _ = _retention(rq, rk, rv, rg, s0_t, dec_b[l], gn_col, batch=bs)
        diff_t = _diff_attention(dq_t, dk_tok, dv_t, sq[:, 0], jnp.maximum(sq[:, 1], ctx_sq[:, 1]),
                                 (ctx_dk, ctx_dv_t), dl, subln_col, batch=bs, tq=tq_s, tk=tk_s, hps=1,
                                 lam_init=lam_init)
        mla_t = _mla_attention(mq_t, mk_tok, mv_t, sq[:, 2], jnp.maximum(sq[:, 3], ctx_sq[:, 0]),
                               (ctx_mk, ctx_mv_t), batch=bs, tq=tq_s, tk=tk_s, hps=2)
        xs = _post(xs, ret, diff_t, mla_t, (ms[2], ms[3], ms[4], ms[5]), lw, is_ctx=False, tm=tm,
                   tiles_per_batch=ss // tm)

    y_p = xp.T.reshape(bp, sp, d)
    y_s = xs.T.reshape(bs, ss, d)
    return (y_p, y_s, jnp.stack(new_ret, axis=1), jnp.stack(new_dk, axis=1), jnp.stack(new_dv, axis=1),
            jnp.stack(new_ckv, axis=1), jnp.stack(new_kr, axis=1))
```

```python
import functools
import math

import jax
import jax.numpy as jnp
from jax import lax
from jax.experimental import pallas as pl
from jax.experimental.pallas import tpu as pltpu

F32 = jnp.float32
BF16 = jnp.bfloat16
EPS = 1e-6

GRID_W = 64
RET_HEADS = 4
RET_DK = 64
RET_DV = 64
RET_CHUNK = 128
DIFF_HEADS = 4
DIFF_DK = 32
DIFF_DV = 64
MLA_HEADS = 8
MLA_NOPE = 64
MLA_ROPE = 32
MLA_DQK = MLA_NOPE + MLA_ROPE
MLA_DV = 64
MLA_Q_RANK = 768
MLA_KV_RANK = 256
ROPE_BASE = 10000.0

RET_W = RET_HEADS * RET_DV
DIFF_W = DIFF_HEADS * DIFF_DV
MLA_W = MLA_HEADS * MLA_DV
DIFF_QK_W = DIFF_HEADS * 2 * DIFF_DK
MLA_PAD = 128
MLA_QK_PAD_W = MLA_HEADS * MLA_PAD
V_PAD = 16
DV_EXT = MLA_DV + V_PAD

_SPLITS = (RET_HEADS * RET_DK, RET_HEADS * RET_DK, RET_W, RET_W,
           DIFF_QK_W, DIFF_QK_W, DIFF_W, MLA_Q_RANK, MLA_KV_RANK, MLA_ROPE)
W_IN_MLA0 = int(sum(_SPLITS[:7]))
_PERM_SPLITS = _SPLITS[7:] + _SPLITS[:7]
_OFFS = tuple(int(sum(_PERM_SPLITS[:i])) for i in range(len(_PERM_SPLITS) + 1))
O_CQ, O_CKV, O_KR, O_RQ, O_RK, O_RV, O_RG, O_DQ, O_DK, O_DV, O_END = _OFFS

LOG2E = math.log2(math.e)
SCORE_LOOKAHEAD = 3
INTERLEAVE_PROBLEMS = True
SAFE_EXP_RANGE = 100.0
BOUND_SLACK = 1.0 + 2.0 ** -6
LANE = 128
VMEM_LIMIT = 56 * 1024 * 1024


def _cparams(n_grid):
    return pltpu.CompilerParams(dimension_semantics=("arbitrary",) * n_grid,
                                vmem_limit_bytes=VMEM_LIMIT)


def _full(shape):
    nd = len(shape)
    return pl.BlockSpec(shape, lambda *_: (0,) * nd)


def _silu(x):
    return x / (1.0 + jnp.exp(-x))


def _rms_scale(x, n):
    return lax.rsqrt(jnp.sum(x * x, axis=0, keepdims=True) * (1.0 / n) + EPS)


def _rope_pair(x1, x2, c, s):
    return x1 * c - x2 * s, x2 * c + x1 * s


def _with_ones_rows(v, heads):
    t = v.shape[1]
    dv = v.shape[0] // heads
    extra = jnp.where(lax.broadcasted_iota(jnp.int32, (V_PAD, t), 0) == 0, 1.0, 0.0).astype(BF16)
    parts = []
    for hd in range(heads):
        parts += [v[hd * dv:(hd + 1) * dv].astype(BF16), extra]
    return jnp.concatenate(parts, axis=0)


def _mod_kernel(c_ref, w_ref, b_ref, o_ref):
    a = _silu(c_ref[...]).astype(BF16)
    o_ref[...] = jnp.dot(a, w_ref[...].astype(BF16), preferred_element_type=F32) + b_ref[...]


def _modulation(cond, w_mod, b_mod):
    depth, d, d6 = w_mod.shape
    r = cond.shape[0]
    tn = 1536
    return pl.pallas_call(
        _mod_kernel,
        grid=(depth, d6 // tn),
        in_specs=[pl.BlockSpec((r, d), lambda l, j: (0, 0)),
                  pl.BlockSpec((None, d, tn), lambda l, j: (l, 0, j)),
                  pl.BlockSpec((None, 1, tn), lambda l, j: (l, 0, j))],
        out_specs=pl.BlockSpec((None, r, tn), lambda l, j: (l, 0, j)),
        out_shape=jax.ShapeDtypeStruct((depth, r, d6), F32),
        compiler_params=_cparams(2),
        name="adaln_mod",
    )(cond, w_mod, b_mod.reshape(depth, 1, d6))


def _mla_kv_matmuls(ckvn_bf, w_uk_ref, w_uv_ref):
    return (jnp.dot(w_uk_ref[...], ckvn_bf, preferred_element_type=F32),
            jnp.dot(w_uv_ref[...], ckvn_bf, preferred_element_type=F32))


def _mla_keys(kn, kr, g_mk_ref, rope):
    t = kr.shape[1]
    kr_ss = jnp.sum(kr * kr, axis=0, keepdims=True)
    zpad = jnp.zeros((MLA_PAD - MLA_DQK, t), F32)
    half = MLA_ROPE // 2
    heads = []
    for hd in range(MLA_HEADS):
        r0 = hd * MLA_PAD
        kh = kn[r0:r0 + MLA_NOPE]
        ss = jnp.sum(kh * kh, axis=0, keepdims=True) + kr_ss
        r = lax.rsqrt(ss * (1.0 / MLA_DQK) + EPS)
        g = g_mk_ref[r0:r0 + MLA_PAD, :]
        y_nope = kh * r * g[:MLA_NOPE]
        y_r = kr * r * g[MLA_NOPE:MLA_DQK]
        if rope is not None:
            c, s = rope
            y1, y2 = _rope_pair(y_r[:half], y_r[half:], c, s)
            heads += [y_nope, y1, y2, zpad]
        else:
            heads += [y_nope, y_r, zpad]
    return jnp.concatenate(heads, axis=0)


def _group_sq_norms(a, rows_per_group):
    groups = a.shape[0] // rows_per_group
    return jnp.concatenate([jnp.sum(jnp.square(a[g * rows_per_group:(g + 1) * rows_per_group]), axis=0, keepdims=True)
                            for g in range(groups)], axis=0)


def _lane_tile_max(sq):
    return functools.reduce(jnp.maximum, [sq[:, t * LANE:(t + 1) * LANE] for t in range(sq.shape[1] // LANE)])


def _ctx_kv_kernel(ckv_ref, kr_ref, dk_ref, w_uk_ref, w_uv_ref, g_mk_ref, mk_ref, mv_ref, sq_ref):
    kn, v = _mla_kv_matmuls(ckv_ref[...].astype(BF16), w_uk_ref, w_uv_ref)
    k = _mla_keys(kn, kr_ref[...], g_mk_ref, None)
    mk_ref[...] = k.T.astype(BF16)
    mv_ref[...] = _with_ones_rows(v, MLA_HEADS)
    sq_ref[0] = _lane_tile_max(_group_sq_norms(k, MLA_PAD))
    sq_ref[1] = _lane_tile_max(_group_sq_norms(dk_ref[...], DIFF_DK))


def _ctx_mla_kv(ckv_t, kr_t, dk_t, w_uk, w_uv, g_mk):
    b, _, l = ckv_t.shape
    return pl.pallas_call(
        _ctx_kv_kernel,
        grid=(b,),
        in_specs=[pl.BlockSpec((None, MLA_KV_RANK, l), lambda i: (i, 0, 0)),
                  pl.BlockSpec((None, MLA_ROPE, l), lambda i: (i, 0, 0)),
                  pl.BlockSpec((None, DIFF_QK_W, l), lambda i: (i, 0, 0)),
                  _full(w_uk.shape), _full(w_uv.shape), _full(g_mk.shape)],
        out_specs=[pl.BlockSpec((None, l, MLA_QK_PAD_W), lambda i: (i, 0, 0)),
                   pl.BlockSpec((None, MLA_HEADS * DV_EXT, l), lambda i: (i, 0, 0)),
                   pl.BlockSpec((None, 2, 8, LANE), lambda i: (i, 0, 0, 0))],
        out_shape=[jax.ShapeDtypeStruct((b, l, MLA_QK_PAD_W), BF16),
                   jax.ShapeDtypeStruct((b, MLA_HEADS * DV_EXT, l), BF16),
                   jax.ShapeDtypeStruct((b, 2, 8, LANE), F32)],
        compiler_params=_cparams(1),
        name="ctx_mla_kv",
    )(ckv_t, kr_t, dk_t, w_uk, w_uv, g_mk)


def _premix_kernel(*refs, is_ctx, tm, bpt):
    it = iter(refs)
    x_ref, shift_ref, scale_ref, n1_ref, w_in_ref = (next(it) for _ in range(5))
    g_dq_ref, g_dk_ref, g_qn_ref, g_kvn_ref = (next(it) for _ in range(4))
    w_uq_ref, g_mq_ref, w_uk_ref, w_uv_ref, g_mk_ref = (next(it) for _ in range(5))
    if is_ctx:
        rope_r = rope_s = None
    else:
        cr_ref, sr_ref, cs_ref, ss_ref = (next(it) for _ in range(4))
        rope_r = (cr_ref[...], sr_ref[...])
        rope_s = (cs_ref[...], ss_ref[...])
    rq_ref, rk_ref, rv_ref, rg_ref = (next(it) for _ in range(4))
    dq_ref, dk_ref, dv_ref, mq_ref, mk_ref, mv_ref, sq_ref = (next(it) for _ in range(7))
    if is_ctx:
        dkf_ref, dvf_ref, ckvf_ref, krf_ref = (next(it) for _ in range(4))

    def put_sq_max(idx, a, rows_per_group):
        sq = _group_sq_norms(a, rows_per_group)
        w = tm // bpt
        for bi in range(bpt):
            sq_ref[bi, idx] = _lane_tile_max(sq[:, bi * w:(bi + 1) * w])

    x = x_ref[...]
    d = x.shape[0]
    h = (x * _rms_scale(x, d) * (n1_ref[...] * (1.0 + scale_ref[...])) + shift_ref[...]).astype(BF16)

    nch = tm // LANE

    def put_chunks(ref, val):
        for j in range(nch):
            ref[j] = val[:, j * LANE:(j + 1) * LANE].astype(ref.dtype)

    def proj(lo, hi):
        return jnp.dot(w_in_ref[lo:hi, :], h, preferred_element_type=F32)

    p_mla = proj(O_CQ, O_RQ)
    cq, ckv, kr = p_mla[O_CQ:O_CKV], p_mla[O_CKV:O_KR], p_mla[O_KR:O_RQ]
    p_ret = proj(O_RQ, O_DQ)
    rq, rk, rv, rg = (p_ret[o - O_RQ:o - O_RQ + RET_W] for o in (O_RQ, O_RK, O_RV, O_RG))
    cqn = (cq * _rms_scale(cq, MLA_Q_RANK) * g_qn_ref[...]).astype(BF16)
    ckvn = ckv * _rms_scale(ckv, MLA_KV_RANK) * g_kvn_ref[...]
    mq = jnp.dot(w_uq_ref[...], cqn, preferred_element_type=F32)
    kn, mv = _mla_kv_matmuls(ckvn.astype(BF16), w_uk_ref, w_uv_ref)
    p_diff = proj(O_DQ, O_END)
    dq_raw, dk_raw, dv = (p_diff[o - O_DQ:o - O_DQ + DIFF_QK_W] for o in (O_DQ, O_DK, O_DV))

    if rope_r is not None:
        c, s = rope_r
        hk = RET_DK // 2

        def rope_heads(a):
            parts = []
            for hd in range(RET_HEADS):
                r0 = hd * RET_DK
                parts += list(_rope_pair(a[r0:r0 + hk], a[r0 + hk:r0 + RET_DK], c, s))
            return jnp.concatenate(parts, axis=0)

        rq = rope_heads(rq)
        rk = rope_heads(rk)
    put_chunks(rq_ref, rq)
    put_chunks(rk_ref, rk * (RET_DK ** -0.5))
    put_chunks(rv_ref, rv)
    put_chunks(rg_ref, rg)

    def diff_qk(a, g_ref, out_scale):
        parts = []
        hs = DIFF_DK // 2
        for seg in range(2 * DIFF_HEADS):
            r0 = seg * DIFF_DK
            xs = a[r0:r0 + DIFF_DK]
            y = xs * _rms_scale(xs, DIFF_DK) * g_ref[r0:r0 + DIFF_DK, :]
            if out_scale != 1.0:
                y = y * out_scale
            if rope_s is not None:
                parts += list(_rope_pair(y[:hs], y[hs:], rope_s[0], rope_s[1]))
            else:
                parts.append(y)
        return jnp.concatenate(parts, axis=0)

    dq = diff_qk(dq_raw, g_dq_ref, DIFF_DK ** -0.5 * LOG2E)
    dq_ref[...] = dq.astype(BF16)
    put_sq_max(0, dq, DIFF_DK)
    dk = diff_qk(dk_raw, g_dk_ref, 1.0)
    dk_t = dk.T
    dk_ref[...] = dk_t.astype(BF16)
    put_sq_max(1, dk, DIFF_DK)
    dv_ref[...] = _with_ones_rows(dv, DIFF_HEADS)
    if is_ctx:
        dkf_ref[...] = dk_t
        dvf_ref[...] = dv.T

    half = MLA_ROPE // 2
    parts = []
    for hd in range(MLA_HEADS):
        r0 = hd * MLA_PAD
        xs = mq[r0:r0 + MLA_PAD]
        y = xs * _rms_scale(xs, MLA_DQK) * (g_mq_ref[r0:r0 + MLA_PAD, :] * (MLA_DQK ** -0.5 * LOG2E))
        if rope_s is not None:
            y1, y2 = _rope_pair(y[MLA_NOPE:MLA_NOPE + half], y[MLA_NOPE + half:MLA_DQK], rope_s[0], rope_s[1])
            parts += [y[:MLA_NOPE], y1, y2, y[MLA_DQK:]]
        else:
            parts.append(y)
    mq_n = jnp.concatenate(parts, axis=0)
    mq_ref[...] = mq_n.astype(BF16)
    put_sq_max(2, mq_n, MLA_PAD)

    mk = _mla_keys(kn, kr, g_mk_ref, rope_s)
    mk_ref[...] = mk.T.astype(BF16)
    mv_ref[...] = _with_ones_rows(mv, MLA_HEADS)
    put_sq_max(3, mk, MLA_PAD)
    if is_ctx:
        ckvf_ref[...] = ckvn.T
        krf_ref[...] = jnp.concatenate([kr, jnp.zeros((LANE - MLA_ROPE, tm), F32)], axis=0).T


def _premix(x_t, shift, scale, lw, rope, *, is_ctx, tm, seq):
    d, n = x_t.shape
    nt = n // tm
    nch = tm // LANE
    tiles_per_batch = max(1, seq // tm)
    bpt = max(1, tm // seq)
    if is_ctx:
        bidx = lambda j: (0, 0, 0)
    else:
        bidx = lambda j: (j // tiles_per_batch, 0, 0)
    in_arrays = [x_t, shift, scale, lw["n1"], lw["w_in"], lw["g_dq"], lw["g_dk"], lw["g_qn"], lw["g_kvn"],
                 lw["w_uq"], lw["g_mq"], lw["w_uk"], lw["w_uv"], lw["g_mk"]]
    in_specs = [pl.BlockSpec((d, tm), lambda j: (0, j)),
                pl.BlockSpec((None, d, 1), bidx), pl.BlockSpec((None, d, 1), bidx)]
    in_specs += [_full(a.shape) for a in in_arrays[3:]]
    if not is_ctx:
        for tab in rope:
            in_arrays.append(tab)
            in_specs.append(pl.BlockSpec((tab.shape[0], tm), lambda j: (0, j % tiles_per_batch)))

    chunk_spec = pl.BlockSpec((nch, RET_W, LANE), lambda j: (j, 0, 0))
    fm = lambda rows: pl.BlockSpec((rows, tm), lambda j: (0, j))
    tok = lambda cols: pl.BlockSpec((tm, cols), lambda j: (j, 0))
    nck = n // LANE
    out_specs = [chunk_spec] * 4 + [fm(DIFF_QK_W), tok(DIFF_QK_W), fm(DIFF_HEADS * DV_EXT),
                                    fm(MLA_QK_PAD_W), tok(MLA_QK_PAD_W), fm(MLA_HEADS * DV_EXT),
                                    pl.BlockSpec((bpt, 4, 8, LANE), lambda j: (j, 0, 0, 0))]
    out_shape = [jax.ShapeDtypeStruct((nck, RET_W, LANE), BF16), jax.ShapeDtypeStruct((nck, RET_W, LANE), F32),
                 jax.ShapeDtypeStruct((nck, RET_W, LANE), BF16), jax.ShapeDtypeStruct((nck, RET_W, LANE), F32),
                 jax.ShapeDtypeStruct((DIFF_QK_W, n), BF16), jax.ShapeDtypeStruct((n, DIFF_QK_W), BF16),
                 jax.ShapeDtypeStruct((DIFF_HEADS * DV_EXT, n), BF16),
                 jax.ShapeDtypeStruct((MLA_QK_PAD_W, n), BF16), jax.ShapeDtypeStruct((n, MLA_QK_PAD_W), BF16),
                 jax.ShapeDtypeStruct((MLA_HEADS * DV_EXT, n), BF16),
                 jax.ShapeDtypeStruct((nt * bpt, 4, 8, LANE), F32)]
    if is_ctx:
        out_specs += [tok(DIFF_QK_W), tok(DIFF_W), tok(MLA_KV_RANK), tok(LANE)]
        out_shape += [jax.ShapeDtypeStruct((n, DIFF_QK_W), F32), jax.ShapeDtypeStruct((n, DIFF_W), F32),
                      jax.ShapeDtypeStruct((n, MLA_KV_RANK), F32), jax.ShapeDtypeStruct((n, LANE), F32)]
    return pl.pallas_call(
        functools.partial(_premix_kernel, is_ctx=is_ctx, tm=tm, bpt=bpt),
        grid=(nt,),
        in_specs=in_specs, out_specs=out_specs, out_shape=out_shape,
        compiler_params=_cparams(1),
        name="premix_ctx" if is_ctx else "premix_smp",
    )(*in_arrays)


def _retention_kernel(rq_ref, rk_ref, rv_ref, rg_ref, s0_ref, dec_ref, gn_ref, out_ref, sfin_ref,
                      o_acc, s_acc, intra_ref, vec_ref, *, nc):
    C = RET_CHUNK
    n_idx = lax.broadcasted_iota(jnp.int32, (C, C), 1).astype(F32)
    m_idx = lax.broadcasted_iota(jnp.int32, (C, C), 0).astype(F32)
    lane = lax.broadcasted_iota(jnp.int32, (8, C), 1).astype(F32)
    for d in range(2):
        for hd in range(RET_HEADS):
            i = d * RET_HEADS + hd
            z = dec_ref[i]
            lg8 = jnp.minimum(z, 0.0) - jnp.log1p(jnp.exp(-jnp.abs(z)))
            lg = jnp.broadcast_to(lg8[0:1, :], (C, C))
            dist = (n_idx - m_idx) if d == 0 else (m_idx - n_idx)
            ok = dist >= 0.0
            intra_ref[i] = jnp.where(ok, jnp.exp(lg * jnp.where(ok, dist, 0.0)), 0.0)
            if d == 0:
                qdec = jnp.exp(lg8 * (lane + 1.0))
                kdec = jnp.exp(lg8 * (C - 1.0 - lane))
            else:
                qdec = jnp.exp(lg8 * (C - lane))
                kdec = jnp.exp(lg8 * lane)
            vec_ref[i, 0] = qdec
            vec_ref[i, 1] = kdec
            vec_ref[i, 2] = jnp.exp(lg8 * float(C))
    s_acc[...] = s0_ref[...]
    o_acc[...] = jnp.zeros_like(o_acc)

    row_head = lax.broadcasted_iota(jnp.int32, (RET_HEADS * RET_DK, C), 0) // RET_DK

    def step(c, carry):
        first, vs = [], []
        for d in range(2):
            cc = c if d == 0 else nc - 1 - c
            q_all = rq_ref[cc]
            k_all = rk_ref[cc]
            k_tok = k_all.T.astype(BF16)
            for hd in range(RET_HEADS):
                i = d * RET_HEADS + hd
                r0 = hd * RET_DK
                q = q_all[r0:r0 + RET_DK]
                k = k_all[r0:r0 + RET_DK]
                v = rv_ref[cc, r0:r0 + RET_DV, :]
                q_m = jnp.where(row_head == hd, q_all, jnp.zeros_like(q_all))
                att_t = jnp.dot(k_tok, q_m, preferred_element_type=F32)
                s_t = s_acc[i]
                cross = jnp.dot(s_t.astype(BF16), q, preferred_element_type=F32)
                kd = (k * vec_ref[i, 1][0:1, :]).astype(BF16)
                kv = lax.dot_general(v, kd, (((1,), (1,)), ((), ())), preferred_element_type=F32)
                s_acc[i] = s_t * vec_ref[i, 2][0:1, 0:RET_DK] + kv
                first.append((att_t, cross))
                vs.append((i, cc, r0, v))
        masked = [(att_t * intra_ref[i]).astype(BF16) for (att_t, _), (i, _, _, _) in zip(first, vs)]
        for (_, cross), att_m, (i, cc, r0, v) in zip(first, masked, vs):
            o = jnp.dot(v, att_m, preferred_element_type=F32) + cross * vec_ref[i, 0][0:1, :]
            o_acc[cc, r0:r0 + RET_DV, :] += o
        return carry

    lax.fori_loop(0, nc, step, 0)
    sfin_ref[...] = s_acc[...]

    def finish(c, carry):
        o = o_acc[c]
        g = rg_ref[c]
        parts = []
        for hd in range(RET_HEADS):
            r0 = hd * RET_DV
            oh = o[r0:r0 + RET_DV]
            mu = jnp.mean(oh, axis=0, keepdims=True)
            var = jnp.mean(jnp.square(oh - mu), axis=0, keepdims=True)
            y = (oh - mu) * lax.rsqrt(var + EPS) * gn_ref[r0:r0 + RET_DV, :]
            parts.append(y * _silu(g[r0:r0 + RET_DV]))
        out_ref[c] = jnp.concatenate(parts, axis=0).astype(BF16)
        return carry

    lax.fori_loop(0, nc, finish, 0)


def _retention(rq, rk, rv, rg, s0_t, dec_b, gn_col, *, batch):
    nck = rq.shape[0]
    nc = nck // batch
    blk = pl.BlockSpec((nc, RET_W, LANE), lambda b: (b, 0, 0))
    nst = 2 * RET_HEADS
    st_spec = pl.BlockSpec((None, nst, RET_DV, RET_DK), lambda b: (b, 0, 0, 0))
    return pl.pallas_call(
        functools.partial(_retention_kernel, nc=nc),
        grid=(batch,),
        in_specs=[blk, blk, blk, blk, st_spec, _full(dec_b.shape), _full(gn_col.shape)],
        out_specs=[blk, st_spec],
        out_shape=[jax.ShapeDtypeStruct((nck, RET_W, LANE), BF16),
                   jax.ShapeDtypeStruct((batch, nst, RET_DV, RET_DK), F32)],
        scratch_shapes=[pltpu.VMEM((nc, RET_W, LANE), F32),
                        pltpu.VMEM((nst, RET_DV, RET_DK), F32),
                        pltpu.VMEM((nst, RET_CHUNK, RET_CHUNK), F32),
                        pltpu.VMEM((nst, 3, 8, LANE), F32)],
        compiler_params=_cparams(1),
        name="retention",
    )(rq, rk, rv, rg, s0_t, dec_b, gn_col)


def _attend(problems, tk, shifts=None):
    dv = DV_EXT - V_PAD
    flat = [(pi, part, r0) for pi, (_, parts) in enumerate(problems) for part in parts
            for r0 in range(0, part[4], tk)]
    if INTERLEAVE_PROBLEMS:
        per = [[it for it in flat if it[0] == pi] for pi in range(len(problems))]
        flat = [per[pi][b] for b in range(max(len(x) for x in per)) for pi in range(len(per)) if b < len(per[pi])]

    def scores(item):
        pi, (k_ref, c0, _, _, _), r0 = item
        q_t = problems[pi][0]
        return jnp.dot(k_ref[r0:r0 + tk, c0:c0 + q_t.shape[0]], q_t, preferred_element_type=F32)

    state = [(jnp.full((1, q_t.shape[1]), -jnp.inf, F32), jnp.zeros((DV_EXT, q_t.shape[1]), F32))
             for q_t, _ in problems]
    pending = [scores(it) for it in flat[:SCORE_LOOKAHEAD]]
    for i, (pi, (_, _, v_ref, v0, _), r0) in enumerate(flat):
        if i + SCORE_LOOKAHEAD < len(flat):
            pending.append(scores(flat[i + SCORE_LOOKAHEAD]))
        s = pending.pop(0)
        m, acc = state[pi]
        if shifts is None:
            m_new = jnp.maximum(m, jnp.max(s, axis=0, keepdims=True))
            acc = acc * jnp.exp2(m - m_new)
        else:
            m_new = shifts[pi]
        p = jnp.exp2(s - m_new)
        pv = jnp.dot(v_ref[v0:v0 + DV_EXT, r0:r0 + tk], p.astype(BF16), preferred_element_type=F32)
        state[pi] = (m_new, acc + pv)
    return [acc[:dv] / acc[dv:dv + 1] for _, acc in state]


def _attend_guarded(problems, key_sq_max, safe, tk):
    shifts = []
    for (q_t, _), ksq in zip(problems, key_sq_max):
        q32 = q_t.astype(F32)
        qsq = jnp.sum(q32 * q32, axis=0, keepdims=True)
        shifts.append(jnp.sqrt(qsq * ksq) * BOUND_SLACK)
    outs = lax.cond(safe,
                    lambda: tuple(_attend(problems, tk, shifts)),
                    lambda: tuple(_attend(problems, tk, None)))
    return list(outs)


def _row_bcast(a, row, tq):
    rid = lax.broadcasted_iota(jnp.int32, a.shape, 0)
    r = jnp.max(jnp.where(rid == row, a, 0.0), axis=0, keepdims=True)
    return jnp.concatenate([r] * (tq // LANE), axis=1)


def _diff_attn_kernel(*refs, has_ctx, rows_main, rows_ctx, tk, hps, lam_init):
    it = iter(refs)
    safe_ref, q_ref, k_ref, v_ref, ksq_ref = (next(it) for _ in range(5))
    if has_ctx:
        kc_ref, vc_ref = (next(it) for _ in range(2))
    dl_ref, g_ref, out_ref = (next(it) for _ in range(3))
    grp = pl.program_id(1)
    tq = q_ref.shape[1]
    segs_per_tile = LANE // DIFF_DK
    seg = lax.broadcasted_iota(jnp.int32, (LANE, tq), 0) // DIFF_DK
    problems, key_sq_max = [], []
    for j in range(hps):
        tile = (2 * j) // segs_per_tile
        q_tile = q_ref[tile * LANE:(tile + 1) * LANE, :]
        parts = [(k_ref, tile * LANE, v_ref, j * DV_EXT, rows_main)]
        if has_ctx:
            parts.append((kc_ref, tile * LANE, vc_ref, j * DV_EXT, rows_ctx))
        for mm in range(2):
            q_m = jnp.where(seg == (2 * j + mm) % segs_per_tile, q_tile, jnp.zeros_like(q_tile))
            problems.append((q_m, parts))
            key_sq_max.append(_row_bcast(ksq_ref[...], 2 * (grp * hps + j) + mm, tq))
    dl = dl_ref[...]
    lam = (jnp.exp(jnp.sum(dl[0:1] * dl[1:2], axis=1, keepdims=True))
           - jnp.exp(jnp.sum(dl[2:3] * dl[3:4], axis=1, keepdims=True)) + lam_init)
    outs = _attend_guarded(problems, key_sq_max, safe_ref[pl.program_id(0), grp] != 0, tk)
    for j in range(hps):
        o = outs[2 * j] - lam * outs[2 * j + 1]
        y = o * _rms_scale(o, DIFF_DV) * g_ref[...]
        out_ref[j * DIFF_DV:(j + 1) * DIFF_DV, :] = (y * (1.0 - lam_init)).astype(BF16)


def _safe_flags(q_sq, k_sq, per_group):
    bound = jnp.sqrt(q_sq * k_sq) * (BOUND_SLACK * BOUND_SLACK)
    ok = (2.0 * bound <= SAFE_EXP_RANGE).reshape(q_sq.shape[0], -1, per_group)
    return jnp.all(ok, axis=-1).astype(jnp.int32)


def _diff_attention(dq_t, dk_tok, dv_t, q_sq, k_sq, ctx, dl, g_col, *, batch, tq, tk, hps, lam_init):
    n_all = dq_t.shape[1]
    n = n_all // batch
    qt = n // tq
    nseg = 2 * DIFF_HEADS
    has_ctx = ctx is not None
    rows_ctx = ctx[0].shape[1] if has_ctx else 0
    assert (hps * 2 * DIFF_DK) % LANE == 0, "a grid step must cover whole 128-lane key tiles"
    qk_w = hps * 2 * DIFF_DK
    safe = _safe_flags(q_sq, k_sq, 2 * hps)
    k_sq_b = jnp.broadcast_to(k_sq[:, :, None], (batch, nseg, LANE))
    in_arrays = [safe, dq_t, dk_tok, dv_t, k_sq_b]
    in_specs = [pl.BlockSpec(memory_space=pltpu.SMEM),
                pl.BlockSpec((qk_w, tq), lambda b, g, i: (g, b * qt + i)),
                pl.BlockSpec((n, qk_w), lambda b, g, i: (b, g)),
                pl.BlockSpec((hps * DV_EXT, n), lambda b, g, i: (g, b)),
                pl.BlockSpec((None, nseg, LANE), lambda b, g, i: (b, 0, 0))]
    if has_ctx:
        in_arrays += list(ctx)
        in_specs += [pl.BlockSpec((None, rows_ctx, qk_w), lambda b, g, i: (b, 0, g)),
                     pl.BlockSpec((None, hps * DV_EXT, rows_ctx), lambda b, g, i: (b, g, 0))]
    in_arrays += [dl, g_col]
    in_specs += [_full(dl.shape), _full(g_col.shape)]
    return pl.pallas_call(
        functools.partial(_diff_attn_kernel, has_ctx=has_ctx, rows_main=n, rows_ctx=rows_ctx, tk=tk, hps=hps,
                          lam_init=lam_init),
        grid=(batch, DIFF_HEADS // hps, qt),
        in_specs=in_specs,
        out_specs=pl.BlockSpec((hps * DIFF_DV, tq), lambda b, g, i: (g, b * qt + i)),
        out_shape=jax.ShapeDtypeStruct((DIFF_W, n_all), BF16),
        compiler_params=_cparams(3),
        name="diff_attn_smp" if has_ctx else "diff_attn_ctx",
    )(*in_arrays)


def _mla_attn_kernel(*refs, has_ctx, rows_main, rows_ctx, tk, hps):
    it = iter(refs)
    safe_ref, q_ref, k_ref, v_ref, ksq_ref = (next(it) for _ in range(5))
    if has_ctx:
        kc_ref, vc_ref = (next(it) for _ in range(2))
    out_ref = next(it)
    grp = pl.program_id(1)
    tq = q_ref.shape[1]
    problems, key_sq_max = [], []
    for j in range(hps):
        parts = [(k_ref, j * MLA_PAD, v_ref, j * DV_EXT, rows_main)]
        if has_ctx:
            parts.append((kc_ref, j * MLA_PAD, vc_ref, j * DV_EXT, rows_ctx))
        problems.append((q_ref[j * MLA_PAD:(j + 1) * MLA_PAD, :], parts))
        key_sq_max.append(_row_bcast(ksq_ref[...], grp * hps + j, tq))
    outs = _attend_guarded(problems, key_sq_max, safe_ref[pl.program_id(0), grp] != 0, tk)
    for j, o in enumerate(outs):
        out_ref[j * MLA_DV:(j + 1) * MLA_DV, :] = o.astype(BF16)


def _mla_attention(mq_t, mk_tok, mv_t, q_sq, k_sq, ctx, *, batch, tq, tk, hps):
    n_all = mq_t.shape[1]
    n = n_all // batch
    qt = n // tq
    has_ctx = ctx is not None
    rows_ctx = ctx[0].shape[1] if has_ctx else 0
    safe = _safe_flags(q_sq, k_sq, hps)
    k_sq_b = jnp.broadcast_to(k_sq[:, :, None], (batch, MLA_HEADS, LANE))
    in_arrays = [safe, mq_t, mk_tok, mv_t, k_sq_b]
    in_specs = [pl.BlockSpec(memory_space=pltpu.SMEM),
                pl.BlockSpec((hps * MLA_PAD, tq), lambda b, g, i: (g, b * qt + i)),
                pl.BlockSpec((n, hps * MLA_PAD), lambda b, g, i: (b, g)),
                pl.BlockSpec((hps * DV_EXT, n), lambda b, g, i: (g, b)),
                pl.BlockSpec((None, MLA_HEADS, LANE), lambda b, g, i: (b, 0, 0))]
    if has_ctx:
        in_arrays += list(ctx)
        in_specs += [pl.BlockSpec((None, rows_ctx, hps * MLA_PAD), lambda b, g, i: (b, 0, g)),
                     pl.BlockSpec((None, hps * DV_EXT, rows_ctx), lambda b, g, i: (b, g, 0))]
    return pl.pallas_call(
        functools.partial(_mla_attn_kernel, has_ctx=has_ctx, rows_main=n, rows_ctx=rows_ctx, tk=tk, hps=hps),
        grid=(batch, MLA_HEADS // hps, qt),
        in_specs=in_specs,
        out_specs=pl.BlockSpec((hps * MLA_DV, tq), lambda b, g, i: (g, b * qt + i)),
        out_shape=jax.ShapeDtypeStruct((MLA_W, n_all), BF16),
        compiler_params=_cparams(3),
        name="mla_attn_smp" if has_ctx else "mla_attn_ctx",
    )(*in_arrays)


def _post_kernel(x_ref, ret_ref, diff_ref, mla_ref, g1_ref, sh2_ref, sc2_ref, g2_ref, n2_ref,
                 w_out_ref, w_gu_ref, w_down_ref, out_ref, *, tm, d_ff, ff_chunk):
    nch = tm // LANE
    ret = jnp.concatenate([ret_ref[j] for j in range(nch)], axis=1)
    mixed = jnp.concatenate([ret, diff_ref[...], mla_ref[...]], axis=0)
    x = x_ref[...]
    d = x.shape[0]
    x1 = x + g1_ref[...] * jnp.dot(w_out_ref[...], mixed, preferred_element_type=F32)
    h = (x1 * _rms_scale(x1, d) * (n2_ref[...] * (1.0 + sc2_ref[...])) + sh2_ref[...]).astype(BF16)
    ffn = None
    for lo in range(0, d_ff, ff_chunk):
        g = jnp.dot(w_gu_ref[lo:lo + ff_chunk, :], h, preferred_element_type=F32)
        u = jnp.dot(w_gu_ref[d_ff + lo:d_ff + lo + ff_chunk, :], h, preferred_element_type=F32)
        a = (_silu(g) * u).astype(BF16)
        part = jnp.dot(w_down_ref[:, lo:lo + ff_chunk], a, preferred_element_type=F32)
        ffn = part if ffn is None else ffn + part
    out_ref[...] = x1 + g2_ref[...] * ffn


def _post(x_t, ret, diff_t, mla_t, mods, lw, *, is_ctx, tm, tiles_per_batch):
    d, n = x_t.shape
    nt = n // tm
    nch = tm // LANE
    d_ff = lw["w_down"].shape[1]
    ff_chunk = d_ff // 2
    if is_ctx:
        bidx = lambda j: (0, 0, 0)
    else:
        bidx = lambda j: (j // tiles_per_batch, 0, 0)
    col = pl.BlockSpec((None, d, 1), bidx)
    const = lambda a: pl.BlockSpec(a.shape, lambda j: (0,) * a.ndim, pipeline_mode=pl.Buffered(1))
    g1, sh2, sc2, g2 = mods
    return pl.pallas_call(
        functools.partial(_post_kernel, tm=tm, d_ff=d_ff, ff_chunk=ff_chunk),
        grid=(nt,),
        in_specs=[pl.BlockSpec((d, tm), lambda j: (0, j)),
                  pl.BlockSpec((nch, RET_W, LANE), lambda j: (j, 0, 0)),
                  pl.BlockSpec((DIFF_W, tm), lambda j: (0, j)),
                  pl.BlockSpec((MLA_W, tm), lambda j: (0, j)),
                  col, col, col, col, const(lw["n2"]),
                  const(lw["w_out"]), const(lw["w_gu"]), const(lw["w_down"])],
        out_specs=pl.BlockSpec((d, tm), lambda j: (0, j)),
        out_shape=jax.ShapeDtypeStruct((d, n), F32),
        compiler_params=_cparams(1),
        name="post_ctx" if is_ctx else "post_smp",
    )(x_t, ret, diff_t, mla_t, g1, sh2, sc2, g2, lw["n2"], lw["w_out"], lw["w_gu"], lw["w_down"])


def _rope_tables_t(n, rot_dim):
    rows = n // GRID_W
    row = jnp.repeat(jnp.arange(rows, dtype=F32), GRID_W)
    col = jnp.tile(jnp.arange(GRID_W, dtype=F32), rows)
    n_freq = rot_dim // 4
    inv = 1.0 / (ROPE_BASE ** (jnp.arange(n_freq, dtype=F32) / n_freq))
    ang = jnp.concatenate([inv[:, None] * row[None, :], inv[:, None] * col[None, :]], axis=0)
    return jnp.cos(ang), jnp.sin(ang)


def _col(v):
    return v.astype(F32).reshape(-1, 1)


def _pad_heads_rows(w_t, used, pad):
    hk = w_t.shape[0] // used
    w3 = w_t.reshape(hk, used, w_t.shape[1])
    w3 = jnp.pad(w3, ((0, 0), (0, pad - used), (0, 0)))
    return w3.reshape(hk * pad, w_t.shape[1])


def _layer_weights(l, w_in, norm1, norm2, diff_qk_gain, mla_q_norm, mla_kv_norm, w_uq, w_ukv, mla_qk_gain,
                   w_out, w_gu, w_down):
    w_ukv_t = w_ukv[l].T.reshape(MLA_HEADS, MLA_NOPE + MLA_DV, MLA_KV_RANK)
    w_uk = jnp.pad(w_ukv_t[:, :MLA_NOPE], ((0, 0), (0, MLA_PAD - MLA_NOPE), (0, 0)))
    g_pad = lambda g: jnp.tile(jnp.pad(g.astype(F32), (0, MLA_PAD - MLA_DQK)), MLA_HEADS).reshape(-1, 1)
    return {
        "n1": _col(norm1[l]), "n2": _col(norm2[l]),
        "w_in": jnp.concatenate([w_in[l][:, W_IN_MLA0:], w_in[l][:, :W_IN_MLA0]], axis=1).T.astype(BF16),
        "g_dq": _col(jnp.tile(diff_qk_gain[l, 0], 2 * DIFF_HEADS)),
        "g_dk": _col(jnp.tile(diff_qk_gain[l, 1], 2 * DIFF_HEADS)),
        "g_qn": _col(mla_q_norm[l]), "g_kvn": _col(mla_kv_norm[l]),
        "w_uq": _pad_heads_rows(w_uq[l].T, MLA_DQK, MLA_PAD).astype(BF16),
        "g_mq": g_pad(mla_qk_gain[l, 0]), "g_mk": g_pad(mla_qk_gain[l, 1]),
        "w_uk": w_uk.reshape(MLA_QK_PAD_W, MLA_KV_RANK).astype(BF16),
        "w_uv": w_ukv_t[:, MLA_NOPE:].reshape(MLA_W, MLA_KV_RANK).astype(BF16),
        "w_out": w_out[l].T.astype(BF16), "w_gu": w_gu[l].T.astype(BF16), "w_down": w_down[l].T.astype(BF16),
    }


def kernel(x_prompt, x_sample, c, state_ret, cache_diff_k, cache_diff_v, cache_mla_ckv, cache_mla_kr, c_ctx,
           w_mod, b_mod, norm1, norm2, w_in, ret_decay, ret_gn_gain, diff_qk_gain, diff_lambda, diff_subln_gain,
           mla_q_norm, mla_kv_norm, w_uq, w_ukv, mla_qk_gain, w_out, w_gu, w_down):
    depth = w_in.shape[0]
    bp, sp, d = x_prompt.shape
    bs, ss, _ = x_sample.shape
    past = cache_diff_k.shape[2]

    tm = 512
    tq_s, tk_s = 512, 256
    tq_p = tk_p = sp

    n_cond = 1 + bs
    r_pad = -(-n_cond // 16) * 16
    cond = jnp.concatenate([c_ctx[None, :], c, jnp.zeros((r_pad - n_cond, d), F32)], axis=0)
    mod = _modulation(cond, w_mod, b_mod)
    mod = mod.reshape(depth, r_pad, 6, d, 1)

    xp = x_prompt.reshape(bp * sp, d).T
    xs = x_sample.reshape(bs * ss, d).T
    rope = _rope_tables_t(ss, RET_DK) + _rope_tables_t(ss, DIFF_DK)
    dec_b = jnp.broadcast_to(ret_decay.astype(F32).reshape(depth, 2 * RET_HEADS, 1, 1),
                             (depth, 2 * RET_HEADS, 8, LANE))
    s0_zero = jnp.zeros((bp, 2 * RET_HEADS, RET_DV, RET_DK), F32)
    ones_rows = jnp.zeros((bs, DIFF_HEADS, V_PAD, past), BF16).at[:, :, 0, :].set(1.0)

    new_ret, new_dk, new_dv, new_ckv, new_kr = [], [], [], [], []
    for l in range(depth):
        lw = _layer_weights(l, w_in, norm1, norm2, diff_qk_gain, mla_q_norm, mla_kv_norm, w_uq, w_ukv,
                            mla_qk_gain, w_out, w_gu, w_down)
        lam_init = 0.8 - 0.6 * math.exp(-0.3 * l)
        gn_col = _col(ret_gn_gain[l])
        subln_col = _col(diff_subln_gain[l])
        dl = diff_lambda[l].astype(F32)

        mc = [mod[l, 0:1, i] for i in range(6)]
        (rq, rk, rv, rg, dq_t, dk_tok, dv_t, mq_t, mk_tok, mv_t, sq, dk_f, dv_f, ckv_f, kr_f) = _premix(
            xp, mc[0], mc[1], lw, None, is_ctx=True, tm=tm, seq=sp)
        sq = jnp.max(sq.reshape(bp, -1, 4, 8, LANE), axis=(1, 4))
        ret, s_fin = _retention(rq, rk, rv, rg, s0_zero, dec_b[l], gn_col, batch=bp)
        diff_t = _diff_attention(dq_t, dk_tok, dv_t, sq[:, 0], sq[:, 1], None, dl, subln_col, batch=bp, tq=tq_p,
                                 tk=tk_p, hps=DIFF_HEADS, lam_init=lam_init)
        mla_t = _mla_attention(mq_t, mk_tok, mv_t, sq[:, 2], sq[:, 3], None, batch=bp, tq=tq_p, tk=tk_p,
                               hps=MLA_HEADS)
        xp = _post(xp, ret, diff_t, mla_t, (mc[2], mc[3], mc[4], mc[5]), lw, is_ctx=True, tm=tm,
                   tiles_per_batch=1)
        new_ret.append(jnp.swapaxes(s_fin.reshape(bp, 2, RET_HEADS, RET_DV, RET_DK), -1, -2))
        new_dk.append(dk_f.reshape(bp, sp, DIFF_HEADS, 2, DIFF_DK))
        new_dv.append(dv_f.reshape(bp, sp, DIFF_HEADS, DIFF_DV))
        new_ckv.append(ckv_f.reshape(bp, sp, MLA_KV_RANK))
        new_kr.append(kr_f[:, :MLA_ROPE].reshape(bp, sp, MLA_ROPE))

        ms = [mod[l, 1:1 + bs, i] for i in range(6)]
        s0_t = jnp.swapaxes(state_ret[:, l].astype(F32), -1, -2).reshape(bs, 2 * RET_HEADS, RET_DV, RET_DK)
        ctx_dk_f = cache_diff_k[:, l].reshape(bs, past, DIFF_QK_W).astype(F32)
        ctx_dk = ctx_dk_f.astype(BF16)
        ctx_dv_t = jnp.swapaxes(cache_diff_v[:, l].reshape(bs, past, DIFF_HEADS, DIFF_DV), 1, 3)
        ctx_dv_t = jnp.concatenate([jnp.swapaxes(ctx_dv_t, 1, 2).astype(BF16), ones_rows], axis=2)
        ctx_dv_t = ctx_dv_t.reshape(bs, DIFF_HEADS * DV_EXT, past)
        ctx_mk, ctx_mv_t, ctx_sq = _ctx_mla_kv(
            jnp.swapaxes(cache_mla_ckv[:, l].astype(F32), 1, 2), jnp.swapaxes(cache_mla_kr[:, l].astype(F32), 1, 2),
            jnp.swapaxes(ctx_dk_f, 1, 2), lw["w_uk"], lw["w_uv"], lw["g_mk"])
        ctx_sq = jnp.max(ctx_sq, axis=-1)
        (rq, rk, rv, rg, dq_t, dk_tok, dv_t, mq_t, mk_tok, mv_t, sq) = _premix(
            xs, ms[0], ms[1], lw, rope, is_ctx=False, tm=tm, seq=ss)
        sq = jnp.max(sq.reshape(bs, -1, 4, 8, LANE), axis=(1, 4))
        ret, _ = _retention(rq, rk, rv, rg, s0_t, dec_b[l], gn_col, batch=bs)
        diff_t = _diff_attention(dq_t, dk_tok, dv_t, sq[:, 0], jnp.maximum(sq[:, 1], ctx_sq[:, 1]),
                                 (ctx_dk, ctx_dv_t), dl, subln_col, batch=bs, tq=tq_s, tk=tk_s, hps=2,
                                 lam_init=lam_init)
        mla_t = _mla_attention(mq_t, mk_tok, mv_t, sq[:, 2], jnp.maximum(sq[:, 3], ctx_sq[:, 0]),
                               (ctx_mk, ctx_mv_t), batch=bs, tq=tq_s, tk=tk_s, hps=2)
        xs = _post(xs, ret, diff_t, mla_t, (ms[2], ms[3], ms[4], ms[5]), lw, is_ctx=False, tm=tm,
                   tiles_per_batch=ss // tm)

    y_p = xp.T.reshape(bp, sp, d)
    y_s = xs.T.reshape(bs, ss, d)
    return (y_p, y_s, jnp.stack(new_ret, axis=1), jnp.stack(new_dk, axis=1), jnp.stack(new_dv, axis=1),
            jnp.stack(new_ckv, axis=1), jnp.stack(new_kr, axis=1))
```

```python
import functools
import math

import jax
import jax.numpy as jnp
from jax import lax
from jax.experimental import pallas as pl
from jax.experimental.pallas import tpu as pltpu

F32 = jnp.float32
BF16 = jnp.bfloat16
EPS = 1e-6

GRID_W = 64
RET_HEADS = 4
RET_DK = 64
RET_DV = 64
RET_CHUNK = 128
DIFF_HEADS = 4
DIFF_DK = 32
DIFF_DV = 64
MLA_HEADS = 8
MLA_NOPE = 64
MLA_ROPE = 32
MLA_DQK = MLA_NOPE + MLA_ROPE
MLA_DV = 64
MLA_Q_RANK = 768
MLA_KV_RANK = 256
ROPE_BASE = 10000.0

RET_W = RET_HEADS * RET_DV
DIFF_W = DIFF_HEADS * DIFF_DV
MLA_W = MLA_HEADS * MLA_DV
DIFF_QK_W = DIFF_HEADS * 2 * DIFF_DK
MLA_PAD = 128
MLA_QK_PAD_W = MLA_HEADS * MLA_PAD
V_PAD = 16
DV_EXT = MLA_DV + V_PAD

_SPLITS = (RET_HEADS * RET_DK, RET_HEADS * RET_DK, RET_W, RET_W,
           DIFF_QK_W, DIFF_QK_W, DIFF_W, MLA_Q_RANK, MLA_KV_RANK, MLA_ROPE)
W_IN_MLA0 = int(sum(_SPLITS[:7]))
_PERM_SPLITS = _SPLITS[7:] + _SPLITS[:7]
_OFFS = tuple(int(sum(_PERM_SPLITS[:i])) for i in range(len(_PERM_SPLITS) + 1))
O_CQ, O_CKV, O_KR, O_RQ, O_RK, O_RV, O_RG, O_DQ, O_DK, O_DV, O_END = _OFFS

LOG2E = math.log2(math.e)
SCORE_LOOKAHEAD = 3
INTERLEAVE_PROBLEMS = True
SAFE_EXP_RANGE = 100.0
BOUND_SLACK = 1.0 + 2.0 ** -6
LANE = 128
VMEM_LIMIT = 56 * 1024 * 1024


def _cparams(n_grid):
    return pltpu.CompilerParams(dimension_semantics=("arbitrary",) * n_grid,
                                vmem_limit_bytes=VMEM_LIMIT)


def _full(shape):
    nd = len(shape)
    return pl.BlockSpec(shape, lambda *_: (0,) * nd)


def _silu(x):
    return x / (1.0 + jnp.exp(-x))


def _rms_scale(x, n):
    return lax.rsqrt(jnp.sum(x * x, axis=0, keepdims=True) * (1.0 / n) + EPS)


def _rope_pair(x1, x2, c, s):
    return x1 * c - x2 * s, x2 * c + x1 * s


def _with_ones_rows(v, heads):
    t = v.shape[1]
    dv = v.shape[0] // heads
    extra = jnp.where(lax.broadcasted_iota(jnp.int32, (V_PAD, t), 0) == 0, 1.0, 0.0).astype(BF16)
    parts = []
    for hd in range(heads):
        parts += [v[hd * dv:(hd + 1) * dv].astype(BF16), extra]
    return jnp.concatenate(parts, axis=0)


def _mod_kernel(c_ref, w_ref, b_ref, o_ref):
    a = _silu(c_ref[...]).astype(BF16)
    o_ref[...] = jnp.dot(a, w_ref[...].astype(BF16), preferred_element_type=F32) + b_ref[...]


def _modulation(cond, w_mod, b_mod):
    depth, d, d6 = w_mod.shape
    r = cond.shape[0]
    tn = 1536
    return pl.pallas_call(
        _mod_kernel,
        grid=(depth, d6 // tn),
        in_specs=[pl.BlockSpec((r, d), lambda l, j: (0, 0)),
                  pl.BlockSpec((None, d, tn), lambda l, j: (l, 0, j)),
                  pl.BlockSpec((None, 1, tn), lambda l, j: (l, 0, j))],
        out_specs=pl.BlockSpec((None, r, tn), lambda l, j: (l, 0, j)),
        out_shape=jax.ShapeDtypeStruct((depth, r, d6), F32),
        compiler_params=_cparams(2),
        name="adaln_mod",
    )(cond, w_mod, b_mod.reshape(depth, 1, d6))


def _mla_kv_matmuls(ckvn_bf, w_uk_ref, w_uv_ref):
    return (jnp.dot(w_uk_ref[...], ckvn_bf, preferred_element_type=F32),
            jnp.dot(w_uv_ref[...], ckvn_bf, preferred_element_type=F32))


def _mla_keys(kn, kr, g_mk_ref, rope):
    t = kr.shape[1]
    kr_ss = jnp.sum(kr * kr, axis=0, keepdims=True)
    zpad = jnp.zeros((MLA_PAD - MLA_DQK, t), F32)
    half = MLA_ROPE // 2
    heads = []
    for hd in range(MLA_HEADS):
        r0 = hd * MLA_PAD
        kh = kn[r0:r0 + MLA_NOPE]
        ss = jnp.sum(kh * kh, axis=0, keepdims=True) + kr_ss
        r = lax.rsqrt(ss * (1.0 / MLA_DQK) + EPS)
        g = g_mk_ref[r0:r0 + MLA_PAD, :]
        y_nope = kh * r * g[:MLA_NOPE]
        y_r = kr * r * g[MLA_NOPE:MLA_DQK]
        if rope is not None:
            c, s = rope
            y1, y2 = _rope_pair(y_r[:half], y_r[half:], c, s)
            heads += [y_nope, y1, y2, zpad]
        else:
            heads += [y_nope, y_r, zpad]
    return jnp.concatenate(heads, axis=0)


def _group_sq_norms(a, rows_per_group):
    groups = a.shape[0] // rows_per_group
    return jnp.concatenate([jnp.sum(jnp.square(a[g * rows_per_group:(g + 1) * rows_per_group]), axis=0, keepdims=True)
                            for g in range(groups)], axis=0)


def _lane_tile_max(sq):
    return functools.reduce(jnp.maximum, [sq[:, t * LANE:(t + 1) * LANE] for t in range(sq.shape[1] // LANE)])


def _ctx_kv_kernel(ckv_ref, kr_ref, dk_ref, w_uk_ref, w_uv_ref, g_mk_ref, mk_ref, mv_ref, sq_ref):
    kn, v = _mla_kv_matmuls(ckv_ref[...].astype(BF16), w_uk_ref, w_uv_ref)
    k = _mla_keys(kn, kr_ref[...], g_mk_ref, None)
    mk_ref[...] = k.T.astype(BF16)
    mv_ref[...] = _with_ones_rows(v, MLA_HEADS)
    sq_ref[0] = _lane_tile_max(_group_sq_norms(k, MLA_PAD))
    sq_ref[1] = _lane_tile_max(_group_sq_norms(dk_ref[...], DIFF_DK))


def _ctx_mla_kv(ckv_t, kr_t, dk_t, w_uk, w_uv, g_mk):
    b, _, l = ckv_t.shape
    return pl.pallas_call(
        _ctx_kv_kernel,
        grid=(b,),
        in_specs=[pl.BlockSpec((None, MLA_KV_RANK, l), lambda i: (i, 0, 0)),
                  pl.BlockSpec((None, MLA_ROPE, l), lambda i: (i, 0, 0)),
                  pl.BlockSpec((None, DIFF_QK_W, l), lambda i: (i, 0, 0)),
                  _full(w_uk.shape), _full(w_uv.shape), _full(g_mk.shape)],
        out_specs=[pl.BlockSpec((None, l, MLA_QK_PAD_W), lambda i: (i, 0, 0)),
                   pl.BlockSpec((None, MLA_HEADS * DV_EXT, l), lambda i: (i, 0, 0)),
                   pl.BlockSpec((None, 2, 8, LANE), lambda i: (i, 0, 0, 0))],
        out_shape=[jax.ShapeDtypeStruct((b, l, MLA_QK_PAD_W), BF16),
                   jax.ShapeDtypeStruct((b, MLA_HEADS * DV_EXT, l), BF16),
                   jax.ShapeDtypeStruct((b, 2, 8, LANE), F32)],
        compiler_params=_cparams(1),
        name="ctx_mla_kv",
    )(ckv_t, kr_t, dk_t, w_uk, w_uv, g_mk)


def _premix_kernel(*refs, is_ctx, tm):
    it = iter(refs)
    x_ref, shift_ref, scale_ref, n1_ref, w_in_ref = (next(it) for _ in range(5))
    g_dq_ref, g_dk_ref, g_qn_ref, g_kvn_ref = (next(it) for _ in range(4))
    w_uq_ref, g_mq_ref, w_uk_ref, w_uv_ref, g_mk_ref = (next(it) for _ in range(5))
    if is_ctx:
        rope_r = rope_s = None
    else:
        cr_ref, sr_ref, cs_ref, ss_ref = (next(it) for _ in range(4))
        rope_r = (cr_ref[...], sr_ref[...])
        rope_s = (cs_ref[...], ss_ref[...])
    rq_ref, rk_ref, rv_ref, rg_ref = (next(it) for _ in range(4))
    dq_ref, dk_ref, dv_ref, mq_ref, mk_ref, mv_ref = (next(it) for _ in range(6))
    if is_ctx:
        dkf_ref, dvf_ref, ckvf_ref, krf_ref = (next(it) for _ in range(4))

    x = x_ref[...]
    d = x.shape[0]
    h = (x * _rms_scale(x, d) * (n1_ref[...] * (1.0 + scale_ref[...])) + shift_ref[...]).astype(BF16)

    nch = tm // LANE

    def put_chunks(ref, val):
        for j in range(nch):
            ref[j] = val[:, j * LANE:(j + 1) * LANE].astype(ref.dtype)

    def proj(lo, hi):
        return jnp.dot(w_in_ref[lo:hi, :], h, preferred_element_type=F32)

    p_mla = proj(O_CQ, O_RQ)
    cq, ckv, kr = p_mla[O_CQ:O_CKV], p_mla[O_CKV:O_KR], p_mla[O_KR:O_RQ]
    p_ret = proj(O_RQ, O_DQ)
    rq, rk, rv, rg = (p_ret[o - O_RQ:o - O_RQ + RET_W] for o in (O_RQ, O_RK, O_RV, O_RG))
    cqn = (cq * _rms_scale(cq, MLA_Q_RANK) * g_qn_ref[...]).astype(BF16)
    ckvn = ckv * _rms_scale(ckv, MLA_KV_RANK) * g_kvn_ref[...]
    mq = jnp.dot(w_uq_ref[...], cqn, preferred_element_type=F32)
    kn, mv = _mla_kv_matmuls(ckvn.astype(BF16), w_uk_ref, w_uv_ref)
    p_diff = proj(O_DQ, O_END)
    dq_raw, dk_raw, dv = (p_diff[o - O_DQ:o - O_DQ + DIFF_QK_W] for o in (O_DQ, O_DK, O_DV))

    if rope_r is not None:
        c, s = rope_r
        hk = RET_DK // 2

        def rope_heads(a):
            parts = []
            for hd in range(RET_HEADS):
                r0 = hd * RET_DK
                parts += list(_rope_pair(a[r0:r0 + hk], a[r0 + hk:r0 + RET_DK], c, s))
            return jnp.concatenate(parts, axis=0)

        rq = rope_heads(rq)
        rk = rope_heads(rk)
    put_chunks(rq_ref, rq)
    put_chunks(rk_ref, rk * (RET_DK ** -0.5))
    put_chunks(rv_ref, rv)
    put_chunks(rg_ref, rg)

    def diff_qk(a, g_ref, out_scale):
        parts = []
        hs = DIFF_DK // 2
        for seg in range(2 * DIFF_HEADS):
            r0 = seg * DIFF_DK
            xs = a[r0:r0 + DIFF_DK]
            y = xs * _rms_scale(xs, DIFF_DK) * g_ref[r0:r0 + DIFF_DK, :]
            if out_scale != 1.0:
                y = y * out_scale
            if rope_s is not None:
                parts += list(_rope_pair(y[:hs], y[hs:], rope_s[0], rope_s[1]))
            else:
                parts.append(y)
        return jnp.concatenate(parts, axis=0)

    dq = diff_qk(dq_raw, g_dq_ref, DIFF_DK ** -0.5 * LOG2E)
    dq_ref[...] = dq.astype(BF16)
    dk = diff_qk(dk_raw, g_dk_ref, 1.0)
    dk_t = dk.T
    dk_ref[...] = dk_t.astype(BF16)
    dv_ref[...] = _with_ones_rows(dv, DIFF_HEADS)
    if is_ctx:
        dkf_ref[...] = dk_t
        dvf_ref[...] = dv.T

    half = MLA_ROPE // 2
    parts = []
    for hd in range(MLA_HEADS):
        r0 = hd * MLA_PAD
        xs = mq[r0:r0 + MLA_PAD]
        y = xs * _rms_scale(xs, MLA_DQK) * (g_mq_ref[r0:r0 + MLA_PAD, :] * (MLA_DQK ** -0.5 * LOG2E))
        if rope_s is not None:
            y1, y2 = _rope_pair(y[MLA_NOPE:MLA_NOPE + half], y[MLA_NOPE + half:MLA_DQK], rope_s[0], rope_s[1])
            parts += [y[:MLA_NOPE], y1, y2, y[MLA_DQK:]]
        else:
            parts.append(y)
    mq_ref[...] = jnp.concatenate(parts, axis=0).astype(BF16)

    mk = _mla_keys(kn, kr, g_mk_ref, rope_s)
    mk_ref[...] = mk.T.astype(BF16)
    mv_ref[...] = _with_ones_rows(mv, MLA_HEADS)
    if is_ctx:
        ckvf_ref[...] = ckvn.T
        krf_ref[...] = jnp.concatenate([kr, jnp.zeros((LANE - MLA_ROPE, tm), F32)], axis=0).T


def _premix(x_t, shift, scale, lw, rope, *, is_ctx, tm, seq):
    d, n = x_t.shape
    nt = n // tm
    nch = tm // LANE
    tiles_per_batch = max(1, seq // tm)
    if is_ctx:
        bidx = lambda j: (0, 0, 0)
    else:
        bidx = lambda j: (j // tiles_per_batch, 0, 0)
    in_arrays = [x_t, shift, scale, lw["n1"], lw["w_in"], lw["g_dq"], lw["g_dk"], lw["g_qn"], lw["g_kvn"],
                 lw["w_uq"], lw["g_mq"], lw["w_uk"], lw["w_uv"], lw["g_mk"]]
    in_specs = [pl.BlockSpec((d, tm), lambda j: (0, j)),
                pl.BlockSpec((None, d, 1), bidx), pl.BlockSpec((None, d, 1), bidx)]
    in_specs += [_full(a.shape) for a in in_arrays[3:]]
    if not is_ctx:
        for tab in rope:
            in_arrays.append(tab)
            in_specs.append(pl.BlockSpec((tab.shape[0], tm), lambda j: (0, j % tiles_per_batch)))

    chunk_spec = pl.BlockSpec((nch, RET_W, LANE), lambda j: (j, 0, 0))
    fm = lambda rows: pl.BlockSpec((rows, tm), lambda j: (0, j))
    tok = lambda cols: pl.BlockSpec((tm, cols), lambda j: (j, 0))
    nck = n // LANE
    out_specs = [chunk_spec] * 4 + [fm(DIFF_QK_W), tok(DIFF_QK_W), fm(DIFF_HEADS * DV_EXT),
                                    fm(MLA_QK_PAD_W), tok(MLA_QK_PAD_W), fm(MLA_HEADS * DV_EXT)]
    out_shape = [jax.ShapeDtypeStruct((nck, RET_W, LANE), BF16), jax.ShapeDtypeStruct((nck, RET_W, LANE), F32),
                 jax.ShapeDtypeStruct((nck, RET_W, LANE), BF16), jax.ShapeDtypeStruct((nck, RET_W, LANE), F32),
                 jax.ShapeDtypeStruct((DIFF_QK_W, n), BF16), jax.ShapeDtypeStruct((n, DIFF_QK_W), BF16),
                 jax.ShapeDtypeStruct((DIFF_HEADS * DV_EXT, n), BF16),
                 jax.ShapeDtypeStruct((MLA_QK_PAD_W, n), BF16), jax.ShapeDtypeStruct((n, MLA_QK_PAD_W), BF16),
                 jax.ShapeDtypeStruct((MLA_HEADS * DV_EXT, n), BF16)]
    if is_ctx:
        out_specs += [tok(DIFF_QK_W), tok(DIFF_W), tok(MLA_KV_RANK), tok(LANE)]
        out_shape += [jax.ShapeDtypeStruct((n, DIFF_QK_W), F32), jax.ShapeDtypeStruct((n, DIFF_W), F32),
                      jax.ShapeDtypeStruct((n, MLA_KV_RANK), F32), jax.ShapeDtypeStruct((n, LANE), F32)]
    return pl.pallas_call(
        functools.partial(_premix_kernel, is_ctx=is_ctx, tm=tm),
        grid=(nt,),
        in_specs=in_specs, out_specs=out_specs, out_shape=out_shape,
        compiler_params=_cparams(1),
        name="premix_ctx" if is_ctx else "premix_smp",
    )(*in_arrays)


def _retention_kernel(rq_ref, rk_ref, rv_ref, rg_ref, s0_ref, dec_ref, gn_ref, out_ref, sfin_ref,
                      o_acc, s_acc, intra_ref, vec_ref, *, nc):
    C = RET_CHUNK
    n_idx = lax.broadcasted_iota(jnp.int32, (C, C), 1).astype(F32)
    m_idx = lax.broadcasted_iota(jnp.int32, (C, C), 0).astype(F32)
    lane = lax.broadcasted_iota(jnp.int32, (8, C), 1).astype(F32)
    for d in range(2):
        for hd in range(RET_HEADS):
            i = d * RET_HEADS + hd
            z = dec_ref[i]
            lg8 = jnp.minimum(z, 0.0) - jnp.log1p(jnp.exp(-jnp.abs(z)))
            lg = jnp.broadcast_to(lg8[0:1, :], (C, C))
            dist = (n_idx - m_idx) if d == 0 else (m_idx - n_idx)
            ok = dist >= 0.0
            intra_ref[i] = jnp.where(ok, jnp.exp(lg * jnp.where(ok, dist, 0.0)), 0.0)
            if d == 0:
                qdec = jnp.exp(lg8 * (lane + 1.0))
                kdec = jnp.exp(lg8 * (C - 1.0 - lane))
            else:
                qdec = jnp.exp(lg8 * (C - lane))
                kdec = jnp.exp(lg8 * lane)
            vec_ref[i, 0] = qdec
            vec_ref[i, 1] = kdec
            vec_ref[i, 2] = jnp.exp(lg8 * float(C))
    s_acc[...] = s0_ref[...]
    o_acc[...] = jnp.zeros_like(o_acc)

    row_head = lax.broadcasted_iota(jnp.int32, (RET_HEADS * RET_DK, C), 0) // RET_DK

    def step(c, carry):
        first, vs = [], []
        for d in range(2):
            cc = c if d == 0 else nc - 1 - c
            q_all = rq_ref[cc]
            k_all = rk_ref[cc]
            k_tok = k_all.T.astype(BF16)
            q_bd = jnp.concatenate([jnp.where(row_head == hd, q_all, jnp.zeros_like(q_all))
                                    for hd in range(RET_HEADS)], axis=1)
            att_all = jnp.dot(k_tok, q_bd, preferred_element_type=F32)
            for hd in range(RET_HEADS):
                i = d * RET_HEADS + hd
                r0 = hd * RET_DK
                q = q_all[r0:r0 + RET_DK]
                k = k_all[r0:r0 + RET_DK]
                v = rv_ref[cc, r0:r0 + RET_DV, :]
                att_t = att_all[:, hd * C:(hd + 1) * C]
                s_t = s_acc[i]
                cross = jnp.dot(s_t.astype(BF16), q, preferred_element_type=F32)
                kd = (k * vec_ref[i, 1][0:1, :]).astype(BF16)
                kv = lax.dot_general(v, kd, (((1,), (1,)), ((), ())), preferred_element_type=F32)
                s_acc[i] = s_t * vec_ref[i, 2][0:1, 0:RET_DK] + kv
                first.append((att_t, cross))
                vs.append((i, cc, r0, v))
        masked = [(att_t * intra_ref[i]).astype(BF16) for (att_t, _), (i, _, _, _) in zip(first, vs)]
        for (_, cross), att_m, (i, cc, r0, v) in zip(first, masked, vs):
            o = jnp.dot(v, att_m, preferred_element_type=F32) + cross * vec_ref[i, 0][0:1, :]
            o_acc[cc, r0:r0 + RET_DV, :] += o
        return carry

    lax.fori_loop(0, nc, step, 0, unroll=2)
    sfin_ref[...] = s_acc[...]

    def finish(c, carry):
        o = o_acc[c]
        g = rg_ref[c]
        parts = []
        for hd in range(RET_HEADS):
            r0 = hd * RET_DV
            oh = o[r0:r0 + RET_DV]
            mu = jnp.mean(oh, axis=0, keepdims=True)
            var = jnp.mean(jnp.square(oh - mu), axis=0, keepdims=True)
            y = (oh - mu) * lax.rsqrt(var + EPS) * gn_ref[r0:r0 + RET_DV, :]
            parts.append(y * _silu(g[r0:r0 + RET_DV]))
        out_ref[c] = jnp.concatenate(parts, axis=0).astype(BF16)
        return carry

    lax.fori_loop(0, nc, finish, 0)


def _retention(rq, rk, rv, rg, s0_t, dec_b, gn_col, *, batch):
    nck = rq.shape[0]
    nc = nck // batch
    blk = pl.BlockSpec((nc, RET_W, LANE), lambda b: (b, 0, 0))
    nst = 2 * RET_HEADS
    st_spec = pl.BlockSpec((None, nst, RET_DV, RET_DK), lambda b: (b, 0, 0, 0))
    return pl.pallas_call(
        functools.partial(_retention_kernel, nc=nc),
        grid=(batch,),
        in_specs=[blk, blk, blk, blk, st_spec, _full(dec_b.shape), _full(gn_col.shape)],
        out_specs=[blk, st_spec],
        out_shape=[jax.ShapeDtypeStruct((nck, RET_W, LANE), BF16),
                   jax.ShapeDtypeStruct((batch, nst, RET_DV, RET_DK), F32)],
        scratch_shapes=[pltpu.VMEM((nc, RET_W, LANE), F32),
                        pltpu.VMEM((nst, RET_DV, RET_DK), F32),
                        pltpu.VMEM((nst, RET_CHUNK, RET_CHUNK), F32),
                        pltpu.VMEM((nst, 3, 8, LANE), F32)],
        compiler_params=_cparams(1),
        name="retention",
    )(rq, rk, rv, rg, s0_t, dec_b, gn_col)


def _attend(problems, tk, shifts=None):
    dv = DV_EXT - V_PAD
    flat = [(pi, part, r0) for pi, (_, parts) in enumerate(problems) for part in parts
            for r0 in range(0, part[4], tk)]
    if INTERLEAVE_PROBLEMS:
        per = [[it for it in flat if it[0] == pi] for pi in range(len(problems))]
        flat = [per[pi][b] for b in range(max(len(x) for x in per)) for pi in range(len(per)) if b < len(per[pi])]

    def scores(item):
        pi, (k_ref, c0, _, _, _), r0 = item
        q_t = problems[pi][0]
        return jnp.dot(k_ref[r0:r0 + tk, c0:c0 + q_t.shape[0]], q_t, preferred_element_type=F32)

    state = [(jnp.full((1, q_t.shape[1]), -jnp.inf, F32), jnp.zeros((DV_EXT, q_t.shape[1]), F32))
             for q_t, _ in problems]
    pending = [scores(it) for it in flat[:SCORE_LOOKAHEAD]]
    for i, (pi, (_, _, v_ref, v0, _), r0) in enumerate(flat):
        if i + SCORE_LOOKAHEAD < len(flat):
            pending.append(scores(flat[i + SCORE_LOOKAHEAD]))
        s = pending.pop(0)
        m, acc = state[pi]
        if shifts is None:
            m_new = jnp.maximum(m, jnp.max(s, axis=0, keepdims=True))
            acc = acc * jnp.exp2(m - m_new)
        else:
            m_new = shifts[pi]
        p = jnp.exp2(s - m_new)
        pv = jnp.dot(v_ref[v0:v0 + DV_EXT, r0:r0 + tk], p.astype(BF16), preferred_element_type=F32)
        state[pi] = (m_new, acc + pv)
    return [acc[:dv] / acc[dv:dv + 1] for _, acc in state]


def _attend_guarded(problems, key_sq_max, safe, tk):
    shifts = []
    for (q_t, _), ksq in zip(problems, key_sq_max):
        q32 = q_t.astype(F32)
        qsq = jnp.sum(q32 * q32, axis=0, keepdims=True)
        shifts.append(jnp.sqrt(qsq * ksq) * BOUND_SLACK)
    outs = lax.cond(safe,
                    lambda: tuple(_attend(problems, tk, shifts)),
                    lambda: tuple(_attend(problems, tk, None)))
    return list(outs)


def _row_bcast(a, row, tq):
    rid = lax.broadcasted_iota(jnp.int32, a.shape, 0)
    r = jnp.max(jnp.where(rid == row, a, 0.0), axis=0, keepdims=True)
    return jnp.concatenate([r] * (tq // LANE), axis=1)


def _diff_attn_kernel(*refs, has_ctx, rows_main, rows_ctx, tk, hps, lam_init):
    it = iter(refs)
    safe_ref, q_ref, k_ref, v_ref, ksq_ref = (next(it) for _ in range(5))
    if has_ctx:
        kc_ref, vc_ref = (next(it) for _ in range(2))
    dl_ref, g_ref, out_ref = (next(it) for _ in range(3))
    grp = pl.program_id(1)
    tq = q_ref.shape[1]
    segs_per_tile = LANE // DIFF_DK
    seg = lax.broadcasted_iota(jnp.int32, (LANE, tq), 0) // DIFF_DK
    problems, key_sq_max = [], []
    for j in range(hps):
        tile = (2 * j) // segs_per_tile
        q_tile = q_ref[tile * LANE:(tile + 1) * LANE, :]
        parts = [(k_ref, tile * LANE, v_ref, j * DV_EXT, rows_main)]
        if has_ctx:
            parts.append((kc_ref, tile * LANE, vc_ref, j * DV_EXT, rows_ctx))
        for mm in range(2):
            q_m = jnp.where(seg == (2 * j + mm) % segs_per_tile, q_tile, jnp.zeros_like(q_tile))
            problems.append((q_m, parts))
            key_sq_max.append(_row_bcast(ksq_ref[...], 2 * (grp * hps + j) + mm, tq))
    dl = dl_ref[...]
    lam = (jnp.exp(jnp.sum(dl[0:1] * dl[1:2], axis=1, keepdims=True))
           - jnp.exp(jnp.sum(dl[2:3] * dl[3:4], axis=1, keepdims=True)) + lam_init)
    outs = _attend_guarded(problems, key_sq_max, safe_ref[pl.program_id(0), grp] != 0, tk)
    for j in range(hps):
        o = outs[2 * j] - lam * outs[2 * j + 1]
        y = o * _rms_scale(o, DIFF_DV) * g_ref[...]
        out_ref[j * DIFF_DV:(j + 1) * DIFF_DV, :] = (y * (1.0 - lam_init)).astype(BF16)


def _safe_flags(q_sq, k_sq, per_group):
    bound = jnp.sqrt(q_sq * k_sq) * (BOUND_SLACK * BOUND_SLACK)
    ok = (2.0 * bound <= SAFE_EXP_RANGE).reshape(q_sq.shape[0], -1, per_group)
    return jnp.all(ok, axis=-1).astype(jnp.int32)


def _diff_attention(dq_t, dk_tok, dv_t, q_sq, k_sq, ctx, dl, g_col, *, batch, tq, tk, hps, lam_init):
    n_all = dq_t.shape[1]
    n = n_all // batch
    qt = n // tq
    nseg = 2 * DIFF_HEADS
    has_ctx = ctx is not None
    rows_ctx = ctx[0].shape[1] if has_ctx else 0
    assert (hps * 2 * DIFF_DK) % LANE == 0, "a grid step must cover whole 128-lane key tiles"
    qk_w = hps * 2 * DIFF_DK
    safe = _safe_flags(q_sq, k_sq, 2 * hps)
    k_sq_b = jnp.broadcast_to(k_sq[:, :, None], (batch, nseg, LANE))
    in_arrays = [safe, dq_t, dk_tok, dv_t, k_sq_b]
    in_specs = [pl.BlockSpec(memory_space=pltpu.SMEM),
                pl.BlockSpec((qk_w, tq), lambda b, g, i: (g, b * qt + i)),
                pl.BlockSpec((n, qk_w), lambda b, g, i: (b, g)),
                pl.BlockSpec((hps * DV_EXT, n), lambda b, g, i: (g, b)),
                pl.BlockSpec((None, nseg, LANE), lambda b, g, i: (b, 0, 0))]
    if has_ctx:
        in_arrays += list(ctx)
        in_specs += [pl.BlockSpec((None, rows_ctx, qk_w), lambda b, g, i: (b, 0, g)),
                     pl.BlockSpec((None, hps * DV_EXT, rows_ctx), lambda b, g, i: (b, g, 0))]
    in_arrays += [dl, g_col]
    in_specs += [_full(dl.shape), _full(g_col.shape)]
    return pl.pallas_call(
        functools.partial(_diff_attn_kernel, has_ctx=has_ctx, rows_main=n, rows_ctx=rows_ctx, tk=tk, hps=hps,
                          lam_init=lam_init),
        grid=(batch, DIFF_HEADS // hps, qt),
        in_specs=in_specs,
        out_specs=pl.BlockSpec((hps * DIFF_DV, tq), lambda b, g, i: (g, b * qt + i)),
        out_shape=jax.ShapeDtypeStruct((DIFF_W, n_all), BF16),
        compiler_params=_cparams(3),
        name="diff_attn_smp" if has_ctx else "diff_attn_ctx",
    )(*in_arrays)


def _mla_attn_kernel(*refs, has_ctx, rows_main, rows_ctx, tk, hps):
    it = iter(refs)
    safe_ref, q_ref, k_ref, v_ref, ksq_ref = (next(it) for _ in range(5))
    if has_ctx:
        kc_ref, vc_ref = (next(it) for _ in range(2))
    out_ref = next(it)
    grp = pl.program_id(1)
    tq = q_ref.shape[1]
    problems, key_sq_max = [], []
    for j in range(hps):
        parts = [(k_ref, j * MLA_PAD, v_ref, j * DV_EXT, rows_main)]
        if has_ctx:
            parts.append((kc_ref, j * MLA_PAD, vc_ref, j * DV_EXT, rows_ctx))
        problems.append((q_ref[j * MLA_PAD:(j + 1) * MLA_PAD, :], parts))
        key_sq_max.append(_row_bcast(ksq_ref[...], grp * hps + j, tq))
    outs = _attend_guarded(problems, key_sq_max, safe_ref[pl.program_id(0), grp] != 0, tk)
    for j, o in enumerate(outs):
        out_ref[j * MLA_DV:(j + 1) * MLA_DV, :] = o.astype(BF16)


def _mla_attention(mq_t, mk_tok, mv_t, q_sq, k_sq, ctx, *, batch, tq, tk, hps):
    n_all = mq_t.shape[1]
    n = n_all // batch
    qt = n // tq
    has_ctx = ctx is not None
    rows_ctx = ctx[0].shape[1] if has_ctx else 0
    safe = _safe_flags(q_sq, k_sq, hps)
    k_sq_b = jnp.broadcast_to(k_sq[:, :, None], (batch, MLA_HEADS, LANE))
    in_arrays = [safe, mq_t, mk_tok, mv_t, k_sq_b]
    in_specs = [pl.BlockSpec(memory_space=pltpu.SMEM),
                pl.BlockSpec((hps * MLA_PAD, tq), lambda b, g, i: (g, b * qt + i)),
                pl.BlockSpec((n, hps * MLA_PAD), lambda b, g, i: (b, g)),
                pl.BlockSpec((hps * DV_EXT, n), lambda b, g, i: (g, b)),
                pl.BlockSpec((None, MLA_HEADS, LANE), lambda b, g, i: (b, 0, 0))]
    if has_ctx:
        in_arrays += list(ctx)
        in_specs += [pl.BlockSpec((None, rows_ctx, hps * MLA_PAD), lambda b, g, i: (b, 0, g)),
                     pl.BlockSpec((None, hps * DV_EXT, rows_ctx), lambda b, g, i: (b, g, 0))]
    return pl.pallas_call(
        functools.partial(_mla_attn_kernel, has_ctx=has_ctx, rows_main=n, rows_ctx=rows_ctx, tk=tk, hps=hps),
        grid=(batch, MLA_HEADS // hps, qt),
        in_specs=in_specs,
        out_specs=pl.BlockSpec((hps * MLA_DV, tq), lambda b, g, i: (g, b * qt + i)),
        out_shape=jax.ShapeDtypeStruct((MLA_W, n_all), BF16),
        compiler_params=_cparams(3),
        name="mla_attn_smp" if has_ctx else "mla_attn_ctx",
    )(*in_arrays)


def _post_kernel(x_ref, ret_ref, diff_ref, mla_ref, g1_ref, sh2_ref, sc2_ref, g2_ref, n2_ref,
                 w_out_ref, w_gu_ref, w_down_ref, out_ref, *, tm, d_ff, ff_chunk):
    nch = tm // LANE
    ret = jnp.concatenate([ret_ref[j] for j in range(nch)], axis=1)
    mixed = jnp.concatenate([ret, diff_ref[...], mla_ref[...]], axis=0)
    x = x_ref[...]
    d = x.shape[0]
    x1 = x + g1_ref[...] * jnp.dot(w_out_ref[...], mixed, preferred_element_type=F32)
    h = (x1 * _rms_scale(x1, d) * (n2_ref[...] * (1.0 + sc2_ref[...])) + sh2_ref[...]).astype(BF16)
    ffn = None
    for lo in range(0, d_ff, ff_chunk):
        g = jnp.dot(w_gu_ref[lo:lo + ff_chunk, :], h, preferred_element_type=F32)
        u = jnp.dot(w_gu_ref[d_ff + lo:d_ff + lo + ff_chunk, :], h, preferred_element_type=F32)
        a = (_silu(g) * u).astype(BF16)
        part = jnp.dot(w_down_ref[:, lo:lo + ff_chunk], a, preferred_element_type=F32)
        ffn = part if ffn is None else ffn + part
    out_ref[...] = x1 + g2_ref[...] * ffn


def _post(x_t, ret, diff_t, mla_t, mods, lw, *, is_ctx, tm, tiles_per_batch):
    d, n = x_t.shape
    nt = n // tm
    nch = tm // LANE
    d_ff = lw["w_down"].shape[1]
    ff_chunk = d_ff // 2
    if is_ctx:
        bidx = lambda j: (0, 0, 0)
    else:
        bidx = lambda j: (j // tiles_per_batch, 0, 0)
    col = pl.BlockSpec((None, d, 1), bidx)
    const = lambda a: pl.BlockSpec(a.shape, lambda j: (0,) * a.ndim, pipeline_mode=pl.Buffered(1))
    g1, sh2, sc2, g2 = mods
    return pl.pallas_call(
        functools.partial(_post_kernel, tm=tm, d_ff=d_ff, ff_chunk=ff_chunk),
        grid=(nt,),
        in_specs=[pl.BlockSpec((d, tm), lambda j: (0, j)),
                  pl.BlockSpec((nch, RET_W, LANE), lambda j: (j, 0, 0)),
                  pl.BlockSpec((DIFF_W, tm), lambda j: (0, j)),
                  pl.BlockSpec((MLA_W, tm), lambda j: (0, j)),
                  col, col, col, col, const(lw["n2"]),
                  const(lw["w_out"]), const(lw["w_gu"]), const(lw["w_down"])],
        out_specs=pl.BlockSpec((d, tm), lambda j: (0, j)),
        out_shape=jax.ShapeDtypeStruct((d, n), F32),
        compiler_params=_cparams(1),
        name="post_ctx" if is_ctx else "post_smp",
    )(x_t, ret, diff_t, mla_t, g1, sh2, sc2, g2, lw["n2"], lw["w_out"], lw["w_gu"], lw["w_down"])


def _rope_tables_t(n, rot_dim):
    rows = n // GRID_W
    row = jnp.repeat(jnp.arange(rows, dtype=F32), GRID_W)
    col = jnp.tile(jnp.arange(GRID_W, dtype=F32), rows)
    n_freq = rot_dim // 4
    inv = 1.0 / (ROPE_BASE ** (jnp.arange(n_freq, dtype=F32) / n_freq))
    ang = jnp.concatenate([inv[:, None] * row[None, :], inv[:, None] * col[None, :]], axis=0)
    return jnp.cos(ang), jnp.sin(ang)


def _col(v):
    return v.astype(F32).reshape(-1, 1)


def _pad_heads_rows(w_t, used, pad):
    hk = w_t.shape[0] // used
    w3 = w_t.reshape(hk, used, w_t.shape[1])
    w3 = jnp.pad(w3, ((0, 0), (0, pad - used), (0, 0)))
    return w3.reshape(hk * pad, w_t.shape[1])


def _layer_weights(l, w_in, norm1, norm2, diff_qk_gain, mla_q_norm, mla_kv_norm, w_uq, w_ukv, mla_qk_gain,
                   w_out, w_gu, w_down):
    w_ukv_t = w_ukv[l].T.reshape(MLA_HEADS, MLA_NOPE + MLA_DV, MLA_KV_RANK)
    w_uk = jnp.pad(w_ukv_t[:, :MLA_NOPE], ((0, 0), (0, MLA_PAD - MLA_NOPE), (0, 0)))
    g_pad = lambda g: jnp.tile(jnp.pad(g.astype(F32), (0, MLA_PAD - MLA_DQK)), MLA_HEADS).reshape(-1, 1)
    return {
        "n1": _col(norm1[l]), "n2": _col(norm2[l]),
        "w_in": jnp.concatenate([w_in[l][:, W_IN_MLA0:], w_in[l][:, :W_IN_MLA0]], axis=1).T.astype(BF16),
        "g_dq": _col(jnp.tile(diff_qk_gain[l, 0], 2 * DIFF_HEADS)),
        "g_dk": _col(jnp.tile(diff_qk_gain[l, 1], 2 * DIFF_HEADS)),
        "g_qn": _col(mla_q_norm[l]), "g_kvn": _col(mla_kv_norm[l]),
        "w_uq": _pad_heads_rows(w_uq[l].T, MLA_DQK, MLA_PAD).astype(BF16),
        "g_mq": g_pad(mla_qk_gain[l, 0]), "g_mk": g_pad(mla_qk_gain[l, 1]),
        "w_uk": w_uk.reshape(MLA_QK_PAD_W, MLA_KV_RANK).astype(BF16),
        "w_uv": w_ukv_t[:, MLA_NOPE:].reshape(MLA_W, MLA_KV_RANK).astype(BF16),
        "w_out": w_out[l].T.astype(BF16), "w_gu": w_gu[l].T.astype(BF16), "w_down": w_down[l].T.astype(BF16),
    }


def kernel(x_prompt, x_sample, c, state_ret, cache_diff_k, cache_diff_v, cache_mla_ckv, cache_mla_kr, c_ctx,
           w_mod, b_mod, norm1, norm2, w_in, ret_decay, ret_gn_gain, diff_qk_gain, diff_lambda, diff_subln_gain,
           mla_q_norm, mla_kv_norm, w_uq, w_ukv, mla_qk_gain, w_out, w_gu, w_down):
    depth = w_in.shape[0]
    bp, sp, d = x_prompt.shape
    bs, ss, _ = x_sample.shape
    past = cache_diff_k.shape[2]

    tm = 512
    tq_s, tk_s = 512, 256
    tq_p = tk_p = sp

    n_cond = 1 + bs
    r_pad = -(-n_cond // 16) * 16
    cond = jnp.concatenate([c_ctx[None, :], c, jnp.zeros((r_pad - n_cond, d), F32)], axis=0)
    mod = _modulation(cond, w_mod, b_mod)
    mod = mod.reshape(depth, r_pad, 6, d, 1)

    xp = x_prompt.reshape(bp * sp, d).T
    xs = x_sample.reshape(bs * ss, d).T
    rope = _rope_tables_t(ss, RET_DK) + _rope_tables_t(ss, DIFF_DK)
    dec_b = jnp.broadcast_to(ret_decay.astype(F32).reshape(depth, 2 * RET_HEADS, 1, 1),
                             (depth, 2 * RET_HEADS, 8, LANE))
    s0_zero = jnp.zeros((bp, 2 * RET_HEADS, RET_DV, RET_DK), F32)
    ones_rows = jnp.zeros((bs, DIFF_HEADS, V_PAD, past), BF16).at[:, :, 0, :].set(1.0)

    new_ret, new_dk, new_dv, new_ckv, new_kr = [], [], [], [], []
    for l in range(depth):
        lw = _layer_weights(l, w_in, norm1, norm2, diff_qk_gain, mla_q_norm, mla_kv_norm, w_uq, w_ukv,
                            mla_qk_gain, w_out, w_gu, w_down)
        lam_init = 0.8 - 0.6 * math.exp(-0.3 * l)
        gn_col = _col(ret_gn_gain[l])
        subln_col = _col(diff_subln_gain[l])
        dl = diff_lambda[l].astype(F32)

        mc = [mod[l, 0:1, i] for i in range(6)]
        def norm_bound(gain, n, out_scale, batch):
            return jnp.full((batch, 8), n * out_scale * out_scale, F32) * jnp.max(jnp.square(gain.astype(F32)))

        dq_sq = functools.partial(norm_bound, diff_qk_gain[l, 0], DIFF_DK, DIFF_DK ** -0.5 * LOG2E)
        dk_sq = functools.partial(norm_bound, diff_qk_gain[l, 1], DIFF_DK, 1.0)
        mq_sq = functools.partial(norm_bound, mla_qk_gain[l, 0], MLA_DQK, MLA_DQK ** -0.5 * LOG2E)
        mk_sq = functools.partial(norm_bound, mla_qk_gain[l, 1], MLA_DQK, 1.0)

        (rq, rk, rv, rg, dq_t, dk_tok, dv_t, mq_t, mk_tok, mv_t, dk_f, dv_f, ckv_f, kr_f) = _premix(
            xp, mc[0], mc[1], lw, None, is_ctx=True, tm=tm, seq=sp)
        ret, s_fin = _retention(rq, rk, rv, rg, s0_zero, dec_b[l], gn_col, batch=bp)
        diff_t = _diff_attention(dq_t, dk_tok, dv_t, dq_sq(bp), dk_sq(bp), None, dl, subln_col, batch=bp, tq=tq_p,
                                 tk=tk_p, hps=DIFF_HEADS, lam_init=lam_init)
        mla_t = _mla_attention(mq_t, mk_tok, mv_t, mq_sq(bp), mk_sq(bp), None, batch=bp, tq=tq_p, tk=tk_p,
                               hps=MLA_HEADS)
        xp = _post(xp, ret, diff_t, mla_t, (mc[2], mc[3], mc[4], mc[5]), lw, is_ctx=True, tm=tm,
                   tiles_per_batch=1)
        new_ret.append(jnp.swapaxes(s_fin.reshape(bp, 2, RET_HEADS, RET_DV, RET_DK), -1, -2))
        new_dk.append(dk_f.reshape(bp, sp, DIFF_HEADS, 2, DIFF_DK))
        new_dv.append(dv_f.reshape(bp, sp, DIFF_HEADS, DIFF_DV))
        new_ckv.append(ckv_f.reshape(bp, sp, MLA_KV_RANK))
        new_kr.append(kr_f[:, :MLA_ROPE].reshape(bp, sp, MLA_ROPE))

        ms = [mod[l, 1:1 + bs, i] for i in range(6)]
        s0_t = jnp.swapaxes(state_ret[:, l].astype(F32), -1, -2).reshape(bs, 2 * RET_HEADS, RET_DV, RET_DK)
        ctx_dk_f = cache_diff_k[:, l].reshape(bs, past, DIFF_QK_W).astype(F32)
        ctx_dk = ctx_dk_f.astype(BF16)
        ctx_dv_t = jnp.swapaxes(cache_diff_v[:, l].reshape(bs, past, DIFF_HEADS, DIFF_DV), 1, 3)
        ctx_dv_t = jnp.concatenate([jnp.swapaxes(ctx_dv_t, 1, 2).astype(BF16), ones_rows], axis=2)
        ctx_dv_t = ctx_dv_t.reshape(bs, DIFF_HEADS * DV_EXT, past)
        ctx_mk, ctx_mv_t, ctx_sq = _ctx_mla_kv(
            jnp.swapaxes(cache_mla_ckv[:, l].astype(F32), 1, 2), jnp.swapaxes(cache_mla_kr[:, l].astype(F32), 1, 2),
            jnp.swapaxes(ctx_dk_f, 1, 2), lw["w_uk"], lw["w_uv"], lw["g_mk"])
        ctx_sq = jnp.max(ctx_sq, axis=-1)
        (rq, rk, rv, rg, dq_t, dk_tok, dv_t, mq_t, mk_tok, mv_t) = _premix(
            xs, ms[0], ms[1], lw, rope, is_ctx=False, tm=tm, seq=ss)
        ret, _ = _retention(rq, rk, rv, rg, s0_t, dec_b[l], gn_col, batch=bs)
        diff_t = _diff_attention(dq_t, dk_tok, dv_t, dq_sq(bs), jnp.maximum(dk_sq(bs), ctx_sq[:, 1]),
                                 (ctx_dk, ctx_dv_t), dl, subln_col, batch=bs, tq=tq_s, tk=tk_s, hps=2,
                                 lam_init=lam_init)
        mla_t = _mla_attention(mq_t, mk_tok, mv_t, mq_sq(bs), jnp.maximum(mk_sq(bs), ctx_sq[:, 0]),
                               (ctx_mk, ctx_mv_t), batch=bs, tq=tq_s, tk=tk_s, hps=4)
        xs = _post(xs, ret, diff_t, mla_t, (ms[2], ms[3], ms[4], ms[5]), lw, is_ctx=False, tm=tm,
                   tiles_per_batch=ss // tm)

    y_p = xp.T.reshape(bp, sp, d)
    y_s = xs.T.reshape(bs, ss, d)
    return (y_p, y_s, jnp.stack(new_ret, axis=1), jnp.stack(new_dk, axis=1), jnp.stack(new_dv, axis=1),
            jnp.stack(new_ckv, axis=1), jnp.stack(new_kr, axis=1))
```

```python
import functools
import math

import jax
import jax.numpy as jnp
from jax import lax
from jax.experimental import pallas as pl
from jax.experimental.pallas import tpu as pltpu

F32 = jnp.float32
BF16 = jnp.bfloat16
EPS = 1e-6

GRID_W = 64
RET_HEADS = 4
RET_DK = 64
RET_DV = 64
RET_CHUNK = 128
DIFF_HEADS = 4
DIFF_DK = 32
DIFF_DV = 64
MLA_HEADS = 8
MLA_NOPE = 64
MLA_ROPE = 32
MLA_DQK = MLA_NOPE + MLA_ROPE
MLA_DV = 64
MLA_Q_RANK = 768
MLA_KV_RANK = 256
ROPE_BASE = 10000.0

RET_W = RET_HEADS * RET_DV
DIFF_W = DIFF_HEADS * DIFF_DV
MLA_W = MLA_HEADS * MLA_DV
DIFF_QK_W = DIFF_HEADS * 2 * DIFF_DK
MLA_PAD = 128
MLA_QK_PAD_W = MLA_HEADS * MLA_PAD
V_PAD = 16
DV_EXT = MLA_DV + V_PAD

_SPLITS = (RET_HEADS * RET_DK, RET_HEADS * RET_DK, RET_W, RET_W,
           DIFF_QK_W, DIFF_QK_W, DIFF_W, MLA_Q_RANK, MLA_KV_RANK, MLA_ROPE)
W_IN_MLA0 = int(sum(_SPLITS[:7]))
_PERM_SPLITS = _SPLITS[7:] + _SPLITS[:7]
_OFFS = tuple(int(sum(_PERM_SPLITS[:i])) for i in range(len(_PERM_SPLITS) + 1))
O_CQ, O_CKV, O_KR, O_RQ, O_RK, O_RV, O_RG, O_DQ, O_DK, O_DV, O_END = _OFFS

LOG2E = math.log2(math.e)
SCORE_LOOKAHEAD = 3
INTERLEAVE_PROBLEMS = True
SAFE_EXP_RANGE = 100.0
BOUND_SLACK = 1.0 + 2.0 ** -6
LANE = 128
PREMIX_GROUPS = 2
VMEM_LIMIT = 56 * 1024 * 1024


def _cparams(n_grid):
    return pltpu.CompilerParams(dimension_semantics=("arbitrary",) * n_grid,
                                vmem_limit_bytes=VMEM_LIMIT)


def _full(shape):
    nd = len(shape)
    return pl.BlockSpec(shape, lambda *_: (0,) * nd)


def _silu(x):
    return x / (1.0 + jnp.exp(-x))


def _rms_scale(x, n):
    return lax.rsqrt(jnp.sum(x * x, axis=0, keepdims=True) * (1.0 / n) + EPS)


def _rope_pair(x1, x2, c, s):
    return x1 * c - x2 * s, x2 * c + x1 * s


def _with_ones_rows(v, heads):
    t = v.shape[1]
    dv = v.shape[0] // heads
    extra = jnp.where(lax.broadcasted_iota(jnp.int32, (V_PAD, t), 0) == 0, 1.0, 0.0).astype(BF16)
    parts = []
    for hd in range(heads):
        parts += [v[hd * dv:(hd + 1) * dv].astype(BF16), extra]
    return jnp.concatenate(parts, axis=0)


def _mod_kernel(c_ref, w_ref, b_ref, o_ref):
    a = _silu(c_ref[...]).astype(BF16)
    o_ref[...] = jnp.dot(a, w_ref[...].astype(BF16), preferred_element_type=F32) + b_ref[...]


def _modulation(cond, w_mod, b_mod):
    depth, d, d6 = w_mod.shape
    r = cond.shape[0]
    tn = 1536
    return pl.pallas_call(
        _mod_kernel,
        grid=(depth, d6 // tn),
        in_specs=[pl.BlockSpec((r, d), lambda l, j: (0, 0)),
                  pl.BlockSpec((None, d, tn), lambda l, j: (l, 0, j)),
                  pl.BlockSpec((None, 1, tn), lambda l, j: (l, 0, j))],
        out_specs=pl.BlockSpec((None, r, tn), lambda l, j: (l, 0, j)),
        out_shape=jax.ShapeDtypeStruct((depth, r, d6), F32),
        compiler_params=_cparams(2),
        name="adaln_mod",
    )(cond, w_mod, b_mod.reshape(depth, 1, d6))


def _mla_kv_matmuls(ckvn_bf, w_uk_ref, w_uv_ref):
    return (jnp.dot(w_uk_ref[...], ckvn_bf, preferred_element_type=F32),
            jnp.dot(w_uv_ref[...], ckvn_bf, preferred_element_type=F32))


def _mla_keys(kn, kr, g_mk_ref, rope):
    t = kr.shape[1]
    kr_ss = jnp.sum(kr * kr, axis=0, keepdims=True)
    zpad = jnp.zeros((MLA_PAD - MLA_DQK, t), F32)
    half = MLA_ROPE // 2
    heads = []
    for hd in range(MLA_HEADS):
        r0 = hd * MLA_PAD
        kh = kn[r0:r0 + MLA_NOPE]
        ss = jnp.sum(kh * kh, axis=0, keepdims=True) + kr_ss
        r = lax.rsqrt(ss * (1.0 / MLA_DQK) + EPS)
        g = g_mk_ref[r0:r0 + MLA_PAD, :]
        y_nope = kh * r * g[:MLA_NOPE]
        y_r = kr * r * g[MLA_NOPE:MLA_DQK]
        if rope is not None:
            c, s = rope
            y1, y2 = _rope_pair(y_r[:half], y_r[half:], c, s)
            heads += [y_nope, y1, y2, zpad]
        else:
            heads += [y_nope, y_r, zpad]
    return jnp.concatenate(heads, axis=0)


def _group_sq_norms(a, rows_per_group):
    groups = a.shape[0] // rows_per_group
    return jnp.concatenate([jnp.sum(jnp.square(a[g * rows_per_group:(g + 1) * rows_per_group]), axis=0, keepdims=True)
                            for g in range(groups)], axis=0)


def _lane_tile_max(sq):
    return functools.reduce(jnp.maximum, [sq[:, t * LANE:(t + 1) * LANE] for t in range(sq.shape[1] // LANE)])


def _ctx_kv_kernel(ckv_ref, kr_ref, dk_ref, w_uk_ref, w_uv_ref, g_mk_ref, mk_ref, mv_ref, sq_ref):
    kn, v = _mla_kv_matmuls(ckv_ref[...].astype(BF16), w_uk_ref, w_uv_ref)
    k = _mla_keys(kn, kr_ref[...], g_mk_ref, None)
    mk_ref[...] = k.T.astype(BF16)
    mv_ref[...] = _with_ones_rows(v, MLA_HEADS)
    sq_ref[0] = _lane_tile_max(_group_sq_norms(k, MLA_PAD))
    sq_ref[1] = _lane_tile_max(_group_sq_norms(dk_ref[...], DIFF_DK))


def _ctx_mla_kv(ckv_t, kr_t, dk_t, w_uk, w_uv, g_mk):
    b, _, l = ckv_t.shape
    return pl.pallas_call(
        _ctx_kv_kernel,
        grid=(b,),
        in_specs=[pl.BlockSpec((None, MLA_KV_RANK, l), lambda i: (i, 0, 0)),
                  pl.BlockSpec((None, MLA_ROPE, l), lambda i: (i, 0, 0)),
                  pl.BlockSpec((None, DIFF_QK_W, l), lambda i: (i, 0, 0)),
                  _full(w_uk.shape), _full(w_uv.shape), _full(g_mk.shape)],
        out_specs=[pl.BlockSpec((None, l, MLA_QK_PAD_W), lambda i: (i, 0, 0)),
                   pl.BlockSpec((None, MLA_HEADS * DV_EXT, l), lambda i: (i, 0, 0)),
                   pl.BlockSpec((None, 2, 8, LANE), lambda i: (i, 0, 0, 0))],
        out_shape=[jax.ShapeDtypeStruct((b, l, MLA_QK_PAD_W), BF16),
                   jax.ShapeDtypeStruct((b, MLA_HEADS * DV_EXT, l), BF16),
                   jax.ShapeDtypeStruct((b, 2, 8, LANE), F32)],
        compiler_params=_cparams(1),
        name="ctx_mla_kv",
    )(ckv_t, kr_t, dk_t, w_uk, w_uv, g_mk)


def _premix_kernel(*refs, is_ctx, tm):
    it = iter(refs)
    x_ref, shift_ref, scale_ref, n1_ref, w_in_ref = (next(it) for _ in range(5))
    g_dq_ref, g_dk_ref, g_qn_ref, g_kvn_ref = (next(it) for _ in range(4))
    w_uq_ref, g_mq_ref, w_uk_ref, w_uv_ref, g_mk_ref = (next(it) for _ in range(5))
    if not is_ctx:
        cr_ref, sr_ref, cs_ref, ss_ref = (next(it) for _ in range(4))
    rq_ref, rk_ref, rv_ref, rg_ref = (next(it) for _ in range(4))
    dq_ref, dk_ref, dv_ref, mq_ref, mk_ref, mv_ref = (next(it) for _ in range(6))
    if is_ctx:
        dkf_ref, dvf_ref, ckvf_ref, krf_ref = (next(it) for _ in range(4))
    d = x_ref.shape[0]

    def token_group(c0, w):
        cols = slice(c0, c0 + w)
        if is_ctx:
            rope_r = rope_s = None
        else:
            rope_r = (cr_ref[:, cols], sr_ref[:, cols])
            rope_s = (cs_ref[:, cols], ss_ref[:, cols])
        x = x_ref[:, cols]
        h = (x * _rms_scale(x, d) * (n1_ref[...] * (1.0 + scale_ref[...])) + shift_ref[...]).astype(BF16)
        yield

        def put_chunks(ref, val):
            for j in range(w // LANE):
                ref[c0 // LANE + j] = val[:, j * LANE:(j + 1) * LANE].astype(ref.dtype)

        def proj(lo, hi):
            return jnp.dot(w_in_ref[lo:hi, :], h, preferred_element_type=F32)

        p_mla = proj(O_CQ, O_RQ)
        cq, ckv, kr = p_mla[O_CQ:O_CKV], p_mla[O_CKV:O_KR], p_mla[O_KR:O_RQ]
        p_ret = proj(O_RQ, O_DQ)
        rq, rk, rv, rg = (p_ret[o - O_RQ:o - O_RQ + RET_W] for o in (O_RQ, O_RK, O_RV, O_RG))
        yield
        cqn = (cq * _rms_scale(cq, MLA_Q_RANK) * g_qn_ref[...]).astype(BF16)
        ckvn = ckv * _rms_scale(ckv, MLA_KV_RANK) * g_kvn_ref[...]
        mq = jnp.dot(w_uq_ref[...], cqn, preferred_element_type=F32)
        kn, mv = _mla_kv_matmuls(ckvn.astype(BF16), w_uk_ref, w_uv_ref)
        p_diff = proj(O_DQ, O_END)
        dq_raw, dk_raw, dv = (p_diff[o - O_DQ:o - O_DQ + DIFF_QK_W] for o in (O_DQ, O_DK, O_DV))
        yield

        if rope_r is not None:
            c, s = rope_r
            hk = RET_DK // 2

            def rope_heads(a):
                parts = []
                for hd in range(RET_HEADS):
                    r0 = hd * RET_DK
                    parts += list(_rope_pair(a[r0:r0 + hk], a[r0 + hk:r0 + RET_DK], c, s))
                return jnp.concatenate(parts, axis=0)

            rq = rope_heads(rq)
            rk = rope_heads(rk)
        put_chunks(rq_ref, rq)
        put_chunks(rk_ref, rk * (RET_DK ** -0.5))
        put_chunks(rv_ref, rv)
        put_chunks(rg_ref, rg)

        def diff_qk(a, g_ref, out_scale):
            parts = []
            hs = DIFF_DK // 2
            for seg in range(2 * DIFF_HEADS):
                r0 = seg * DIFF_DK
                xs = a[r0:r0 + DIFF_DK]
                y = xs * _rms_scale(xs, DIFF_DK) * g_ref[r0:r0 + DIFF_DK, :]
                if out_scale != 1.0:
                    y = y * out_scale
                if rope_s is not None:
                    parts += list(_rope_pair(y[:hs], y[hs:], rope_s[0], rope_s[1]))
                else:
                    parts.append(y)
            return jnp.concatenate(parts, axis=0)

        dq = diff_qk(dq_raw, g_dq_ref, DIFF_DK ** -0.5 * LOG2E)
        dq_ref[:, cols] = dq.astype(BF16)
        dk = diff_qk(dk_raw, g_dk_ref, 1.0)
        dk_t = dk.T
        dk_ref[cols, :] = dk_t.astype(BF16)
        dv_ref[:, cols] = _with_ones_rows(dv, DIFF_HEADS)
        if is_ctx:
            dkf_ref[cols, :] = dk_t
            dvf_ref[cols, :] = dv.T

        half = MLA_ROPE // 2
        parts = []
        for hd in range(MLA_HEADS):
            r0 = hd * MLA_PAD
            xs = mq[r0:r0 + MLA_PAD]
            y = xs * _rms_scale(xs, MLA_DQK) * (g_mq_ref[r0:r0 + MLA_PAD, :] * (MLA_DQK ** -0.5 * LOG2E))
            if rope_s is not None:
                y1, y2 = _rope_pair(y[MLA_NOPE:MLA_NOPE + half], y[MLA_NOPE + half:MLA_DQK], rope_s[0], rope_s[1])
                parts += [y[:MLA_NOPE], y1, y2, y[MLA_DQK:]]
            else:
                parts.append(y)
        mq_ref[:, cols] = jnp.concatenate(parts, axis=0).astype(BF16)

        mk = _mla_keys(kn, kr, g_mk_ref, rope_s)
        mk_ref[cols, :] = mk.T.astype(BF16)
        mv_ref[:, cols] = _with_ones_rows(mv, MLA_HEADS)
        if is_ctx:
            ckvf_ref[cols, :] = ckvn.T
            krf_ref[cols, :] = jnp.concatenate([kr, jnp.zeros((LANE - MLA_ROPE, w), F32)], axis=0).T
        yield

    groups = [token_group(i * (tm // PREMIX_GROUPS), tm // PREMIX_GROUPS) for i in range(PREMIX_GROUPS)]
    for _ in range(4):
        for g in groups:
            next(g)


def _premix(x_t, shift, scale, lw, rope, *, is_ctx, tm, seq):
    d, n = x_t.shape
    nt = n // tm
    nch = tm // LANE
    tiles_per_batch = max(1, seq // tm)
    if is_ctx:
        bidx = lambda j: (0, 0, 0)
    else:
        bidx = lambda j: (j // tiles_per_batch, 0, 0)
    in_arrays = [x_t, shift, scale, lw["n1"], lw["w_in"], lw["g_dq"], lw["g_dk"], lw["g_qn"], lw["g_kvn"],
                 lw["w_uq"], lw["g_mq"], lw["w_uk"], lw["w_uv"], lw["g_mk"]]
    in_specs = [pl.BlockSpec((d, tm), lambda j: (0, j)),
                pl.BlockSpec((None, d, 1), bidx), pl.BlockSpec((None, d, 1), bidx)]
    in_specs += [_full(a.shape) for a in in_arrays[3:]]
    if not is_ctx:
        for tab in rope:
            in_arrays.append(tab)
            in_specs.append(pl.BlockSpec((tab.shape[0], tm), lambda j: (0, j % tiles_per_batch)))

    chunk_spec = pl.BlockSpec((nch, RET_W, LANE), lambda j: (j, 0, 0))
    fm = lambda rows: pl.BlockSpec((rows, tm), lambda j: (0, j))
    tok = lambda cols: pl.BlockSpec((tm, cols), lambda j: (j, 0))
    nck = n // LANE
    out_specs = [chunk_spec] * 4 + [fm(DIFF_QK_W), tok(DIFF_QK_W), fm(DIFF_HEADS * DV_EXT),
                                    fm(MLA_QK_PAD_W), tok(MLA_QK_PAD_W), fm(MLA_HEADS * DV_EXT)]
    out_shape = [jax.ShapeDtypeStruct((nck, RET_W, LANE), BF16), jax.ShapeDtypeStruct((nck, RET_W, LANE), F32),
                 jax.ShapeDtypeStruct((nck, RET_W, LANE), BF16), jax.ShapeDtypeStruct((nck, RET_W, LANE), F32),
                 jax.ShapeDtypeStruct((DIFF_QK_W, n), BF16), jax.ShapeDtypeStruct((n, DIFF_QK_W), BF16),
                 jax.ShapeDtypeStruct((DIFF_HEADS * DV_EXT, n), BF16),
                 jax.ShapeDtypeStruct((MLA_QK_PAD_W, n), BF16), jax.ShapeDtypeStruct((n, MLA_QK_PAD_W), BF16),
                 jax.ShapeDtypeStruct((MLA_HEADS * DV_EXT, n), BF16)]
    if is_ctx:
        out_specs += [tok(DIFF_QK_W), tok(DIFF_W), tok(MLA_KV_RANK), tok(LANE)]
        out_shape += [jax.ShapeDtypeStruct((n, DIFF_QK_W), F32), jax.ShapeDtypeStruct((n, DIFF_W), F32),
                      jax.ShapeDtypeStruct((n, MLA_KV_RANK), F32), jax.ShapeDtypeStruct((n, LANE), F32)]
    return pl.pallas_call(
        functools.partial(_premix_kernel, is_ctx=is_ctx, tm=tm),
        grid=(nt,),
        in_specs=in_specs, out_specs=out_specs, out_shape=out_shape,
        compiler_params=_cparams(1),
        name="premix_ctx" if is_ctx else "premix_smp",
    )(*in_arrays)


def _retention_kernel(rq_ref, rk_ref, rv_ref, rg_ref, s0_ref, dec_ref, gn_ref, out_ref, sfin_ref,
                      o_acc, s_acc, intra_ref, vec_ref, *, nc):
    C = RET_CHUNK
    n_idx = lax.broadcasted_iota(jnp.int32, (C, C), 1).astype(F32)
    m_idx = lax.broadcasted_iota(jnp.int32, (C, C), 0).astype(F32)
    lane = lax.broadcasted_iota(jnp.int32, (8, C), 1).astype(F32)
    for d in range(2):
        for hd in range(RET_HEADS):
            i = d * RET_HEADS + hd
            z = dec_ref[i]
            lg8 = jnp.minimum(z, 0.0) - jnp.log1p(jnp.exp(-jnp.abs(z)))
            lg = jnp.broadcast_to(lg8[0:1, :], (C, C))
            dist = (n_idx - m_idx) if d == 0 else (m_idx - n_idx)
            ok = dist >= 0.0
            intra_ref[i] = jnp.where(ok, jnp.exp(lg * jnp.where(ok, dist, 0.0)), 0.0)
            if d == 0:
                qdec = jnp.exp(lg8 * (lane + 1.0))
                kdec = jnp.exp(lg8 * (C - 1.0 - lane))
            else:
                qdec = jnp.exp(lg8 * (C - lane))
                kdec = jnp.exp(lg8 * lane)
            vec_ref[i, 0] = qdec
            vec_ref[i, 1] = kdec
            vec_ref[i, 2] = jnp.exp(lg8 * float(C))
    s_acc[...] = s0_ref[...]
    o_acc[...] = jnp.zeros_like(o_acc)

    row_head = lax.broadcasted_iota(jnp.int32, (RET_HEADS * RET_DK, C), 0) // RET_DK

    def step(c, carry):
        first, vs = [], []
        for d in range(2):
            cc = c if d == 0 else nc - 1 - c
            q_all = rq_ref[cc]
            k_all = rk_ref[cc]
            k_tok = k_all.T.astype(BF16)
            q_bd = jnp.concatenate([jnp.where(row_head == hd, q_all, jnp.zeros_like(q_all))
                                    for hd in range(RET_HEADS)], axis=1)
            att_all = jnp.dot(k_tok, q_bd, preferred_element_type=F32)
            for hd in range(RET_HEADS):
                i = d * RET_HEADS + hd
                r0 = hd * RET_DK
                q = q_all[r0:r0 + RET_DK]
                k = k_all[r0:r0 + RET_DK]
                v = rv_ref[cc, r0:r0 + RET_DV, :]
                att_t = att_all[:, hd * C:(hd + 1) * C]
                s_t = s_acc[i]
                cross = jnp.dot(s_t.astype(BF16), q, preferred_element_type=F32)
                kd = (k * vec_ref[i, 1][0:1, :]).astype(BF16)
                kv = lax.dot_general(v, kd, (((1,), (1,)), ((), ())), preferred_element_type=F32)
                s_acc[i] = s_t * vec_ref[i, 2][0:1, 0:RET_DK] + kv
                first.append((att_t, cross))
                vs.append((i, cc, r0, v))
        masked = [(att_t * intra_ref[i]).astype(BF16) for (att_t, _), (i, _, _, _) in zip(first, vs)]
        for (_, cross), att_m, (i, cc, r0, v) in zip(first, masked, vs):
            o = jnp.dot(v, att_m, preferred_element_type=F32) + cross * vec_ref[i, 0][0:1, :]
            o_acc[cc, r0:r0 + RET_DV, :] += o
        return carry

    lax.fori_loop(0, nc, step, 0, unroll=2)
    sfin_ref[...] = s_acc[...]

    def finish(c, carry):
        o = o_acc[c]
        g = rg_ref[c]
        parts = []
        for hd in range(RET_HEADS):
            r0 = hd * RET_DV
            oh = o[r0:r0 + RET_DV]
            mu = jnp.mean(oh, axis=0, keepdims=True)
            var = jnp.mean(jnp.square(oh - mu), axis=0, keepdims=True)
            y = (oh - mu) * lax.rsqrt(var + EPS) * gn_ref[r0:r0 + RET_DV, :]
            parts.append(y * _silu(g[r0:r0 + RET_DV]))
        out_ref[c] = jnp.concatenate(parts, axis=0).astype(BF16)
        return carry

    lax.fori_loop(0, nc, finish, 0)


def _retention(rq, rk, rv, rg, s0_t, dec_b, gn_col, *, batch):
    nck = rq.shape[0]
    nc = nck // batch
    blk = pl.BlockSpec((nc, RET_W, LANE), lambda b: (b, 0, 0))
    nst = 2 * RET_HEADS
    st_spec = pl.BlockSpec((None, nst, RET_DV, RET_DK), lambda b: (b, 0, 0, 0))
    return pl.pallas_call(
        functools.partial(_retention_kernel, nc=nc),
        grid=(batch,),
        in_specs=[blk, blk, blk, blk, st_spec, _full(dec_b.shape), _full(gn_col.shape)],
        out_specs=[blk, st_spec],
        out_shape=[jax.ShapeDtypeStruct((nck, RET_W, LANE), BF16),
                   jax.ShapeDtypeStruct((batch, nst, RET_DV, RET_DK), F32)],
        scratch_shapes=[pltpu.VMEM((nc, RET_W, LANE), F32),
                        pltpu.VMEM((nst, RET_DV, RET_DK), F32),
                        pltpu.VMEM((nst, RET_CHUNK, RET_CHUNK), F32),
                        pltpu.VMEM((nst, 3, 8, LANE), F32)],
        compiler_params=_cparams(1),
        name="retention",
    )(rq, rk, rv, rg, s0_t, dec_b, gn_col)


def _attend(problems, tk, shifts=None):
    dv = DV_EXT - V_PAD
    flat = [(pi, part, r0, min(tk, part[4])) for pi, (_, parts) in enumerate(problems) for part in parts
            for r0 in range(0, part[4], min(tk, part[4]))]
    if INTERLEAVE_PROBLEMS:
        per = [[it for it in flat if it[0] == pi] for pi in range(len(problems))]
        flat = [per[pi][b] for b in range(max(len(x) for x in per)) for pi in range(len(per)) if b < len(per[pi])]

    def scores(item):
        pi, (k_ref, c0, _, _, _), r0, nk = item
        q_t = problems[pi][0]
        return jnp.dot(k_ref[r0:r0 + nk, c0:c0 + q_t.shape[0]], q_t, preferred_element_type=F32)

    state = [(jnp.full((1, q_t.shape[1]), -jnp.inf, F32), jnp.zeros((DV_EXT, q_t.shape[1]), F32))
             for q_t, _ in problems]
    pending = [scores(it) for it in flat[:SCORE_LOOKAHEAD]]
    for i, (pi, (_, _, v_ref, v0, _), r0, nk) in enumerate(flat):
        if i + SCORE_LOOKAHEAD < len(flat):
            pending.append(scores(flat[i + SCORE_LOOKAHEAD]))
        s = pending.pop(0)
        m, acc = state[pi]
        if shifts is None:
            m_new = jnp.maximum(m, jnp.max(s, axis=0, keepdims=True))
            acc = acc * jnp.exp2(m - m_new)
        else:
            m_new = shifts[pi]
        p = jnp.exp2(s - m_new)
        pv = jnp.dot(v_ref[v0:v0 + DV_EXT, r0:r0 + nk], p.astype(BF16), preferred_element_type=F32)
        state[pi] = (m_new, acc + pv)
    return [acc[:dv] / acc[dv:dv + 1] for _, acc in state]


def _attend_guarded(problems, key_sq_max, safe, tk):
    shifts = []
    for (q_t, _), ksq in zip(problems, key_sq_max):
        q32 = q_t.astype(F32)
        qsq = jnp.sum(q32 * q32, axis=0, keepdims=True)
        shifts.append(jnp.sqrt(qsq * ksq) * BOUND_SLACK)
    outs = lax.cond(safe,
                    lambda: tuple(_attend(problems, tk, shifts)),
                    lambda: tuple(_attend(problems, tk, None)))
    return list(outs)


def _row_bcast(a, row, tq):
    rid = lax.broadcasted_iota(jnp.int32, a.shape, 0)
    r = jnp.max(jnp.where(rid == row, a, 0.0), axis=0, keepdims=True)
    return jnp.concatenate([r] * (tq // LANE), axis=1)


def _diff_attn_kernel(*refs, has_ctx, rows_main, rows_ctx, tk, hps, lam_init):
    it = iter(refs)
    safe_ref, q_ref, k_ref, v_ref, ksq_ref = (next(it) for _ in range(5))
    if has_ctx:
        kc_ref, vc_ref = (next(it) for _ in range(2))
    dl_ref, g_ref, out_ref = (next(it) for _ in range(3))
    grp = pl.program_id(1)
    tq = q_ref.shape[1]
    segs_per_tile = LANE // DIFF_DK
    seg = lax.broadcasted_iota(jnp.int32, (LANE, tq), 0) // DIFF_DK
    problems, key_sq_max = [], []
    for j in range(hps):
        tile = (2 * j) // segs_per_tile
        q_tile = q_ref[tile * LANE:(tile + 1) * LANE, :]
        parts = [(k_ref, tile * LANE, v_ref, j * DV_EXT, rows_main)]
        if has_ctx:
            parts.append((kc_ref, tile * LANE, vc_ref, j * DV_EXT, rows_ctx))
        for mm in range(2):
            q_m = jnp.where(seg == (2 * j + mm) % segs_per_tile, q_tile, jnp.zeros_like(q_tile))
            problems.append((q_m, parts))
            key_sq_max.append(_row_bcast(ksq_ref[...], 2 * (grp * hps + j) + mm, tq))
    dl = dl_ref[...]
    lam = (jnp.exp(jnp.sum(dl[0:1] * dl[1:2], axis=1, keepdims=True))
           - jnp.exp(jnp.sum(dl[2:3] * dl[3:4], axis=1, keepdims=True)) + lam_init)
    outs = _attend_guarded(problems, key_sq_max, safe_ref[pl.program_id(0), grp] != 0, tk)
    for j in range(hps):
        o = outs[2 * j] - lam * outs[2 * j + 1]
        y = o * _rms_scale(o, DIFF_DV) * g_ref[...]
        out_ref[j * DIFF_DV:(j + 1) * DIFF_DV, :] = (y * (1.0 - lam_init)).astype(BF16)


def _safe_flags(q_sq, k_sq, per_group):
    bound = jnp.sqrt(q_sq * k_sq) * (BOUND_SLACK * BOUND_SLACK)
    ok = (2.0 * bound <= SAFE_EXP_RANGE).reshape(q_sq.shape[0], -1, per_group)
    return jnp.all(ok, axis=-1).astype(jnp.int32)


def _diff_attention(dq_t, dk_tok, dv_t, q_sq, k_sq, ctx, dl, g_col, *, batch, tq, tk, hps, lam_init):
    n_all = dq_t.shape[1]
    n = n_all // batch
    qt = n // tq
    nseg = 2 * DIFF_HEADS
    has_ctx = ctx is not None
    rows_ctx = ctx[0].shape[1] if has_ctx else 0
    assert (hps * 2 * DIFF_DK) % LANE == 0, "a grid step must cover whole 128-lane key tiles"
    qk_w = hps * 2 * DIFF_DK
    safe = _safe_flags(q_sq, k_sq, 2 * hps)
    k_sq_b = jnp.broadcast_to(k_sq[:, :, None], (batch, nseg, LANE))
    in_arrays = [safe, dq_t, dk_tok, dv_t, k_sq_b]
    in_specs = [pl.BlockSpec(memory_space=pltpu.SMEM),
                pl.BlockSpec((qk_w, tq), lambda b, g, i: (g, b * qt + i)),
                pl.BlockSpec((n, qk_w), lambda b, g, i: (b, g)),
                pl.BlockSpec((hps * DV_EXT, n), lambda b, g, i: (g, b)),
                pl.BlockSpec((None, nseg, LANE), lambda b, g, i: (b, 0, 0))]
    if has_ctx:
        in_arrays += list(ctx)
        in_specs += [pl.BlockSpec((None, rows_ctx, qk_w), lambda b, g, i: (b, 0, g)),
                     pl.BlockSpec((None, hps * DV_EXT, rows_ctx), lambda b, g, i: (b, g, 0))]
    in_arrays += [dl, g_col]
    in_specs += [_full(dl.shape), _full(g_col.shape)]
    return pl.pallas_call(
        functools.partial(_diff_attn_kernel, has_ctx=has_ctx, rows_main=n, rows_ctx=rows_ctx, tk=tk, hps=hps,
                          lam_init=lam_init),
        grid=(batch, DIFF_HEADS // hps, qt),
        in_specs=in_specs,
        out_specs=pl.BlockSpec((hps * DIFF_DV, tq), lambda b, g, i: (g, b * qt + i)),
        out_shape=jax.ShapeDtypeStruct((DIFF_W, n_all), BF16),
        compiler_params=_cparams(3),
        name="diff_attn_smp" if has_ctx else "diff_attn_ctx",
    )(*in_arrays)


def _mla_attn_kernel(*refs, has_ctx, rows_main, rows_ctx, tk, hps):
    it = iter(refs)
    safe_ref, q_ref, k_ref, v_ref, ksq_ref = (next(it) for _ in range(5))
    if has_ctx:
        kc_ref, vc_ref = (next(it) for _ in range(2))
    out_ref = next(it)
    grp = pl.program_id(1)
    tq = q_ref.shape[1]
    problems, key_sq_max = [], []
    for j in range(hps):
        parts = [(k_ref, j * MLA_PAD, v_ref, j * DV_EXT, rows_main)]
        if has_ctx:
            parts.append((kc_ref, j * MLA_PAD, vc_ref, j * DV_EXT, rows_ctx))
        problems.append((q_ref[j * MLA_PAD:(j + 1) * MLA_PAD, :], parts))
        key_sq_max.append(_row_bcast(ksq_ref[...], grp * hps + j, tq))
    outs = _attend_guarded(problems, key_sq_max, safe_ref[pl.program_id(0), grp] != 0, tk)
    for j, o in enumerate(outs):
        out_ref[j * MLA_DV:(j + 1) * MLA_DV, :] = o.astype(BF16)


def _mla_attention(mq_t, mk_tok, mv_t, q_sq, k_sq, ctx, *, batch, tq, tk, hps):
    n_all = mq_t.shape[1]
    n = n_all // batch
    qt = n // tq
    has_ctx = ctx is not None
    rows_ctx = ctx[0].shape[1] if has_ctx else 0
    safe = _safe_flags(q_sq, k_sq, hps)
    k_sq_b = jnp.broadcast_to(k_sq[:, :, None], (batch, MLA_HEADS, LANE))
    in_arrays = [safe, mq_t, mk_tok, mv_t, k_sq_b]
    in_specs = [pl.BlockSpec(memory_space=pltpu.SMEM),
                pl.BlockSpec((hps * MLA_PAD, tq), lambda b, g, i: (g, b * qt + i)),
                pl.BlockSpec((n, hps * MLA_PAD), lambda b, g, i: (b, g)),
                pl.BlockSpec((hps * DV_EXT, n), lambda b, g, i: (g, b)),
                pl.BlockSpec((None, MLA_HEADS, LANE), lambda b, g, i: (b, 0, 0))]
    if has_ctx:
        in_arrays += list(ctx)
        in_specs += [pl.BlockSpec((None, rows_ctx, hps * MLA_PAD), lambda b, g, i: (b, 0, g)),
                     pl.BlockSpec((None, hps * DV_EXT, rows_ctx), lambda b, g, i: (b, g, 0))]
    return pl.pallas_call(
        functools.partial(_mla_attn_kernel, has_ctx=has_ctx, rows_main=n, rows_ctx=rows_ctx, tk=tk, hps=hps),
        grid=(batch, MLA_HEADS // hps, qt),
        in_specs=in_specs,
        out_specs=pl.BlockSpec((hps * MLA_DV, tq), lambda b, g, i: (g, b * qt + i)),
        out_shape=jax.ShapeDtypeStruct((MLA_W, n_all), BF16),
        compiler_params=_cparams(3),
        name="mla_attn_smp" if has_ctx else "mla_attn_ctx",
    )(*in_arrays)


def _post_kernel(x_ref, ret_ref, diff_ref, mla_ref, g1_ref, sh2_ref, sc2_ref, g2_ref, n2_ref,
                 w_out_ref, w_gu_ref, w_down_ref, out_ref, *, tm, d_ff, ff_cuts):
    nch = tm // LANE
    ret = jnp.concatenate([ret_ref[j] for j in range(nch)], axis=1)
    mixed = jnp.concatenate([ret, diff_ref[...], mla_ref[...]], axis=0)
    d = x_ref.shape[0]
    halves = [slice(i * (tm // 2), (i + 1) * (tm // 2)) for i in range(2)]
    attn = [jnp.dot(w_out_ref[...], mixed[:, c], preferred_element_type=F32) for c in halves]
    x1 = [x_ref[:, c] + g1_ref[...] * a for c, a in zip(halves, attn)]
    gain2 = n2_ref[...] * (1.0 + sc2_ref[...])
    h = [(v * _rms_scale(v, d) * gain2 + sh2_ref[...]).astype(BF16) for v in x1]
    ffn = [None, None]
    for lo, hi in zip((0,) + ff_cuts, ff_cuts + (d_ff,)):
        for i in range(2):
            g = jnp.dot(w_gu_ref[lo:hi, :], h[i], preferred_element_type=F32)
            u = jnp.dot(w_gu_ref[d_ff + lo:d_ff + hi, :], h[i], preferred_element_type=F32)
            a = (_silu(g) * u).astype(BF16)
            part = jnp.dot(w_down_ref[:, lo:hi], a, preferred_element_type=F32)
            ffn[i] = part if ffn[i] is None else ffn[i] + part
    for i, c in enumerate(halves):
        out_ref[:, c] = x1[i] + g2_ref[...] * ffn[i]


def _post(x_t, ret, diff_t, mla_t, mods, lw, *, is_ctx, tm, tiles_per_batch):
    d, n = x_t.shape
    nt = n // tm
    nch = tm // LANE
    d_ff = lw["w_down"].shape[1]
    ff_cuts = (d_ff // 2,)
    if is_ctx:
        bidx = lambda j: (0, 0, 0)
    else:
        bidx = lambda j: (j // tiles_per_batch, 0, 0)
    col = pl.BlockSpec((None, d, 1), bidx)
    const = lambda a: pl.BlockSpec(a.shape, lambda j: (0,) * a.ndim, pipeline_mode=pl.Buffered(1))
    g1, sh2, sc2, g2 = mods
    return pl.pallas_call(
        functools.partial(_post_kernel, tm=tm, d_ff=d_ff, ff_cuts=ff_cuts),
        grid=(nt,),
        in_specs=[pl.BlockSpec((d, tm), lambda j: (0, j)),
                  pl.BlockSpec((nch, RET_W, LANE), lambda j: (j, 0, 0)),
                  pl.BlockSpec((DIFF_W, tm), lambda j: (0, j)),
                  pl.BlockSpec((MLA_W, tm), lambda j: (0, j)),
                  col, col, col, col, const(lw["n2"]),
                  const(lw["w_out"]), const(lw["w_gu"]), const(lw["w_down"])],
        out_specs=pl.BlockSpec((d, tm), lambda j: (0, j)),
        out_shape=jax.ShapeDtypeStruct((d, n), F32),
        compiler_params=_cparams(1),
        name="post_ctx" if is_ctx else "post_smp",
    )(x_t, ret, diff_t, mla_t, g1, sh2, sc2, g2, lw["n2"], lw["w_out"], lw["w_gu"], lw["w_down"])


def _rope_tables_t(n, rot_dim):
    rows = n // GRID_W
    row = jnp.repeat(jnp.arange(rows, dtype=F32), GRID_W)
    col = jnp.tile(jnp.arange(GRID_W, dtype=F32), rows)
    n_freq = rot_dim // 4
    inv = 1.0 / (ROPE_BASE ** (jnp.arange(n_freq, dtype=F32) / n_freq))
    ang = jnp.concatenate([inv[:, None] * row[None, :], inv[:, None] * col[None, :]], axis=0)
    return jnp.cos(ang), jnp.sin(ang)


def _col(v):
    return v.astype(F32).reshape(-1, 1)


def _pad_heads_rows(w_t, used, pad):
    hk = w_t.shape[0] // used
    w3 = w_t.reshape(hk, used, w_t.shape[1])
    w3 = jnp.pad(w3, ((0, 0), (0, pad - used), (0, 0)))
    return w3.reshape(hk * pad, w_t.shape[1])


def _layer_weights(l, w_in, norm1, norm2, diff_qk_gain, mla_q_norm, mla_kv_norm, w_uq, w_ukv, mla_qk_gain,
                   w_out, w_gu, w_down):
    w_ukv_t = w_ukv[l].T.reshape(MLA_HEADS, MLA_NOPE + MLA_DV, MLA_KV_RANK)
    w_uk = jnp.pad(w_ukv_t[:, :MLA_NOPE], ((0, 0), (0, MLA_PAD - MLA_NOPE), (0, 0)))
    g_pad = lambda g: jnp.tile(jnp.pad(g.astype(F32), (0, MLA_PAD - MLA_DQK)), MLA_HEADS).reshape(-1, 1)
    return {
        "n1": _col(norm1[l]), "n2": _col(norm2[l]),
        "w_in": jnp.concatenate([w_in[l][:, W_IN_MLA0:], w_in[l][:, :W_IN_MLA0]], axis=1).T.astype(BF16),
        "g_dq": _col(jnp.tile(diff_qk_gain[l, 0], 2 * DIFF_HEADS)),
        "g_dk": _col(jnp.tile(diff_qk_gain[l, 1], 2 * DIFF_HEADS)),
        "g_qn": _col(mla_q_norm[l]), "g_kvn": _col(mla_kv_norm[l]),
        "w_uq": _pad_heads_rows(w_uq[l].T, MLA_DQK, MLA_PAD).astype(BF16),
        "g_mq": g_pad(mla_qk_gain[l, 0]), "g_mk": g_pad(mla_qk_gain[l, 1]),
        "w_uk": w_uk.reshape(MLA_QK_PAD_W, MLA_KV_RANK).astype(BF16),
        "w_uv": w_ukv_t[:, MLA_NOPE:].reshape(MLA_W, MLA_KV_RANK).astype(BF16),
        "w_out": w_out[l].T.astype(BF16), "w_gu": w_gu[l].T.astype(BF16), "w_down": w_down[l].T.astype(BF16),
    }


def kernel(x_prompt, x_sample, c, state_ret, cache_diff_k, cache_diff_v, cache_mla_ckv, cache_mla_kr, c_ctx,
           w_mod, b_mod, norm1, norm2, w_in, ret_decay, ret_gn_gain, diff_qk_gain, diff_lambda, diff_subln_gain,
           mla_q_norm, mla_kv_norm, w_uq, w_ukv, mla_qk_gain, w_out, w_gu, w_down):
    depth = w_in.shape[0]
    bp, sp, d = x_prompt.shape
    bs, ss, _ = x_sample.shape
    past = cache_diff_k.shape[2]

    tm = 512
    tq_s, tk_s = 512, 256
    tq_p = tk_p = sp

    n_cond = 1 + bs
    r_pad = -(-n_cond // 16) * 16
    cond = jnp.concatenate([c_ctx[None, :], c, jnp.zeros((r_pad - n_cond, d), F32)], axis=0)
    mod = _modulation(cond, w_mod, b_mod)
    mod = mod.reshape(depth, r_pad, 6, d, 1)

    xp = x_prompt.reshape(bp * sp, d).T
    xs = x_sample.reshape(bs * ss, d).T
    rope = _rope_tables_t(ss, RET_DK) + _rope_tables_t(ss, DIFF_DK)
    dec_b = jnp.broadcast_to(ret_decay.astype(F32).reshape(depth, 2 * RET_HEADS, 1, 1),
                             (depth, 2 * RET_HEADS, 8, LANE))
    s0_zero = jnp.zeros((bp, 2 * RET_HEADS, RET_DV, RET_DK), F32)
    ones_rows = jnp.zeros((bs, DIFF_HEADS, V_PAD, past), BF16).at[:, :, 0, :].set(1.0)

    new_ret, new_dk, new_dv, new_ckv, new_kr = [], [], [], [], []
    for l in range(depth):
        lw = _layer_weights(l, w_in, norm1, norm2, diff_qk_gain, mla_q_norm, mla_kv_norm, w_uq, w_ukv,
                            mla_qk_gain, w_out, w_gu, w_down)
        lam_init = 0.8 - 0.6 * math.exp(-0.3 * l)
        gn_col = _col(ret_gn_gain[l])
        subln_col = _col(diff_subln_gain[l])
        dl = diff_lambda[l].astype(F32)

        mc = [mod[l, 0:1, i] for i in range(6)]
        def norm_bound(gain, n, out_scale, batch):
            return jnp.full((batch, 8), n * out_scale * out_scale, F32) * jnp.max(jnp.square(gain.astype(F32)))

        dq_sq = functools.partial(norm_bound, diff_qk_gain[l, 0], DIFF_DK, DIFF_DK ** -0.5 * LOG2E)
        dk_sq = functools.partial(norm_bound, diff_qk_gain[l, 1], DIFF_DK, 1.0)
        mq_sq = functools.partial(norm_bound, mla_qk_gain[l, 0], MLA_DQK, MLA_DQK ** -0.5 * LOG2E)
        mk_sq = functools.partial(norm_bound, mla_qk_gain[l, 1], MLA_DQK, 1.0)

        (rq, rk, rv, rg, dq_t, dk_tok, dv_t, mq_t, mk_tok, mv_t, dk_f, dv_f, ckv_f, kr_f) = _premix(
            xp, mc[0], mc[1], lw, None, is_ctx=True, tm=tm, seq=sp)
        ret, s_fin = _retention(rq, rk, rv, rg, s0_zero, dec_b[l], gn_col, batch=bp)
        diff_t = _diff_attention(dq_t, dk_tok, dv_t, dq_sq(bp), dk_sq(bp), None, dl, subln_col, batch=bp, tq=tq_p,
                                 tk=tk_p, hps=DIFF_HEADS, lam_init=lam_init)
        mla_t = _mla_attention(mq_t, mk_tok, mv_t, mq_sq(bp), mk_sq(bp), None, batch=bp, tq=tq_p, tk=tk_p,
                               hps=MLA_HEADS)
        xp = _post(xp, ret, diff_t, mla_t, (mc[2], mc[3], mc[4], mc[5]), lw, is_ctx=True, tm=tm,
                   tiles_per_batch=1)
        new_ret.append(jnp.swapaxes(s_fin.reshape(bp, 2, RET_HEADS, RET_DV, RET_DK), -1, -2))
        new_dk.append(dk_f.reshape(bp, sp, DIFF_HEADS, 2, DIFF_DK))
        new_dv.append(dv_f.reshape(bp, sp, DIFF_HEADS, DIFF_DV))
        new_ckv.append(ckv_f.reshape(bp, sp, MLA_KV_RANK))
        new_kr.append(kr_f[:, :MLA_ROPE].reshape(bp, sp, MLA_ROPE))

        ms = [mod[l, 1:1 + bs, i] for i in range(6)]
        s0_t = jnp.swapaxes(state_ret[:, l].astype(F32), -1, -2).reshape(bs, 2 * RET_HEADS, RET_DV, RET_DK)
        ctx_dk_f = cache_diff_k[:, l].reshape(bs, past, DIFF_QK_W).astype(F32)
        ctx_dk = ctx_dk_f.astype(BF16)
        ctx_dv_t = jnp.swapaxes(cache_diff_v[:, l].reshape(bs, past, DIFF_HEADS, DIFF_DV), 1, 3)
        ctx_dv_t = jnp.concatenate([jnp.swapaxes(ctx_dv_t, 1, 2).astype(BF16), ones_rows], axis=2)
        ctx_dv_t = ctx_dv_t.reshape(bs, DIFF_HEADS * DV_EXT, past)
        ctx_mk, ctx_mv_t, ctx_sq = _ctx_mla_kv(
            jnp.swapaxes(cache_mla_ckv[:, l].astype(F32), 1, 2), jnp.swapaxes(cache_mla_kr[:, l].astype(F32), 1, 2),
            jnp.swapaxes(ctx_dk_f, 1, 2), lw["w_uk"], lw["w_uv"], lw["g_mk"])
        ctx_sq = jnp.max(ctx_sq, axis=-1)
        (rq, rk, rv, rg, dq_t, dk_tok, dv_t, mq_t, mk_tok, mv_t) = _premix(
            xs, ms[0], ms[1], lw, rope, is_ctx=False, tm=tm, seq=ss)
        ret, _ = _retention(rq, rk, rv, rg, s0_t, dec_b[l], gn_col, batch=bs)
        diff_t = _diff_attention(dq_t, dk_tok, dv_t, dq_sq(bs), jnp.maximum(dk_sq(bs), ctx_sq[:, 1]),
                                 (ctx_dk, ctx_dv_t), dl, subln_col, batch=bs, tq=tq_s, tk=tk_s, hps=2,
                                 lam_init=lam_init)
        mla_t = _mla_attention(mq_t, mk_tok, mv_t, mq_sq(bs), jnp.maximum(mk_sq(bs), ctx_sq[:, 0]),
                               (ctx_mk, ctx_mv_t), batch=bs, tq=tq_s, tk=tk_s, hps=4)
        xs = _post(xs, ret, diff_t, mla_t, (ms[2], ms[3], ms[4], ms[5]), lw, is_ctx=False, tm=tm,
                   tiles_per_batch=ss // tm)

    y_p = xp.T.reshape(bp, sp, d)
    y_s = xs.T.reshape(bs, ss, d)
    return (y_p, y_s, jnp.stack(new_ret, axis=1), jnp.stack(new_dk, axis=1), jnp.stack(new_dv, axis=1),
            jnp.stack(new_ckv, axis=1), jnp.stack(new_kr, axis=1))
```

```python
import functools
import math

import jax
import jax.numpy as jnp
from jax import lax
from jax.experimental import pallas as pl
from jax.experimental.pallas import tpu as pltpu

F32 = jnp.float32
BF16 = jnp.bfloat16
EPS = 1e-6

GRID_W = 64
RET_HEADS = 4
RET_DK = 64
RET_DV = 64
RET_CHUNK = 128
DIFF_HEADS = 4
DIFF_DK = 32
DIFF_DV = 64
MLA_HEADS = 8
MLA_NOPE = 64
MLA_ROPE = 32
MLA_DQK = MLA_NOPE + MLA_ROPE
MLA_DV = 64
MLA_Q_RANK = 768
MLA_KV_RANK = 256
ROPE_BASE = 10000.0

RET_W = RET_HEADS * RET_DV
DIFF_W = DIFF_HEADS * DIFF_DV
MLA_W = MLA_HEADS * MLA_DV
DIFF_QK_W = DIFF_HEADS * 2 * DIFF_DK
MLA_PAD = 128
MLA_QK_PAD_W = MLA_HEADS * MLA_PAD
assert DIFF_DV == MLA_DV
HEAD_DV = MLA_DV

_SPLITS = (RET_HEADS * RET_DK, RET_HEADS * RET_DK, RET_W, RET_W,
           DIFF_QK_W, DIFF_QK_W, DIFF_W, MLA_Q_RANK, MLA_KV_RANK, MLA_ROPE)
W_IN_MLA0 = int(sum(_SPLITS[:7]))
_PERM_SPLITS = _SPLITS[7:] + _SPLITS[:7]
_OFFS = tuple(int(sum(_PERM_SPLITS[:i])) for i in range(len(_PERM_SPLITS) + 1))
O_CQ, O_CKV, O_KR, O_RQ, O_RK, O_RV, O_RG, O_DQ, O_DK, O_DV, O_END = _OFFS

LOG2E = math.log2(math.e)
SCORE_LOOKAHEAD = 2
SAFE_EXP_RANGE = 100.0
BOUND_SLACK = 1.0 + 2.0 ** -6
LANE = 128
PREMIX_GROUPS = 2
VMEM_LIMIT = 56 * 1024 * 1024


def _cparams(n_grid):
    return pltpu.CompilerParams(dimension_semantics=("arbitrary",) * n_grid,
                                vmem_limit_bytes=VMEM_LIMIT)


def _full(shape):
    nd = len(shape)
    return pl.BlockSpec(shape, lambda *_: (0,) * nd)


def _silu(x):
    return x / (1.0 + jnp.exp(-x))


def _rms_scale(x, n):
    return lax.rsqrt(jnp.sum(x * x, axis=0, keepdims=True) * (1.0 / n) + EPS)


def _rope_pair(x1, x2, c, s):
    return x1 * c - x2 * s, x2 * c + x1 * s


def _mod_kernel(c_ref, w_ref, b_ref, o_ref):
    a = _silu(c_ref[...]).astype(BF16)
    o_ref[...] = jnp.dot(a, w_ref[...].astype(BF16), preferred_element_type=F32) + b_ref[...]


def _modulation(cond, w_mod, b_mod):
    depth, d, d6 = w_mod.shape
    r = cond.shape[0]
    tn = 1536
    return pl.pallas_call(
        _mod_kernel,
        grid=(depth, d6 // tn),
        in_specs=[pl.BlockSpec((r, d), lambda l, j: (0, 0)),
                  pl.BlockSpec((None, d, tn), lambda l, j: (l, 0, j)),
                  pl.BlockSpec((None, 1, tn), lambda l, j: (l, 0, j))],
        out_specs=pl.BlockSpec((None, r, tn), lambda l, j: (l, 0, j)),
        out_shape=jax.ShapeDtypeStruct((depth, r, d6), F32),
        compiler_params=_cparams(2),
        name="adaln_mod",
    )(cond, w_mod, b_mod.reshape(depth, 1, d6))


def _mla_kv_matmuls(ckvn_bf, w_uk_ref, w_uv_ref):
    return (jnp.dot(w_uk_ref[...], ckvn_bf, preferred_element_type=F32),
            jnp.dot(w_uv_ref[...], ckvn_bf, preferred_element_type=F32))


def _mla_keys(kn, kr, g_mk_ref, rope):
    t = kr.shape[1]
    kr_ss = jnp.sum(kr * kr, axis=0, keepdims=True)
    zpad = jnp.zeros((MLA_PAD - MLA_DQK, t), F32)
    half = MLA_ROPE // 2
    heads = []
    for hd in range(MLA_HEADS):
        r0 = hd * MLA_PAD
        kh = kn[r0:r0 + MLA_NOPE]
        ss = jnp.sum(kh * kh, axis=0, keepdims=True) + kr_ss
        r = lax.rsqrt(ss * (1.0 / MLA_DQK) + EPS)
        g = g_mk_ref[r0:r0 + MLA_PAD, :]
        y_nope = kh * r * g[:MLA_NOPE]
        y_r = kr * r * g[MLA_NOPE:MLA_DQK]
        if rope is not None:
            c, s = rope
            y1, y2 = _rope_pair(y_r[:half], y_r[half:], c, s)
            heads += [y_nope, y1, y2, zpad]
        else:
            heads += [y_nope, y_r, zpad]
    return jnp.concatenate(heads, axis=0)


def _group_sq_norms(a, rows_per_group):
    groups = a.shape[0] // rows_per_group
    return jnp.concatenate([jnp.sum(jnp.square(a[g * rows_per_group:(g + 1) * rows_per_group]), axis=0, keepdims=True)
                            for g in range(groups)], axis=0)


def _lane_tile_max(sq):
    return functools.reduce(jnp.maximum, [sq[:, t * LANE:(t + 1) * LANE] for t in range(sq.shape[1] // LANE)])


def _ctx_kv_kernel(ckv_ref, kr_ref, dk_ref, w_uk_ref, w_uv_ref, g_mk_ref, mk_ref, mv_ref, sq_ref):
    kn, v = _mla_kv_matmuls(ckv_ref[...].astype(BF16), w_uk_ref, w_uv_ref)
    k = _mla_keys(kn, kr_ref[...], g_mk_ref, None)
    mk_ref[...] = k.T.astype(BF16)
    mv_ref[...] = v.astype(BF16)
    sq_ref[0] = _lane_tile_max(_group_sq_norms(k, MLA_PAD))
    sq_ref[1] = _lane_tile_max(_group_sq_norms(dk_ref[...], DIFF_DK))


def _ctx_mla_kv(ckv_t, kr_t, dk_t, w_uk, w_uv, g_mk):
    b, _, l = ckv_t.shape
    return pl.pallas_call(
        _ctx_kv_kernel,
        grid=(b,),
        in_specs=[pl.BlockSpec((None, MLA_KV_RANK, l), lambda i: (i, 0, 0)),
                  pl.BlockSpec((None, MLA_ROPE, l), lambda i: (i, 0, 0)),
                  pl.BlockSpec((None, DIFF_QK_W, l), lambda i: (i, 0, 0)),
                  _full(w_uk.shape), _full(w_uv.shape), _full(g_mk.shape)],
        out_specs=[pl.BlockSpec((None, l, MLA_QK_PAD_W), lambda i: (i, 0, 0)),
                   pl.BlockSpec((None, MLA_W, l), lambda i: (i, 0, 0)),
                   pl.BlockSpec((None, 2, 8, LANE), lambda i: (i, 0, 0, 0))],
        out_shape=[jax.ShapeDtypeStruct((b, l, MLA_QK_PAD_W), BF16),
                   jax.ShapeDtypeStruct((b, MLA_W, l), BF16),
                   jax.ShapeDtypeStruct((b, 2, 8, LANE), F32)],
        compiler_params=_cparams(1),
        name="ctx_mla_kv",
    )(ckv_t, kr_t, dk_t, w_uk, w_uv, g_mk)


def _premix_kernel(*refs, is_ctx, tm):
    it = iter(refs)
    x_ref, shift_ref, scale_ref, n1_ref, w_in_ref = (next(it) for _ in range(5))
    g_dq_ref, g_dk_ref, g_qn_ref, g_kvn_ref = (next(it) for _ in range(4))
    w_uq_ref, g_mq_ref, w_uk_ref, w_uv_ref, g_mk_ref = (next(it) for _ in range(5))
    if not is_ctx:
        cr_ref, sr_ref, cs_ref, ss_ref = (next(it) for _ in range(4))
    rq_ref, rk_ref, rv_ref, rg_ref = (next(it) for _ in range(4))
    dq_ref, dk_ref, dv_ref, mq_ref, mk_ref, mv_ref = (next(it) for _ in range(6))
    if is_ctx:
        dkf_ref, dvf_ref, ckvf_ref, krf_ref = (next(it) for _ in range(4))
    d = x_ref.shape[0]

    def token_group(c0, w):
        cols = slice(c0, c0 + w)
        if is_ctx:
            rope_r = rope_s = None
        else:
            rope_r = (cr_ref[:, cols], sr_ref[:, cols])
            rope_s = (cs_ref[:, cols], ss_ref[:, cols])
        x = x_ref[:, cols]
        h = (x * _rms_scale(x, d) * (n1_ref[...] * (1.0 + scale_ref[...])) + shift_ref[...]).astype(BF16)
        yield

        def put_chunks(ref, val):
            for j in range(w // LANE):
                ref[c0 // LANE + j] = val[:, j * LANE:(j + 1) * LANE].astype(ref.dtype)

        def proj(lo, hi):
            return jnp.dot(w_in_ref[lo:hi, :], h, preferred_element_type=F32)

        p_mla = proj(O_CQ, O_RQ)
        cq, ckv, kr = p_mla[O_CQ:O_CKV], p_mla[O_CKV:O_KR], p_mla[O_KR:O_RQ]
        p_ret = proj(O_RQ, O_DQ)
        rq, rk, rv, rg = (p_ret[o - O_RQ:o - O_RQ + RET_W] for o in (O_RQ, O_RK, O_RV, O_RG))
        yield
        cqn = (cq * _rms_scale(cq, MLA_Q_RANK) * g_qn_ref[...]).astype(BF16)
        ckvn = ckv * _rms_scale(ckv, MLA_KV_RANK) * g_kvn_ref[...]
        mq = jnp.dot(w_uq_ref[...], cqn, preferred_element_type=F32)
        kn, mv = _mla_kv_matmuls(ckvn.astype(BF16), w_uk_ref, w_uv_ref)
        p_diff = proj(O_DQ, O_END)
        dq_raw, dk_raw, dv = (p_diff[o - O_DQ:o - O_DQ + DIFF_QK_W] for o in (O_DQ, O_DK, O_DV))
        yield

        if rope_r is not None:
            c, s = rope_r
            hk = RET_DK // 2

            def rope_heads(a):
                parts = []
                for hd in range(RET_HEADS):
                    r0 = hd * RET_DK
                    parts += list(_rope_pair(a[r0:r0 + hk], a[r0 + hk:r0 + RET_DK], c, s))
                return jnp.concatenate(parts, axis=0)

            rq = rope_heads(rq)
            rk = rope_heads(rk)
        put_chunks(rq_ref, rq)
        put_chunks(rk_ref, rk * (RET_DK ** -0.5))
        put_chunks(rv_ref, rv)
        put_chunks(rg_ref, rg)

        def diff_qk(a, g_ref, out_scale):
            parts = []
            hs = DIFF_DK // 2
            for seg in range(2 * DIFF_HEADS):
                r0 = seg * DIFF_DK
                xs = a[r0:r0 + DIFF_DK]
                y = xs * _rms_scale(xs, DIFF_DK) * g_ref[r0:r0 + DIFF_DK, :]
                if out_scale != 1.0:
                    y = y * out_scale
                if rope_s is not None:
                    parts += list(_rope_pair(y[:hs], y[hs:], rope_s[0], rope_s[1]))
                else:
                    parts.append(y)
            return jnp.concatenate(parts, axis=0)

        dq = diff_qk(dq_raw, g_dq_ref, DIFF_DK ** -0.5 * LOG2E)
        dq_ref[:, cols] = dq.astype(BF16)
        dk = diff_qk(dk_raw, g_dk_ref, 1.0)
        dk_t = dk.T
        dk_ref[cols, :] = dk_t.astype(BF16)
        dv_ref[:, cols] = dv.astype(BF16)
        if is_ctx:
            dkf_ref[cols, :] = dk_t
            dvf_ref[cols, :] = dv.T

        half = MLA_ROPE // 2
        parts = []
        for hd in range(MLA_HEADS):
            r0 = hd * MLA_PAD
            xs = mq[r0:r0 + MLA_PAD]
            y = xs * _rms_scale(xs, MLA_DQK) * (g_mq_ref[r0:r0 + MLA_PAD, :] * (MLA_DQK ** -0.5 * LOG2E))
            if rope_s is not None:
                y1, y2 = _rope_pair(y[MLA_NOPE:MLA_NOPE + half], y[MLA_NOPE + half:MLA_DQK], rope_s[0], rope_s[1])
                parts += [y[:MLA_NOPE], y1, y2, y[MLA_DQK:]]
            else:
                parts.append(y)
        mq_ref[:, cols] = jnp.concatenate(parts, axis=0).astype(BF16)

        mk = _mla_keys(kn, kr, g_mk_ref, rope_s)
        mk_ref[cols, :] = mk.T.astype(BF16)
        mv_ref[:, cols] = mv.astype(BF16)
        if is_ctx:
            ckvf_ref[cols, :] = ckvn.T
            krf_ref[cols, :] = jnp.concatenate([kr, jnp.zeros((LANE - MLA_ROPE, w), F32)], axis=0).T
        yield

    groups = [token_group(i * (tm // PREMIX_GROUPS), tm // PREMIX_GROUPS) for i in range(PREMIX_GROUPS)]
    for _ in range(4):
        for g in groups:
            next(g)


def _premix(x_t, shift, scale, lw, rope, *, is_ctx, tm, seq):
    d, n = x_t.shape
    nt = n // tm
    nch = tm // LANE
    tiles_per_batch = max(1, seq // tm)
    if is_ctx:
        bidx = lambda j: (0, 0, 0)
    else:
        bidx = lambda j: (j // tiles_per_batch, 0, 0)
    in_arrays = [x_t, shift, scale, lw["n1"], lw["w_in"], lw["g_dq"], lw["g_dk"], lw["g_qn"], lw["g_kvn"],
                 lw["w_uq"], lw["g_mq"], lw["w_uk"], lw["w_uv"], lw["g_mk"]]
    in_specs = [pl.BlockSpec((d, tm), lambda j: (0, j)),
                pl.BlockSpec((None, d, 1), bidx), pl.BlockSpec((None, d, 1), bidx)]
    in_specs += [_full(a.shape) for a in in_arrays[3:]]
    if not is_ctx:
        for tab in rope:
            in_arrays.append(tab)
            in_specs.append(pl.BlockSpec((tab.shape[0], tm), lambda j: (0, j % tiles_per_batch)))

    chunk_spec = pl.BlockSpec((nch, RET_W, LANE), lambda j: (j, 0, 0))
    fm = lambda rows: pl.BlockSpec((rows, tm), lambda j: (0, j))
    tok = lambda cols: pl.BlockSpec((tm, cols), lambda j: (j, 0))
    nck = n // LANE
    out_specs = [chunk_spec] * 4 + [fm(DIFF_QK_W), tok(DIFF_QK_W), fm(DIFF_W),
                                    fm(MLA_QK_PAD_W), tok(MLA_QK_PAD_W), fm(MLA_W)]
    out_shape = [jax.ShapeDtypeStruct((nck, RET_W, LANE), BF16), jax.ShapeDtypeStruct((nck, RET_W, LANE), F32),
                 jax.ShapeDtypeStruct((nck, RET_W, LANE), BF16), jax.ShapeDtypeStruct((nck, RET_W, LANE), F32),
                 jax.ShapeDtypeStruct((DIFF_QK_W, n), BF16), jax.ShapeDtypeStruct((n, DIFF_QK_W), BF16),
                 jax.ShapeDtypeStruct((DIFF_W, n), BF16),
                 jax.ShapeDtypeStruct((MLA_QK_PAD_W, n), BF16), jax.ShapeDtypeStruct((n, MLA_QK_PAD_W), BF16),
                 jax.ShapeDtypeStruct((MLA_W, n), BF16)]
    if is_ctx:
        out_specs += [tok(DIFF_QK_W), tok(DIFF_W), tok(MLA_KV_RANK), tok(LANE)]
        out_shape += [jax.ShapeDtypeStruct((n, DIFF_QK_W), F32), jax.ShapeDtypeStruct((n, DIFF_W), F32),
                      jax.ShapeDtypeStruct((n, MLA_KV_RANK), F32), jax.ShapeDtypeStruct((n, LANE), F32)]
    return pl.pallas_call(
        functools.partial(_premix_kernel, is_ctx=is_ctx, tm=tm),
        grid=(nt,),
        in_specs=in_specs, out_specs=out_specs, out_shape=out_shape,
        compiler_params=_cparams(1),
        name="premix_ctx" if is_ctx else "premix_smp",
    )(*in_arrays)


def _retention_kernel(rq_ref, rk_ref, rv_ref, rg_ref, s0_ref, dec_ref, gn_ref, out_ref, sfin_ref,
                      o_acc, s_acc, intra_ref, vec_ref, *, nc):
    C = RET_CHUNK
    n_idx = lax.broadcasted_iota(jnp.int32, (C, C), 1).astype(F32)
    m_idx = lax.broadcasted_iota(jnp.int32, (C, C), 0).astype(F32)
    lane = lax.broadcasted_iota(jnp.int32, (8, C), 1).astype(F32)
    for d in range(2):
        for hd in range(RET_HEADS):
            i = d * RET_HEADS + hd
            z = dec_ref[i]
            lg8 = jnp.minimum(z, 0.0) - jnp.log1p(jnp.exp(-jnp.abs(z)))
            lg = jnp.broadcast_to(lg8[0:1, :], (C, C))
            dist = (n_idx - m_idx) if d == 0 else (m_idx - n_idx)
            ok = dist >= 0.0
            intra_ref[i] = jnp.where(ok, jnp.exp(lg * jnp.where(ok, dist, 0.0)), 0.0)
            if d == 0:
                qdec = jnp.exp(lg8 * (lane + 1.0))
                kdec = jnp.exp(lg8 * (C - 1.0 - lane))
            else:
                qdec = jnp.exp(lg8 * (C - lane))
                kdec = jnp.exp(lg8 * lane)
            vec_ref[i, 0] = qdec
            vec_ref[i, 1] = kdec
            vec_ref[i, 2] = jnp.exp(lg8 * float(C))
    s_acc[...] = s0_ref[...]
    o_acc[...] = jnp.zeros_like(o_acc)

    row_head = lax.broadcasted_iota(jnp.int32, (RET_HEADS * RET_DK, C), 0) // RET_DK

    def step(c, carry):
        first, vs = [], []
        for d in range(2):
            cc = c if d == 0 else nc - 1 - c
            q_all = rq_ref[cc]
            k_all = rk_ref[cc]
            k_tok = k_all.T.astype(BF16)
            q_bd = jnp.concatenate([jnp.where(row_head == hd, q_all, jnp.zeros_like(q_all))
                                    for hd in range(RET_HEADS)], axis=1)
            att_all = jnp.dot(k_tok, q_bd, preferred_element_type=F32)
            for hd in range(RET_HEADS):
                i = d * RET_HEADS + hd
                r0 = hd * RET_DK
                q = q_all[r0:r0 + RET_DK]
                k = k_all[r0:r0 + RET_DK]
                v = rv_ref[cc, r0:r0 + RET_DV, :]
                att_t = att_all[:, hd * C:(hd + 1) * C]
                s_t = s_acc[i]
                cross = jnp.dot(s_t.astype(BF16), q, preferred_element_type=F32)
                kd = (k * vec_ref[i, 1][0:1, :]).astype(BF16)
                kv = lax.dot_general(v, kd, (((1,), (1,)), ((), ())), preferred_element_type=F32)
                s_acc[i] = s_t * vec_ref[i, 2][0:1, 0:RET_DK] + kv
                first.append((att_t, cross))
                vs.append((i, cc, r0, v))
        masked = [(att_t * intra_ref[i]).astype(BF16) for (att_t, _), (i, _, _, _) in zip(first, vs)]
        for (_, cross), att_m, (i, cc, r0, v) in zip(first, masked, vs):
            o = jnp.dot(v, att_m, preferred_element_type=F32) + cross * vec_ref[i, 0][0:1, :]
            o_acc[cc, r0:r0 + RET_DV, :] += o
        return carry

    lax.fori_loop(0, nc, step, 0, unroll=2)
    sfin_ref[...] = s_acc[...]

    def finish(c, carry):
        o = o_acc[c]
        g = rg_ref[c]
        parts = []
        for hd in range(RET_HEADS):
            r0 = hd * RET_DV
            oh = o[r0:r0 + RET_DV]
            mu = jnp.mean(oh, axis=0, keepdims=True)
            var = jnp.mean(jnp.square(oh - mu), axis=0, keepdims=True)
            y = (oh - mu) * lax.rsqrt(var + EPS) * gn_ref[r0:r0 + RET_DV, :]
            parts.append(y * _silu(g[r0:r0 + RET_DV]))
        out_ref[c] = jnp.concatenate(parts, axis=0).astype(BF16)
        return carry

    lax.fori_loop(0, nc, finish, 0)


def _retention(rq, rk, rv, rg, s0_t, dec_b, gn_col, *, batch):
    nck = rq.shape[0]
    nc = nck // batch
    blk = pl.BlockSpec((nc, RET_W, LANE), lambda b: (b, 0, 0))
    nst = 2 * RET_HEADS
    st_spec = pl.BlockSpec((None, nst, RET_DV, RET_DK), lambda b: (b, 0, 0, 0))
    return pl.pallas_call(
        functools.partial(_retention_kernel, nc=nc),
        grid=(batch,),
        in_specs=[blk, blk, blk, blk, st_spec, _full(dec_b.shape), _full(gn_col.shape)],
        out_specs=[blk, st_spec],
        out_shape=[jax.ShapeDtypeStruct((nck, RET_W, LANE), BF16),
                   jax.ShapeDtypeStruct((batch, nst, RET_DV, RET_DK), F32)],
        scratch_shapes=[pltpu.VMEM((nc, RET_W, LANE), F32),
                        pltpu.VMEM((nst, RET_DV, RET_DK), F32),
                        pltpu.VMEM((nst, RET_CHUNK, RET_CHUNK), F32),
                        pltpu.VMEM((nst, 3, 8, LANE), F32)],
        compiler_params=_cparams(1),
        name="retention",
    )(rq, rk, rv, rg, s0_t, dec_b, gn_col)


def _attend(problems, tk, shifts=None):
    flat = [(pi, part, r0, min(tk, part[4])) for pi, (_, parts) in enumerate(problems) for part in parts
            for r0 in range(0, part[4], min(tk, part[4]))]

    def scores(item):
        pi, (k_ref, c0, _, _, _), r0, nk = item
        q_t = problems[pi][0]
        return jnp.dot(k_ref[r0:r0 + nk, c0:c0 + q_t.shape[0]], q_t, preferred_element_type=F32)

    state = [(jnp.full((1, q_t.shape[1]), -jnp.inf, F32), jnp.zeros((HEAD_DV, q_t.shape[1]), F32),
              jnp.zeros((1, q_t.shape[1]), F32)) for q_t, _ in problems]
    pending = [scores(it) for it in flat[:SCORE_LOOKAHEAD]]
    for i, (pi, (_, _, v_ref, v0, _), r0, nk) in enumerate(flat):
        if i + SCORE_LOOKAHEAD < len(flat):
            pending.append(scores(flat[i + SCORE_LOOKAHEAD]))
        s = pending.pop(0)
        m, acc, l = state[pi]
        if shifts is None:
            m_new = jnp.maximum(m, jnp.max(s, axis=0, keepdims=True))
            alpha = jnp.exp2(m - m_new)
            acc, l = acc * alpha, l * alpha
        else:
            m_new = shifts[pi]
        p = jnp.exp2(s - m_new)
        l = l + jnp.sum(p, axis=0, keepdims=True)
        pv = jnp.dot(v_ref[v0:v0 + HEAD_DV, r0:r0 + nk], p.astype(BF16), preferred_element_type=F32)
        state[pi] = (m_new, acc + pv, l)
    return [acc / l for _, acc, l in state]


def _attend_guarded(problems, key_sq_max, safe, tk):
    shifts = []
    for (q_t, _), ksq in zip(problems, key_sq_max):
        q32 = q_t.astype(F32)
        qsq = jnp.sum(q32 * q32, axis=0, keepdims=True)
        shifts.append(jnp.sqrt(qsq * ksq) * BOUND_SLACK)
    outs = lax.cond(safe,
                    lambda: tuple(_attend(problems, tk, shifts)),
                    lambda: tuple(_attend(problems, tk, None)))
    return list(outs)


def _row_bcast(a, row, tq):
    rid = lax.broadcasted_iota(jnp.int32, a.shape, 0)
    r = jnp.max(jnp.where(rid == row, a, 0.0), axis=0, keepdims=True)
    return jnp.concatenate([r] * (tq // LANE), axis=1)


def _diff_attn_kernel(*refs, has_ctx, rows_main, rows_ctx, tk, hps, lam_init):
    it = iter(refs)
    safe_ref, q_ref, k_ref, v_ref, ksq_ref = (next(it) for _ in range(5))
    if has_ctx:
        kc_ref, vc_ref = (next(it) for _ in range(2))
    dl_ref, g_ref, out_ref = (next(it) for _ in range(3))
    grp = pl.program_id(1)
    tq = q_ref.shape[1]
    segs_per_tile = LANE // DIFF_DK
    seg = lax.broadcasted_iota(jnp.int32, (LANE, tq), 0) // DIFF_DK
    problems, key_sq_max = [], []
    for j in range(hps):
        tile = (2 * j) // segs_per_tile
        q_tile = q_ref[tile * LANE:(tile + 1) * LANE, :]
        parts = [(k_ref, tile * LANE, v_ref, j * HEAD_DV, rows_main)]
        if has_ctx:
            parts.append((kc_ref, tile * LANE, vc_ref, j * HEAD_DV, rows_ctx))
        for mm in range(2):
            q_m = jnp.where(seg == (2 * j + mm) % segs_per_tile, q_tile, jnp.zeros_like(q_tile))
            problems.append((q_m, parts))
            key_sq_max.append(_row_bcast(ksq_ref[...], 2 * (grp * hps + j) + mm, tq))
    dl = dl_ref[...]
    lam = (jnp.exp(jnp.sum(dl[0:1] * dl[1:2], axis=1, keepdims=True))
           - jnp.exp(jnp.sum(dl[2:3] * dl[3:4], axis=1, keepdims=True)) + lam_init)
    outs = _attend_guarded(problems, key_sq_max, safe_ref[pl.program_id(0), grp] != 0, tk)
    for j in range(hps):
        o = outs[2 * j] - lam * outs[2 * j + 1]
        y = o * _rms_scale(o, DIFF_DV) * g_ref[...]
        out_ref[j * DIFF_DV:(j + 1) * DIFF_DV, :] = (y * (1.0 - lam_init)).astype(BF16)


def _safe_flags(q_sq, k_sq, per_group):
    bound = jnp.sqrt(q_sq * k_sq) * (BOUND_SLACK * BOUND_SLACK)
    ok = (2.0 * bound <= SAFE_EXP_RANGE).reshape(q_sq.shape[0], -1, per_group)
    return jnp.all(ok, axis=-1).astype(jnp.int32)


def _diff_attention(dq_t, dk_tok, dv_t, q_sq, k_sq, ctx, dl, g_col, *, batch, tq, tk, hps, lam_init):
    n_all = dq_t.shape[1]
    n = n_all // batch
    qt = n // tq
    nseg = 2 * DIFF_HEADS
    has_ctx = ctx is not None
    rows_ctx = ctx[0].shape[1] if has_ctx else 0
    assert (hps * 2 * DIFF_DK) % LANE == 0, "a grid step must cover whole 128-lane key tiles"
    qk_w = hps * 2 * DIFF_DK
    safe = _safe_flags(q_sq, k_sq, 2 * hps)
    k_sq_b = jnp.broadcast_to(k_sq[:, :, None], (batch, nseg, LANE))
    in_arrays = [safe, dq_t, dk_tok, dv_t, k_sq_b]
    in_specs = [pl.BlockSpec(memory_space=pltpu.SMEM),
                pl.BlockSpec((qk_w, tq), lambda b, g, i: (g, b * qt + i)),
                pl.BlockSpec((n, qk_w), lambda b, g, i: (b, g)),
                pl.BlockSpec((hps * HEAD_DV, n), lambda b, g, i: (g, b)),
                pl.BlockSpec((None, nseg, LANE), lambda b, g, i: (b, 0, 0))]
    if has_ctx:
        in_arrays += list(ctx)
        in_specs += [pl.BlockSpec((None, rows_ctx, qk_w), lambda b, g, i: (b, 0, g)),
                     pl.BlockSpec((None, hps * HEAD_DV, rows_ctx), lambda b, g, i: (b, g, 0))]
    in_arrays += [dl, g_col]
    in_specs += [_full(dl.shape), _full(g_col.shape)]
    return pl.pallas_call(
        functools.partial(_diff_attn_kernel, has_ctx=has_ctx, rows_main=n, rows_ctx=rows_ctx, tk=tk, hps=hps,
                          lam_init=lam_init),
        grid=(batch, DIFF_HEADS // hps, qt),
        in_specs=in_specs,
        out_specs=pl.BlockSpec((hps * DIFF_DV, tq), lambda b, g, i: (g, b * qt + i)),
        out_shape=jax.ShapeDtypeStruct((DIFF_W, n_all), BF16),
        compiler_params=_cparams(3),
        name="diff_attn_smp" if has_ctx else "diff_attn_ctx",
    )(*in_arrays)


def _mla_attn_kernel(*refs, has_ctx, rows_main, rows_ctx, tk, hps):
    it = iter(refs)
    safe_ref, q_ref, k_ref, v_ref, ksq_ref = (next(it) for _ in range(5))
    if has_ctx:
        kc_ref, vc_ref = (next(it) for _ in range(2))
    out_ref = next(it)
    grp = pl.program_id(1)
    tq = q_ref.shape[1]
    problems, key_sq_max = [], []
    for j in range(hps):
        parts = [(k_ref, j * MLA_PAD, v_ref, j * HEAD_DV, rows_main)]
        if has_ctx:
            parts.append((kc_ref, j * MLA_PAD, vc_ref, j * HEAD_DV, rows_ctx))
        problems.append((q_ref[j * MLA_PAD:(j + 1) * MLA_PAD, :], parts))
        key_sq_max.append(_row_bcast(ksq_ref[...], grp * hps + j, tq))
    outs = _attend_guarded(problems, key_sq_max, safe_ref[pl.program_id(0), grp] != 0, tk)
    for j, o in enumerate(outs):
        out_ref[j * MLA_DV:(j + 1) * MLA_DV, :] = o.astype(BF16)


def _mla_attention(mq_t, mk_tok, mv_t, q_sq, k_sq, ctx, *, batch, tq, tk, hps):
    n_all = mq_t.shape[1]
    n = n_all // batch
    qt = n // tq
    has_ctx = ctx is not None
    rows_ctx = ctx[0].shape[1] if has_ctx else 0
    safe = _safe_flags(q_sq, k_sq, hps)
    k_sq_b = jnp.broadcast_to(k_sq[:, :, None], (batch, MLA_HEADS, LANE))
    in_arrays = [safe, mq_t, mk_tok, mv_t, k_sq_b]
    in_specs = [pl.BlockSpec(memory_space=pltpu.SMEM),
                pl.BlockSpec((hps * MLA_PAD, tq), lambda b, g, i: (g, b * qt + i)),
                pl.BlockSpec((n, hps * MLA_PAD), lambda b, g, i: (b, g)),
                pl.BlockSpec((hps * HEAD_DV, n), lambda b, g, i: (g, b)),
                pl.BlockSpec((None, MLA_HEADS, LANE), lambda b, g, i: (b, 0, 0))]
    if has_ctx:
        in_arrays += list(ctx)
        in_specs += [pl.BlockSpec((None, rows_ctx, hps * MLA_PAD), lambda b, g, i: (b, 0, g)),
                     pl.BlockSpec((None, hps * HEAD_DV, rows_ctx), lambda b, g, i: (b, g, 0))]
    return pl.pallas_call(
        functools.partial(_mla_attn_kernel, has_ctx=has_ctx, rows_main=n, rows_ctx=rows_ctx, tk=tk, hps=hps),
        grid=(batch, MLA_HEADS // hps, qt),
        in_specs=in_specs,
        out_specs=pl.BlockSpec((hps * MLA_DV, tq), lambda b, g, i: (g, b * qt + i)),
        out_shape=jax.ShapeDtypeStruct((MLA_W, n_all), BF16),
        compiler_params=_cparams(3),
        name="mla_attn_smp" if has_ctx else "mla_attn_ctx",
    )(*in_arrays)


def _post_kernel(x_ref, ret_ref, diff_ref, mla_ref, g1_ref, sh2_ref, sc2_ref, g2_ref, n2_ref,
                 w_out_ref, w_gu_ref, w_down_ref, out_ref, *, tm, d_ff, ff_cuts):
    nch = tm // LANE
    ret = jnp.concatenate([ret_ref[j] for j in range(nch)], axis=1)
    mixed = jnp.concatenate([ret, diff_ref[...], mla_ref[...]], axis=0)
    d = x_ref.shape[0]
    halves = [slice(i * (tm // 2), (i + 1) * (tm // 2)) for i in range(2)]
    attn = [jnp.dot(w_out_ref[...], mixed[:, c], preferred_element_type=F32) for c in halves]
    x1 = [x_ref[:, c] + g1_ref[...] * a for c, a in zip(halves, attn)]
    gain2 = n2_ref[...] * (1.0 + sc2_ref[...])
    h = [(v * _rms_scale(v, d) * gain2 + sh2_ref[...]).astype(BF16) for v in x1]
    ffn = [None, None]
    for lo, hi in zip((0,) + ff_cuts, ff_cuts + (d_ff,)):
        for i in range(2):
            g = jnp.dot(w_gu_ref[lo:hi, :], h[i], preferred_element_type=F32)
            u = jnp.dot(w_gu_ref[d_ff + lo:d_ff + hi, :], h[i], preferred_element_type=F32)
            a = (_silu(g) * u).astype(BF16)
            part = jnp.dot(w_down_ref[:, lo:hi], a, preferred_element_type=F32)
            ffn[i] = part if ffn[i] is None else ffn[i] + part
    for i, c in enumerate(halves):
        out_ref[:, c] = x1[i] + g2_ref[...] * ffn[i]


def _post(x_t, ret, diff_t, mla_t, mods, lw, *, is_ctx, tm, tiles_per_batch):
    d, n = x_t.shape
    nt = n // tm
    nch = tm // LANE
    d_ff = lw["w_down"].shape[1]
    ff_cuts = (d_ff // 2,)
    if is_ctx:
        bidx = lambda j: (0, 0, 0)
    else:
        bidx = lambda j: (j // tiles_per_batch, 0, 0)
    col = pl.BlockSpec((None, d, 1), bidx)
    const = lambda a: pl.BlockSpec(a.shape, lambda j: (0,) * a.ndim, pipeline_mode=pl.Buffered(1))
    g1, sh2, sc2, g2 = mods
    return pl.pallas_call(
        functools.partial(_post_kernel, tm=tm, d_ff=d_ff, ff_cuts=ff_cuts),
        grid=(nt,),
        in_specs=[pl.BlockSpec((d, tm), lambda j: (0, j)),
                  pl.BlockSpec((nch, RET_W, LANE), lambda j: (j, 0, 0)),
                  pl.BlockSpec((DIFF_W, tm), lambda j: (0, j)),
                  pl.BlockSpec((MLA_W, tm), lambda j: (0, j)),
                  col, col, col, col, const(lw["n2"]),
                  const(lw["w_out"]), const(lw["w_gu"]), const(lw["w_down"])],
        out_specs=pl.BlockSpec((d, tm), lambda j: (0, j)),
        out_shape=jax.ShapeDtypeStruct((d, n), F32),
        compiler_params=_cparams(1),
        name="post_ctx" if is_ctx else "post_smp",
    )(x_t, ret, diff_t, mla_t, g1, sh2, sc2, g2, lw["n2"], lw["w_out"], lw["w_gu"], lw["w_down"])


def _rope_tables_t(n, rot_dim):
    rows = n // GRID_W
    row = jnp.repeat(jnp.arange(rows, dtype=F32), GRID_W)
    col = jnp.tile(jnp.arange(GRID_W, dtype=F32), rows)
    n_freq = rot_dim // 4
    inv = 1.0 / (ROPE_BASE ** (jnp.arange(n_freq, dtype=F32) / n_freq))
    ang = jnp.concatenate([inv[:, None] * row[None, :], inv[:, None] * col[None, :]], axis=0)
    return jnp.cos(ang), jnp.sin(ang)


def _col(v):
    return v.astype(F32).reshape(-1, 1)


def _pad_heads_rows(w_t, used, pad):
    hk = w_t.shape[0] // used
    w3 = w_t.reshape(hk, used, w_t.shape[1])
    w3 = jnp.pad(w3, ((0, 0), (0, pad - used), (0, 0)))
    return w3.reshape(hk * pad, w_t.shape[1])


def _layer_weights(l, w_in, norm1, norm2, diff_qk_gain, mla_q_norm, mla_kv_norm, w_uq, w_ukv, mla_qk_gain,
                   w_out, w_gu, w_down):
    w_ukv_t = w_ukv[l].T.reshape(MLA_HEADS, MLA_NOPE + MLA_DV, MLA_KV_RANK)
    w_uk = jnp.pad(w_ukv_t[:, :MLA_NOPE], ((0, 0), (0, MLA_PAD - MLA_NOPE), (0, 0)))
    g_pad = lambda g: jnp.tile(jnp.pad(g.astype(F32), (0, MLA_PAD - MLA_DQK)), MLA_HEADS).reshape(-1, 1)
    return {
        "n1": _col(norm1[l]), "n2": _col(norm2[l]),
        "w_in": jnp.concatenate([w_in[l][:, W_IN_MLA0:], w_in[l][:, :W_IN_MLA0]], axis=1).T.astype(BF16),
        "g_dq": _col(jnp.tile(diff_qk_gain[l, 0], 2 * DIFF_HEADS)),
        "g_dk": _col(jnp.tile(diff_qk_gain[l, 1], 2 * DIFF_HEADS)),
        "g_qn": _col(mla_q_norm[l]), "g_kvn": _col(mla_kv_norm[l]),
        "w_uq": _pad_heads_rows(w_uq[l].T, MLA_DQK, MLA_PAD).astype(BF16),
        "g_mq": g_pad(mla_qk_gain[l, 0]), "g_mk": g_pad(mla_qk_gain[l, 1]),
        "w_uk": w_uk.reshape(MLA_QK_PAD_W, MLA_KV_RANK).astype(BF16),
        "w_uv": w_ukv_t[:, MLA_NOPE:].reshape(MLA_W, MLA_KV_RANK).astype(BF16),
        "w_out": w_out[l].T.astype(BF16), "w_gu": w_gu[l].T.astype(BF16), "w_down": w_down[l].T.astype(BF16),
    }


def kernel(x_prompt, x_sample, c, state_ret, cache_diff_k, cache_diff_v, cache_mla_ckv, cache_mla_kr, c_ctx,
           w_mod, b_mod, norm1, norm2, w_in, ret_decay, ret_gn_gain, diff_qk_gain, diff_lambda, diff_subln_gain,
           mla_q_norm, mla_kv_norm, w_uq, w_ukv, mla_qk_gain, w_out, w_gu, w_down):
    depth = w_in.shape[0]
    bp, sp, d = x_prompt.shape
    bs, ss, _ = x_sample.shape
    past = cache_diff_k.shape[2]

    tm = 512
    tq_s, tk_s = 512, 256
    tq_p = tk_p = sp

    n_cond = 1 + bs
    r_pad = -(-n_cond // 16) * 16
    cond = jnp.concatenate([c_ctx[None, :], c, jnp.zeros((r_pad - n_cond, d), F32)], axis=0)
    mod = _modulation(cond, w_mod, b_mod)
    mod = mod.reshape(depth, r_pad, 6, d, 1)

    xp = x_prompt.reshape(bp * sp, d).T
    xs = x_sample.reshape(bs * ss, d).T
    rope = _rope_tables_t(ss, RET_DK) + _rope_tables_t(ss, DIFF_DK)
    dec_b = jnp.broadcast_to(ret_decay.astype(F32).reshape(depth, 2 * RET_HEADS, 1, 1),
                             (depth, 2 * RET_HEADS, 8, LANE))
    s0_zero = jnp.zeros((bp, 2 * RET_HEADS, RET_DV, RET_DK), F32)

    new_ret, new_dk, new_dv, new_ckv, new_kr = [], [], [], [], []
    for l in range(depth):
        lw = _layer_weights(l, w_in, norm1, norm2, diff_qk_gain, mla_q_norm, mla_kv_norm, w_uq, w_ukv,
                            mla_qk_gain, w_out, w_gu, w_down)
        lam_init = 0.8 - 0.6 * math.exp(-0.3 * l)
        gn_col = _col(ret_gn_gain[l])
        subln_col = _col(diff_subln_gain[l])
        dl = diff_lambda[l].astype(F32)

        mc = [mod[l, 0:1, i] for i in range(6)]
        def norm_bound(gain, n, out_scale, batch):
            return jnp.full((batch, 8), n * out_scale * out_scale, F32) * jnp.max(jnp.square(gain.astype(F32)))

        dq_sq = functools.partial(norm_bound, diff_qk_gain[l, 0], DIFF_DK, DIFF_DK ** -0.5 * LOG2E)
        dk_sq = functools.partial(norm_bound, diff_qk_gain[l, 1], DIFF_DK, 1.0)
        mq_sq = functools.partial(norm_bound, mla_qk_gain[l, 0], MLA_DQK, MLA_DQK ** -0.5 * LOG2E)
        mk_sq = functools.partial(norm_bound, mla_qk_gain[l, 1], MLA_DQK, 1.0)

        (rq, rk, rv, rg, dq_t, dk_tok, dv_t, mq_t, mk_tok, mv_t, dk_f, dv_f, ckv_f, kr_f) = _premix(
            xp, mc[0], mc[1], lw, None, is_ctx=True, tm=tm, seq=sp)
        ret, s_fin = _retention(rq, rk, rv, rg, s0_zero, dec_b[l], gn_col, batch=bp)
        diff_t = _diff_attention(dq_t, dk_tok, dv_t, dq_sq(bp), dk_sq(bp), None, dl, subln_col, batch=bp, tq=tq_p,
                                 tk=tk_p, hps=DIFF_HEADS, lam_init=lam_init)
        mla_t = _mla_attention(mq_t, mk_tok, mv_t, mq_sq(bp), mk_sq(bp), None, batch=bp, tq=tq_p, tk=tk_p,
                               hps=MLA_HEADS)
        xp = _post(xp, ret, diff_t, mla_t, (mc[2], mc[3], mc[4], mc[5]), lw, is_ctx=True, tm=tm,
                   tiles_per_batch=1)
        new_ret.append(jnp.swapaxes(s_fin.reshape(bp, 2, RET_HEADS, RET_DV, RET_DK), -1, -2))
        new_dk.append(dk_f.reshape(bp, sp, DIFF_HEADS, 2, DIFF_DK))
        new_dv.append(dv_f.reshape(bp, sp, DIFF_HEADS, DIFF_DV))
        new_ckv.append(ckv_f.reshape(bp, sp, MLA_KV_RANK))
        new_kr.append(kr_f[:, :MLA_ROPE].reshape(bp, sp, MLA_ROPE))

        ms = [mod[l, 1:1 + bs, i] for i in range(6)]
        s0_t = jnp.swapaxes(state_ret[:, l].astype(F32), -1, -2).reshape(bs, 2 * RET_HEADS, RET_DV, RET_DK)
        ctx_dk_f = cache_diff_k[:, l].reshape(bs, past, DIFF_QK_W).astype(F32)
        ctx_dk = ctx_dk_f.astype(BF16)
        ctx_dv_t = jnp.swapaxes(cache_diff_v[:, l].reshape(bs, past, DIFF_W), 1, 2).astype(BF16)
        ctx_mk, ctx_mv_t, ctx_sq = _ctx_mla_kv(
            jnp.swapaxes(cache_mla_ckv[:, l].astype(F32), 1, 2), jnp.swapaxes(cache_mla_kr[:, l].astype(F32), 1, 2),
            jnp.swapaxes(ctx_dk_f, 1, 2), lw["w_uk"], lw["w_uv"], lw["g_mk"])
        ctx_sq = jnp.max(ctx_sq, axis=-1)
        (rq, rk, rv, rg, dq_t, dk_tok, dv_t, mq_t, mk_tok, mv_t) = _premix(
            xs, ms[0], ms[1], lw, rope, is_ctx=False, tm=tm, seq=ss)
        ret, _ = _retention(rq, rk, rv, rg, s0_t, dec_b[l], gn_col, batch=bs)
        diff_t = _diff_attention(dq_t, dk_tok, dv_t, dq_sq(bs), jnp.maximum(dk_sq(bs), ctx_sq[:, 1]),
                                 (ctx_dk, ctx_dv_t), dl, subln_col, batch=bs, tq=tq_s, tk=tk_s, hps=2,
                                 lam_init=lam_init)
        mla_t = _mla_attention(mq_t, mk_tok, mv_t, mq_sq(bs), jnp.maximum(mk_sq(bs), ctx_sq[:, 0]),
                               (ctx_mk, ctx_mv_t), batch=bs, tq=tq_s, tk=tk_s, hps=4)
        xs = _post(xs, ret, diff_t, mla_t, (ms[2], ms[3], ms[4], ms[5]), lw, is_ctx=False, tm=tm,
                   tiles_per_batch=ss // tm)

    y_p = xp.T.reshape(bp, sp, d)
    y_s = xs.T.reshape(bs, ss, d)
    return (y_p, y_s, jnp.stack(new_ret, axis=1), jnp.stack(new_dk, axis=1), jnp.stack(new_dv, axis=1),
            jnp.stack(new_ckv, axis=1), jnp.stack(new_kr, axis=1))
```

```python
import functools
import math

import jax
import jax.numpy as jnp
from jax import lax
from jax.experimental import pallas as pl
from jax.experimental.pallas import tpu as pltpu

F32 = jnp.float32
BF16 = jnp.bfloat16
EPS = 1e-6

GRID_W = 64
RET_HEADS = 4
RET_DK = 64
RET_DV = 64
RET_CHUNK = 128
DIFF_HEADS = 4
DIFF_DK = 32
DIFF_DV = 64
MLA_HEADS = 8
MLA_NOPE = 64
MLA_ROPE = 32
MLA_DQK = MLA_NOPE + MLA_ROPE
MLA_DV = 64
MLA_Q_RANK = 768
MLA_KV_RANK = 256
ROPE_BASE = 10000.0

RET_W = RET_HEADS * RET_DV
DIFF_W = DIFF_HEADS * DIFF_DV
MLA_W = MLA_HEADS * MLA_DV
DIFF_QK_W = DIFF_HEADS * 2 * DIFF_DK
MLA_PAD = 128
MLA_QK_PAD_W = MLA_HEADS * MLA_PAD
assert DIFF_DV == MLA_DV
HEAD_DV = MLA_DV

_SPLITS = (RET_HEADS * RET_DK, RET_HEADS * RET_DK, RET_W, RET_W,
           DIFF_QK_W, DIFF_QK_W, DIFF_W, MLA_Q_RANK, MLA_KV_RANK, MLA_ROPE)
W_IN_MLA0 = int(sum(_SPLITS[:7]))
_PERM_SPLITS = _SPLITS[7:] + _SPLITS[:7]
_OFFS = tuple(int(sum(_PERM_SPLITS[:i])) for i in range(len(_PERM_SPLITS) + 1))
O_CQ, O_CKV, O_KR, O_RQ, O_RK, O_RV, O_RG, O_DQ, O_DK, O_DV, O_END = _OFFS

LOG2E = math.log2(math.e)
SCORE_LOOKAHEAD = 2
SAFE_EXP_RANGE = 100.0
BOUND_SLACK = 1.0 + 2.0 ** -6
LANE = 128
MXU_DEPTH = 256
PREMIX_GROUPS = 2
VMEM_LIMIT = 56 * 1024 * 1024


def _cparams(n_grid):
    return pltpu.CompilerParams(dimension_semantics=("arbitrary",) * n_grid,
                                vmem_limit_bytes=VMEM_LIMIT)


def _full(shape):
    nd = len(shape)
    return pl.BlockSpec(shape, lambda *_: (0,) * nd)


def _silu(x):
    return x / (1.0 + jnp.exp(-x))


def _rms_scale(x, n):
    return lax.rsqrt(jnp.sum(x * x, axis=0, keepdims=True) * (1.0 / n) + EPS)


def _rope_pair(x1, x2, c, s):
    return x1 * c - x2 * s, x2 * c + x1 * s


def _mod_kernel(c_ref, w_ref, b_ref, o_ref):
    a = _silu(c_ref[...]).astype(BF16)
    o_ref[...] = jnp.dot(a, w_ref[...].astype(BF16), preferred_element_type=F32) + b_ref[...]


def _modulation(cond, w_mod, b_mod):
    depth, d, d6 = w_mod.shape
    r = cond.shape[0]
    tn = 1536
    return pl.pallas_call(
        _mod_kernel,
        grid=(depth, d6 // tn),
        in_specs=[pl.BlockSpec((r, d), lambda l, j: (0, 0)),
                  pl.BlockSpec((None, d, tn), lambda l, j: (l, 0, j)),
                  pl.BlockSpec((None, 1, tn), lambda l, j: (l, 0, j))],
        out_specs=pl.BlockSpec((None, r, tn), lambda l, j: (l, 0, j)),
        out_shape=jax.ShapeDtypeStruct((depth, r, d6), F32),
        compiler_params=_cparams(2),
        name="adaln_mod",
    )(cond, w_mod, b_mod.reshape(depth, 1, d6))


def _mla_kv_matmuls(ckvn_bf, w_uk_ref, w_uv_ref):
    return (jnp.dot(w_uk_ref[...], ckvn_bf, preferred_element_type=F32),
            jnp.dot(w_uv_ref[...], ckvn_bf, preferred_element_type=F32))


def _mla_keys(kn, kr, g_mk_ref, rope):
    t = kr.shape[1]
    kr_ss = jnp.sum(kr * kr, axis=0, keepdims=True)
    zpad = jnp.zeros((MLA_PAD - MLA_DQK, t), F32)
    half = MLA_ROPE // 2
    heads = []
    for hd in range(MLA_HEADS):
        r0 = hd * MLA_PAD
        kh = kn[r0:r0 + MLA_NOPE]
        ss = jnp.sum(kh * kh, axis=0, keepdims=True) + kr_ss
        r = lax.rsqrt(ss * (1.0 / MLA_DQK) + EPS)
        g = g_mk_ref[r0:r0 + MLA_PAD, :]
        y_nope = kh * r * g[:MLA_NOPE]
        y_r = kr * r * g[MLA_NOPE:MLA_DQK]
        if rope is not None:
            c, s = rope
            y1, y2 = _rope_pair(y_r[:half], y_r[half:], c, s)
            heads += [y_nope, y1, y2, zpad]
        else:
            heads += [y_nope, y_r, zpad]
    return jnp.concatenate(heads, axis=0)


def _group_sq_norms(a, rows_per_group):
    groups = a.shape[0] // rows_per_group
    return jnp.concatenate([jnp.sum(jnp.square(a[g * rows_per_group:(g + 1) * rows_per_group]), axis=0, keepdims=True)
                            for g in range(groups)], axis=0)


def _lane_tile_max(sq):
    return functools.reduce(jnp.maximum, [sq[:, t * LANE:(t + 1) * LANE] for t in range(sq.shape[1] // LANE)])


def _ctx_kv_kernel(ckv_ref, kr_ref, dk_ref, w_uk_ref, w_uv_ref, g_mk_ref, mk_ref, mv_ref, sq_ref):
    kn, v = _mla_kv_matmuls(ckv_ref[...].astype(BF16), w_uk_ref, w_uv_ref)
    k = _mla_keys(kn, kr_ref[...], g_mk_ref, None)
    mk_ref[...] = k.T.astype(BF16)
    mv_ref[...] = v.astype(BF16)
    sq_ref[0] = _lane_tile_max(_group_sq_norms(k, MLA_PAD))
    sq_ref[1] = _lane_tile_max(_group_sq_norms(dk_ref[...], DIFF_DK))


def _ctx_mla_kv(ckv_t, kr_t, dk_t, w_uk, w_uv, g_mk):
    b, _, l = ckv_t.shape
    return pl.pallas_call(
        _ctx_kv_kernel,
        grid=(b,),
        in_specs=[pl.BlockSpec((None, MLA_KV_RANK, l), lambda i: (i, 0, 0)),
                  pl.BlockSpec((None, MLA_ROPE, l), lambda i: (i, 0, 0)),
                  pl.BlockSpec((None, DIFF_QK_W, l), lambda i: (i, 0, 0)),
                  _full(w_uk.shape), _full(w_uv.shape), _full(g_mk.shape)],
        out_specs=[pl.BlockSpec((None, l, MLA_QK_PAD_W), lambda i: (i, 0, 0)),
                   pl.BlockSpec((None, MLA_W, l), lambda i: (i, 0, 0)),
                   pl.BlockSpec((None, 2, 8, LANE), lambda i: (i, 0, 0, 0))],
        out_shape=[jax.ShapeDtypeStruct((b, l, MLA_QK_PAD_W), BF16),
                   jax.ShapeDtypeStruct((b, MLA_W, l), BF16),
                   jax.ShapeDtypeStruct((b, 2, 8, LANE), F32)],
        compiler_params=_cparams(1),
        name="ctx_mla_kv",
    )(ckv_t, kr_t, dk_t, w_uk, w_uv, g_mk)


def _premix_kernel(*refs, is_ctx, tm):
    it = iter(refs)
    x_ref, shift_ref, scale_ref, n1_ref, w_in_ref = (next(it) for _ in range(5))
    g_dq_ref, g_dk_ref, g_qn_ref, g_kvn_ref = (next(it) for _ in range(4))
    w_uq_ref, g_mq_ref, w_uk_ref, w_uv_ref, g_mk_ref = (next(it) for _ in range(5))
    if not is_ctx:
        cr_ref, sr_ref, cs_ref, ss_ref = (next(it) for _ in range(4))
    rq_ref, rk_ref, rv_ref, rg_ref = (next(it) for _ in range(4))
    dq_ref, dk_ref, dv_ref, mq_ref, mk_ref, mv_ref = (next(it) for _ in range(6))
    if is_ctx:
        dkf_ref, dvf_ref, ckvf_ref, krf_ref = (next(it) for _ in range(4))
    d = x_ref.shape[0]

    def token_group(c0, w):
        cols = slice(c0, c0 + w)
        if is_ctx:
            rope_r = rope_s = None
        else:
            rope_r = (cr_ref[:, cols], sr_ref[:, cols])
            rope_s = (cs_ref[:, cols], ss_ref[:, cols])
        x = x_ref[:, cols]
        h = (x * _rms_scale(x, d) * (n1_ref[...] * (1.0 + scale_ref[...])) + shift_ref[...]).astype(BF16)
        yield

        def put_chunks(ref, val):
            for j in range(w // LANE):
                ref[c0 // LANE + j] = val[:, j * LANE:(j + 1) * LANE].astype(ref.dtype)

        def proj(lo, hi):
            return jnp.dot(w_in_ref[lo:hi, :], h, preferred_element_type=F32)

        p_mla = proj(O_CQ, O_RQ)
        cq, ckv, kr = p_mla[O_CQ:O_CKV], p_mla[O_CKV:O_KR], p_mla[O_KR:O_RQ]
        p_ret = proj(O_RQ, O_DQ)
        rq, rk, rv, rg = (p_ret[o - O_RQ:o - O_RQ + RET_W] for o in (O_RQ, O_RK, O_RV, O_RG))
        yield
        cqn = (cq * _rms_scale(cq, MLA_Q_RANK) * g_qn_ref[...]).astype(BF16)
        ckvn = ckv * _rms_scale(ckv, MLA_KV_RANK) * g_kvn_ref[...]
        mq = jnp.dot(w_uq_ref[...], cqn, preferred_element_type=F32)
        kn, mv = _mla_kv_matmuls(ckvn.astype(BF16), w_uk_ref, w_uv_ref)
        p_diff = proj(O_DQ, O_END)
        dq_raw, dk_raw, dv = (p_diff[o - O_DQ:o - O_DQ + DIFF_QK_W] for o in (O_DQ, O_DK, O_DV))
        yield

        if rope_r is not None:
            c, s = rope_r
            hk = RET_DK // 2

            def rope_heads(a):
                parts = []
                for hd in range(RET_HEADS):
                    r0 = hd * RET_DK
                    parts += list(_rope_pair(a[r0:r0 + hk], a[r0 + hk:r0 + RET_DK], c, s))
                return jnp.concatenate(parts, axis=0)

            rq = rope_heads(rq)
            rk = rope_heads(rk)
        put_chunks(rq_ref, rq)
        put_chunks(rk_ref, rk * (RET_DK ** -0.5))
        put_chunks(rv_ref, rv)
        put_chunks(rg_ref, rg)

        def diff_qk(a, g_ref, out_scale):
            parts = []
            hs = DIFF_DK // 2
            for seg in range(2 * DIFF_HEADS):
                r0 = seg * DIFF_DK
                xs = a[r0:r0 + DIFF_DK]
                y = xs * _rms_scale(xs, DIFF_DK) * g_ref[r0:r0 + DIFF_DK, :]
                if out_scale != 1.0:
                    y = y * out_scale
                if rope_s is not None:
                    parts += list(_rope_pair(y[:hs], y[hs:], rope_s[0], rope_s[1]))
                else:
                    parts.append(y)
            return jnp.concatenate(parts, axis=0)

        dq = diff_qk(dq_raw, g_dq_ref, DIFF_DK ** -0.5 * LOG2E)
        dq_ref[:, cols] = dq.astype(BF16)
        dk = diff_qk(dk_raw, g_dk_ref, 1.0)
        dk_t = dk.T
        dk_ref[cols, :] = dk_t.astype(BF16)
        dv_ref[:, cols] = dv.astype(BF16)
        if is_ctx:
            dkf_ref[cols, :] = dk_t
            dvf_ref[cols, :] = dv.T

        half = MLA_ROPE // 2
        parts = []
        for hd in range(MLA_HEADS):
            r0 = hd * MLA_PAD
            xs = mq[r0:r0 + MLA_PAD]
            y = xs * _rms_scale(xs, MLA_DQK) * (g_mq_ref[r0:r0 + MLA_PAD, :] * (MLA_DQK ** -0.5 * LOG2E))
            if rope_s is not None:
                y1, y2 = _rope_pair(y[MLA_NOPE:MLA_NOPE + half], y[MLA_NOPE + half:MLA_DQK], rope_s[0], rope_s[1])
                parts += [y[:MLA_NOPE], y1, y2, y[MLA_DQK:]]
            else:
                parts.append(y)
        mq_ref[:, cols] = jnp.concatenate(parts, axis=0).astype(BF16)

        mk = _mla_keys(kn, kr, g_mk_ref, rope_s)
        mk_ref[cols, :] = mk.T.astype(BF16)
        mv_ref[:, cols] = mv.astype(BF16)
        if is_ctx:
            ckvf_ref[cols, :] = ckvn.T
            krf_ref[cols, :] = jnp.concatenate([kr, jnp.zeros((LANE - MLA_ROPE, w), F32)], axis=0).T
        yield

    groups = [token_group(i * (tm // PREMIX_GROUPS), tm // PREMIX_GROUPS) for i in range(PREMIX_GROUPS)]
    for _ in range(4):
        for g in groups:
            next(g)


def _premix(x_t, shift, scale, lw, rope, *, is_ctx, tm, seq):
    d, n = x_t.shape
    nt = n // tm
    nch = tm // LANE
    tiles_per_batch = max(1, seq // tm)
    if is_ctx:
        bidx = lambda j: (0, 0, 0)
    else:
        bidx = lambda j: (j // tiles_per_batch, 0, 0)
    in_arrays = [x_t, shift, scale, lw["n1"], lw["w_in"], lw["g_dq"], lw["g_dk"], lw["g_qn"], lw["g_kvn"],
                 lw["w_uq"], lw["g_mq"], lw["w_uk"], lw["w_uv"], lw["g_mk"]]
    in_specs = [pl.BlockSpec((d, tm), lambda j: (0, j)),
                pl.BlockSpec((None, d, 1), bidx), pl.BlockSpec((None, d, 1), bidx)]
    in_specs += [_full(a.shape) for a in in_arrays[3:]]
    if not is_ctx:
        for tab in rope:
            in_arrays.append(tab)
            in_specs.append(pl.BlockSpec((tab.shape[0], tm), lambda j: (0, j % tiles_per_batch)))

    chunk_spec = pl.BlockSpec((nch, RET_W, LANE), lambda j: (j, 0, 0))
    fm = lambda rows: pl.BlockSpec((rows, tm), lambda j: (0, j))
    tok = lambda cols: pl.BlockSpec((tm, cols), lambda j: (j, 0))
    nck = n // LANE
    out_specs = [chunk_spec] * 4 + [fm(DIFF_QK_W), tok(DIFF_QK_W), fm(DIFF_W),
                                    fm(MLA_QK_PAD_W), tok(MLA_QK_PAD_W), fm(MLA_W)]
    out_shape = [jax.ShapeDtypeStruct((nck, RET_W, LANE), BF16), jax.ShapeDtypeStruct((nck, RET_W, LANE), F32),
                 jax.ShapeDtypeStruct((nck, RET_W, LANE), BF16), jax.ShapeDtypeStruct((nck, RET_W, LANE), F32),
                 jax.ShapeDtypeStruct((DIFF_QK_W, n), BF16), jax.ShapeDtypeStruct((n, DIFF_QK_W), BF16),
                 jax.ShapeDtypeStruct((DIFF_W, n), BF16),
                 jax.ShapeDtypeStruct((MLA_QK_PAD_W, n), BF16), jax.ShapeDtypeStruct((n, MLA_QK_PAD_W), BF16),
                 jax.ShapeDtypeStruct((MLA_W, n), BF16)]
    if is_ctx:
        out_specs += [tok(DIFF_QK_W), tok(DIFF_W), tok(MLA_KV_RANK), tok(LANE)]
        out_shape += [jax.ShapeDtypeStruct((n, DIFF_QK_W), F32), jax.ShapeDtypeStruct((n, DIFF_W), F32),
                      jax.ShapeDtypeStruct((n, MLA_KV_RANK), F32), jax.ShapeDtypeStruct((n, LANE), F32)]
    return pl.pallas_call(
        functools.partial(_premix_kernel, is_ctx=is_ctx, tm=tm),
        grid=(nt,),
        in_specs=in_specs, out_specs=out_specs, out_shape=out_shape,
        compiler_params=_cparams(1),
        name="premix_ctx" if is_ctx else "premix_smp",
    )(*in_arrays)


def _retention_kernel(rq_ref, rk_ref, rv_ref, rg_ref, s0_ref, dec_ref, gn_ref, out_ref, sfin_ref,
                      o_acc, s_acc, intra_ref, vec_ref, *, nc):
    C = RET_CHUNK
    n_idx = lax.broadcasted_iota(jnp.int32, (C, C), 1).astype(F32)
    m_idx = lax.broadcasted_iota(jnp.int32, (C, C), 0).astype(F32)
    lane = lax.broadcasted_iota(jnp.int32, (8, C), 1).astype(F32)
    for d in range(2):
        for hd in range(RET_HEADS):
            i = d * RET_HEADS + hd
            z = dec_ref[i]
            lg8 = jnp.minimum(z, 0.0) - jnp.log1p(jnp.exp(-jnp.abs(z)))
            lg = jnp.broadcast_to(lg8[0:1, :], (C, C))
            dist = (n_idx - m_idx) if d == 0 else (m_idx - n_idx)
            ok = dist >= 0.0
            intra_ref[i] = jnp.where(ok, jnp.exp(lg * jnp.where(ok, dist, 0.0)), 0.0)
            if d == 0:
                qdec = jnp.exp(lg8 * (lane + 1.0))
                kdec = jnp.exp(lg8 * (C - 1.0 - lane))
            else:
                qdec = jnp.exp(lg8 * (C - lane))
                kdec = jnp.exp(lg8 * lane)
            vec_ref[i, 0] = qdec
            vec_ref[i, 1] = kdec
            vec_ref[i, 2] = jnp.exp(lg8 * float(C))
    s_acc[...] = s0_ref[...]
    o_acc[...] = jnp.zeros_like(o_acc)

    row_head = lax.broadcasted_iota(jnp.int32, (RET_HEADS * RET_DK, C), 0) // RET_DK

    def step(c, carry):
        first, vs = [], []
        for d in range(2):
            cc = c if d == 0 else nc - 1 - c
            q_all = rq_ref[cc]
            k_all = rk_ref[cc]
            k_tok = k_all.T.astype(BF16)
            q_bd = jnp.concatenate([jnp.where(row_head == hd, q_all, jnp.zeros_like(q_all))
                                    for hd in range(RET_HEADS)], axis=1)
            att_all = jnp.dot(k_tok, q_bd, preferred_element_type=F32)
            for hd in range(RET_HEADS):
                i = d * RET_HEADS + hd
                r0 = hd * RET_DK
                q = q_all[r0:r0 + RET_DK]
                k = k_all[r0:r0 + RET_DK]
                v = rv_ref[cc, r0:r0 + RET_DV, :]
                att_t = att_all[:, hd * C:(hd + 1) * C]
                s_t = s_acc[i]
                cross = jnp.dot(s_t.astype(BF16), q, preferred_element_type=F32)
                kd = (k * vec_ref[i, 1][0:1, :]).astype(BF16)
                kv = lax.dot_general(v, kd, (((1,), (1,)), ((), ())), preferred_element_type=F32)
                s_acc[i] = s_t * vec_ref[i, 2][0:1, 0:RET_DK] + kv
                first.append((att_t, cross))
                vs.append((i, cc, r0, v))
        masked = [(att_t * intra_ref[i]).astype(BF16) for (att_t, _), (i, _, _, _) in zip(first, vs)]
        for (_, cross), att_m, (i, cc, r0, v) in zip(first, masked, vs):
            o = jnp.dot(v, att_m, preferred_element_type=F32) + cross * vec_ref[i, 0][0:1, :]
            o_acc[cc, r0:r0 + RET_DV, :] += o
        return carry

    lax.fori_loop(0, nc, step, 0, unroll=2)
    sfin_ref[...] = s_acc[...]

    def finish(c, carry):
        o = o_acc[c]
        g = rg_ref[c]
        parts = []
        for hd in range(RET_HEADS):
            r0 = hd * RET_DV
            oh = o[r0:r0 + RET_DV]
            mu = jnp.mean(oh, axis=0, keepdims=True)
            var = jnp.mean(jnp.square(oh - mu), axis=0, keepdims=True)
            y = (oh - mu) * lax.rsqrt(var + EPS) * gn_ref[r0:r0 + RET_DV, :]
            parts.append(y * _silu(g[r0:r0 + RET_DV]))
        out_ref[c] = jnp.concatenate(parts, axis=0).astype(BF16)
        return carry

    lax.fori_loop(0, nc, finish, 0)


def _retention(rq, rk, rv, rg, s0_t, dec_b, gn_col, *, batch):
    nck = rq.shape[0]
    nc = nck // batch
    blk = pl.BlockSpec((nc, RET_W, LANE), lambda b: (b, 0, 0))
    nst = 2 * RET_HEADS
    st_spec = pl.BlockSpec((None, nst, RET_DV, RET_DK), lambda b: (b, 0, 0, 0))
    return pl.pallas_call(
        functools.partial(_retention_kernel, nc=nc),
        grid=(batch,),
        in_specs=[blk, blk, blk, blk, st_spec, _full(dec_b.shape), _full(gn_col.shape)],
        out_specs=[blk, st_spec],
        out_shape=[jax.ShapeDtypeStruct((nck, RET_W, LANE), BF16),
                   jax.ShapeDtypeStruct((batch, nst, RET_DV, RET_DK), F32)],
        scratch_shapes=[pltpu.VMEM((nc, RET_W, LANE), F32),
                        pltpu.VMEM((nst, RET_DV, RET_DK), F32),
                        pltpu.VMEM((nst, RET_CHUNK, RET_CHUNK), F32),
                        pltpu.VMEM((nst, 3, 8, LANE), F32)],
        compiler_params=_cparams(1),
        name="retention",
    )(rq, rk, rv, rg, s0_t, dec_b, gn_col)


def _attend(problems, tk, shifts=None):
    flat = [(pi, part, r0, min(tk, part[4])) for pi, (_, parts) in enumerate(problems) for part in parts
            for r0 in range(0, part[4], min(tk, part[4]))]

    def scores(item):
        pi, (k_ref, c0, _, _, _), r0, nk = item
        q_t = problems[pi][0]
        return jnp.dot(k_ref[r0:r0 + nk, c0:c0 + q_t.shape[0]], q_t, preferred_element_type=F32)

    state = [(jnp.full((1, q_t.shape[1]), -jnp.inf, F32), jnp.zeros((HEAD_DV, q_t.shape[1]), F32),
              jnp.zeros((1, q_t.shape[1]), F32)) for q_t, _ in problems]
    pending = [scores(it) for it in flat[:SCORE_LOOKAHEAD]]
    for i, (pi, (_, _, v_ref, v0, _), r0, nk) in enumerate(flat):
        if i + SCORE_LOOKAHEAD < len(flat):
            pending.append(scores(flat[i + SCORE_LOOKAHEAD]))
        s = pending.pop(0)
        m, acc, l = state[pi]
        if shifts is None:
            m_new = jnp.maximum(m, jnp.max(s, axis=0, keepdims=True))
            alpha = jnp.exp2(m - m_new)
            acc, l = acc * alpha, l * alpha
        else:
            m_new = shifts[pi]
        p = jnp.exp2(s - m_new)
        l = l + jnp.sum(p, axis=0, keepdims=True)
        pv = jnp.dot(v_ref[v0:v0 + HEAD_DV, r0:r0 + nk], p.astype(BF16), preferred_element_type=F32)
        state[pi] = (m_new, acc + pv, l)
    return [acc / l for _, acc, l in state]


def _attend_guarded(problems, key_sq_max, safe, tk):
    shifts = []
    for (q_t, _), ksq in zip(problems, key_sq_max):
        q32 = q_t.astype(F32)
        qsq = jnp.sum(q32 * q32, axis=0, keepdims=True)
        shifts.append(jnp.sqrt(qsq * ksq) * BOUND_SLACK)
    outs = lax.cond(safe,
                    lambda: tuple(_attend(problems, tk, shifts)),
                    lambda: tuple(_attend(problems, tk, None)))
    return list(outs)


def _row_bcast(a, row, tq):
    rid = lax.broadcasted_iota(jnp.int32, a.shape, 0)
    r = jnp.max(jnp.where(rid == row, a, 0.0), axis=0, keepdims=True)
    return jnp.concatenate([r] * (tq // LANE), axis=1)


def _diff_attn_kernel(*refs, has_ctx, rows_main, rows_ctx, tk, hps, lam_init):
    it = iter(refs)
    safe_ref, q_ref, k_ref, v_ref, ksq_ref = (next(it) for _ in range(5))
    if has_ctx:
        kc_ref, vc_ref = (next(it) for _ in range(2))
    dl_ref, g_ref, out_ref = (next(it) for _ in range(3))
    grp = pl.program_id(1)
    tq = q_ref.shape[1]
    segs_per_tile = LANE // DIFF_DK
    seg = lax.broadcasted_iota(jnp.int32, (LANE, tq), 0) // DIFF_DK
    problems, key_sq_max = [], []
    for j in range(hps):
        tile = (2 * j) // segs_per_tile
        q_tile = q_ref[tile * LANE:(tile + 1) * LANE, :]
        parts = [(k_ref, tile * LANE, v_ref, j * HEAD_DV, rows_main)]
        if has_ctx:
            parts.append((kc_ref, tile * LANE, vc_ref, j * HEAD_DV, rows_ctx))
        for mm in range(2):
            q_m = jnp.where(seg == (2 * j + mm) % segs_per_tile, q_tile, jnp.zeros_like(q_tile))
            problems.append((q_m, parts))
            key_sq_max.append(_row_bcast(ksq_ref[...], 2 * (grp * hps + j) + mm, tq))
    dl = dl_ref[...]
    lam = (jnp.exp(jnp.sum(dl[0:1] * dl[1:2], axis=1, keepdims=True))
           - jnp.exp(jnp.sum(dl[2:3] * dl[3:4], axis=1, keepdims=True)) + lam_init)
    outs = _attend_guarded(problems, key_sq_max, safe_ref[pl.program_id(0), grp] != 0, tk)
    for j in range(hps):
        o = outs[2 * j] - lam * outs[2 * j + 1]
        y = o * _rms_scale(o, DIFF_DV) * g_ref[...]
        out_ref[j * DIFF_DV:(j + 1) * DIFF_DV, :] = (y * (1.0 - lam_init)).astype(BF16)


def _safe_flags(q_sq, k_sq, per_group):
    bound = jnp.sqrt(q_sq * k_sq) * (BOUND_SLACK * BOUND_SLACK)
    ok = (2.0 * bound <= SAFE_EXP_RANGE).reshape(q_sq.shape[0], -1, per_group)
    return jnp.all(ok, axis=-1).astype(jnp.int32)


def _diff_attention(dq_t, dk_tok, dv_t, q_sq, k_sq, ctx, dl, g_col, *, batch, tq, tk, hps, lam_init):
    n_all = dq_t.shape[1]
    n = n_all // batch
    qt = n // tq
    nseg = 2 * DIFF_HEADS
    has_ctx = ctx is not None
    rows_ctx = ctx[0].shape[1] if has_ctx else 0
    assert (hps * 2 * DIFF_DK) % LANE == 0, "a grid step must cover whole 128-lane key tiles"
    qk_w = hps * 2 * DIFF_DK
    safe = _safe_flags(q_sq, k_sq, 2 * hps)
    k_sq_b = jnp.broadcast_to(k_sq[:, :, None], (batch, nseg, LANE))
    in_arrays = [safe, dq_t, dk_tok, dv_t, k_sq_b]
    in_specs = [pl.BlockSpec(memory_space=pltpu.SMEM),
                pl.BlockSpec((qk_w, tq), lambda b, g, i: (g, b * qt + i)),
                pl.BlockSpec((n, qk_w), lambda b, g, i: (b, g)),
                pl.BlockSpec((hps * HEAD_DV, n), lambda b, g, i: (g, b)),
                pl.BlockSpec((None, nseg, LANE), lambda b, g, i: (b, 0, 0))]
    if has_ctx:
        in_arrays += list(ctx)
        in_specs += [pl.BlockSpec((None, rows_ctx, qk_w), lambda b, g, i: (b, 0, g)),
                     pl.BlockSpec((None, hps * HEAD_DV, rows_ctx), lambda b, g, i: (b, g, 0))]
    in_arrays += [dl, g_col]
    in_specs += [_full(dl.shape), _full(g_col.shape)]
    return pl.pallas_call(
        functools.partial(_diff_attn_kernel, has_ctx=has_ctx, rows_main=n, rows_ctx=rows_ctx, tk=tk, hps=hps,
                          lam_init=lam_init),
        grid=(batch, DIFF_HEADS // hps, qt),
        in_specs=in_specs,
        out_specs=pl.BlockSpec((hps * DIFF_DV, tq), lambda b, g, i: (g, b * qt + i)),
        out_shape=jax.ShapeDtypeStruct((DIFF_W, n_all), BF16),
        compiler_params=_cparams(3),
        name="diff_attn_smp" if has_ctx else "diff_attn_ctx",
    )(*in_arrays)


def _mla_attn_kernel(*refs, has_ctx, rows_main, rows_ctx, tk, hps):
    it = iter(refs)
    safe_ref, q_ref, k_ref, v_ref, ksq_ref = (next(it) for _ in range(5))
    if has_ctx:
        kc_ref, vc_ref = (next(it) for _ in range(2))
    out_ref = next(it)
    grp = pl.program_id(1)
    tq = q_ref.shape[1]
    problems, key_sq_max = [], []
    for j in range(hps):
        parts = [(k_ref, j * MLA_PAD, v_ref, j * HEAD_DV, rows_main)]
        if has_ctx:
            parts.append((kc_ref, j * MLA_PAD, vc_ref, j * HEAD_DV, rows_ctx))
        problems.append((q_ref[j * MLA_PAD:(j + 1) * MLA_PAD, :], parts))
        key_sq_max.append(_row_bcast(ksq_ref[...], grp * hps + j, tq))
    outs = _attend_guarded(problems, key_sq_max, safe_ref[pl.program_id(0), grp] != 0, tk)
    for j, o in enumerate(outs):
        out_ref[j * MLA_DV:(j + 1) * MLA_DV, :] = o.astype(BF16)


def _mla_attention(mq_t, mk_tok, mv_t, q_sq, k_sq, ctx, *, batch, tq, tk, hps):
    n_all = mq_t.shape[1]
    n = n_all // batch
    qt = n // tq
    has_ctx = ctx is not None
    rows_ctx = ctx[0].shape[1] if has_ctx else 0
    safe = _safe_flags(q_sq, k_sq, hps)
    k_sq_b = jnp.broadcast_to(k_sq[:, :, None], (batch, MLA_HEADS, LANE))
    in_arrays = [safe, mq_t, mk_tok, mv_t, k_sq_b]
    in_specs = [pl.BlockSpec(memory_space=pltpu.SMEM),
                pl.BlockSpec((hps * MLA_PAD, tq), lambda b, g, i: (g, b * qt + i)),
                pl.BlockSpec((n, hps * MLA_PAD), lambda b, g, i: (b, g)),
                pl.BlockSpec((hps * HEAD_DV, n), lambda b, g, i: (g, b)),
                pl.BlockSpec((None, MLA_HEADS, LANE), lambda b, g, i: (b, 0, 0))]
    if has_ctx:
        in_arrays += list(ctx)
        in_specs += [pl.BlockSpec((None, rows_ctx, hps * MLA_PAD), lambda b, g, i: (b, 0, g)),
                     pl.BlockSpec((None, hps * HEAD_DV, rows_ctx), lambda b, g, i: (b, g, 0))]
    return pl.pallas_call(
        functools.partial(_mla_attn_kernel, has_ctx=has_ctx, rows_main=n, rows_ctx=rows_ctx, tk=tk, hps=hps),
        grid=(batch, MLA_HEADS // hps, qt),
        in_specs=in_specs,
        out_specs=pl.BlockSpec((hps * MLA_DV, tq), lambda b, g, i: (g, b * qt + i)),
        out_shape=jax.ShapeDtypeStruct((MLA_W, n_all), BF16),
        compiler_params=_cparams(3),
        name="mla_attn_smp" if has_ctx else "mla_attn_ctx",
    )(*in_arrays)


def _post_kernel(x_ref, ret_ref, diff_ref, mla_ref, g1_ref, sh2_ref, sc2_ref, g2_ref, n2_ref,
                 w_out_ref, w_gu_ref, w_down_ref, out_ref, *, tm, d_ff, ff_cuts):
    nch = tm // LANE
    ret = jnp.concatenate([ret_ref[j] for j in range(nch)], axis=1)
    mixed = jnp.concatenate([ret, diff_ref[...], mla_ref[...]], axis=0)
    d = x_ref.shape[0]
    halves = [slice(i * (tm // 2), (i + 1) * (tm // 2)) for i in range(2)]
    attn = [jnp.dot(w_out_ref[...], mixed[:, c], preferred_element_type=F32) for c in halves]
    x1 = [x_ref[:, c] + g1_ref[...] * a for c, a in zip(halves, attn)]
    gain2 = n2_ref[...] * (1.0 + sc2_ref[...])
    h = [(v * _rms_scale(v, d) * gain2 + sh2_ref[...]).astype(BF16) for v in x1]
    ffn = None
    for lo, hi in zip((0,) + ff_cuts, ff_cuts + (d_ff,)):
        acts = []
        for i in range(2):
            g = jnp.dot(w_gu_ref[lo:hi, :], h[i], preferred_element_type=F32)
            u = jnp.dot(w_gu_ref[d_ff + lo:d_ff + hi, :], h[i], preferred_element_type=F32)
            acts.append((_silu(g) * u).astype(BF16))
        part = jnp.dot(w_down_ref[:, lo:hi], jnp.concatenate(acts, axis=1), preferred_element_type=F32)
        ffn = part if ffn is None else ffn + part
    out_ref[...] = jnp.concatenate(x1, axis=1) + g2_ref[...] * ffn


def _post(x_t, ret, diff_t, mla_t, mods, lw, *, is_ctx, tm, tiles_per_batch):
    d, n = x_t.shape
    nt = n // tm
    nch = tm // LANE
    d_ff = lw["w_down"].shape[1]
    ff_cuts = ((d_ff // 2) // MXU_DEPTH * MXU_DEPTH,)
    if is_ctx:
        bidx = lambda j: (0, 0, 0)
    else:
        bidx = lambda j: (j // tiles_per_batch, 0, 0)
    col = pl.BlockSpec((None, d, 1), bidx)
    const = lambda a: pl.BlockSpec(a.shape, lambda j: (0,) * a.ndim, pipeline_mode=pl.Buffered(1))
    g1, sh2, sc2, g2 = mods
    return pl.pallas_call(
        functools.partial(_post_kernel, tm=tm, d_ff=d_ff, ff_cuts=ff_cuts),
        grid=(nt,),
        in_specs=[pl.BlockSpec((d, tm), lambda j: (0, j)),
                  pl.BlockSpec((nch, RET_W, LANE), lambda j: (j, 0, 0)),
                  pl.BlockSpec((DIFF_W, tm), lambda j: (0, j)),
                  pl.BlockSpec((MLA_W, tm), lambda j: (0, j)),
                  col, col, col, col, const(lw["n2"]),
                  const(lw["w_out"]), const(lw["w_gu"]), const(lw["w_down"])],
        out_specs=pl.BlockSpec((d, tm), lambda j: (0, j)),
        out_shape=jax.ShapeDtypeStruct((d, n), F32),
        compiler_params=_cparams(1),
        name="post_ctx" if is_ctx else "post_smp",
    )(x_t, ret, diff_t, mla_t, g1, sh2, sc2, g2, lw["n2"], lw["w_out"], lw["w_gu"], lw["w_down"])


def _rope_tables_t(n, rot_dim):
    rows = n // GRID_W
    row = jnp.repeat(jnp.arange(rows, dtype=F32), GRID_W)
    col = jnp.tile(jnp.arange(GRID_W, dtype=F32), rows)
    n_freq = rot_dim // 4
    inv = 1.0 / (ROPE_BASE ** (jnp.arange(n_freq, dtype=F32) / n_freq))
    ang = jnp.concatenate([inv[:, None] * row[None, :], inv[:, None] * col[None, :]], axis=0)
    return jnp.cos(ang), jnp.sin(ang)


def _col(v):
    return v.astype(F32).reshape(-1, 1)


def _pad_heads_rows(w_t, used, pad):
    hk = w_t.shape[0] // used
    w3 = w_t.reshape(hk, used, w_t.shape[1])
    w3 = jnp.pad(w3, ((0, 0), (0, pad - used), (0, 0)))
    return w3.reshape(hk * pad, w_t.shape[1])


def _layer_weights(l, w_in, norm1, norm2, diff_qk_gain, mla_q_norm, mla_kv_norm, w_uq, w_ukv, mla_qk_gain,
                   w_out, w_gu, w_down):
    w_ukv_t = w_ukv[l].T.reshape(MLA_HEADS, MLA_NOPE + MLA_DV, MLA_KV_RANK)
    w_uk = jnp.pad(w_ukv_t[:, :MLA_NOPE], ((0, 0), (0, MLA_PAD - MLA_NOPE), (0, 0)))
    g_pad = lambda g: jnp.tile(jnp.pad(g.astype(F32), (0, MLA_PAD - MLA_DQK)), MLA_HEADS).reshape(-1, 1)
    return {
        "n1": _col(norm1[l]), "n2": _col(norm2[l]),
        "w_in": jnp.concatenate([w_in[l][:, W_IN_MLA0:], w_in[l][:, :W_IN_MLA0]], axis=1).T.astype(BF16),
        "g_dq": _col(jnp.tile(diff_qk_gain[l, 0], 2 * DIFF_HEADS)),
        "g_dk": _col(jnp.tile(diff_qk_gain[l, 1], 2 * DIFF_HEADS)),
        "g_qn": _col(mla_q_norm[l]), "g_kvn": _col(mla_kv_norm[l]),
        "w_uq": _pad_heads_rows(w_uq[l].T, MLA_DQK, MLA_PAD).astype(BF16),
        "g_mq": g_pad(mla_qk_gain[l, 0]), "g_mk": g_pad(mla_qk_gain[l, 1]),
        "w_uk": w_uk.reshape(MLA_QK_PAD_W, MLA_KV_RANK).astype(BF16),
        "w_uv": w_ukv_t[:, MLA_NOPE:].reshape(MLA_W, MLA_KV_RANK).astype(BF16),
        "w_out": w_out[l].T.astype(BF16), "w_gu": w_gu[l].T.astype(BF16), "w_down": w_down[l].T.astype(BF16),
    }


def kernel(x_prompt, x_sample, c, state_ret, cache_diff_k, cache_diff_v, cache_mla_ckv, cache_mla_kr, c_ctx,
           w_mod, b_mod, norm1, norm2, w_in, ret_decay, ret_gn_gain, diff_qk_gain, diff_lambda, diff_subln_gain,
           mla_q_norm, mla_kv_norm, w_uq, w_ukv, mla_qk_gain, w_out, w_gu, w_down):
    depth = w_in.shape[0]
    bp, sp, d = x_prompt.shape
    bs, ss, _ = x_sample.shape
    past = cache_diff_k.shape[2]

    tm = 512
    tq_s, tk_s = 512, 256
    tq_p = tk_p = sp

    n_cond = 1 + bs
    r_pad = -(-n_cond // 16) * 16
    cond = jnp.concatenate([c_ctx[None, :], c, jnp.zeros((r_pad - n_cond, d), F32)], axis=0)
    mod = _modulation(cond, w_mod, b_mod)
    mod = mod.reshape(depth, r_pad, 6, d, 1)

    xp = x_prompt.reshape(bp * sp, d).T
    xs = x_sample.reshape(bs * ss, d).T
    rope = _rope_tables_t(ss, RET_DK) + _rope_tables_t(ss, DIFF_DK)
    dec_b = jnp.broadcast_to(ret_decay.astype(F32).reshape(depth, 2 * RET_HEADS, 1, 1),
                             (depth, 2 * RET_HEADS, 8, LANE))
    s0_zero = jnp.zeros((bp, 2 * RET_HEADS, RET_DV, RET_DK), F32)

    new_ret, new_dk, new_dv, new_ckv, new_kr = [], [], [], [], []
    for l in range(depth):
        lw = _layer_weights(l, w_in, norm1, norm2, diff_qk_gain, mla_q_norm, mla_kv_norm, w_uq, w_ukv,
                            mla_qk_gain, w_out, w_gu, w_down)
        lam_init = 0.8 - 0.6 * math.exp(-0.3 * l)
        gn_col = _col(ret_gn_gain[l])
        subln_col = _col(diff_subln_gain[l])
        dl = diff_lambda[l].astype(F32)

        mc = [mod[l, 0:1, i] for i in range(6)]
        def norm_bound(gain, n, out_scale, batch):
            return jnp.full((batch, 8), n * out_scale * out_scale, F32) * jnp.max(jnp.square(gain.astype(F32)))

        dq_sq = functools.partial(norm_bound, diff_qk_gain[l, 0], DIFF_DK, DIFF_DK ** -0.5 * LOG2E)
        dk_sq = functools.partial(norm_bound, diff_qk_gain[l, 1], DIFF_DK, 1.0)
        mq_sq = functools.partial(norm_bound, mla_qk_gain[l, 0], MLA_DQK, MLA_DQK ** -0.5 * LOG2E)
        mk_sq = functools.partial(norm_bound, mla_qk_gain[l, 1], MLA_DQK, 1.0)

        (rq, rk, rv, rg, dq_t, dk_tok, dv_t, mq_t, mk_tok, mv_t, dk_f, dv_f, ckv_f, kr_f) = _premix(
            xp, mc[0], mc[1], lw, None, is_ctx=True, tm=tm, seq=sp)
        ret, s_fin = _retention(rq, rk, rv, rg, s0_zero, dec_b[l], gn_col, batch=bp)
        diff_t = _diff_attention(dq_t, dk_tok, dv_t, dq_sq(bp), dk_sq(bp), None, dl, subln_col, batch=bp, tq=tq_p,
                                 tk=tk_p, hps=DIFF_HEADS, lam_init=lam_init)
        mla_t = _mla_attention(mq_t, mk_tok, mv_t, mq_sq(bp), mk_sq(bp), None, batch=bp, tq=tq_p, tk=tk_p,
                               hps=MLA_HEADS)
        xp = _post(xp, ret, diff_t, mla_t, (mc[2], mc[3], mc[4], mc[5]), lw, is_ctx=True, tm=tm,
                   tiles_per_batch=1)
        new_ret.append(jnp.swapaxes(s_fin.reshape(bp, 2, RET_HEADS, RET_DV, RET_DK), -1, -2))
        new_dk.append(dk_f.reshape(bp, sp, DIFF_HEADS, 2, DIFF_DK))
        new_dv.append(dv_f.reshape(bp, sp, DIFF_HEADS, DIFF_DV))
        new_ckv.append(ckv_f.reshape(bp, sp, MLA_KV_RANK))
        new_kr.append(kr_f[:, :MLA_ROPE].reshape(bp, sp, MLA_ROPE))

        ms = [mod[l, 1:1 + bs, i] for i in range(6)]
        s0_t = jnp.swapaxes(state_ret[:, l].astype(F32), -1, -2).reshape(bs, 2 * RET_HEADS, RET_DV, RET_DK)
        ctx_dk_f = cache_diff_k[:, l].reshape(bs, past, DIFF_QK_W).astype(F32)
        ctx_dk = ctx_dk_f.astype(BF16)
        ctx_dv_t = jnp.swapaxes(cache_diff_v[:, l].reshape(bs, past, DIFF_W), 1, 2).astype(BF16)
        ctx_mk, ctx_mv_t, ctx_sq = _ctx_mla_kv(
            jnp.swapaxes(cache_mla_ckv[:, l].astype(F32), 1, 2), jnp.swapaxes(cache_mla_kr[:, l].astype(F32), 1, 2),
            jnp.swapaxes(ctx_dk_f, 1, 2), lw["w_uk"], lw["w_uv"], lw["g_mk"])
        ctx_sq = jnp.max(ctx_sq, axis=-1)
        (rq, rk, rv, rg, dq_t, dk_tok, dv_t, mq_t, mk_tok, mv_t) = _premix(
            xs, ms[0], ms[1], lw, rope, is_ctx=False, tm=tm, seq=ss)
        ret, _ = _retention(rq, rk, rv, rg, s0_t, dec_b[l], gn_col, batch=bs)
        diff_t = _diff_attention(dq_t, dk_tok, dv_t, dq_sq(bs), jnp.maximum(dk_sq(bs), ctx_sq[:, 1]),
                                 (ctx_dk, ctx_dv_t), dl, subln_col, batch=bs, tq=tq_s, tk=tk_s, hps=2,
                                 lam_init=lam_init)
        mla_t = _mla_attention(mq_t, mk_tok, mv_t, mq_sq(bs), jnp.maximum(mk_sq(bs), ctx_sq[:, 0]),
                               (ctx_mk, ctx_mv_t), batch=bs, tq=tq_s, tk=tk_s, hps=4)
        xs = _post(xs, ret, diff_t, mla_t, (ms[2], ms[3], ms[4], ms[5]), lw, is_ctx=False, tm=tm,
                   tiles_per_batch=ss // tm)

    y_p = xp.T.reshape(bp, sp, d)
    y_s = xs.T.reshape(bs, ss, d)
    return (y_p, y_s, jnp.stack(new_ret, axis=1), jnp.stack(new_dk, axis=1), jnp.stack(new_dv, axis=1),
            jnp.stack(new_ckv, axis=1), jnp.stack(new_kr, axis=1))
```

```python
import functools
import math

import jax
import jax.numpy as jnp
from jax import lax
from jax.experimental import pallas as pl
from jax.experimental.pallas import tpu as pltpu

F32 = jnp.float32
BF16 = jnp.bfloat16
EPS = 1e-6

GRID_W = 64
RET_HEADS = 4
RET_DK = 64
RET_DV = 64
RET_CHUNK = 128
DIFF_HEADS = 4
DIFF_DK = 32
DIFF_DV = 64
MLA_HEADS = 8
MLA_NOPE = 64
MLA_ROPE = 32
MLA_DQK = MLA_NOPE + MLA_ROPE
MLA_DV = 64
MLA_Q_RANK = 768
MLA_KV_RANK = 256
ROPE_BASE = 10000.0

RET_W = RET_HEADS * RET_DV
DIFF_W = DIFF_HEADS * DIFF_DV
MLA_W = MLA_HEADS * MLA_DV
DIFF_QK_W = DIFF_HEADS * 2 * DIFF_DK
MLA_PAD = 128
MLA_QK_PAD_W = MLA_HEADS * MLA_PAD
assert DIFF_DV == MLA_DV
HEAD_DV = MLA_DV

_SPLITS = (RET_HEADS * RET_DK, RET_HEADS * RET_DK, RET_W, RET_W,
           DIFF_QK_W, DIFF_QK_W, DIFF_W, MLA_Q_RANK, MLA_KV_RANK, MLA_ROPE)
W_IN_MLA0 = int(sum(_SPLITS[:7]))
_PERM_SPLITS = _SPLITS[7:] + _SPLITS[:7]
_OFFS = tuple(int(sum(_PERM_SPLITS[:i])) for i in range(len(_PERM_SPLITS) + 1))
O_CQ, O_CKV, O_KR, O_RQ, O_RK, O_RV, O_RG, O_DQ, O_DK, O_DV, O_END = _OFFS

LOG2E = math.log2(math.e)
SCORE_LOOKAHEAD = 2
SAFE_EXP_RANGE = 100.0
BOUND_SLACK = 1.0 + 2.0 ** -6
LANE = 128
MXU_DEPTH = 256
PREMIX_GROUPS = 2
VMEM_LIMIT = 56 * 1024 * 1024


def _cparams(n_grid):
    return pltpu.CompilerParams(dimension_semantics=("arbitrary",) * n_grid,
                                vmem_limit_bytes=VMEM_LIMIT)


def _full(shape):
    nd = len(shape)
    return pl.BlockSpec(shape, lambda *_: (0,) * nd)


def _silu(x):
    return x / (1.0 + jnp.exp(-x))


def _rms_scale(x, n):
    return lax.rsqrt(jnp.sum(x * x, axis=0, keepdims=True) * (1.0 / n) + EPS)


def _rope_pair(x1, x2, c, s):
    return x1 * c - x2 * s, x2 * c + x1 * s


def _mod_kernel(c_ref, w_ref, b_ref, o_ref):
    a = _silu(c_ref[...]).astype(BF16)
    o_ref[...] = jnp.dot(a, w_ref[...].astype(BF16), preferred_element_type=F32) + b_ref[...]


def _modulation(cond, w_mod, b_mod):
    depth, d, d6 = w_mod.shape
    r = cond.shape[0]
    tn = 1536
    return pl.pallas_call(
        _mod_kernel,
        grid=(depth, d6 // tn),
        in_specs=[pl.BlockSpec((r, d), lambda l, j: (0, 0)),
                  pl.BlockSpec((None, d, tn), lambda l, j: (l, 0, j)),
                  pl.BlockSpec((None, 1, tn), lambda l, j: (l, 0, j))],
        out_specs=pl.BlockSpec((None, r, tn), lambda l, j: (l, 0, j)),
        out_shape=jax.ShapeDtypeStruct((depth, r, d6), F32),
        compiler_params=_cparams(2),
        name="adaln_mod",
    )(cond, w_mod, b_mod.reshape(depth, 1, d6))


def _mla_kv_matmuls(ckvn_bf, w_uk_ref, w_uv_ref):
    return (jnp.dot(w_uk_ref[...], ckvn_bf, preferred_element_type=F32),
            jnp.dot(w_uv_ref[...], ckvn_bf, preferred_element_type=F32))


def _mla_keys(kn, kr, g_mk_ref, rope):
    t = kr.shape[1]
    kr_ss = jnp.sum(kr * kr, axis=0, keepdims=True)
    zpad = jnp.zeros((MLA_PAD - MLA_DQK, t), F32)
    half = MLA_ROPE // 2
    heads = []
    for hd in range(MLA_HEADS):
        r0 = hd * MLA_PAD
        kh = kn[r0:r0 + MLA_NOPE]
        ss = jnp.sum(kh * kh, axis=0, keepdims=True) + kr_ss
        r = lax.rsqrt(ss * (1.0 / MLA_DQK) + EPS)
        g = g_mk_ref[r0:r0 + MLA_PAD, :]
        y_nope = kh * r * g[:MLA_NOPE]
        y_r = kr * r * g[MLA_NOPE:MLA_DQK]
        if rope is not None:
            c, s = rope
            y1, y2 = _rope_pair(y_r[:half], y_r[half:], c, s)
            heads += [y_nope, y1, y2, zpad]
        else:
            heads += [y_nope, y_r, zpad]
    return jnp.concatenate(heads, axis=0)


def _group_sq_norms(a, rows_per_group):
    groups = a.shape[0] // rows_per_group
    return jnp.concatenate([jnp.sum(jnp.square(a[g * rows_per_group:(g + 1) * rows_per_group]), axis=0, keepdims=True)
                            for g in range(groups)], axis=0)


def _lane_tile_max(sq):
    return functools.reduce(jnp.maximum, [sq[:, t * LANE:(t + 1) * LANE] for t in range(sq.shape[1] // LANE)])


def _ctx_kv_kernel(ckv_ref, kr_ref, dk_ref, w_uk_ref, w_uv_ref, g_mk_ref, mk_ref, mv_ref, sq_ref):
    kn, v = _mla_kv_matmuls(ckv_ref[...].astype(BF16), w_uk_ref, w_uv_ref)
    k = _mla_keys(kn, kr_ref[...], g_mk_ref, None)
    mk_ref[...] = k.T.astype(BF16)
    mv_ref[...] = v.astype(BF16)
    sq_ref[0] = _lane_tile_max(_group_sq_norms(k, MLA_PAD))
    sq_ref[1] = _lane_tile_max(_group_sq_norms(dk_ref[...], DIFF_DK))


def _ctx_mla_kv(ckv_t, kr_t, dk_t, w_uk, w_uv, g_mk):
    b, _, l = ckv_t.shape
    return pl.pallas_call(
        _ctx_kv_kernel,
        grid=(b,),
        in_specs=[pl.BlockSpec((None, MLA_KV_RANK, l), lambda i: (i, 0, 0)),
                  pl.BlockSpec((None, MLA_ROPE, l), lambda i: (i, 0, 0)),
                  pl.BlockSpec((None, DIFF_QK_W, l), lambda i: (i, 0, 0)),
                  _full(w_uk.shape), _full(w_uv.shape), _full(g_mk.shape)],
        out_specs=[pl.BlockSpec((None, l, MLA_QK_PAD_W), lambda i: (i, 0, 0)),
                   pl.BlockSpec((None, MLA_W, l), lambda i: (i, 0, 0)),
                   pl.BlockSpec((None, 2, 8, LANE), lambda i: (i, 0, 0, 0))],
        out_shape=[jax.ShapeDtypeStruct((b, l, MLA_QK_PAD_W), BF16),
                   jax.ShapeDtypeStruct((b, MLA_W, l), BF16),
                   jax.ShapeDtypeStruct((b, 2, 8, LANE), F32)],
        compiler_params=_cparams(1),
        name="ctx_mla_kv",
    )(ckv_t, kr_t, dk_t, w_uk, w_uv, g_mk)


def _premix_kernel(*refs, is_ctx, tm):
    it = iter(refs)
    x_ref, shift_ref, scale_ref, n1_ref, w_in_ref = (next(it) for _ in range(5))
    g_dq_ref, g_dk_ref, g_qn_ref, g_kvn_ref = (next(it) for _ in range(4))
    w_uq_ref, g_mq_ref, w_uk_ref, w_uv_ref, g_mk_ref = (next(it) for _ in range(5))
    if not is_ctx:
        cr_ref, sr_ref, cs_ref, ss_ref = (next(it) for _ in range(4))
    rq_ref, rk_ref, rv_ref, rg_ref = (next(it) for _ in range(4))
    dq_ref, dk_ref, dv_ref, mq_ref, mk_ref, mv_ref = (next(it) for _ in range(6))
    if is_ctx:
        dkf_ref, dvf_ref, ckvf_ref, krf_ref = (next(it) for _ in range(4))
    d = x_ref.shape[0]

    def token_group(c0, w):
        cols = slice(c0, c0 + w)
        if is_ctx:
            rope_r = rope_s = None
        else:
            rope_r = (cr_ref[:, cols], sr_ref[:, cols])
            rope_s = (cs_ref[:, cols], ss_ref[:, cols])
        x = x_ref[:, cols]
        h = (x * _rms_scale(x, d) * (n1_ref[...] * (1.0 + scale_ref[...])) + shift_ref[...]).astype(BF16)
        yield

        def put_chunks(ref, val):
            for j in range(w // LANE):
                ref[c0 // LANE + j] = val[:, j * LANE:(j + 1) * LANE].astype(ref.dtype)

        def proj(lo, hi):
            return jnp.dot(w_in_ref[lo:hi, :], h, preferred_element_type=F32)

        p_mla = proj(O_CQ, O_RQ)
        cq, ckv, kr = p_mla[O_CQ:O_CKV], p_mla[O_CKV:O_KR], p_mla[O_KR:O_RQ]
        p_ret = proj(O_RQ, O_DQ)
        rq, rk, rv, rg = (p_ret[o - O_RQ:o - O_RQ + RET_W] for o in (O_RQ, O_RK, O_RV, O_RG))
        yield
        cqn = (cq * _rms_scale(cq, MLA_Q_RANK) * g_qn_ref[...]).astype(BF16)
        ckvn = ckv * _rms_scale(ckv, MLA_KV_RANK) * g_kvn_ref[...]
        mq = jnp.dot(w_uq_ref[...], cqn, preferred_element_type=F32)
        kn, mv = _mla_kv_matmuls(ckvn.astype(BF16), w_uk_ref, w_uv_ref)
        p_diff = proj(O_DQ, O_END)
        dq_raw, dk_raw, dv = (p_diff[o - O_DQ:o - O_DQ + DIFF_QK_W] for o in (O_DQ, O_DK, O_DV))
        yield

        if rope_r is not None:
            c, s = rope_r
            hk = RET_DK // 2

            def rope_heads(a):
                parts = []
                for hd in range(RET_HEADS):
                    r0 = hd * RET_DK
                    parts += list(_rope_pair(a[r0:r0 + hk], a[r0 + hk:r0 + RET_DK], c, s))
                return jnp.concatenate(parts, axis=0)

            rq = rope_heads(rq)
            rk = rope_heads(rk)
        put_chunks(rq_ref, rq)
        put_chunks(rk_ref, rk * (RET_DK ** -0.5))
        put_chunks(rv_ref, rv)
        put_chunks(rg_ref, rg)

        def diff_qk(a, g_ref, out_scale):
            parts = []
            hs = DIFF_DK // 2
            for seg in range(2 * DIFF_HEADS):
                r0 = seg * DIFF_DK
                xs = a[r0:r0 + DIFF_DK]
                y = xs * _rms_scale(xs, DIFF_DK) * g_ref[r0:r0 + DIFF_DK, :]
                if out_scale != 1.0:
                    y = y * out_scale
                if rope_s is not None:
                    parts += list(_rope_pair(y[:hs], y[hs:], rope_s[0], rope_s[1]))
                else:
                    parts.append(y)
            return jnp.concatenate(parts, axis=0)

        dq = diff_qk(dq_raw, g_dq_ref, DIFF_DK ** -0.5 * LOG2E)
        dq_ref[:, cols] = dq.astype(BF16)
        dk = diff_qk(dk_raw, g_dk_ref, 1.0)
        dk_t = dk.T
        dk_ref[cols, :] = dk_t.astype(BF16)
        dv_ref[:, cols] = dv.astype(BF16)
        if is_ctx:
            dkf_ref[cols, :] = dk_t
            dvf_ref[cols, :] = dv.T

        half = MLA_ROPE // 2
        parts = []
        for hd in range(MLA_HEADS):
            r0 = hd * MLA_PAD
            xs = mq[r0:r0 + MLA_PAD]
            y = xs * _rms_scale(xs, MLA_DQK) * (g_mq_ref[r0:r0 + MLA_PAD, :] * (MLA_DQK ** -0.5 * LOG2E))
            if rope_s is not None:
                y1, y2 = _rope_pair(y[MLA_NOPE:MLA_NOPE + half], y[MLA_NOPE + half:MLA_DQK], rope_s[0], rope_s[1])
                parts += [y[:MLA_NOPE], y1, y2, y[MLA_DQK:]]
            else:
                parts.append(y)
        mq_ref[:, cols] = jnp.concatenate(parts, axis=0).astype(BF16)

        mk = _mla_keys(kn, kr, g_mk_ref, rope_s)
        mk_ref[cols, :] = mk.T.astype(BF16)
        mv_ref[:, cols] = mv.astype(BF16)
        if is_ctx:
            ckvf_ref[cols, :] = ckvn.T
            krf_ref[cols, :] = jnp.concatenate([kr, jnp.zeros((LANE - MLA_ROPE, w), F32)], axis=0).T
        yield

    groups = [token_group(i * (tm // PREMIX_GROUPS), tm // PREMIX_GROUPS) for i in range(PREMIX_GROUPS)]
    for _ in range(4):
        for g in groups:
            next(g)


def _premix(x_t, shift, scale, lw, rope, *, is_ctx, tm, seq):
    d, n = x_t.shape
    nt = n // tm
    nch = tm // LANE
    tiles_per_batch = max(1, seq // tm)
    if is_ctx:
        bidx = lambda j: (0, 0, 0)
    else:
        bidx = lambda j: (j // tiles_per_batch, 0, 0)
    in_arrays = [x_t, shift, scale, lw["n1"], lw["w_in"], lw["g_dq"], lw["g_dk"], lw["g_qn"], lw["g_kvn"],
                 lw["w_uq"], lw["g_mq"], lw["w_uk"], lw["w_uv"], lw["g_mk"]]
    in_specs = [pl.BlockSpec((d, tm), lambda j: (0, j)),
                pl.BlockSpec((None, d, 1), bidx), pl.BlockSpec((None, d, 1), bidx)]
    in_specs += [_full(a.shape) for a in in_arrays[3:]]
    if not is_ctx:
        for tab in rope:
            in_arrays.append(tab)
            in_specs.append(pl.BlockSpec((tab.shape[0], tm), lambda j: (0, j % tiles_per_batch)))

    chunk_spec = pl.BlockSpec((nch, RET_W, LANE), lambda j: (j, 0, 0))
    fm = lambda rows: pl.BlockSpec((rows, tm), lambda j: (0, j))
    tok = lambda cols: pl.BlockSpec((tm, cols), lambda j: (j, 0))
    nck = n // LANE
    out_specs = [chunk_spec] * 4 + [fm(DIFF_QK_W), tok(DIFF_QK_W), fm(DIFF_W),
                                    fm(MLA_QK_PAD_W), tok(MLA_QK_PAD_W), fm(MLA_W)]
    out_shape = [jax.ShapeDtypeStruct((nck, RET_W, LANE), BF16), jax.ShapeDtypeStruct((nck, RET_W, LANE), F32),
                 jax.ShapeDtypeStruct((nck, RET_W, LANE), BF16), jax.ShapeDtypeStruct((nck, RET_W, LANE), F32),
                 jax.ShapeDtypeStruct((DIFF_QK_W, n), BF16), jax.ShapeDtypeStruct((n, DIFF_QK_W), BF16),
                 jax.ShapeDtypeStruct((DIFF_W, n), BF16),
                 jax.ShapeDtypeStruct((MLA_QK_PAD_W, n), BF16), jax.ShapeDtypeStruct((n, MLA_QK_PAD_W), BF16),
                 jax.ShapeDtypeStruct((MLA_W, n), BF16)]
    if is_ctx:
        out_specs += [tok(DIFF_QK_W), tok(DIFF_W), tok(MLA_KV_RANK), tok(LANE)]
        out_shape += [jax.ShapeDtypeStruct((n, DIFF_QK_W), F32), jax.ShapeDtypeStruct((n, DIFF_W), F32),
                      jax.ShapeDtypeStruct((n, MLA_KV_RANK), F32), jax.ShapeDtypeStruct((n, LANE), F32)]
    return pl.pallas_call(
        functools.partial(_premix_kernel, is_ctx=is_ctx, tm=tm),
        grid=(nt,),
        in_specs=in_specs, out_specs=out_specs, out_shape=out_shape,
        compiler_params=_cparams(1),
        name="premix_ctx" if is_ctx else "premix_smp",
    )(*in_arrays)


def _retention_kernel(rq_ref, rk_ref, rv_ref, rg_ref, s0_ref, dec_ref, gn_ref, out_ref, sfin_ref,
                      o_acc, s_acc, intra_ref, vec_ref, *, nc):
    C = RET_CHUNK

    @pl.when(pl.program_id(0) == 0)
    def _():
        n_idx = lax.broadcasted_iota(jnp.int32, (C, C), 1).astype(F32)
        m_idx = lax.broadcasted_iota(jnp.int32, (C, C), 0).astype(F32)
        lane = lax.broadcasted_iota(jnp.int32, (8, C), 1).astype(F32)
        for d in range(2):
            for hd in range(RET_HEADS):
                i = d * RET_HEADS + hd
                z = dec_ref[i]
                lg8 = jnp.minimum(z, 0.0) - jnp.log1p(jnp.exp(-jnp.abs(z)))
                lg = jnp.broadcast_to(lg8[0:1, :], (C, C))
                dist = (n_idx - m_idx) if d == 0 else (m_idx - n_idx)
                ok = dist >= 0.0
                intra_ref[i] = jnp.where(ok, jnp.exp(lg * jnp.where(ok, dist, 0.0)), 0.0)
                if d == 0:
                    qdec = jnp.exp(lg8 * (lane + 1.0))
                    kdec = jnp.exp(lg8 * (C - 1.0 - lane))
                else:
                    qdec = jnp.exp(lg8 * (C - lane))
                    kdec = jnp.exp(lg8 * lane)
                vec_ref[i, 0] = qdec
                vec_ref[i, 1] = kdec
                vec_ref[i, 2] = jnp.exp(lg8 * float(C))

    s_acc[...] = s0_ref[...]
    o_acc[...] = jnp.zeros_like(o_acc)

    row_head = lax.broadcasted_iota(jnp.int32, (RET_HEADS * RET_DK, C), 0) // RET_DK

    def step(c, carry):
        first, vs = [], []
        for d in range(2):
            cc = c if d == 0 else nc - 1 - c
            q_all = rq_ref[cc]
            k_all = rk_ref[cc]
            k_tok = k_all.T.astype(BF16)
            q_bd = jnp.concatenate([jnp.where(row_head == hd, q_all, jnp.zeros_like(q_all))
                                    for hd in range(RET_HEADS)], axis=1)
            att_all = jnp.dot(k_tok, q_bd, preferred_element_type=F32)
            for hd in range(RET_HEADS):
                i = d * RET_HEADS + hd
                r0 = hd * RET_DK
                q = q_all[r0:r0 + RET_DK]
                k = k_all[r0:r0 + RET_DK]
                v = rv_ref[cc, r0:r0 + RET_DV, :]
                att_t = att_all[:, hd * C:(hd + 1) * C]
                s_t = s_acc[i]
                cross = jnp.dot(s_t.astype(BF16), q, preferred_element_type=F32)
                kd = (k * vec_ref[i, 1][0:1, :]).astype(BF16)
                kv = lax.dot_general(v, kd, (((1,), (1,)), ((), ())), preferred_element_type=F32)
                s_acc[i] = s_t * vec_ref[i, 2][0:1, 0:RET_DK] + kv
                first.append((att_t, cross))
                vs.append((i, cc, r0, v))
        masked = [(att_t * intra_ref[i]).astype(BF16) for (att_t, _), (i, _, _, _) in zip(first, vs)]
        for (_, cross), att_m, (i, cc, r0, v) in zip(first, masked, vs):
            o = jnp.dot(v, att_m, preferred_element_type=F32) + cross * vec_ref[i, 0][0:1, :]
            o_acc[cc, r0:r0 + RET_DV, :] += o
        return carry

    lax.fori_loop(0, nc, step, 0, unroll=min(nc, 8))
    sfin_ref[...] = s_acc[...]

    def finish(c, carry):
        o = o_acc[c]
        g = rg_ref[c]
        parts = []
        for hd in range(RET_HEADS):
            r0 = hd * RET_DV
            oh = o[r0:r0 + RET_DV]
            mu = jnp.mean(oh, axis=0, keepdims=True)
            var = jnp.mean(jnp.square(oh - mu), axis=0, keepdims=True)
            y = (oh - mu) * lax.rsqrt(var + EPS) * gn_ref[r0:r0 + RET_DV, :]
            parts.append(y * _silu(g[r0:r0 + RET_DV]))
        out_ref[c] = jnp.concatenate(parts, axis=0).astype(BF16)
        return carry

    lax.fori_loop(0, nc, finish, 0, unroll=min(nc, 4))


def _retention(rq, rk, rv, rg, s0_t, dec_b, gn_col, *, batch):
    nck = rq.shape[0]
    nc = nck // batch
    blk = pl.BlockSpec((nc, RET_W, LANE), lambda b: (b, 0, 0))
    nst = 2 * RET_HEADS
    st_spec = pl.BlockSpec((None, nst, RET_DV, RET_DK), lambda b: (b, 0, 0, 0))
    return pl.pallas_call(
        functools.partial(_retention_kernel, nc=nc),
        grid=(batch,),
        in_specs=[blk, blk, blk, blk, st_spec, _full(dec_b.shape), _full(gn_col.shape)],
        out_specs=[blk, st_spec],
        out_shape=[jax.ShapeDtypeStruct((nck, RET_W, LANE), BF16),
                   jax.ShapeDtypeStruct((batch, nst, RET_DV, RET_DK), F32)],
        scratch_shapes=[pltpu.VMEM((nc, RET_W, LANE), F32),
                        pltpu.VMEM((nst, RET_DV, RET_DK), F32),
                        pltpu.VMEM((nst, RET_CHUNK, RET_CHUNK), F32),
                        pltpu.VMEM((nst, 3, 8, LANE), F32)],
        compiler_params=_cparams(1),
        name="retention",
    )(rq, rk, rv, rg, s0_t, dec_b, gn_col)


def _attend(problems, tk, shifts=None):
    flat = [(pi, part, r0, min(tk, part[4])) for pi, (_, parts) in enumerate(problems) for part in parts
            for r0 in range(0, part[4], min(tk, part[4]))]

    def scores(item):
        pi, (k_ref, c0, _, _, _), r0, nk = item
        q_t = problems[pi][0]
        return jnp.dot(k_ref[r0:r0 + nk, c0:c0 + q_t.shape[0]], q_t, preferred_element_type=F32)

    state = [(jnp.full((1, q_t.shape[1]), -jnp.inf, F32), jnp.zeros((HEAD_DV, q_t.shape[1]), F32),
              jnp.zeros((1, q_t.shape[1]), F32)) for q_t, _ in problems]
    pending = [scores(it) for it in flat[:SCORE_LOOKAHEAD]]
    for i, (pi, (_, _, v_ref, v0, _), r0, nk) in enumerate(flat):
        if i + SCORE_LOOKAHEAD < len(flat):
            pending.append(scores(flat[i + SCORE_LOOKAHEAD]))
        s = pending.pop(0)
        m, acc, l = state[pi]
        if shifts is None:
            m_new = jnp.maximum(m, jnp.max(s, axis=0, keepdims=True))
            alpha = jnp.exp2(m - m_new)
            acc, l = acc * alpha, l * alpha
        else:
            m_new = shifts[pi]
        p = jnp.exp2(s - m_new)
        l = l + jnp.sum(p, axis=0, keepdims=True)
        pv = jnp.dot(v_ref[v0:v0 + HEAD_DV, r0:r0 + nk], p.astype(BF16), preferred_element_type=F32)
        state[pi] = (m_new, acc + pv, l)
    return [acc / l for _, acc, l in state]


def _attend_guarded(problems, key_sq_max, safe, tk):
    shifts = []
    for (q_t, _), ksq in zip(problems, key_sq_max):
        q32 = q_t.astype(F32)
        qsq = jnp.sum(q32 * q32, axis=0, keepdims=True)
        shifts.append(jnp.sqrt(qsq * ksq) * BOUND_SLACK)
    outs = lax.cond(safe,
                    lambda: tuple(_attend(problems, tk, shifts)),
                    lambda: tuple(_attend(problems, tk, None)))
    return list(outs)


def _row_bcast(a, row, tq):
    rid = lax.broadcasted_iota(jnp.int32, a.shape, 0)
    r = jnp.max(jnp.where(rid == row, a, 0.0), axis=0, keepdims=True)
    return jnp.concatenate([r] * (tq // LANE), axis=1)


def _diff_attn_kernel(*refs, has_ctx, rows_main, rows_ctx, tk, hps, lam_init):
    it = iter(refs)
    safe_ref, q_ref, k_ref, v_ref, ksq_ref = (next(it) for _ in range(5))
    if has_ctx:
        kc_ref, vc_ref = (next(it) for _ in range(2))
    dl_ref, g_ref, out_ref = (next(it) for _ in range(3))
    grp = pl.program_id(1)
    tq = q_ref.shape[1]
    segs_per_tile = LANE // DIFF_DK
    seg = lax.broadcasted_iota(jnp.int32, (LANE, tq), 0) // DIFF_DK
    problems, key_sq_max = [], []
    for j in range(hps):
        tile = (2 * j) // segs_per_tile
        q_tile = q_ref[tile * LANE:(tile + 1) * LANE, :]
        parts = [(k_ref, tile * LANE, v_ref, j * HEAD_DV, rows_main)]
        if has_ctx:
            parts.append((kc_ref, tile * LANE, vc_ref, j * HEAD_DV, rows_ctx))
        for mm in range(2):
            q_m = jnp.where(seg == (2 * j + mm) % segs_per_tile, q_tile, jnp.zeros_like(q_tile))
            problems.append((q_m, parts))
            key_sq_max.append(_row_bcast(ksq_ref[...], 2 * (grp * hps + j) + mm, tq))
    dl = dl_ref[...]
    lam = (jnp.exp(jnp.sum(dl[0:1] * dl[1:2], axis=1, keepdims=True))
           - jnp.exp(jnp.sum(dl[2:3] * dl[3:4], axis=1, keepdims=True)) + lam_init)
    outs = _attend_guarded(problems, key_sq_max, safe_ref[pl.program_id(0), grp] != 0, tk)
    for j in range(hps):
        o = outs[2 * j] - lam * outs[2 * j + 1]
        y = o * _rms_scale(o, DIFF_DV) * g_ref[...]
        out_ref[j * DIFF_DV:(j + 1) * DIFF_DV, :] = (y * (1.0 - lam_init)).astype(BF16)


def _safe_flags(q_sq, k_sq, per_group):
    bound = jnp.sqrt(q_sq * k_sq) * (BOUND_SLACK * BOUND_SLACK)
    ok = (2.0 * bound <= SAFE_EXP_RANGE).reshape(q_sq.shape[0], -1, per_group)
    return jnp.all(ok, axis=-1).astype(jnp.int32)


def _diff_attention(dq_t, dk_tok, dv_t, q_sq, k_sq, ctx, dl, g_col, *, batch, tq, tk, hps, lam_init):
    n_all = dq_t.shape[1]
    n = n_all // batch
    qt = n // tq
    nseg = 2 * DIFF_HEADS
    has_ctx = ctx is not None
    rows_ctx = ctx[0].shape[1] if has_ctx else 0
    assert (hps * 2 * DIFF_DK) % LANE == 0, "a grid step must cover whole 128-lane key tiles"
    qk_w = hps * 2 * DIFF_DK
    safe = _safe_flags(q_sq, k_sq, 2 * hps)
    k_sq_b = jnp.broadcast_to(k_sq[:, :, None], (batch, nseg, LANE))
    in_arrays = [safe, dq_t, dk_tok, dv_t, k_sq_b]
    in_specs = [pl.BlockSpec(memory_space=pltpu.SMEM),
                pl.BlockSpec((qk_w, tq), lambda b, g, i: (g, b * qt + i)),
                pl.BlockSpec((n, qk_w), lambda b, g, i: (b, g)),
                pl.BlockSpec((hps * HEAD_DV, n), lambda b, g, i: (g, b)),
                pl.BlockSpec((None, nseg, LANE), lambda b, g, i: (b, 0, 0))]
    if has_ctx:
        in_arrays += list(ctx)
        in_specs += [pl.BlockSpec((None, rows_ctx, qk_w), lambda b, g, i: (b, 0, g)),
                     pl.BlockSpec((None, hps * HEAD_DV, rows_ctx), lambda b, g, i: (b, g, 0))]
    in_arrays += [dl, g_col]
    in_specs += [_full(dl.shape), _full(g_col.shape)]
    return pl.pallas_call(
        functools.partial(_diff_attn_kernel, has_ctx=has_ctx, rows_main=n, rows_ctx=rows_ctx, tk=tk, hps=hps,
                          lam_init=lam_init),
        grid=(batch, DIFF_HEADS // hps, qt),
        in_specs=in_specs,
        out_specs=pl.BlockSpec((hps * DIFF_DV, tq), lambda b, g, i: (g, b * qt + i)),
        out_shape=jax.ShapeDtypeStruct((DIFF_W, n_all), BF16),
        compiler_params=_cparams(3),
        name="diff_attn_smp" if has_ctx else "diff_attn_ctx",
    )(*in_arrays)


def _mla_attn_kernel(*refs, has_ctx, rows_main, rows_ctx, tk, hps):
    it = iter(refs)
    safe_ref, q_ref, k_ref, v_ref, ksq_ref = (next(it) for _ in range(5))
    if has_ctx:
        kc_ref, vc_ref = (next(it) for _ in range(2))
    out_ref = next(it)
    grp = pl.program_id(1)
    tq = q_ref.shape[1]
    problems, key_sq_max = [], []
    for j in range(hps):
        parts = [(k_ref, j * MLA_PAD, v_ref, j * HEAD_DV, rows_main)]
        if has_ctx:
            parts.append((kc_ref, j * MLA_PAD, vc_ref, j * HEAD_DV, rows_ctx))
        problems.append((q_ref[j * MLA_PAD:(j + 1) * MLA_PAD, :], parts))
        key_sq_max.append(_row_bcast(ksq_ref[...], grp * hps + j, tq))
    outs = _attend_guarded(problems, key_sq_max, safe_ref[pl.program_id(0), grp] != 0, tk)
    for j, o in enumerate(outs):
        out_ref[j * MLA_DV:(j + 1) * MLA_DV, :] = o.astype(BF16)


def _mla_attention(mq_t, mk_tok, mv_t, q_sq, k_sq, ctx, *, batch, tq, tk, hps):
    n_all = mq_t.shape[1]
    n = n_all // batch
    qt = n // tq
    has_ctx = ctx is not None
    rows_ctx = ctx[0].shape[1] if has_ctx else 0
    safe = _safe_flags(q_sq, k_sq, hps)
    k_sq_b = jnp.broadcast_to(k_sq[:, :, None], (batch, MLA_HEADS, LANE))
    in_arrays = [safe, mq_t, mk_tok, mv_t, k_sq_b]
    in_specs = [pl.BlockSpec(memory_space=pltpu.SMEM),
                pl.BlockSpec((hps * MLA_PAD, tq), lambda b, g, i: (g, b * qt + i)),
                pl.BlockSpec((n, hps * MLA_PAD), lambda b, g, i: (b, g)),
                pl.BlockSpec((hps * HEAD_DV, n), lambda b, g, i: (g, b)),
                pl.BlockSpec((None, MLA_HEADS, LANE), lambda b, g, i: (b, 0, 0))]
    if has_ctx:
        in_arrays += list(ctx)
        in_specs += [pl.BlockSpec((None, rows_ctx, hps * MLA_PAD), lambda b, g, i: (b, 0, g)),
                     pl.BlockSpec((None, hps * HEAD_DV, rows_ctx), lambda b, g, i: (b, g, 0))]
    return pl.pallas_call(
        functools.partial(_mla_attn_kernel, has_ctx=has_ctx, rows_main=n, rows_ctx=rows_ctx, tk=tk, hps=hps),
        grid=(batch, MLA_HEADS // hps, qt),
        in_specs=in_specs,
        out_specs=pl.BlockSpec((hps * MLA_DV, tq), lambda b, g, i: (g, b * qt + i)),
        out_shape=jax.ShapeDtypeStruct((MLA_W, n_all), BF16),
        compiler_params=_cparams(3),
        name="mla_attn_smp" if has_ctx else "mla_attn_ctx",
    )(*in_arrays)


def _post_kernel(x_ref, ret_ref, diff_ref, mla_ref, g1_ref, sh2_ref, sc2_ref, g2_ref, n2_ref,
                 w_out_ref, w_gu_ref, w_down_ref, out_ref, *, tm, d_ff, ff_cuts):
    nch = tm // LANE
    ret = jnp.concatenate([ret_ref[j] for j in range(nch)], axis=1)
    mixed = jnp.concatenate([ret, diff_ref[...], mla_ref[...]], axis=0)
    d = x_ref.shape[0]
    halves = [slice(i * (tm // 2), (i + 1) * (tm // 2)) for i in range(2)]
    attn = [jnp.dot(w_out_ref[...], mixed[:, c], preferred_element_type=F32) for c in halves]
    x1 = [x_ref[:, c] + g1_ref[...] * a for c, a in zip(halves, attn)]
    gain2 = n2_ref[...] * (1.0 + sc2_ref[...])
    h = [(v * _rms_scale(v, d) * gain2 + sh2_ref[...]).astype(BF16) for v in x1]
    ffn = None
    for lo, hi in zip((0,) + ff_cuts, ff_cuts + (d_ff,)):
        acts = []
        for i in range(2):
            g = jnp.dot(w_gu_ref[lo:hi, :], h[i], preferred_element_type=F32)
            u = jnp.dot(w_gu_ref[d_ff + lo:d_ff + hi, :], h[i], preferred_element_type=F32)
            acts.append((_silu(g) * u).astype(BF16))
        part = jnp.dot(w_down_ref[:, lo:hi], jnp.concatenate(acts, axis=1), preferred_element_type=F32)
        ffn = part if ffn is None else ffn + part
    out_ref[...] = jnp.concatenate(x1, axis=1) + g2_ref[...] * ffn


def _post(x_t, ret, diff_t, mla_t, mods, lw, *, is_ctx, tm, tiles_per_batch):
    d, n = x_t.shape
    nt = n // tm
    nch = tm // LANE
    d_ff = lw["w_down"].shape[1]
    ff_cuts = ((d_ff // 2) // MXU_DEPTH * MXU_DEPTH,)
    if is_ctx:
        bidx = lambda j: (0, 0, 0)
    else:
        bidx = lambda j: (j // tiles_per_batch, 0, 0)
    col = pl.BlockSpec((None, d, 1), bidx)
    const = lambda a: pl.BlockSpec(a.shape, lambda j: (0,) * a.ndim, pipeline_mode=pl.Buffered(1))
    g1, sh2, sc2, g2 = mods
    return pl.pallas_call(
        functools.partial(_post_kernel, tm=tm, d_ff=d_ff, ff_cuts=ff_cuts),
        grid=(nt,),
        in_specs=[pl.BlockSpec((d, tm), lambda j: (0, j)),
                  pl.BlockSpec((nch, RET_W, LANE), lambda j: (j, 0, 0)),
                  pl.BlockSpec((DIFF_W, tm), lambda j: (0, j)),
                  pl.BlockSpec((MLA_W, tm), lambda j: (0, j)),
                  col, col, col, col, const(lw["n2"]),
                  const(lw["w_out"]), const(lw["w_gu"]), const(lw["w_down"])],
        out_specs=pl.BlockSpec((d, tm), lambda j: (0, j)),
        out_shape=jax.ShapeDtypeStruct((d, n), F32),
        compiler_params=_cparams(1),
        name="post_ctx" if is_ctx else "post_smp",
    )(x_t, ret, diff_t, mla_t, g1, sh2, sc2, g2, lw["n2"], lw["w_out"], lw["w_gu"], lw["w_down"])


def _rope_tables_t(n, rot_dim):
    rows = n // GRID_W
    row = jnp.repeat(jnp.arange(rows, dtype=F32), GRID_W)
    col = jnp.tile(jnp.arange(GRID_W, dtype=F32), rows)
    n_freq = rot_dim // 4
    inv = 1.0 / (ROPE_BASE ** (jnp.arange(n_freq, dtype=F32) / n_freq))
    ang = jnp.concatenate([inv[:, None] * row[None, :], inv[:, None] * col[None, :]], axis=0)
    return jnp.cos(ang), jnp.sin(ang)


def _col(v):
    return v.astype(F32).reshape(-1, 1)


def _pad_heads_rows(w_t, used, pad):
    hk = w_t.shape[0] // used
    w3 = w_t.reshape(hk, used, w_t.shape[1])
    w3 = jnp.pad(w3, ((0, 0), (0, pad - used), (0, 0)))
    return w3.reshape(hk * pad, w_t.shape[1])


def _layer_weights(l, w_in, norm1, norm2, diff_qk_gain, mla_q_norm, mla_kv_norm, w_uq, w_ukv, mla_qk_gain,
                   w_out, w_gu, w_down):
    w_ukv_t = w_ukv[l].T.reshape(MLA_HEADS, MLA_NOPE + MLA_DV, MLA_KV_RANK)
    w_uk = jnp.pad(w_ukv_t[:, :MLA_NOPE], ((0, 0), (0, MLA_PAD - MLA_NOPE), (0, 0)))
    g_pad = lambda g: jnp.tile(jnp.pad(g.astype(F32), (0, MLA_PAD - MLA_DQK)), MLA_HEADS).reshape(-1, 1)
    return {
        "n1": _col(norm1[l]), "n2": _col(norm2[l]),
        "w_in": jnp.concatenate([w_in[l][:, W_IN_MLA0:], w_in[l][:, :W_IN_MLA0]], axis=1).T.astype(BF16),
        "g_dq": _col(jnp.tile(diff_qk_gain[l, 0], 2 * DIFF_HEADS)),
        "g_dk": _col(jnp.tile(diff_qk_gain[l, 1], 2 * DIFF_HEADS)),
        "g_qn": _col(mla_q_norm[l]), "g_kvn": _col(mla_kv_norm[l]),
        "w_uq": _pad_heads_rows(w_uq[l].T, MLA_DQK, MLA_PAD).astype(BF16),
        "g_mq": g_pad(mla_qk_gain[l, 0]), "g_mk": g_pad(mla_qk_gain[l, 1]),
        "w_uk": w_uk.reshape(MLA_QK_PAD_W, MLA_KV_RANK).astype(BF16),
        "w_uv": w_ukv_t[:, MLA_NOPE:].reshape(MLA_W, MLA_KV_RANK).astype(BF16),
        "w_out": w_out[l].T.astype(BF16), "w_gu": w_gu[l].T.astype(BF16), "w_down": w_down[l].T.astype(BF16),
    }


def kernel(x_prompt, x_sample, c, state_ret, cache_diff_k, cache_diff_v, cache_mla_ckv, cache_mla_kr, c_ctx,
           w_mod, b_mod, norm1, norm2, w_in, ret_decay, ret_gn_gain, diff_qk_gain, diff_lambda, diff_subln_gain,
           mla_q_norm, mla_kv_norm, w_uq, w_ukv, mla_qk_gain, w_out, w_gu, w_down):
    depth = w_in.shape[0]
    bp, sp, d = x_prompt.shape
    bs, ss, _ = x_sample.shape
    past = cache_diff_k.shape[2]

    tm = 512
    tq_s, tk_s = 512, 256
    tq_p = tk_p = sp

    n_cond = 1 + bs
    r_pad = -(-n_cond // 16) * 16
    cond = jnp.concatenate([c_ctx[None, :], c, jnp.zeros((r_pad - n_cond, d), F32)], axis=0)
    mod = _modulation(cond, w_mod, b_mod)
    mod = mod.reshape(depth, r_pad, 6, d, 1)

    xp = x_prompt.reshape(bp * sp, d).T
    xs = x_sample.reshape(bs * ss, d).T
    rope = _rope_tables_t(ss, RET_DK) + _rope_tables_t(ss, DIFF_DK)
    dec_b = jnp.broadcast_to(ret_decay.astype(F32).reshape(depth, 2 * RET_HEADS, 1, 1),
                             (depth, 2 * RET_HEADS, 8, LANE))
    s0_zero = jnp.zeros((bp, 2 * RET_HEADS, RET_DV, RET_DK), F32)

    new_ret, new_dk, new_dv, new_ckv, new_kr = [], [], [], [], []
    for l in range(depth):
        lw = _layer_weights(l, w_in, norm1, norm2, diff_qk_gain, mla_q_norm, mla_kv_norm, w_uq, w_ukv,
                            mla_qk_gain, w_out, w_gu, w_down)
        lam_init = 0.8 - 0.6 * math.exp(-0.3 * l)
        gn_col = _col(ret_gn_gain[l])
        subln_col = _col(diff_subln_gain[l])
        dl = diff_lambda[l].astype(F32)

        mc = [mod[l, 0:1, i] for i in range(6)]
        def norm_bound(gain, n, out_scale, batch):
            return jnp.full((batch, 8), n * out_scale * out_scale, F32) * jnp.max(jnp.square(gain.astype(F32)))

        dq_sq = functools.partial(norm_bound, diff_qk_gain[l, 0], DIFF_DK, DIFF_DK ** -0.5 * LOG2E)
        dk_sq = functools.partial(norm_bound, diff_qk_gain[l, 1], DIFF_DK, 1.0)
        mq_sq = functools.partial(norm_bound, mla_qk_gain[l, 0], MLA_DQK, MLA_DQK ** -0.5 * LOG2E)
        mk_sq = functools.partial(norm_bound, mla_qk_gain[l, 1], MLA_DQK, 1.0)

        (rq, rk, rv, rg, dq_t, dk_tok, dv_t, mq_t, mk_tok, mv_t, dk_f, dv_f, ckv_f, kr_f) = _premix(
            xp, mc[0], mc[1], lw, None, is_ctx=True, tm=tm, seq=sp)
        ret, s_fin = _retention(rq, rk, rv, rg, s0_zero, dec_b[l], gn_col, batch=bp)
        diff_t = _diff_attention(dq_t, dk_tok, dv_t, dq_sq(bp), dk_sq(bp), None, dl, subln_col, batch=bp, tq=tq_p,
                                 tk=tk_p, hps=DIFF_HEADS, lam_init=lam_init)
        mla_t = _mla_attention(mq_t, mk_tok, mv_t, mq_sq(bp), mk_sq(bp), None, batch=bp, tq=tq_p, tk=tk_p,
                               hps=MLA_HEADS)
        xp = _post(xp, ret, diff_t, mla_t, (mc[2], mc[3], mc[4], mc[5]), lw, is_ctx=True, tm=tm,
                   tiles_per_batch=1)
        new_ret.append(jnp.swapaxes(s_fin.reshape(bp, 2, RET_HEADS, RET_DV, RET_DK), -1, -2))
        new_dk.append(dk_f.reshape(bp, sp, DIFF_HEADS, 2, DIFF_DK))
        new_dv.append(dv_f.reshape(bp, sp, DIFF_HEADS, DIFF_DV))
        new_ckv.append(ckv_f.reshape(bp, sp, MLA_KV_RANK))
        new_kr.append(kr_f[:, :MLA_ROPE].reshape(bp, sp, MLA_ROPE))

        ms = [mod[l, 1:1 + bs, i] for i in range(6)]
        s0_t = jnp.swapaxes(state_ret[:, l].astype(F32), -1, -2).reshape(bs, 2 * RET_HEADS, RET_DV, RET_DK)
        ctx_dk_f = cache_diff_k[:, l].reshape(bs, past, DIFF_QK_W).astype(F32)
        ctx_dk = ctx_dk_f.astype(BF16)
        ctx_dv_t = jnp.swapaxes(cache_diff_v[:, l].reshape(bs, past, DIFF_W), 1, 2).astype(BF16)
        ctx_mk, ctx_mv_t, ctx_sq = _ctx_mla_kv(
            jnp.swapaxes(cache_mla_ckv[:, l].astype(F32), 1, 2), jnp.swapaxes(cache_mla_kr[:, l].astype(F32), 1, 2),
            jnp.swapaxes(ctx_dk_f, 1, 2), lw["w_uk"], lw["w_uv"], lw["g_mk"])
        ctx_sq = jnp.max(ctx_sq, axis=-1)
        (rq, rk, rv, rg, dq_t, dk_tok, dv_t, mq_t, mk_tok, mv_t) = _premix(
            xs, ms[0], ms[1], lw, rope, is_ctx=False, tm=tm, seq=ss)
        ret, _ = _retention(rq, rk, rv, rg, s0_t, dec_b[l], gn_col, batch=bs)
        diff_t = _diff_attention(dq_t, dk_tok, dv_t, dq_sq(bs), jnp.maximum(dk_sq(bs), ctx_sq[:, 1]),
                                 (ctx_dk, ctx_dv_t), dl, subln_col, batch=bs, tq=tq_s, tk=tk_s, hps=2,
                                 lam_init=lam_init)
        mla_t = _mla_attention(mq_t, mk_tok, mv_t, mq_sq(bs), jnp.maximum(mk_sq(bs), ctx_sq[:, 0]),
                               (ctx_mk, ctx_mv_t), batch=bs, tq=tq_s, tk=tk_s, hps=4)
        xs = _post(xs, ret, diff_t, mla_t, (ms[2], ms[3], ms[4], ms[5]), lw, is_ctx=False, tm=tm,
                   tiles_per_batch=ss // tm)

    y_p = xp.T.reshape(bp, sp, d)
    y_s = xs.T.reshape(bs, ss, d)
    return (y_p, y_s, jnp.stack(new_ret, axis=1), jnp.stack(new_dk, axis=1), jnp.stack(new_dv, axis=1),
            jnp.stack(new_ckv, axis=1), jnp.stack(new_kr, axis=1))
```

```python
import functools
import math

import jax
import jax.numpy as jnp
from jax import lax
from jax.experimental import pallas as pl
from jax.experimental.pallas import tpu as pltpu

F32 = jnp.float32
BF16 = jnp.bfloat16
EPS = 1e-6

GRID_W = 64
RET_HEADS = 4
RET_DK = 64
RET_DV = 64
RET_CHUNK = 128
DIFF_HEADS = 4
DIFF_DK = 32
DIFF_DV = 64
MLA_HEADS = 8
MLA_NOPE = 64
MLA_ROPE = 32
MLA_DQK = MLA_NOPE + MLA_ROPE
MLA_DV = 64
MLA_Q_RANK = 768
MLA_KV_RANK = 256
ROPE_BASE = 10000.0

RET_W = RET_HEADS * RET_DV
DIFF_W = DIFF_HEADS * DIFF_DV
MLA_W = MLA_HEADS * MLA_DV
DIFF_QK_W = DIFF_HEADS * 2 * DIFF_DK
MLA_PAD = 128
MLA_QK_PAD_W = MLA_HEADS * MLA_PAD
assert DIFF_DV == MLA_DV
HEAD_DV = MLA_DV

_SPLITS = (RET_HEADS * RET_DK, RET_HEADS * RET_DK, RET_W, RET_W,
           DIFF_QK_W, DIFF_QK_W, DIFF_W, MLA_Q_RANK, MLA_KV_RANK, MLA_ROPE)
W_IN_MLA0 = int(sum(_SPLITS[:7]))
_PERM_SPLITS = _SPLITS[7:] + _SPLITS[:7]
_OFFS = tuple(int(sum(_PERM_SPLITS[:i])) for i in range(len(_PERM_SPLITS) + 1))
O_CQ, O_CKV, O_KR, O_RQ, O_RK, O_RV, O_RG, O_DQ, O_DK, O_DV, O_END = _OFFS

LOG2E = math.log2(math.e)
SCORE_LOOKAHEAD = 2
SAFE_EXP_RANGE = 100.0
BOUND_SLACK = 1.0 + 2.0 ** -6
LANE = 128
MXU_DEPTH = 256
PREMIX_GROUPS = 2
VMEM_LIMIT = 56 * 1024 * 1024


def _cparams(n_grid):
    return pltpu.CompilerParams(dimension_semantics=("arbitrary",) * n_grid,
                                vmem_limit_bytes=VMEM_LIMIT)


def _full(shape):
    nd = len(shape)
    return pl.BlockSpec(shape, lambda *_: (0,) * nd)


def _silu(x):
    return x / (1.0 + jnp.exp(-x))


def _rms_scale(x, n):
    return lax.rsqrt(jnp.sum(x * x, axis=0, keepdims=True) * (1.0 / n) + EPS)


def _rope_pair(x1, x2, c, s):
    return x1 * c - x2 * s, x2 * c + x1 * s


def _mod_kernel(c_ref, w_ref, b_ref, o_ref):
    a = _silu(c_ref[...]).astype(BF16)
    o_ref[...] = jnp.dot(a, w_ref[...].astype(BF16), preferred_element_type=F32) + b_ref[...]


def _modulation(cond, w_mod, b_mod):
    depth, d, d6 = w_mod.shape
    r = cond.shape[0]
    tn = 1536
    return pl.pallas_call(
        _mod_kernel,
        grid=(depth, d6 // tn),
        in_specs=[pl.BlockSpec((r, d), lambda l, j: (0, 0)),
                  pl.BlockSpec((None, d, tn), lambda l, j: (l, 0, j)),
                  pl.BlockSpec((None, 1, tn), lambda l, j: (l, 0, j))],
        out_specs=pl.BlockSpec((None, r, tn), lambda l, j: (l, 0, j)),
        out_shape=jax.ShapeDtypeStruct((depth, r, d6), F32),
        compiler_params=_cparams(2),
        name="adaln_mod",
    )(cond, w_mod, b_mod.reshape(depth, 1, d6))


def _mla_kv_matmuls(ckvn_bf, w_uk_ref, w_uv_ref):
    return (jnp.dot(w_uk_ref[...], ckvn_bf, preferred_element_type=F32),
            jnp.dot(w_uv_ref[...], ckvn_bf, preferred_element_type=F32))


def _mla_keys(kn, kr, g_mk_ref, rope):
    t = kr.shape[1]
    kr_ss = jnp.sum(kr * kr, axis=0, keepdims=True)
    zpad = jnp.zeros((MLA_PAD - MLA_DQK, t), F32)
    half = MLA_ROPE // 2
    heads = []
    for hd in range(MLA_HEADS):
        r0 = hd * MLA_PAD
        kh = kn[r0:r0 + MLA_NOPE]
        ss = jnp.sum(kh * kh, axis=0, keepdims=True) + kr_ss
        r = lax.rsqrt(ss * (1.0 / MLA_DQK) + EPS)
        g = g_mk_ref[r0:r0 + MLA_PAD, :]
        y_nope = kh * r * g[:MLA_NOPE]
        y_r = kr * r * g[MLA_NOPE:MLA_DQK]
        if rope is not None:
            c, s = rope
            y1, y2 = _rope_pair(y_r[:half], y_r[half:], c, s)
            heads += [y_nope, y1, y2, zpad]
        else:
            heads += [y_nope, y_r, zpad]
    return jnp.concatenate(heads, axis=0)


def _group_sq_norms(a, rows_per_group):
    groups = a.shape[0] // rows_per_group
    return jnp.concatenate([jnp.sum(jnp.square(a[g * rows_per_group:(g + 1) * rows_per_group]), axis=0, keepdims=True)
                            for g in range(groups)], axis=0)


def _lane_tile_max(sq):
    return functools.reduce(jnp.maximum, [sq[:, t * LANE:(t + 1) * LANE] for t in range(sq.shape[1] // LANE)])


def _ctx_kv_kernel(ckv_ref, kr_ref, dk_ref, w_uk_ref, w_uv_ref, g_mk_ref, mk_ref, mv_ref, sq_ref):
    kn, v = _mla_kv_matmuls(ckv_ref[...].astype(BF16), w_uk_ref, w_uv_ref)
    k = _mla_keys(kn, kr_ref[...], g_mk_ref, None)
    mk_ref[...] = k.T.astype(BF16)
    mv_ref[...] = v.astype(BF16)
    sq_ref[0] = _lane_tile_max(_group_sq_norms(k, MLA_PAD))
    sq_ref[1] = _lane_tile_max(_group_sq_norms(dk_ref[...], DIFF_DK))


def _ctx_mla_kv(ckv_t, kr_t, dk_t, w_uk, w_uv, g_mk):
    b, _, l = ckv_t.shape
    return pl.pallas_call(
        _ctx_kv_kernel,
        grid=(b,),
        in_specs=[pl.BlockSpec((None, MLA_KV_RANK, l), lambda i: (i, 0, 0)),
                  pl.BlockSpec((None, MLA_ROPE, l), lambda i: (i, 0, 0)),
                  pl.BlockSpec((None, DIFF_QK_W, l), lambda i: (i, 0, 0)),
                  _full(w_uk.shape), _full(w_uv.shape), _full(g_mk.shape)],
        out_specs=[pl.BlockSpec((None, l, MLA_QK_PAD_W), lambda i: (i, 0, 0)),
                   pl.BlockSpec((None, MLA_W, l), lambda i: (i, 0, 0)),
                   pl.BlockSpec((None, 2, 8, LANE), lambda i: (i, 0, 0, 0))],
        out_shape=[jax.ShapeDtypeStruct((b, l, MLA_QK_PAD_W), BF16),
                   jax.ShapeDtypeStruct((b, MLA_W, l), BF16),
                   jax.ShapeDtypeStruct((b, 2, 8, LANE), F32)],
        compiler_params=_cparams(1),
        name="ctx_mla_kv",
    )(ckv_t, kr_t, dk_t, w_uk, w_uv, g_mk)


def _premix_kernel(*refs, is_ctx, tm, n_aliased):
    it = iter(refs)
    x_ref, shift_ref, scale_ref, n1_ref, w_in_ref = (next(it) for _ in range(5))
    g_dq_ref, g_dk_ref, g_qn_ref, g_kvn_ref = (next(it) for _ in range(4))
    w_uq_ref, g_mq_ref, w_uk_ref, w_uv_ref, g_mk_ref = (next(it) for _ in range(5))
    if not is_ctx:
        cr_ref, sr_ref, cs_ref, ss_ref = (next(it) for _ in range(4))
    for _ in range(n_aliased):
        next(it)
    rq_ref, rk_ref, rv_ref, rg_ref = (next(it) for _ in range(4))
    dq_ref, dk_ref, dv_ref, mq_ref, mk_ref, mv_ref = (next(it) for _ in range(6))
    if is_ctx:
        dkf_ref, dvf_ref, ckvf_ref, krf_ref = (next(it) for _ in range(4))
    d = x_ref.shape[0]

    def token_group(c0, w):
        cols = slice(c0, c0 + w)
        if is_ctx:
            rope_r = rope_s = None
        else:
            rope_r = (cr_ref[:, cols], sr_ref[:, cols])
            rope_s = (cs_ref[:, cols], ss_ref[:, cols])
        x = x_ref[:, cols]
        h = (x * _rms_scale(x, d) * (n1_ref[...] * (1.0 + scale_ref[...])) + shift_ref[...]).astype(BF16)
        yield

        def put_chunks(ref, val):
            for j in range(w // LANE):
                ref[c0 // LANE + j] = val[:, j * LANE:(j + 1) * LANE].astype(ref.dtype)

        def proj(lo, hi):
            return jnp.dot(w_in_ref[lo:hi, :], h, preferred_element_type=F32)

        p_mla = proj(O_CQ, O_RQ)
        cq, ckv, kr = p_mla[O_CQ:O_CKV], p_mla[O_CKV:O_KR], p_mla[O_KR:O_RQ]
        p_ret = proj(O_RQ, O_DQ)
        rq, rk, rv, rg = (p_ret[o - O_RQ:o - O_RQ + RET_W] for o in (O_RQ, O_RK, O_RV, O_RG))
        yield
        cqn = (cq * _rms_scale(cq, MLA_Q_RANK) * g_qn_ref[...]).astype(BF16)
        ckvn = ckv * _rms_scale(ckv, MLA_KV_RANK) * g_kvn_ref[...]
        mq = jnp.dot(w_uq_ref[...], cqn, preferred_element_type=F32)
        kn, mv = _mla_kv_matmuls(ckvn.astype(BF16), w_uk_ref, w_uv_ref)
        p_diff = proj(O_DQ, O_END)
        dq_raw, dk_raw, dv = (p_diff[o - O_DQ:o - O_DQ + DIFF_QK_W] for o in (O_DQ, O_DK, O_DV))
        yield

        if rope_r is not None:
            c, s = rope_r
            hk = RET_DK // 2

            def rope_heads(a):
                parts = []
                for hd in range(RET_HEADS):
                    r0 = hd * RET_DK
                    parts += list(_rope_pair(a[r0:r0 + hk], a[r0 + hk:r0 + RET_DK], c, s))
                return jnp.concatenate(parts, axis=0)

            rq = rope_heads(rq)
            rk = rope_heads(rk)
        put_chunks(rq_ref, rq)
        put_chunks(rk_ref, rk * (RET_DK ** -0.5))
        put_chunks(rv_ref, rv)
        put_chunks(rg_ref, rg)

        def diff_qk(a, g_ref, out_scale):
            parts = []
            hs = DIFF_DK // 2
            for seg in range(2 * DIFF_HEADS):
                r0 = seg * DIFF_DK
                xs = a[r0:r0 + DIFF_DK]
                y = xs * _rms_scale(xs, DIFF_DK) * g_ref[r0:r0 + DIFF_DK, :]
                if out_scale != 1.0:
                    y = y * out_scale
                if rope_s is not None:
                    parts += list(_rope_pair(y[:hs], y[hs:], rope_s[0], rope_s[1]))
                else:
                    parts.append(y)
            return jnp.concatenate(parts, axis=0)

        dq = diff_qk(dq_raw, g_dq_ref, DIFF_DK ** -0.5 * LOG2E)
        dq_ref[:, cols] = dq.astype(BF16)
        dk = diff_qk(dk_raw, g_dk_ref, 1.0)
        dk_t = dk.T
        dk_ref[cols, :] = dk_t.astype(BF16)
        dv_ref[:, cols] = dv.astype(BF16)
        if is_ctx:
            dkf_ref[c0 // w] = dk_t
            dvf_ref[c0 // w] = dv.T

        half = MLA_ROPE // 2
        parts = []
        for hd in range(MLA_HEADS):
            r0 = hd * MLA_PAD
            xs = mq[r0:r0 + MLA_PAD]
            y = xs * _rms_scale(xs, MLA_DQK) * (g_mq_ref[r0:r0 + MLA_PAD, :] * (MLA_DQK ** -0.5 * LOG2E))
            if rope_s is not None:
                y1, y2 = _rope_pair(y[MLA_NOPE:MLA_NOPE + half], y[MLA_NOPE + half:MLA_DQK], rope_s[0], rope_s[1])
                parts += [y[:MLA_NOPE], y1, y2, y[MLA_DQK:]]
            else:
                parts.append(y)
        mq_ref[:, cols] = jnp.concatenate(parts, axis=0).astype(BF16)

        mk = _mla_keys(kn, kr, g_mk_ref, rope_s)
        mk_ref[cols, :] = mk.T.astype(BF16)
        mv_ref[:, cols] = mv.astype(BF16)
        if is_ctx:
            ckvf_ref[c0 // w] = ckvn.T
            krf_ref[c0 // w] = jnp.concatenate([kr, jnp.zeros((LANE - MLA_ROPE, w), F32)], axis=0).T
        yield

    groups = [token_group(i * (tm // PREMIX_GROUPS), tm // PREMIX_GROUPS) for i in range(PREMIX_GROUPS)]
    for _ in range(4):
        for g in groups:
            next(g)


def _premix(x_t, shift, scale, lw, rope, *, is_ctx, tm, seq, layer=0, depth=1, stacked=None):
    d, n = x_t.shape
    nt = n // tm
    nch = tm // LANE
    tiles_per_batch = max(1, seq // tm)
    if is_ctx:
        bidx = lambda j: (0, 0, 0)
    else:
        bidx = lambda j: (j // tiles_per_batch, 0, 0)
    in_arrays = [x_t, shift, scale, lw["n1"], lw["w_in"], lw["g_dq"], lw["g_dk"], lw["g_qn"], lw["g_kvn"],
                 lw["w_uq"], lw["g_mq"], lw["w_uk"], lw["w_uv"], lw["g_mk"]]
    in_specs = [pl.BlockSpec((d, tm), lambda j: (0, j)),
                pl.BlockSpec((None, d, 1), bidx), pl.BlockSpec((None, d, 1), bidx)]
    in_specs += [_full(a.shape) for a in in_arrays[3:]]
    if not is_ctx:
        for tab in rope:
            in_arrays.append(tab)
            in_specs.append(pl.BlockSpec((tab.shape[0], tm), lambda j: (0, j % tiles_per_batch)))

    chunk_spec = pl.BlockSpec((nch, RET_W, LANE), lambda j: (j, 0, 0))
    fm = lambda rows: pl.BlockSpec((rows, tm), lambda j: (0, j))
    tok = lambda cols: pl.BlockSpec((tm, cols), lambda j: (j, 0))
    nck = n // LANE
    out_specs = [chunk_spec] * 4 + [fm(DIFF_QK_W), tok(DIFF_QK_W), fm(DIFF_W),
                                    fm(MLA_QK_PAD_W), tok(MLA_QK_PAD_W), fm(MLA_W)]
    out_shape = [jax.ShapeDtypeStruct((nck, RET_W, LANE), BF16), jax.ShapeDtypeStruct((nck, RET_W, LANE), F32),
                 jax.ShapeDtypeStruct((nck, RET_W, LANE), BF16), jax.ShapeDtypeStruct((nck, RET_W, LANE), F32),
                 jax.ShapeDtypeStruct((DIFF_QK_W, n), BF16), jax.ShapeDtypeStruct((n, DIFF_QK_W), BF16),
                 jax.ShapeDtypeStruct((DIFF_W, n), BF16),
                 jax.ShapeDtypeStruct((MLA_QK_PAD_W, n), BF16), jax.ShapeDtypeStruct((n, MLA_QK_PAD_W), BF16),
                 jax.ShapeDtypeStruct((MLA_W, n), BF16)]
    aliases = {}
    if is_ctx:
        assert tm // PREMIX_GROUPS == seq, "each token group of a context tile must be one request"
        for cols in (DIFF_QK_W, DIFF_W, MLA_KV_RANK, LANE):
            out_specs.append(pl.BlockSpec((tm // seq, None, seq, cols), lambda j: (j, layer, 0, 0)))
            out_shape.append(jax.ShapeDtypeStruct((n // seq, depth, seq, cols), F32))
        if stacked is not None:
            n_out = len(out_shape)
            for k, arr in enumerate(stacked):
                aliases[len(in_arrays)] = n_out - len(stacked) + k
                in_arrays.append(arr)
                in_specs.append(pl.BlockSpec(memory_space=pl.ANY))
    return pl.pallas_call(
        functools.partial(_premix_kernel, is_ctx=is_ctx, tm=tm, n_aliased=len(aliases)),
        grid=(nt,),
        in_specs=in_specs, out_specs=out_specs, out_shape=out_shape,
        input_output_aliases=aliases,
        compiler_params=_cparams(1),
        name="premix_ctx" if is_ctx else "premix_smp",
    )(*in_arrays)


def _retention_kernel(rq_ref, rk_ref, rv_ref, rg_ref, s0_ref, dec_ref, gn_ref, out_ref, sfin_ref,
                      o_acc, s_acc, intra_ref, vec_ref, *, nc):
    C = RET_CHUNK

    @pl.when(pl.program_id(0) == 0)
    def _():
        n_idx = lax.broadcasted_iota(jnp.int32, (C, C), 1).astype(F32)
        m_idx = lax.broadcasted_iota(jnp.int32, (C, C), 0).astype(F32)
        lane = lax.broadcasted_iota(jnp.int32, (8, C), 1).astype(F32)
        for d in range(2):
            for hd in range(RET_HEADS):
                i = d * RET_HEADS + hd
                z = dec_ref[i]
                lg8 = jnp.minimum(z, 0.0) - jnp.log1p(jnp.exp(-jnp.abs(z)))
                lg = jnp.broadcast_to(lg8[0:1, :], (C, C))
                dist = (n_idx - m_idx) if d == 0 else (m_idx - n_idx)
                ok = dist >= 0.0
                intra_ref[i] = jnp.where(ok, jnp.exp(lg * jnp.where(ok, dist, 0.0)), 0.0)
                if d == 0:
                    qdec = jnp.exp(lg8 * (lane + 1.0))
                    kdec = jnp.exp(lg8 * (C - 1.0 - lane))
                else:
                    qdec = jnp.exp(lg8 * (C - lane))
                    kdec = jnp.exp(lg8 * lane)
                vec_ref[i, 0] = qdec
                vec_ref[i, 1] = kdec
                vec_ref[i, 2] = jnp.exp(lg8 * float(C))

    s_acc[...] = s0_ref[...]
    o_acc[...] = jnp.zeros_like(o_acc)

    row_head = lax.broadcasted_iota(jnp.int32, (RET_HEADS * RET_DK, C), 0) // RET_DK

    def step(c, carry):
        first, vs = [], []
        for d in range(2):
            cc = c if d == 0 else nc - 1 - c
            q_all = rq_ref[cc]
            k_all = rk_ref[cc]
            k_tok = k_all.T.astype(BF16)
            q_bd = jnp.concatenate([jnp.where(row_head == hd, q_all, jnp.zeros_like(q_all))
                                    for hd in range(RET_HEADS)], axis=1)
            att_all = jnp.dot(k_tok, q_bd, preferred_element_type=F32)
            for hd in range(RET_HEADS):
                i = d * RET_HEADS + hd
                r0 = hd * RET_DK
                q = q_all[r0:r0 + RET_DK]
                k = k_all[r0:r0 + RET_DK]
                v = rv_ref[cc, r0:r0 + RET_DV, :]
                att_t = att_all[:, hd * C:(hd + 1) * C]
                s_t = s_acc[i]
                cross = jnp.dot(s_t.astype(BF16), q, preferred_element_type=F32)
                kd = (k * vec_ref[i, 1][0:1, :]).astype(BF16)
                kv = lax.dot_general(v, kd, (((1,), (1,)), ((), ())), preferred_element_type=F32)
                s_acc[i] = s_t * vec_ref[i, 2][0:1, 0:RET_DK] + kv
                first.append((att_t, cross))
                vs.append((i, cc, r0, v))
        masked = [(att_t * intra_ref[i]).astype(BF16) for (att_t, _), (i, _, _, _) in zip(first, vs)]
        for (_, cross), att_m, (i, cc, r0, v) in zip(first, masked, vs):
            o = jnp.dot(v, att_m, preferred_element_type=F32) + cross * vec_ref[i, 0][0:1, :]
            o_acc[cc, r0:r0 + RET_DV, :] += o
        return carry

    lax.fori_loop(0, nc, step, 0, unroll=min(nc, 8))
    sfin_ref[...] = s_acc[...]

    def finish(c, carry):
        o = o_acc[c]
        g = rg_ref[c]
        parts = []
        for hd in range(RET_HEADS):
            r0 = hd * RET_DV
            oh = o[r0:r0 + RET_DV]
            mu = jnp.mean(oh, axis=0, keepdims=True)
            var = jnp.mean(jnp.square(oh - mu), axis=0, keepdims=True)
            y = (oh - mu) * lax.rsqrt(var + EPS) * gn_ref[r0:r0 + RET_DV, :]
            parts.append(y * _silu(g[r0:r0 + RET_DV]))
        out_ref[c] = jnp.concatenate(parts, axis=0).astype(BF16)
        return carry

    lax.fori_loop(0, nc, finish, 0, unroll=min(nc, 4))


def _retention(rq, rk, rv, rg, s0_t, dec_b, gn_col, *, batch):
    nck = rq.shape[0]
    nc = nck // batch
    blk = pl.BlockSpec((nc, RET_W, LANE), lambda b: (b, 0, 0))
    nst = 2 * RET_HEADS
    st_spec = pl.BlockSpec((None, nst, RET_DV, RET_DK), lambda b: (b, 0, 0, 0))
    return pl.pallas_call(
        functools.partial(_retention_kernel, nc=nc),
        grid=(batch,),
        in_specs=[blk, blk, blk, blk, st_spec, _full(dec_b.shape), _full(gn_col.shape)],
        out_specs=[blk, st_spec],
        out_shape=[jax.ShapeDtypeStruct((nck, RET_W, LANE), BF16),
                   jax.ShapeDtypeStruct((batch, nst, RET_DV, RET_DK), F32)],
        scratch_shapes=[pltpu.VMEM((nc, RET_W, LANE), F32),
                        pltpu.VMEM((nst, RET_DV, RET_DK), F32),
                        pltpu.VMEM((nst, RET_CHUNK, RET_CHUNK), F32),
                        pltpu.VMEM((nst, 3, 8, LANE), F32)],
        compiler_params=_cparams(1),
        name="retention",
    )(rq, rk, rv, rg, s0_t, dec_b, gn_col)


def _attend(problems, tk, shifts=None):
    flat = [(pi, part, r0, min(tk, part[4])) for pi, (_, parts) in enumerate(problems) for part in parts
            for r0 in range(0, part[4], min(tk, part[4]))]

    def scores(item):
        pi, (k_ref, c0, _, _, _), r0, nk = item
        q_t = problems[pi][0]
        return jnp.dot(k_ref[r0:r0 + nk, c0:c0 + q_t.shape[0]], q_t, preferred_element_type=F32)

    state = [(jnp.full((1, q_t.shape[1]), -jnp.inf, F32), jnp.zeros((HEAD_DV, q_t.shape[1]), F32),
              jnp.zeros((1, q_t.shape[1]), F32)) for q_t, _ in problems]
    pending = [scores(it) for it in flat[:SCORE_LOOKAHEAD]]
    for i, (pi, (_, _, v_ref, v0, _), r0, nk) in enumerate(flat):
        if i + SCORE_LOOKAHEAD < len(flat):
            pending.append(scores(flat[i + SCORE_LOOKAHEAD]))
        s = pending.pop(0)
        m, acc, l = state[pi]
        if shifts is None:
            m_new = jnp.maximum(m, jnp.max(s, axis=0, keepdims=True))
            alpha = jnp.exp2(m - m_new)
            acc, l = acc * alpha, l * alpha
        else:
            m_new = shifts[pi]
        p = jnp.exp2(s - m_new)
        l = l + jnp.sum(p, axis=0, keepdims=True)
        pv = jnp.dot(v_ref[v0:v0 + HEAD_DV, r0:r0 + nk], p.astype(BF16), preferred_element_type=F32)
        state[pi] = (m_new, acc + pv, l)
    return [acc / l for _, acc, l in state]


def _attend_guarded(problems, key_sq_max, safe, tk):
    shifts = []
    for (q_t, _), ksq in zip(problems, key_sq_max):
        q32 = q_t.astype(F32)
        qsq = jnp.sum(q32 * q32, axis=0, keepdims=True)
        shifts.append(jnp.sqrt(qsq * ksq) * BOUND_SLACK)
    outs = lax.cond(safe,
                    lambda: tuple(_attend(problems, tk, shifts)),
                    lambda: tuple(_attend(problems, tk, None)))
    return list(outs)


def _row_bcast(a, row, tq):
    rid = lax.broadcasted_iota(jnp.int32, a.shape, 0)
    r = jnp.max(jnp.where(rid == row, a, 0.0), axis=0, keepdims=True)
    return jnp.concatenate([r] * (tq // LANE), axis=1)


def _diff_attn_kernel(*refs, has_ctx, rows_main, rows_ctx, tk, hps, lam_init):
    it = iter(refs)
    safe_ref, q_ref, k_ref, v_ref, ksq_ref = (next(it) for _ in range(5))
    if has_ctx:
        kc_ref, vc_ref = (next(it) for _ in range(2))
    dl_ref, g_ref, out_ref = (next(it) for _ in range(3))
    grp = pl.program_id(1)
    tq = q_ref.shape[1]
    segs_per_tile = LANE // DIFF_DK
    seg = lax.broadcasted_iota(jnp.int32, (LANE, tq), 0) // DIFF_DK
    problems, key_sq_max = [], []
    for j in range(hps):
        tile = (2 * j) // segs_per_tile
        q_tile = q_ref[tile * LANE:(tile + 1) * LANE, :]
        parts = [(k_ref, tile * LANE, v_ref, j * HEAD_DV, rows_main)]
        if has_ctx:
            parts.append((kc_ref, tile * LANE, vc_ref, j * HEAD_DV, rows_ctx))
        for mm in range(2):
            q_m = jnp.where(seg == (2 * j + mm) % segs_per_tile, q_tile, jnp.zeros_like(q_tile))
            problems.append((q_m, parts))
            key_sq_max.append(_row_bcast(ksq_ref[...], 2 * (grp * hps + j) + mm, tq))
    dl = dl_ref[...]
    lam = (jnp.exp(jnp.sum(dl[0:1] * dl[1:2], axis=1, keepdims=True))
           - jnp.exp(jnp.sum(dl[2:3] * dl[3:4], axis=1, keepdims=True)) + lam_init)
    outs = _attend_guarded(problems, key_sq_max, safe_ref[pl.program_id(0), grp] != 0, tk)
    for j in range(hps):
        o = outs[2 * j] - lam * outs[2 * j + 1]
        y = o * _rms_scale(o, DIFF_DV) * g_ref[...]
        out_ref[j * DIFF_DV:(j + 1) * DIFF_DV, :] = (y * (1.0 - lam_init)).astype(BF16)


def _safe_flags(q_sq, k_sq, per_group):
    bound = jnp.sqrt(q_sq * k_sq) * (BOUND_SLACK * BOUND_SLACK)
    ok = (2.0 * bound <= SAFE_EXP_RANGE).reshape(q_sq.shape[0], -1, per_group)
    return jnp.all(ok, axis=-1).astype(jnp.int32)


def _diff_attention(dq_t, dk_tok, dv_t, q_sq, k_sq, ctx, dl, g_col, *, batch, tq, tk, hps, lam_init):
    n_all = dq_t.shape[1]
    n = n_all // batch
    qt = n // tq
    nseg = 2 * DIFF_HEADS
    has_ctx = ctx is not None
    rows_ctx = ctx[0].shape[1] if has_ctx else 0
    assert (hps * 2 * DIFF_DK) % LANE == 0, "a grid step must cover whole 128-lane key tiles"
    qk_w = hps * 2 * DIFF_DK
    safe = _safe_flags(q_sq, k_sq, 2 * hps)
    k_sq_b = jnp.broadcast_to(k_sq[:, :, None], (batch, nseg, LANE))
    in_arrays = [safe, dq_t, dk_tok, dv_t, k_sq_b]
    in_specs = [pl.BlockSpec(memory_space=pltpu.SMEM),
                pl.BlockSpec((qk_w, tq), lambda b, g, i: (g, b * qt + i)),
                pl.BlockSpec((n, qk_w), lambda b, g, i: (b, g)),
                pl.BlockSpec((hps * HEAD_DV, n), lambda b, g, i: (g, b)),
                pl.BlockSpec((None, nseg, LANE), lambda b, g, i: (b, 0, 0))]
    if has_ctx:
        in_arrays += list(ctx)
        in_specs += [pl.BlockSpec((None, rows_ctx, qk_w), lambda b, g, i: (b, 0, g)),
                     pl.BlockSpec((None, hps * HEAD_DV, rows_ctx), lambda b, g, i: (b, g, 0))]
    in_arrays += [dl, g_col]
    in_specs += [_full(dl.shape), _full(g_col.shape)]
    return pl.pallas_call(
        functools.partial(_diff_attn_kernel, has_ctx=has_ctx, rows_main=n, rows_ctx=rows_ctx, tk=tk, hps=hps,
                          lam_init=lam_init),
        grid=(batch, DIFF_HEADS // hps, qt),
        in_specs=in_specs,
        out_specs=pl.BlockSpec((hps * DIFF_DV, tq), lambda b, g, i: (g, b * qt + i)),
        out_shape=jax.ShapeDtypeStruct((DIFF_W, n_all), BF16),
        compiler_params=_cparams(3),
        name="diff_attn_smp" if has_ctx else "diff_attn_ctx",
    )(*in_arrays)


def _mla_attn_kernel(*refs, has_ctx, rows_main, rows_ctx, tk, hps):
    it = iter(refs)
    safe_ref, q_ref, k_ref, v_ref, ksq_ref = (next(it) for _ in range(5))
    if has_ctx:
        kc_ref, vc_ref = (next(it) for _ in range(2))
    out_ref = next(it)
    grp = pl.program_id(1)
    tq = q_ref.shape[1]
    problems, key_sq_max = [], []
    for j in range(hps):
        parts = [(k_ref, j * MLA_PAD, v_ref, j * HEAD_DV, rows_main)]
        if has_ctx:
            parts.append((kc_ref, j * MLA_PAD, vc_ref, j * HEAD_DV, rows_ctx))
        problems.append((q_ref[j * MLA_PAD:(j + 1) * MLA_PAD, :], parts))
        key_sq_max.append(_row_bcast(ksq_ref[...], grp * hps + j, tq))
    outs = _attend_guarded(problems, key_sq_max, safe_ref[pl.program_id(0), grp] != 0, tk)
    for j, o in enumerate(outs):
        out_ref[j * MLA_DV:(j + 1) * MLA_DV, :] = o.astype(BF16)


def _mla_attention(mq_t, mk_tok, mv_t, q_sq, k_sq, ctx, *, batch, tq, tk, hps):
    n_all = mq_t.shape[1]
    n = n_all // batch
    qt = n // tq
    has_ctx = ctx is not None
    rows_ctx = ctx[0].shape[1] if has_ctx else 0
    safe = _safe_flags(q_sq, k_sq, hps)
    k_sq_b = jnp.broadcast_to(k_sq[:, :, None], (batch, MLA_HEADS, LANE))
    in_arrays = [safe, mq_t, mk_tok, mv_t, k_sq_b]
    in_specs = [pl.BlockSpec(memory_space=pltpu.SMEM),
                pl.BlockSpec((hps * MLA_PAD, tq), lambda b, g, i: (g, b * qt + i)),
                pl.BlockSpec((n, hps * MLA_PAD), lambda b, g, i: (b, g)),
                pl.BlockSpec((hps * HEAD_DV, n), lambda b, g, i: (g, b)),
                pl.BlockSpec((None, MLA_HEADS, LANE), lambda b, g, i: (b, 0, 0))]
    if has_ctx:
        in_arrays += list(ctx)
        in_specs += [pl.BlockSpec((None, rows_ctx, hps * MLA_PAD), lambda b, g, i: (b, 0, g)),
                     pl.BlockSpec((None, hps * HEAD_DV, rows_ctx), lambda b, g, i: (b, g, 0))]
    return pl.pallas_call(
        functools.partial(_mla_attn_kernel, has_ctx=has_ctx, rows_main=n, rows_ctx=rows_ctx, tk=tk, hps=hps),
        grid=(batch, MLA_HEADS // hps, qt),
        in_specs=in_specs,
        out_specs=pl.BlockSpec((hps * MLA_DV, tq), lambda b, g, i: (g, b * qt + i)),
        out_shape=jax.ShapeDtypeStruct((MLA_W, n_all), BF16),
        compiler_params=_cparams(3),
        name="mla_attn_smp" if has_ctx else "mla_attn_ctx",
    )(*in_arrays)


def _post_kernel(x_ref, ret_ref, diff_ref, mla_ref, g1_ref, sh2_ref, sc2_ref, g2_ref, n2_ref,
                 w_out_ref, w_gu_ref, w_down_ref, out_ref, *, tm, d_ff, ff_cuts):
    nch = tm // LANE
    ret = jnp.concatenate([ret_ref[j] for j in range(nch)], axis=1)
    mixed = jnp.concatenate([ret, diff_ref[...], mla_ref[...]], axis=0)
    d = x_ref.shape[0]
    halves = [slice(i * (tm // 2), (i + 1) * (tm // 2)) for i in range(2)]
    attn = [jnp.dot(w_out_ref[...], mixed[:, c], preferred_element_type=F32) for c in halves]
    x1 = [x_ref[:, c] + g1_ref[...] * a for c, a in zip(halves, attn)]
    gain2 = n2_ref[...] * (1.0 + sc2_ref[...])
    h = [(v * _rms_scale(v, d) * gain2 + sh2_ref[...]).astype(BF16) for v in x1]
    ffn = None
    for lo, hi in zip((0,) + ff_cuts, ff_cuts + (d_ff,)):
        acts = []
        for i in range(2):
            g = jnp.dot(w_gu_ref[lo:hi, :], h[i], preferred_element_type=F32)
            u = jnp.dot(w_gu_ref[d_ff + lo:d_ff + hi, :], h[i], preferred_element_type=F32)
            acts.append((_silu(g) * u).astype(BF16))
        part = jnp.dot(w_down_ref[:, lo:hi], jnp.concatenate(acts, axis=1), preferred_element_type=F32)
        ffn = part if ffn is None else ffn + part
    out_ref[...] = jnp.concatenate(x1, axis=1) + g2_ref[...] * ffn


def _post(x_t, ret, diff_t, mla_t, mods, lw, *, is_ctx, tm, tiles_per_batch):
    d, n = x_t.shape
    nt = n // tm
    nch = tm // LANE
    d_ff = lw["w_down"].shape[1]
    ff_cuts = ((d_ff // 2) // MXU_DEPTH * MXU_DEPTH,)
    if is_ctx:
        bidx = lambda j: (0, 0, 0)
    else:
        bidx = lambda j: (j // tiles_per_batch, 0, 0)
    col = pl.BlockSpec((None, d, 1), bidx)
    const = lambda a: pl.BlockSpec(a.shape, lambda j: (0,) * a.ndim, pipeline_mode=pl.Buffered(1))
    g1, sh2, sc2, g2 = mods
    return pl.pallas_call(
        functools.partial(_post_kernel, tm=tm, d_ff=d_ff, ff_cuts=ff_cuts),
        grid=(nt,),
        in_specs=[pl.BlockSpec((d, tm), lambda j: (0, j)),
                  pl.BlockSpec((nch, RET_W, LANE), lambda j: (j, 0, 0)),
                  pl.BlockSpec((DIFF_W, tm), lambda j: (0, j)),
                  pl.BlockSpec((MLA_W, tm), lambda j: (0, j)),
                  col, col, col, col, const(lw["n2"]),
                  const(lw["w_out"]), const(lw["w_gu"]), const(lw["w_down"])],
        out_specs=pl.BlockSpec((d, tm), lambda j: (0, j)),
        out_shape=jax.ShapeDtypeStruct((d, n), F32),
        compiler_params=_cparams(1),
        name="post_ctx" if is_ctx else "post_smp",
    )(x_t, ret, diff_t, mla_t, g1, sh2, sc2, g2, lw["n2"], lw["w_out"], lw["w_gu"], lw["w_down"])


def _rope_tables_t(n, rot_dim):
    rows = n // GRID_W
    row = jnp.repeat(jnp.arange(rows, dtype=F32), GRID_W)
    col = jnp.tile(jnp.arange(GRID_W, dtype=F32), rows)
    n_freq = rot_dim // 4
    inv = 1.0 / (ROPE_BASE ** (jnp.arange(n_freq, dtype=F32) / n_freq))
    ang = jnp.concatenate([inv[:, None] * row[None, :], inv[:, None] * col[None, :]], axis=0)
    return jnp.cos(ang), jnp.sin(ang)


def _col(v):
    return v.astype(F32).reshape(-1, 1)


def _pad_heads_rows(w_t, used, pad):
    hk = w_t.shape[0] // used
    w3 = w_t.reshape(hk, used, w_t.shape[1])
    w3 = jnp.pad(w3, ((0, 0), (0, pad - used), (0, 0)))
    return w3.reshape(hk * pad, w_t.shape[1])


def _layer_weights(l, w_in, norm1, norm2, diff_qk_gain, mla_q_norm, mla_kv_norm, w_uq, w_ukv, mla_qk_gain,
                   w_out, w_gu, w_down):
    w_ukv_t = w_ukv[l].T.reshape(MLA_HEADS, MLA_NOPE + MLA_DV, MLA_KV_RANK)
    w_uk = jnp.pad(w_ukv_t[:, :MLA_NOPE], ((0, 0), (0, MLA_PAD - MLA_NOPE), (0, 0)))
    g_pad = lambda g: jnp.tile(jnp.pad(g.astype(F32), (0, MLA_PAD - MLA_DQK)), MLA_HEADS).reshape(-1, 1)
    return {
        "n1": _col(norm1[l]), "n2": _col(norm2[l]),
        "w_in": jnp.concatenate([w_in[l][:, W_IN_MLA0:], w_in[l][:, :W_IN_MLA0]], axis=1).T.astype(BF16),
        "g_dq": _col(jnp.tile(diff_qk_gain[l, 0], 2 * DIFF_HEADS)),
        "g_dk": _col(jnp.tile(diff_qk_gain[l, 1], 2 * DIFF_HEADS)),
        "g_qn": _col(mla_q_norm[l]), "g_kvn": _col(mla_kv_norm[l]),
        "w_uq": _pad_heads_rows(w_uq[l].T, MLA_DQK, MLA_PAD).astype(BF16),
        "g_mq": g_pad(mla_qk_gain[l, 0]), "g_mk": g_pad(mla_qk_gain[l, 1]),
        "w_uk": w_uk.reshape(MLA_QK_PAD_W, MLA_KV_RANK).astype(BF16),
        "w_uv": w_ukv_t[:, MLA_NOPE:].reshape(MLA_W, MLA_KV_RANK).astype(BF16),
        "w_out": w_out[l].T.astype(BF16), "w_gu": w_gu[l].T.astype(BF16), "w_down": w_down[l].T.astype(BF16),
    }


def kernel(x_prompt, x_sample, c, state_ret, cache_diff_k, cache_diff_v, cache_mla_ckv, cache_mla_kr, c_ctx,
           w_mod, b_mod, norm1, norm2, w_in, ret_decay, ret_gn_gain, diff_qk_gain, diff_lambda, diff_subln_gain,
           mla_q_norm, mla_kv_norm, w_uq, w_ukv, mla_qk_gain, w_out, w_gu, w_down):
    depth = w_in.shape[0]
    bp, sp, d = x_prompt.shape
    bs, ss, _ = x_sample.shape
    past = cache_diff_k.shape[2]

    tm = 512
    tq_s, tk_s = 512, 256
    tq_p = tk_p = sp

    n_cond = 1 + bs
    r_pad = -(-n_cond // 16) * 16
    cond = jnp.concatenate([c_ctx[None, :], c, jnp.zeros((r_pad - n_cond, d), F32)], axis=0)
    mod = _modulation(cond, w_mod, b_mod)
    mod = mod.reshape(depth, r_pad, 6, d, 1)

    xp = x_prompt.reshape(bp * sp, d).T
    xs = x_sample.reshape(bs * ss, d).T
    rope = _rope_tables_t(ss, RET_DK) + _rope_tables_t(ss, DIFF_DK)
    dec_b = jnp.broadcast_to(ret_decay.astype(F32).reshape(depth, 2 * RET_HEADS, 1, 1),
                             (depth, 2 * RET_HEADS, 8, LANE))
    s0_zero = jnp.zeros((bp, 2 * RET_HEADS, RET_DV, RET_DK), F32)

    new_ret = []
    ctx_out = None
    for l in range(depth):
        lw = _layer_weights(l, w_in, norm1, norm2, diff_qk_gain, mla_q_norm, mla_kv_norm, w_uq, w_ukv,
                            mla_qk_gain, w_out, w_gu, w_down)
        lam_init = 0.8 - 0.6 * math.exp(-0.3 * l)
        gn_col = _col(ret_gn_gain[l])
        subln_col = _col(diff_subln_gain[l])
        dl = diff_lambda[l].astype(F32)

        mc = [mod[l, 0:1, i] for i in range(6)]
        def norm_bound(gain, n, out_scale, batch):
            return jnp.full((batch, 8), n * out_scale * out_scale, F32) * jnp.max(jnp.square(gain.astype(F32)))

        dq_sq = functools.partial(norm_bound, diff_qk_gain[l, 0], DIFF_DK, DIFF_DK ** -0.5 * LOG2E)
        dk_sq = functools.partial(norm_bound, diff_qk_gain[l, 1], DIFF_DK, 1.0)
        mq_sq = functools.partial(norm_bound, mla_qk_gain[l, 0], MLA_DQK, MLA_DQK ** -0.5 * LOG2E)
        mk_sq = functools.partial(norm_bound, mla_qk_gain[l, 1], MLA_DQK, 1.0)

        (rq, rk, rv, rg, dq_t, dk_tok, dv_t, mq_t, mk_tok, mv_t, *ctx_out) = _premix(
            xp, mc[0], mc[1], lw, None, is_ctx=True, tm=tm, seq=sp, layer=l, depth=depth, stacked=ctx_out)
        ret, s_fin = _retention(rq, rk, rv, rg, s0_zero, dec_b[l], gn_col, batch=bp)
        diff_t = _diff_attention(dq_t, dk_tok, dv_t, dq_sq(bp), dk_sq(bp), None, dl, subln_col, batch=bp, tq=tq_p,
                                 tk=tk_p, hps=DIFF_HEADS, lam_init=lam_init)
        mla_t = _mla_attention(mq_t, mk_tok, mv_t, mq_sq(bp), mk_sq(bp), None, batch=bp, tq=tq_p, tk=tk_p,
                               hps=MLA_HEADS)
        xp = _post(xp, ret, diff_t, mla_t, (mc[2], mc[3], mc[4], mc[5]), lw, is_ctx=True, tm=tm,
                   tiles_per_batch=1)
        new_ret.append(jnp.swapaxes(s_fin.reshape(bp, 2, RET_HEADS, RET_DV, RET_DK), -1, -2))

        ms = [mod[l, 1:1 + bs, i] for i in range(6)]
        s0_t = jnp.swapaxes(state_ret[:, l].astype(F32), -1, -2).reshape(bs, 2 * RET_HEADS, RET_DV, RET_DK)
        ctx_dk_f = cache_diff_k[:, l].reshape(bs, past, DIFF_QK_W).astype(F32)
        ctx_dk = ctx_dk_f.astype(BF16)
        ctx_dv_t = jnp.swapaxes(cache_diff_v[:, l].reshape(bs, past, DIFF_W), 1, 2).astype(BF16)
        ctx_mk, ctx_mv_t, ctx_sq = _ctx_mla_kv(
            jnp.swapaxes(cache_mla_ckv[:, l].astype(F32), 1, 2), jnp.swapaxes(cache_mla_kr[:, l].astype(F32), 1, 2),
            jnp.swapaxes(ctx_dk_f, 1, 2), lw["w_uk"], lw["w_uv"], lw["g_mk"])
        ctx_sq = jnp.max(ctx_sq, axis=-1)
        (rq, rk, rv, rg, dq_t, dk_tok, dv_t, mq_t, mk_tok, mv_t) = _premix(
            xs, ms[0], ms[1], lw, rope, is_ctx=False, tm=tm, seq=ss)
        ret, _ = _retention(rq, rk, rv, rg, s0_t, dec_b[l], gn_col, batch=bs)
        diff_t = _diff_attention(dq_t, dk_tok, dv_t, dq_sq(bs), jnp.maximum(dk_sq(bs), ctx_sq[:, 1]),
                                 (ctx_dk, ctx_dv_t), dl, subln_col, batch=bs, tq=tq_s, tk=tk_s, hps=2,
                                 lam_init=lam_init)
        mla_t = _mla_attention(mq_t, mk_tok, mv_t, mq_sq(bs), jnp.maximum(mk_sq(bs), ctx_sq[:, 0]),
                               (ctx_mk, ctx_mv_t), batch=bs, tq=tq_s, tk=tk_s, hps=4)
        xs = _post(xs, ret, diff_t, mla_t, (ms[2], ms[3], ms[4], ms[5]), lw, is_ctx=False, tm=tm,
                   tiles_per_batch=ss // tm)

    y_p = xp.T.reshape(bp, sp, d)
    y_s = xs.T.reshape(bs, ss, d)
    dk_all, dv_all, ckv_all, kr_all = ctx_out
    return (y_p, y_s, jnp.stack(new_ret, axis=1),
            dk_all.reshape(bp, depth, sp, DIFF_HEADS, 2, DIFF_DK), dv_all.reshape(bp, depth, sp, DIFF_HEADS, DIFF_DV),
            ckv_all, kr_all[..., :MLA_ROPE])
```

```python
import functools
import math

import jax
import jax.numpy as jnp
from jax import lax
from jax.experimental import pallas as pl
from jax.experimental.pallas import tpu as pltpu

F32 = jnp.float32
BF16 = jnp.bfloat16
EPS = 1e-6

GRID_W = 64
RET_HEADS = 4
RET_DK = 64
RET_DV = 64
RET_CHUNK = 128
DIFF_HEADS = 4
DIFF_DK = 32
DIFF_DV = 64
MLA_HEADS = 8
MLA_NOPE = 64
MLA_ROPE = 32
MLA_DQK = MLA_NOPE + MLA_ROPE
MLA_DV = 64
MLA_Q_RANK = 768
MLA_KV_RANK = 256
ROPE_BASE = 10000.0

RET_W = RET_HEADS * RET_DV
DIFF_W = DIFF_HEADS * DIFF_DV
MLA_W = MLA_HEADS * MLA_DV
DIFF_QK_W = DIFF_HEADS * 2 * DIFF_DK
MLA_PAD = 128
MLA_QK_PAD_W = MLA_HEADS * MLA_PAD
assert DIFF_DV == MLA_DV
HEAD_DV = MLA_DV

_SPLITS = (RET_HEADS * RET_DK, RET_HEADS * RET_DK, RET_W, RET_W,
           DIFF_QK_W, DIFF_QK_W, DIFF_W, MLA_Q_RANK, MLA_KV_RANK, MLA_ROPE)
W_IN_MLA0 = int(sum(_SPLITS[:7]))
_PERM_SPLITS = _SPLITS[7:] + _SPLITS[:7]
_OFFS = tuple(int(sum(_PERM_SPLITS[:i])) for i in range(len(_PERM_SPLITS) + 1))
O_CQ, O_CKV, O_KR, O_RQ, O_RK, O_RV, O_RG, O_DQ, O_DK, O_DV, O_END = _OFFS

LOG2E = math.log2(math.e)
SCORE_LOOKAHEAD = 2
SAFE_EXP_RANGE = 100.0
BOUND_SLACK = 1.0 + 2.0 ** -6
LANE = 128
MXU_DEPTH = 256
PREMIX_GROUPS = 2
VMEM_LIMIT = 56 * 1024 * 1024


def _cparams(n_grid):
    return pltpu.CompilerParams(dimension_semantics=("arbitrary",) * n_grid,
                                vmem_limit_bytes=VMEM_LIMIT)


def _full(shape):
    nd = len(shape)
    return pl.BlockSpec(shape, lambda *_: (0,) * nd)


def _silu(x):
    return x / (1.0 + jnp.exp(-x))


def _rms_scale(x, n):
    return lax.rsqrt(jnp.sum(x * x, axis=0, keepdims=True) * (1.0 / n) + EPS)


def _rope_pair(x1, x2, c, s):
    return x1 * c - x2 * s, x2 * c + x1 * s


def _mod_kernel(c_ref, w_ref, b_ref, o_ref):
    a = _silu(c_ref[...]).astype(BF16)
    o_ref[...] = jnp.dot(a, w_ref[...].astype(BF16), preferred_element_type=F32) + b_ref[...]


def _modulation(cond, w_mod, b_mod):
    depth, d, d6 = w_mod.shape
    r = cond.shape[0]
    tn = 1536
    return pl.pallas_call(
        _mod_kernel,
        grid=(depth, d6 // tn),
        in_specs=[pl.BlockSpec((r, d), lambda l, j: (0, 0)),
                  pl.BlockSpec((None, d, tn), lambda l, j: (l, 0, j)),
                  pl.BlockSpec((None, 1, tn), lambda l, j: (l, 0, j))],
        out_specs=pl.BlockSpec((None, r, tn), lambda l, j: (l, 0, j)),
        out_shape=jax.ShapeDtypeStruct((depth, r, d6), F32),
        compiler_params=_cparams(2),
        name="adaln_mod",
    )(cond, w_mod, b_mod.reshape(depth, 1, d6))


def _mla_kv_matmuls(ckvn_bf, w_uk_ref, w_uv_ref):
    return (jnp.dot(w_uk_ref[...], ckvn_bf, preferred_element_type=F32),
            jnp.dot(w_uv_ref[...], ckvn_bf, preferred_element_type=F32))


def _mla_keys(kn, kr, g_mk_ref, rope):
    t = kr.shape[1]
    kr_ss = jnp.sum(kr * kr, axis=0, keepdims=True)
    zpad = jnp.zeros((MLA_PAD - MLA_DQK, t), F32)
    half = MLA_ROPE // 2
    heads = []
    for hd in range(MLA_HEADS):
        r0 = hd * MLA_PAD
        kh = kn[r0:r0 + MLA_NOPE]
        ss = jnp.sum(kh * kh, axis=0, keepdims=True) + kr_ss
        r = lax.rsqrt(ss * (1.0 / MLA_DQK) + EPS)
        g = g_mk_ref[r0:r0 + MLA_PAD, :]
        y_nope = kh * r * g[:MLA_NOPE]
        y_r = kr * r * g[MLA_NOPE:MLA_DQK]
        if rope is not None:
            c, s = rope
            y1, y2 = _rope_pair(y_r[:half], y_r[half:], c, s)
            heads += [y_nope, y1, y2, zpad]
        else:
            heads += [y_nope, y_r, zpad]
    return jnp.concatenate(heads, axis=0)


def _group_sq_norms(a, rows_per_group):
    groups = a.shape[0] // rows_per_group
    return jnp.concatenate([jnp.sum(jnp.square(a[g * rows_per_group:(g + 1) * rows_per_group]), axis=0, keepdims=True)
                            for g in range(groups)], axis=0)


def _lane_tile_max(sq):
    return functools.reduce(jnp.maximum, [sq[:, t * LANE:(t + 1) * LANE] for t in range(sq.shape[1] // LANE)])


def _ctx_kv_kernel(ckv_ref, kr_ref, dk_ref, w_uk_ref, w_uv_ref, g_mk_ref, mk_ref, mv_ref, sq_ref):
    kn, v = _mla_kv_matmuls(ckv_ref[...].astype(BF16), w_uk_ref, w_uv_ref)
    k = _mla_keys(kn, kr_ref[...], g_mk_ref, None)
    mk_ref[...] = k.T.astype(BF16)
    mv_ref[...] = v.astype(BF16)
    sq_ref[0] = _lane_tile_max(_group_sq_norms(k, MLA_PAD))
    sq_ref[1] = _lane_tile_max(_group_sq_norms(dk_ref[...], DIFF_DK))


def _ctx_mla_kv(ckv_t, kr_t, dk_t, w_uk, w_uv, g_mk):
    b, _, l = ckv_t.shape
    return pl.pallas_call(
        _ctx_kv_kernel,
        grid=(b,),
        in_specs=[pl.BlockSpec((None, MLA_KV_RANK, l), lambda i: (i, 0, 0)),
                  pl.BlockSpec((None, MLA_ROPE, l), lambda i: (i, 0, 0)),
                  pl.BlockSpec((None, DIFF_QK_W, l), lambda i: (i, 0, 0)),
                  _full(w_uk.shape), _full(w_uv.shape), _full(g_mk.shape)],
        out_specs=[pl.BlockSpec((None, l, MLA_QK_PAD_W), lambda i: (i, 0, 0)),
                   pl.BlockSpec((None, MLA_W, l), lambda i: (i, 0, 0)),
                   pl.BlockSpec((None, 2, 8, LANE), lambda i: (i, 0, 0, 0))],
        out_shape=[jax.ShapeDtypeStruct((b, l, MLA_QK_PAD_W), BF16),
                   jax.ShapeDtypeStruct((b, MLA_W, l), BF16),
                   jax.ShapeDtypeStruct((b, 2, 8, LANE), F32)],
        compiler_params=_cparams(1),
        name="ctx_mla_kv",
    )(ckv_t, kr_t, dk_t, w_uk, w_uv, g_mk)


def _premix_kernel(*refs, is_ctx, tm, n_aliased):
    it = iter(refs)
    x_ref, shift_ref, scale_ref, n1_ref, w_in_ref = (next(it) for _ in range(5))
    g_dq_ref, g_dk_ref, g_qn_ref, g_kvn_ref = (next(it) for _ in range(4))
    w_uq_ref, g_mq_ref, w_uk_ref, w_uv_ref, g_mk_ref = (next(it) for _ in range(5))
    if not is_ctx:
        cr_ref, sr_ref, cs_ref, ss_ref = (next(it) for _ in range(4))
    for _ in range(n_aliased):
        next(it)
    rq_ref, rk_ref, rv_ref, rg_ref = (next(it) for _ in range(4))
    dq_ref, dk_ref, dv_ref, mq_ref, mk_ref, mv_ref = (next(it) for _ in range(6))
    if is_ctx:
        dkf_ref, dvf_ref, ckvf_ref, krf_ref = (next(it) for _ in range(4))
    d = x_ref.shape[0]

    def token_group(c0, w):
        cols = slice(c0, c0 + w)
        if is_ctx:
            rope_r = rope_s = None
        else:
            rope_r = (cr_ref[:, cols], sr_ref[:, cols])
            rope_s = (cs_ref[:, cols], ss_ref[:, cols])
        x = x_ref[:, cols]
        h = (x * _rms_scale(x, d) * (n1_ref[...] * (1.0 + scale_ref[...])) + shift_ref[...]).astype(BF16)
        yield

        def put_chunks(ref, val):
            for j in range(w // LANE):
                ref[c0 // LANE + j] = val[:, j * LANE:(j + 1) * LANE].astype(ref.dtype)

        def proj(lo, hi):
            return jnp.dot(w_in_ref[lo:hi, :], h, preferred_element_type=F32)

        p_mla = proj(O_CQ, O_RQ)
        cq, ckv, kr = p_mla[O_CQ:O_CKV], p_mla[O_CKV:O_KR], p_mla[O_KR:O_RQ]
        p_ret = proj(O_RQ, O_DQ)
        rq, rk, rv, rg = (p_ret[o - O_RQ:o - O_RQ + RET_W] for o in (O_RQ, O_RK, O_RV, O_RG))
        yield
        cqn = (cq * _rms_scale(cq, MLA_Q_RANK) * g_qn_ref[...]).astype(BF16)
        ckvn = ckv * _rms_scale(ckv, MLA_KV_RANK) * g_kvn_ref[...]
        mq = jnp.dot(w_uq_ref[...], cqn, preferred_element_type=F32)
        kn, mv = _mla_kv_matmuls(ckvn.astype(BF16), w_uk_ref, w_uv_ref)
        p_diff = proj(O_DQ, O_END)
        dq_raw, dk_raw, dv = (p_diff[o - O_DQ:o - O_DQ + DIFF_QK_W] for o in (O_DQ, O_DK, O_DV))
        yield

        if rope_r is not None:
            c, s = rope_r
            hk = RET_DK // 2

            def rope_heads(a):
                parts = []
                for hd in range(RET_HEADS):
                    r0 = hd * RET_DK
                    parts += list(_rope_pair(a[r0:r0 + hk], a[r0 + hk:r0 + RET_DK], c, s))
                return jnp.concatenate(parts, axis=0)

            rq = rope_heads(rq)
            rk = rope_heads(rk)
        put_chunks(rq_ref, rq)
        put_chunks(rk_ref, rk * (RET_DK ** -0.5))
        put_chunks(rv_ref, rv)
        put_chunks(rg_ref, rg)

        def diff_qk(a, g_ref, out_scale):
            parts = []
            hs = DIFF_DK // 2
            for seg in range(2 * DIFF_HEADS):
                r0 = seg * DIFF_DK
                xs = a[r0:r0 + DIFF_DK]
                y = xs * _rms_scale(xs, DIFF_DK) * g_ref[r0:r0 + DIFF_DK, :]
                if out_scale != 1.0:
                    y = y * out_scale
                if rope_s is not None:
                    parts += list(_rope_pair(y[:hs], y[hs:], rope_s[0], rope_s[1]))
                else:
                    parts.append(y)
            return jnp.concatenate(parts, axis=0)

        dq = diff_qk(dq_raw, g_dq_ref, DIFF_DK ** -0.5 * LOG2E)
        dq_ref[:, cols] = dq.astype(BF16)
        dk = diff_qk(dk_raw, g_dk_ref, 1.0)
        dk_t = dk.T
        dk_ref[cols, :] = dk_t.astype(BF16)
        dv_ref[:, cols] = dv.astype(BF16)
        if is_ctx:
            dkf_ref[c0 // w] = dk_t
            dvf_ref[c0 // w] = dv.T

        half = MLA_ROPE // 2
        parts = []
        for hd in range(MLA_HEADS):
            r0 = hd * MLA_PAD
            xs = mq[r0:r0 + MLA_PAD]
            y = xs * _rms_scale(xs, MLA_DQK) * (g_mq_ref[r0:r0 + MLA_PAD, :] * (MLA_DQK ** -0.5 * LOG2E))
            if rope_s is not None:
                y1, y2 = _rope_pair(y[MLA_NOPE:MLA_NOPE + half], y[MLA_NOPE + half:MLA_DQK], rope_s[0], rope_s[1])
                parts += [y[:MLA_NOPE], y1, y2, y[MLA_DQK:]]
            else:
                parts.append(y)
        mq_ref[:, cols] = jnp.concatenate(parts, axis=0).astype(BF16)

        mk = _mla_keys(kn, kr, g_mk_ref, rope_s)
        mk_ref[cols, :] = mk.T.astype(BF16)
        mv_ref[:, cols] = mv.astype(BF16)
        if is_ctx:
            ckvf_ref[c0 // w] = ckvn.T
            kr_t = jnp.concatenate([kr, jnp.zeros((LANE - MLA_ROPE, w), F32)], axis=0).T
            krf_ref[c0 // w] = kr_t[:, :MLA_ROPE]
        yield

    groups = [token_group(i * (tm // PREMIX_GROUPS), tm // PREMIX_GROUPS) for i in range(PREMIX_GROUPS)]
    for _ in range(4):
        for g in groups:
            next(g)


def _premix(x_t, shift, scale, lw, rope, *, is_ctx, tm, seq, layer=0, depth=1, stacked=None):
    d, n = x_t.shape
    nt = n // tm
    nch = tm // LANE
    tiles_per_batch = max(1, seq // tm)
    if is_ctx:
        bidx = lambda j: (0, 0, 0)
    else:
        bidx = lambda j: (j // tiles_per_batch, 0, 0)
    in_arrays = [x_t, shift, scale, lw["n1"], lw["w_in"], lw["g_dq"], lw["g_dk"], lw["g_qn"], lw["g_kvn"],
                 lw["w_uq"], lw["g_mq"], lw["w_uk"], lw["w_uv"], lw["g_mk"]]
    in_specs = [pl.BlockSpec((d, tm), lambda j: (0, j)),
                pl.BlockSpec((None, d, 1), bidx), pl.BlockSpec((None, d, 1), bidx)]
    in_specs += [_full(a.shape) for a in in_arrays[3:]]
    if not is_ctx:
        for tab in rope:
            in_arrays.append(tab)
            in_specs.append(pl.BlockSpec((tab.shape[0], tm), lambda j: (0, j % tiles_per_batch)))

    chunk_spec = pl.BlockSpec((nch, RET_W, LANE), lambda j: (j, 0, 0))
    fm = lambda rows: pl.BlockSpec((rows, tm), lambda j: (0, j))
    tok = lambda cols: pl.BlockSpec((tm, cols), lambda j: (j, 0))
    nck = n // LANE
    out_specs = [chunk_spec] * 4 + [fm(DIFF_QK_W), tok(DIFF_QK_W), fm(DIFF_W),
                                    fm(MLA_QK_PAD_W), tok(MLA_QK_PAD_W), fm(MLA_W)]
    out_shape = [jax.ShapeDtypeStruct((nck, RET_W, LANE), BF16), jax.ShapeDtypeStruct((nck, RET_W, LANE), F32),
                 jax.ShapeDtypeStruct((nck, RET_W, LANE), BF16), jax.ShapeDtypeStruct((nck, RET_W, LANE), F32),
                 jax.ShapeDtypeStruct((DIFF_QK_W, n), BF16), jax.ShapeDtypeStruct((n, DIFF_QK_W), BF16),
                 jax.ShapeDtypeStruct((DIFF_W, n), BF16),
                 jax.ShapeDtypeStruct((MLA_QK_PAD_W, n), BF16), jax.ShapeDtypeStruct((n, MLA_QK_PAD_W), BF16),
                 jax.ShapeDtypeStruct((MLA_W, n), BF16)]
    aliases = {}
    if is_ctx:
        assert tm // PREMIX_GROUPS == seq, "each token group of a context tile must be one request"
        for cols in (DIFF_QK_W, DIFF_W, MLA_KV_RANK, MLA_ROPE):
            out_specs.append(pl.BlockSpec((tm // seq, None, seq, cols), lambda j: (j, layer, 0, 0)))
            out_shape.append(jax.ShapeDtypeStruct((n // seq, depth, seq, cols), F32))
        if stacked is not None:
            n_out = len(out_shape)
            for k, arr in enumerate(stacked):
                aliases[len(in_arrays)] = n_out - len(stacked) + k
                in_arrays.append(arr)
                in_specs.append(pl.BlockSpec(memory_space=pl.ANY))
    return pl.pallas_call(
        functools.partial(_premix_kernel, is_ctx=is_ctx, tm=tm, n_aliased=len(aliases)),
        grid=(nt,),
        in_specs=in_specs, out_specs=out_specs, out_shape=out_shape,
        input_output_aliases=aliases,
        compiler_params=_cparams(1),
        name="premix_ctx" if is_ctx else "premix_smp",
    )(*in_arrays)


def _retention_kernel(rq_ref, rk_ref, rv_ref, rg_ref, s0_ref, dec_ref, gn_ref, out_ref, sfin_ref,
                      o_acc, s_acc, intra_ref, vec_ref, *, nc):
    C = RET_CHUNK

    @pl.when(pl.program_id(0) == 0)
    def _():
        n_idx = lax.broadcasted_iota(jnp.int32, (C, C), 1).astype(F32)
        m_idx = lax.broadcasted_iota(jnp.int32, (C, C), 0).astype(F32)
        lane = lax.broadcasted_iota(jnp.int32, (8, C), 1).astype(F32)
        for d in range(2):
            for hd in range(RET_HEADS):
                i = d * RET_HEADS + hd
                z = dec_ref[i]
                lg8 = jnp.minimum(z, 0.0) - jnp.log1p(jnp.exp(-jnp.abs(z)))
                lg = jnp.broadcast_to(lg8[0:1, :], (C, C))
                dist = (n_idx - m_idx) if d == 0 else (m_idx - n_idx)
                ok = dist >= 0.0
                intra_ref[i] = jnp.where(ok, jnp.exp(lg * jnp.where(ok, dist, 0.0)), 0.0)
                if d == 0:
                    qdec = jnp.exp(lg8 * (lane + 1.0))
                    kdec = jnp.exp(lg8 * (C - 1.0 - lane))
                else:
                    qdec = jnp.exp(lg8 * (C - lane))
                    kdec = jnp.exp(lg8 * lane)
                vec_ref[i, 0] = qdec
                vec_ref[i, 1] = kdec
                vec_ref[i, 2] = jnp.exp(lg8 * float(C))

    s_acc[...] = s0_ref[...]
    o_acc[...] = jnp.zeros_like(o_acc)

    row_head = lax.broadcasted_iota(jnp.int32, (RET_HEADS * RET_DK, C), 0) // RET_DK

    def step(c, carry):
        first, vs = [], []
        for d in range(2):
            cc = c if d == 0 else nc - 1 - c
            q_all = rq_ref[cc]
            k_all = rk_ref[cc]
            k_tok = k_all.T.astype(BF16)
            q_bd = jnp.concatenate([jnp.where(row_head == hd, q_all, jnp.zeros_like(q_all))
                                    for hd in range(RET_HEADS)], axis=1)
            att_all = jnp.dot(k_tok, q_bd, preferred_element_type=F32)
            for hd in range(RET_HEADS):
                i = d * RET_HEADS + hd
                r0 = hd * RET_DK
                q = q_all[r0:r0 + RET_DK]
                k = k_all[r0:r0 + RET_DK]
                v = rv_ref[cc, r0:r0 + RET_DV, :]
                att_t = att_all[:, hd * C:(hd + 1) * C]
                s_t = s_acc[i]
                cross = jnp.dot(s_t.astype(BF16), q, preferred_element_type=F32)
                kd = (k * vec_ref[i, 1][0:1, :]).astype(BF16)
                kv = lax.dot_general(v, kd, (((1,), (1,)), ((), ())), preferred_element_type=F32)
                s_acc[i] = s_t * vec_ref[i, 2][0:1, 0:RET_DK] + kv
                first.append((att_t, cross))
                vs.append((i, cc, r0, v))
        masked = [(att_t * intra_ref[i]).astype(BF16) for (att_t, _), (i, _, _, _) in zip(first, vs)]
        for (_, cross), att_m, (i, cc, r0, v) in zip(first, masked, vs):
            o = jnp.dot(v, att_m, preferred_element_type=F32) + cross * vec_ref[i, 0][0:1, :]
            o_acc[cc, r0:r0 + RET_DV, :] += o
        return carry

    lax.fori_loop(0, nc, step, 0, unroll=min(nc, 8))
    sfin_ref[...] = s_acc[...]

    def finish(c, carry):
        o = o_acc[c]
        g = rg_ref[c]
        parts = []
        for hd in range(RET_HEADS):
            r0 = hd * RET_DV
            oh = o[r0:r0 + RET_DV]
            mu = jnp.mean(oh, axis=0, keepdims=True)
            var = jnp.mean(jnp.square(oh - mu), axis=0, keepdims=True)
            y = (oh - mu) * lax.rsqrt(var + EPS) * gn_ref[r0:r0 + RET_DV, :]
            parts.append(y * _silu(g[r0:r0 + RET_DV]))
        out_ref[c] = jnp.concatenate(parts, axis=0).astype(BF16)
        return carry

    lax.fori_loop(0, nc, finish, 0, unroll=min(nc, 4))


def _retention(rq, rk, rv, rg, s0_t, dec_b, gn_col, *, batch):
    nck = rq.shape[0]
    nc = nck // batch
    blk = pl.BlockSpec((nc, RET_W, LANE), lambda b: (b, 0, 0))
    nst = 2 * RET_HEADS
    st_spec = pl.BlockSpec((None, nst, RET_DV, RET_DK), lambda b: (b, 0, 0, 0))
    return pl.pallas_call(
        functools.partial(_retention_kernel, nc=nc),
        grid=(batch,),
        in_specs=[blk, blk, blk, blk, st_spec, _full(dec_b.shape), _full(gn_col.shape)],
        out_specs=[blk, st_spec],
        out_shape=[jax.ShapeDtypeStruct((nck, RET_W, LANE), BF16),
                   jax.ShapeDtypeStruct((batch, nst, RET_DV, RET_DK), F32)],
        scratch_shapes=[pltpu.VMEM((nc, RET_W, LANE), F32),
                        pltpu.VMEM((nst, RET_DV, RET_DK), F32),
                        pltpu.VMEM((nst, RET_CHUNK, RET_CHUNK), F32),
                        pltpu.VMEM((nst, 3, 8, LANE), F32)],
        compiler_params=_cparams(1),
        name="retention",
    )(rq, rk, rv, rg, s0_t, dec_b, gn_col)


def _attend(problems, tk, shifts=None):
    flat = [(pi, part, r0, min(tk, part[4])) for pi, (_, parts) in enumerate(problems) for part in parts
            for r0 in range(0, part[4], min(tk, part[4]))]

    def scores(item):
        pi, (k_ref, c0, _, _, _), r0, nk = item
        q_t = problems[pi][0]
        return jnp.dot(k_ref[r0:r0 + nk, c0:c0 + q_t.shape[0]], q_t, preferred_element_type=F32)

    state = [(jnp.full((1, q_t.shape[1]), -jnp.inf, F32), jnp.zeros((HEAD_DV, q_t.shape[1]), F32),
              jnp.zeros((1, q_t.shape[1]), F32)) for q_t, _ in problems]
    pending = [scores(it) for it in flat[:SCORE_LOOKAHEAD]]
    for i, (pi, (_, _, v_ref, v0, _), r0, nk) in enumerate(flat):
        if i + SCORE_LOOKAHEAD < len(flat):
            pending.append(scores(flat[i + SCORE_LOOKAHEAD]))
        s = pending.pop(0)
        m, acc, l = state[pi]
        if shifts is None:
            m_new = jnp.maximum(m, jnp.max(s, axis=0, keepdims=True))
            alpha = jnp.exp2(m - m_new)
            acc, l = acc * alpha, l * alpha
        else:
            m_new = shifts[pi]
        p = jnp.exp2(s - m_new)
        l = l + jnp.sum(p, axis=0, keepdims=True)
        pv = jnp.dot(v_ref[v0:v0 + HEAD_DV, r0:r0 + nk], p.astype(BF16), preferred_element_type=F32)
        state[pi] = (m_new, acc + pv, l)
    return [acc / l for _, acc, l in state]


def _attend_guarded(problems, key_sq_max, safe, tk):
    shifts = []
    for (q_t, _), ksq in zip(problems, key_sq_max):
        q32 = q_t.astype(F32)
        qsq = jnp.sum(q32 * q32, axis=0, keepdims=True)
        shifts.append(jnp.sqrt(qsq * ksq) * BOUND_SLACK)
    outs = lax.cond(safe,
                    lambda: tuple(_attend(problems, tk, shifts)),
                    lambda: tuple(_attend(problems, tk, None)))
    return list(outs)


def _row_bcast(a, row, tq):
    rid = lax.broadcasted_iota(jnp.int32, a.shape, 0)
    r = jnp.max(jnp.where(rid == row, a, 0.0), axis=0, keepdims=True)
    return jnp.concatenate([r] * (tq // LANE), axis=1)


def _diff_attn_kernel(*refs, has_ctx, rows_main, rows_ctx, tk, hps, lam_init):
    it = iter(refs)
    safe_ref, q_ref, k_ref, v_ref, ksq_ref = (next(it) for _ in range(5))
    if has_ctx:
        kc_ref, vc_ref = (next(it) for _ in range(2))
    dl_ref, g_ref, out_ref = (next(it) for _ in range(3))
    grp = pl.program_id(1)
    tq = q_ref.shape[1]
    segs_per_tile = LANE // DIFF_DK
    seg = lax.broadcasted_iota(jnp.int32, (LANE, tq), 0) // DIFF_DK
    problems, key_sq_max = [], []
    for j in range(hps):
        tile = (2 * j) // segs_per_tile
        q_tile = q_ref[tile * LANE:(tile + 1) * LANE, :]
        parts = [(k_ref, tile * LANE, v_ref, j * HEAD_DV, rows_main)]
        if has_ctx:
            parts.append((kc_ref, tile * LANE, vc_ref, j * HEAD_DV, rows_ctx))
        for mm in range(2):
            q_m = jnp.where(seg == (2 * j + mm) % segs_per_tile, q_tile, jnp.zeros_like(q_tile))
            problems.append((q_m, parts))
            key_sq_max.append(_row_bcast(ksq_ref[...], 2 * (grp * hps + j) + mm, tq))
    dl = dl_ref[...]
    lam = (jnp.exp(jnp.sum(dl[0:1] * dl[1:2], axis=1, keepdims=True))
           - jnp.exp(jnp.sum(dl[2:3] * dl[3:4], axis=1, keepdims=True)) + lam_init)
    outs = _attend_guarded(problems, key_sq_max, safe_ref[pl.program_id(0), grp] != 0, tk)
    for j in range(hps):
        o = outs[2 * j] - lam * outs[2 * j + 1]
        y = o * _rms_scale(o, DIFF_DV) * g_ref[...]
        out_ref[j * DIFF_DV:(j + 1) * DIFF_DV, :] = (y * (1.0 - lam_init)).astype(BF16)


def _safe_flags(q_sq, k_sq, per_group):
    bound = jnp.sqrt(q_sq * k_sq) * (BOUND_SLACK * BOUND_SLACK)
    ok = (2.0 * bound <= SAFE_EXP_RANGE).reshape(q_sq.shape[0], -1, per_group)
    return jnp.all(ok, axis=-1).astype(jnp.int32)


def _diff_attention(dq_t, dk_tok, dv_t, q_sq, k_sq, ctx, dl, g_col, *, batch, tq, tk, hps, lam_init):
    n_all = dq_t.shape[1]
    n = n_all // batch
    qt = n // tq
    nseg = 2 * DIFF_HEADS
    has_ctx = ctx is not None
    rows_ctx = ctx[0].shape[1] if has_ctx else 0
    assert (hps * 2 * DIFF_DK) % LANE == 0, "a grid step must cover whole 128-lane key tiles"
    qk_w = hps * 2 * DIFF_DK
    safe = _safe_flags(q_sq, k_sq, 2 * hps)
    k_sq_b = jnp.broadcast_to(k_sq[:, :, None], (batch, nseg, LANE))
    in_arrays = [safe, dq_t, dk_tok, dv_t, k_sq_b]
    in_specs = [pl.BlockSpec(memory_space=pltpu.SMEM),
                pl.BlockSpec((qk_w, tq), lambda b, g, i: (g, b * qt + i)),
                pl.BlockSpec((n, qk_w), lambda b, g, i: (b, g)),
                pl.BlockSpec((hps * HEAD_DV, n), lambda b, g, i: (g, b)),
                pl.BlockSpec((None, nseg, LANE), lambda b, g, i: (b, 0, 0))]
    if has_ctx:
        in_arrays += list(ctx)
        in_specs += [pl.BlockSpec((None, rows_ctx, qk_w), lambda b, g, i: (b, 0, g)),
                     pl.BlockSpec((None, hps * HEAD_DV, rows_ctx), lambda b, g, i: (b, g, 0))]
    in_arrays += [dl, g_col]
    in_specs += [_full(dl.shape), _full(g_col.shape)]
    return pl.pallas_call(
        functools.partial(_diff_attn_kernel, has_ctx=has_ctx, rows_main=n, rows_ctx=rows_ctx, tk=tk, hps=hps,
                          lam_init=lam_init),
        grid=(batch, DIFF_HEADS // hps, qt),
        in_specs=in_specs,
        out_specs=pl.BlockSpec((hps * DIFF_DV, tq), lambda b, g, i: (g, b * qt + i)),
        out_shape=jax.ShapeDtypeStruct((DIFF_W, n_all), BF16),
        compiler_params=_cparams(3),
        name="diff_attn_smp" if has_ctx else "diff_attn_ctx",
    )(*in_arrays)


def _mla_attn_kernel(*refs, has_ctx, rows_main, rows_ctx, tk, hps):
    it = iter(refs)
    safe_ref, q_ref, k_ref, v_ref, ksq_ref = (next(it) for _ in range(5))
    if has_ctx:
        kc_ref, vc_ref = (next(it) for _ in range(2))
    out_ref = next(it)
    grp = pl.program_id(1)
    tq = q_ref.shape[1]
    problems, key_sq_max = [], []
    for j in range(hps):
        parts = [(k_ref, j * MLA_PAD, v_ref, j * HEAD_DV, rows_main)]
        if has_ctx:
            parts.append((kc_ref, j * MLA_PAD, vc_ref, j * HEAD_DV, rows_ctx))
        problems.append((q_ref[j * MLA_PAD:(j + 1) * MLA_PAD, :], parts))
        key_sq_max.append(_row_bcast(ksq_ref[...], grp * hps + j, tq))
    outs = _attend_guarded(problems, key_sq_max, safe_ref[pl.program_id(0), grp] != 0, tk)
    for j, o in enumerate(outs):
        out_ref[j * MLA_DV:(j + 1) * MLA_DV, :] = o.astype(BF16)


def _mla_attention(mq_t, mk_tok, mv_t, q_sq, k_sq, ctx, *, batch, tq, tk, hps):
    n_all = mq_t.shape[1]
    n = n_all // batch
    qt = n // tq
    has_ctx = ctx is not None
    rows_ctx = ctx[0].shape[1] if has_ctx else 0
    safe = _safe_flags(q_sq, k_sq, hps)
    k_sq_b = jnp.broadcast_to(k_sq[:, :, None], (batch, MLA_HEADS, LANE))
    in_arrays = [safe, mq_t, mk_tok, mv_t, k_sq_b]
    in_specs = [pl.BlockSpec(memory_space=pltpu.SMEM),
                pl.BlockSpec((hps * MLA_PAD, tq), lambda b, g, i: (g, b * qt + i)),
                pl.BlockSpec((n, hps * MLA_PAD), lambda b, g, i: (b, g)),
                pl.BlockSpec((hps * HEAD_DV, n), lambda b, g, i: (g, b)),
                pl.BlockSpec((None, MLA_HEADS, LANE), lambda b, g, i: (b, 0, 0))]
    if has_ctx:
        in_arrays += list(ctx)
        in_specs += [pl.BlockSpec((None, rows_ctx, hps * MLA_PAD), lambda b, g, i: (b, 0, g)),
                     pl.BlockSpec((None, hps * HEAD_DV, rows_ctx), lambda b, g, i: (b, g, 0))]
    return pl.pallas_call(
        functools.partial(_mla_attn_kernel, has_ctx=has_ctx, rows_main=n, rows_ctx=rows_ctx, tk=tk, hps=hps),
        grid=(batch, MLA_HEADS // hps, qt),
        in_specs=in_specs,
        out_specs=pl.BlockSpec((hps * MLA_DV, tq), lambda b, g, i: (g, b * qt + i)),
        out_shape=jax.ShapeDtypeStruct((MLA_W, n_all), BF16),
        compiler_params=_cparams(3),
        name="mla_attn_smp" if has_ctx else "mla_attn_ctx",
    )(*in_arrays)


def _post_kernel(x_ref, ret_ref, diff_ref, mla_ref, g1_ref, sh2_ref, sc2_ref, g2_ref, n2_ref,
                 w_out_ref, w_gu_ref, w_down_ref, out_ref, *, tm, d_ff, ff_cuts):
    nch = tm // LANE
    ret = jnp.concatenate([ret_ref[j] for j in range(nch)], axis=1)
    mixed = jnp.concatenate([ret, diff_ref[...], mla_ref[...]], axis=0)
    d = x_ref.shape[0]
    halves = [slice(i * (tm // 2), (i + 1) * (tm // 2)) for i in range(2)]
    attn = [jnp.dot(w_out_ref[...], mixed[:, c], preferred_element_type=F32) for c in halves]
    x1 = [x_ref[:, c] + g1_ref[...] * a for c, a in zip(halves, attn)]
    gain2 = n2_ref[...] * (1.0 + sc2_ref[...])
    h = [(v * _rms_scale(v, d) * gain2 + sh2_ref[...]).astype(BF16) for v in x1]
    ffn = None
    for lo, hi in zip((0,) + ff_cuts, ff_cuts + (d_ff,)):
        acts = []
        for i in range(2):
            g = jnp.dot(w_gu_ref[lo:hi, :], h[i], preferred_element_type=F32)
            u = jnp.dot(w_gu_ref[d_ff + lo:d_ff + hi, :], h[i], preferred_element_type=F32)
            acts.append((_silu(g) * u).astype(BF16))
        part = jnp.dot(w_down_ref[:, lo:hi], jnp.concatenate(acts, axis=1), preferred_element_type=F32)
        ffn = part if ffn is None else ffn + part
    out_ref[...] = jnp.concatenate(x1, axis=1) + g2_ref[...] * ffn


def _post(x_t, ret, diff_t, mla_t, mods, lw, *, is_ctx, tm, tiles_per_batch):
    d, n = x_t.shape
    nt = n // tm
    nch = tm // LANE
    d_ff = lw["w_down"].shape[1]
    ff_cuts = ((d_ff // 2) // MXU_DEPTH * MXU_DEPTH,)
    if is_ctx:
        bidx = lambda j: (0, 0, 0)
    else:
        bidx = lambda j: (j // tiles_per_batch, 0, 0)
    col = pl.BlockSpec((None, d, 1), bidx)
    const = lambda a: pl.BlockSpec(a.shape, lambda j: (0,) * a.ndim, pipeline_mode=pl.Buffered(1))
    g1, sh2, sc2, g2 = mods
    return pl.pallas_call(
        functools.partial(_post_kernel, tm=tm, d_ff=d_ff, ff_cuts=ff_cuts),
        grid=(nt,),
        in_specs=[pl.BlockSpec((d, tm), lambda j: (0, j)),
                  pl.BlockSpec((nch, RET_W, LANE), lambda j: (j, 0, 0)),
                  pl.BlockSpec((DIFF_W, tm), lambda j: (0, j)),
                  pl.BlockSpec((MLA_W, tm), lambda j: (0, j)),
                  col, col, col, col, const(lw["n2"]),
                  const(lw["w_out"]), const(lw["w_gu"]), const(lw["w_down"])],
        out_specs=pl.BlockSpec((d, tm), lambda j: (0, j)),
        out_shape=jax.ShapeDtypeStruct((d, n), F32),
        compiler_params=_cparams(1),
        name="post_ctx" if is_ctx else "post_smp",
    )(x_t, ret, diff_t, mla_t, g1, sh2, sc2, g2, lw["n2"], lw["w_out"], lw["w_gu"], lw["w_down"])


def _rope_tables_t(n, rot_dim):
    rows = n // GRID_W
    row = jnp.repeat(jnp.arange(rows, dtype=F32), GRID_W)
    col = jnp.tile(jnp.arange(GRID_W, dtype=F32), rows)
    n_freq = rot_dim // 4
    inv = 1.0 / (ROPE_BASE ** (jnp.arange(n_freq, dtype=F32) / n_freq))
    ang = jnp.concatenate([inv[:, None] * row[None, :], inv[:, None] * col[None, :]], axis=0)
    return jnp.cos(ang), jnp.sin(ang)


def _col(v):
    return v.astype(F32).reshape(-1, 1)


def _pad_heads_rows(w_t, used, pad):
    hk = w_t.shape[0] // used
    w3 = w_t.reshape(hk, used, w_t.shape[1])
    w3 = jnp.pad(w3, ((0, 0), (0, pad - used), (0, 0)))
    return w3.reshape(hk * pad, w_t.shape[1])


def _layer_weights(l, w_in, norm1, norm2, diff_qk_gain, mla_q_norm, mla_kv_norm, w_uq, w_ukv, mla_qk_gain,
                   w_out, w_gu, w_down):
    w_ukv_t = w_ukv[l].T.reshape(MLA_HEADS, MLA_NOPE + MLA_DV, MLA_KV_RANK)
    w_uk = jnp.pad(w_ukv_t[:, :MLA_NOPE], ((0, 0), (0, MLA_PAD - MLA_NOPE), (0, 0)))
    g_pad = lambda g: jnp.tile(jnp.pad(g.astype(F32), (0, MLA_PAD - MLA_DQK)), MLA_HEADS).reshape(-1, 1)
    return {
        "n1": _col(norm1[l]), "n2": _col(norm2[l]),
        "w_in": jnp.concatenate([w_in[l][:, W_IN_MLA0:], w_in[l][:, :W_IN_MLA0]], axis=1).T.astype(BF16),
        "g_dq": _col(jnp.tile(diff_qk_gain[l, 0], 2 * DIFF_HEADS)),
        "g_dk": _col(jnp.tile(diff_qk_gain[l, 1], 2 * DIFF_HEADS)),
        "g_qn": _col(mla_q_norm[l]), "g_kvn": _col(mla_kv_norm[l]),
        "w_uq": _pad_heads_rows(w_uq[l].T, MLA_DQK, MLA_PAD).astype(BF16),
        "g_mq": g_pad(mla_qk_gain[l, 0]), "g_mk": g_pad(mla_qk_gain[l, 1]),
        "w_uk": w_uk.reshape(MLA_QK_PAD_W, MLA_KV_RANK).astype(BF16),
        "w_uv": w_ukv_t[:, MLA_NOPE:].reshape(MLA_W, MLA_KV_RANK).astype(BF16),
        "w_out": w_out[l].T.astype(BF16), "w_gu": w_gu[l].T.astype(BF16), "w_down": w_down[l].T.astype(BF16),
    }


def kernel(x_prompt, x_sample, c, state_ret, cache_diff_k, cache_diff_v, cache_mla_ckv, cache_mla_kr, c_ctx,
           w_mod, b_mod, norm1, norm2, w_in, ret_decay, ret_gn_gain, diff_qk_gain, diff_lambda, diff_subln_gain,
           mla_q_norm, mla_kv_norm, w_uq, w_ukv, mla_qk_gain, w_out, w_gu, w_down):
    depth = w_in.shape[0]
    bp, sp, d = x_prompt.shape
    bs, ss, _ = x_sample.shape
    past = cache_diff_k.shape[2]

    tm = 512
    tq_s, tk_s = 512, 256
    tq_p = tk_p = sp

    n_cond = 1 + bs
    r_pad = -(-n_cond // 16) * 16
    cond = jnp.concatenate([c_ctx[None, :], c, jnp.zeros((r_pad - n_cond, d), F32)], axis=0)
    mod = _modulation(cond, w_mod, b_mod)
    mod = mod.reshape(depth, r_pad, 6, d, 1)

    xp = x_prompt.reshape(bp * sp, d).T
    xs = x_sample.reshape(bs * ss, d).T
    rope = _rope_tables_t(ss, RET_DK) + _rope_tables_t(ss, DIFF_DK)
    dec_b = jnp.broadcast_to(ret_decay.astype(F32).reshape(depth, 2 * RET_HEADS, 1, 1),
                             (depth, 2 * RET_HEADS, 8, LANE))
    s0_zero = jnp.zeros((bp, 2 * RET_HEADS, RET_DV, RET_DK), F32)

    new_ret = []
    ctx_out = None
    for l in range(depth):
        lw = _layer_weights(l, w_in, norm1, norm2, diff_qk_gain, mla_q_norm, mla_kv_norm, w_uq, w_ukv,
                            mla_qk_gain, w_out, w_gu, w_down)
        lam_init = 0.8 - 0.6 * math.exp(-0.3 * l)
        gn_col = _col(ret_gn_gain[l])
        subln_col = _col(diff_subln_gain[l])
        dl = diff_lambda[l].astype(F32)

        mc = [mod[l, 0:1, i] for i in range(6)]
        def norm_bound(gain, n, out_scale, batch):
            return jnp.full((batch, 8), n * out_scale * out_scale, F32) * jnp.max(jnp.square(gain.astype(F32)))

        dq_sq = functools.partial(norm_bound, diff_qk_gain[l, 0], DIFF_DK, DIFF_DK ** -0.5 * LOG2E)
        dk_sq = functools.partial(norm_bound, diff_qk_gain[l, 1], DIFF_DK, 1.0)
        mq_sq = functools.partial(norm_bound, mla_qk_gain[l, 0], MLA_DQK, MLA_DQK ** -0.5 * LOG2E)
        mk_sq = functools.partial(norm_bound, mla_qk_gain[l, 1], MLA_DQK, 1.0)

        (rq, rk, rv, rg, dq_t, dk_tok, dv_t, mq_t, mk_tok, mv_t, *ctx_out) = _premix(
            xp, mc[0], mc[1], lw, None, is_ctx=True, tm=tm, seq=sp, layer=l, depth=depth, stacked=ctx_out)
        ret, s_fin = _retention(rq, rk, rv, rg, s0_zero, dec_b[l], gn_col, batch=bp)
        diff_t = _diff_attention(dq_t, dk_tok, dv_t, dq_sq(bp), dk_sq(bp), None, dl, subln_col, batch=bp, tq=tq_p,
                                 tk=tk_p, hps=DIFF_HEADS, lam_init=lam_init)
        mla_t = _mla_attention(mq_t, mk_tok, mv_t, mq_sq(bp), mk_sq(bp), None, batch=bp, tq=tq_p, tk=tk_p,
                               hps=MLA_HEADS)
        xp = _post(xp, ret, diff_t, mla_t, (mc[2], mc[3], mc[4], mc[5]), lw, is_ctx=True, tm=tm,
                   tiles_per_batch=1)
        new_ret.append(jnp.swapaxes(s_fin.reshape(bp, 2, RET_HEADS, RET_DV, RET_DK), -1, -2))

        ms = [mod[l, 1:1 + bs, i] for i in range(6)]
        s0_t = jnp.swapaxes(state_ret[:, l].astype(F32), -1, -2).reshape(bs, 2 * RET_HEADS, RET_DV, RET_DK)
        ctx_dk_f = cache_diff_k[:, l].reshape(bs, past, DIFF_QK_W).astype(F32)
        ctx_dk = ctx_dk_f.astype(BF16)
        ctx_dv_t = jnp.swapaxes(cache_diff_v[:, l].reshape(bs, past, DIFF_W), 1, 2).astype(BF16)
        ctx_mk, ctx_mv_t, ctx_sq = _ctx_mla_kv(
            jnp.swapaxes(cache_mla_ckv[:, l].astype(F32), 1, 2), jnp.swapaxes(cache_mla_kr[:, l].astype(F32), 1, 2),
            jnp.swapaxes(ctx_dk_f, 1, 2), lw["w_uk"], lw["w_uv"], lw["g_mk"])
        ctx_sq = jnp.max(ctx_sq, axis=-1)
        (rq, rk, rv, rg, dq_t, dk_tok, dv_t, mq_t, mk_tok, mv_t) = _premix(
            xs, ms[0], ms[1], lw, rope, is_ctx=False, tm=tm, seq=ss)
        ret, _ = _retention(rq, rk, rv, rg, s0_t, dec_b[l], gn_col, batch=bs)
        diff_t = _diff_attention(dq_t, dk_tok, dv_t, dq_sq(bs), jnp.maximum(dk_sq(bs), ctx_sq[:, 1]),
                                 (ctx_dk, ctx_dv_t), dl, subln_col, batch=bs, tq=tq_s, tk=tk_s, hps=2,
                                 lam_init=lam_init)
        mla_t = _mla_attention(mq_t, mk_tok, mv_t, mq_sq(bs), jnp.maximum(mk_sq(bs), ctx_sq[:, 0]),
                               (ctx_mk, ctx_mv_t), batch=bs, tq=tq_s, tk=tk_s, hps=4)
        xs = _post(xs, ret, diff_t, mla_t, (ms[2], ms[3], ms[4], ms[5]), lw, is_ctx=False, tm=tm,
                   tiles_per_batch=ss // tm)

    y_p = xp.T.reshape(bp, sp, d)
    y_s = xs.T.reshape(bs, ss, d)
    dk_all, dv_all, ckv_all, kr_all = ctx_out
    return (y_p, y_s, jnp.stack(new_ret, axis=1),
            dk_all.reshape(bp, depth, sp, DIFF_HEADS, 2, DIFF_DK), dv_all.reshape(bp, depth, sp, DIFF_HEADS, DIFF_DV),
            ckv_all, kr_all)
```

```python
import functools
import math

import jax
import jax.numpy as jnp
from jax import lax
from jax.experimental import pallas as pl
from jax.experimental.pallas import tpu as pltpu

F32 = jnp.float32
BF16 = jnp.bfloat16
EPS = 1e-6

GRID_W = 64
RET_HEADS = 4
RET_DK = 64
RET_DV = 64
RET_CHUNK = 128
DIFF_HEADS = 4
DIFF_DK = 32
DIFF_DV = 64
MLA_HEADS = 8
MLA_NOPE = 64
MLA_ROPE = 32
MLA_DQK = MLA_NOPE + MLA_ROPE
MLA_DV = 64
MLA_Q_RANK = 768
MLA_KV_RANK = 256
ROPE_BASE = 10000.0

RET_W = RET_HEADS * RET_DV
DIFF_W = DIFF_HEADS * DIFF_DV
MLA_W = MLA_HEADS * MLA_DV
DIFF_QK_W = DIFF_HEADS * 2 * DIFF_DK
MLA_PAD = 128
MLA_QK_PAD_W = MLA_HEADS * MLA_PAD
assert DIFF_DV == MLA_DV
HEAD_DV = MLA_DV

_SPLITS = (RET_HEADS * RET_DK, RET_HEADS * RET_DK, RET_W, RET_W,
           DIFF_QK_W, DIFF_QK_W, DIFF_W, MLA_Q_RANK, MLA_KV_RANK, MLA_ROPE)
W_IN_MLA0 = int(sum(_SPLITS[:7]))
_PERM_SPLITS = _SPLITS[7:] + _SPLITS[:7]
_OFFS = tuple(int(sum(_PERM_SPLITS[:i])) for i in range(len(_PERM_SPLITS) + 1))
O_CQ, O_CKV, O_KR, O_RQ, O_RK, O_RV, O_RG, O_DQ, O_DK, O_DV, O_END = _OFFS

LOG2E = math.log2(math.e)
SCORE_LOOKAHEAD = 2
SAFE_EXP_RANGE = 100.0
BOUND_SLACK = 1.0 + 2.0 ** -6
LANE = 128
MXU_DEPTH = 256
PREMIX_GROUPS = 2
VMEM_LIMIT = 56 * 1024 * 1024


def _cparams(n_grid):
    return pltpu.CompilerParams(dimension_semantics=("arbitrary",) * n_grid,
                                vmem_limit_bytes=VMEM_LIMIT)


def _full(shape):
    nd = len(shape)
    return pl.BlockSpec(shape, lambda *_: (0,) * nd)


def _silu(x):
    return x / (1.0 + jnp.exp(-x))


def _rms_scale(x, n):
    return lax.rsqrt(jnp.sum(x * x, axis=0, keepdims=True) * (1.0 / n) + EPS)


def _rope_pair(x1, x2, c, s):
    return x1 * c - x2 * s, x2 * c + x1 * s


def _mod_kernel(c_ref, w_ref, b_ref, o_ref):
    a = _silu(c_ref[...]).astype(BF16)
    o_ref[...] = jnp.dot(a, w_ref[...].astype(BF16), preferred_element_type=F32) + b_ref[...]


def _modulation(cond, w_mod, b_mod):
    depth, d, d6 = w_mod.shape
    r = cond.shape[0]
    tn = 1536
    return pl.pallas_call(
        _mod_kernel,
        grid=(depth, d6 // tn),
        in_specs=[pl.BlockSpec((r, d), lambda l, j: (0, 0)),
                  pl.BlockSpec((None, d, tn), lambda l, j: (l, 0, j)),
                  pl.BlockSpec((None, 1, tn), lambda l, j: (l, 0, j))],
        out_specs=pl.BlockSpec((None, r, tn), lambda l, j: (l, 0, j)),
        out_shape=jax.ShapeDtypeStruct((depth, r, d6), F32),
        compiler_params=_cparams(2),
        name="adaln_mod",
    )(cond, w_mod, b_mod.reshape(depth, 1, d6))


def _mla_kv_matmuls(ckvn_bf, w_uk_ref, w_uv_ref):
    return (jnp.dot(w_uk_ref[...], ckvn_bf, preferred_element_type=F32),
            jnp.dot(w_uv_ref[...], ckvn_bf, preferred_element_type=F32))


def _mla_keys(kn, kr, g_mk_ref, rope):
    t = kr.shape[1]
    kr_ss = jnp.sum(kr * kr, axis=0, keepdims=True)
    zpad = jnp.zeros((MLA_PAD - MLA_DQK, t), F32)
    half = MLA_ROPE // 2
    heads = []
    for hd in range(MLA_HEADS):
        r0 = hd * MLA_PAD
        kh = kn[r0:r0 + MLA_NOPE]
        ss = jnp.sum(kh * kh, axis=0, keepdims=True) + kr_ss
        r = lax.rsqrt(ss * (1.0 / MLA_DQK) + EPS)
        g = g_mk_ref[r0:r0 + MLA_PAD, :]
        y_nope = kh * r * g[:MLA_NOPE]
        y_r = kr * r * g[MLA_NOPE:MLA_DQK]
        if rope is not None:
            c, s = rope
            y1, y2 = _rope_pair(y_r[:half], y_r[half:], c, s)
            heads += [y_nope, y1, y2, zpad]
        else:
            heads += [y_nope, y_r, zpad]
    return jnp.concatenate(heads, axis=0)


def _group_sq_norms(a, rows_per_group):
    groups = a.shape[0] // rows_per_group
    return jnp.concatenate([jnp.sum(jnp.square(a[g * rows_per_group:(g + 1) * rows_per_group]), axis=0, keepdims=True)
                            for g in range(groups)], axis=0)


def _lane_tile_max(sq):
    return functools.reduce(jnp.maximum, [sq[:, t * LANE:(t + 1) * LANE] for t in range(sq.shape[1] // LANE)])


def _ctx_kv_kernel(ckv_ref, kr_ref, dk_ref, w_uk_ref, w_uv_ref, g_mk_ref, mk_ref, mv_ref, sq_ref):
    kn, v = _mla_kv_matmuls(ckv_ref[...].astype(BF16), w_uk_ref, w_uv_ref)
    k = _mla_keys(kn, kr_ref[...], g_mk_ref, None)
    mk_ref[...] = k.T.astype(BF16)
    mv_ref[...] = v.astype(BF16)
    sq_ref[0] = _lane_tile_max(_group_sq_norms(k, MLA_PAD))
    sq_ref[1] = _lane_tile_max(_group_sq_norms(dk_ref[...], DIFF_DK))


def _ctx_mla_kv(ckv_t, kr_t, dk_t, w_uk, w_uv, g_mk):
    b, _, l = ckv_t.shape
    return pl.pallas_call(
        _ctx_kv_kernel,
        grid=(b,),
        in_specs=[pl.BlockSpec((None, MLA_KV_RANK, l), lambda i: (i, 0, 0)),
                  pl.BlockSpec((None, MLA_ROPE, l), lambda i: (i, 0, 0)),
                  pl.BlockSpec((None, DIFF_QK_W, l), lambda i: (i, 0, 0)),
                  _full(w_uk.shape), _full(w_uv.shape), _full(g_mk.shape)],
        out_specs=[pl.BlockSpec((None, l, MLA_QK_PAD_W), lambda i: (i, 0, 0)),
                   pl.BlockSpec((None, MLA_W, l), lambda i: (i, 0, 0)),
                   pl.BlockSpec((None, 2, 8, LANE), lambda i: (i, 0, 0, 0))],
        out_shape=[jax.ShapeDtypeStruct((b, l, MLA_QK_PAD_W), BF16),
                   jax.ShapeDtypeStruct((b, MLA_W, l), BF16),
                   jax.ShapeDtypeStruct((b, 2, 8, LANE), F32)],
        compiler_params=_cparams(1),
        name="ctx_mla_kv",
    )(ckv_t, kr_t, dk_t, w_uk, w_uv, g_mk)


def _premix_kernel(*refs, is_ctx, tm, n_aliased):
    it = iter(refs)
    x_ref, shift_ref, scale_ref, n1_ref, w_in_ref = (next(it) for _ in range(5))
    g_dq_ref, g_dk_ref, g_qn_ref, g_kvn_ref = (next(it) for _ in range(4))
    w_uq_ref, g_mq_ref, w_uk_ref, w_uv_ref, g_mk_ref = (next(it) for _ in range(5))
    if not is_ctx:
        cr_ref, sr_ref, cs_ref, ss_ref = (next(it) for _ in range(4))
    for _ in range(n_aliased):
        next(it)
    rq_ref, rk_ref, rv_ref, rg_ref = (next(it) for _ in range(4))
    dq_ref, dk_ref, dv_ref, mq_ref, mk_ref, mv_ref = (next(it) for _ in range(6))
    if is_ctx:
        dkf_ref, dvf_ref, ckvf_ref, krf_ref = (next(it) for _ in range(4))
    d = x_ref.shape[0]

    def token_group(c0, w):
        cols = slice(c0, c0 + w)
        if is_ctx:
            rope_r = rope_s = None
        else:
            rope_r = (cr_ref[:, cols], sr_ref[:, cols])
            rope_s = (cs_ref[:, cols], ss_ref[:, cols])
        x = x_ref[:, cols]
        h = (x * _rms_scale(x, d) * (n1_ref[...] * (1.0 + scale_ref[...])) + shift_ref[...]).astype(BF16)
        yield

        def put_chunks(ref, val):
            for j in range(w // LANE):
                ref[c0 // LANE + j] = val[:, j * LANE:(j + 1) * LANE].astype(ref.dtype)

        def proj(lo, hi):
            return jnp.dot(w_in_ref[lo:hi, :], h, preferred_element_type=F32)

        p_mla = proj(O_CQ, O_RQ)
        cq, ckv, kr = p_mla[O_CQ:O_CKV], p_mla[O_CKV:O_KR], p_mla[O_KR:O_RQ]
        p_ret = proj(O_RQ, O_DQ)
        rq, rk, rv, rg = (p_ret[o - O_RQ:o - O_RQ + RET_W] for o in (O_RQ, O_RK, O_RV, O_RG))
        yield
        cqn = (cq * _rms_scale(cq, MLA_Q_RANK) * g_qn_ref[...]).astype(BF16)
        ckvn = ckv * _rms_scale(ckv, MLA_KV_RANK) * g_kvn_ref[...]
        mq = jnp.dot(w_uq_ref[...], cqn, preferred_element_type=F32)
        kn, mv = _mla_kv_matmuls(ckvn.astype(BF16), w_uk_ref, w_uv_ref)
        p_diff = proj(O_DQ, O_END)
        dq_raw, dk_raw, dv = (p_diff[o - O_DQ:o - O_DQ + DIFF_QK_W] for o in (O_DQ, O_DK, O_DV))
        yield

        if rope_r is not None:
            c, s = rope_r
            hk = RET_DK // 2

            def rope_heads(a):
                parts = []
                for hd in range(RET_HEADS):
                    r0 = hd * RET_DK
                    parts += list(_rope_pair(a[r0:r0 + hk], a[r0 + hk:r0 + RET_DK], c, s))
                return jnp.concatenate(parts, axis=0)

            rq = rope_heads(rq)
            rk = rope_heads(rk)
        put_chunks(rq_ref, rq)
        put_chunks(rk_ref, rk * (RET_DK ** -0.5))
        put_chunks(rv_ref, rv)
        put_chunks(rg_ref, rg)

        def diff_qk(a, g_ref, out_scale):
            parts = []
            hs = DIFF_DK // 2
            for seg in range(2 * DIFF_HEADS):
                r0 = seg * DIFF_DK
                xs = a[r0:r0 + DIFF_DK]
                y = xs * _rms_scale(xs, DIFF_DK) * g_ref[r0:r0 + DIFF_DK, :]
                if out_scale != 1.0:
                    y = y * out_scale
                if rope_s is not None:
                    parts += list(_rope_pair(y[:hs], y[hs:], rope_s[0], rope_s[1]))
                else:
                    parts.append(y)
            return jnp.concatenate(parts, axis=0)

        dq = diff_qk(dq_raw, g_dq_ref, DIFF_DK ** -0.5 * LOG2E)
        dq_ref[:, cols] = dq.astype(BF16)
        dk = diff_qk(dk_raw, g_dk_ref, 1.0)
        dk_t = dk.T
        dk_ref[cols, :] = dk_t.astype(BF16)
        dv_ref[:, cols] = dv.astype(BF16)
        if is_ctx:
            dkf_ref[c0 // w] = dk_t
            dvf_ref[c0 // w] = dv.T

        half = MLA_ROPE // 2
        parts = []
        for hd in range(MLA_HEADS):
            r0 = hd * MLA_PAD
            xs = mq[r0:r0 + MLA_PAD]
            y = xs * _rms_scale(xs, MLA_DQK) * (g_mq_ref[r0:r0 + MLA_PAD, :] * (MLA_DQK ** -0.5 * LOG2E))
            if rope_s is not None:
                y1, y2 = _rope_pair(y[MLA_NOPE:MLA_NOPE + half], y[MLA_NOPE + half:MLA_DQK], rope_s[0], rope_s[1])
                parts += [y[:MLA_NOPE], y1, y2, y[MLA_DQK:]]
            else:
                parts.append(y)
        mq_ref[:, cols] = jnp.concatenate(parts, axis=0).astype(BF16)

        mk = _mla_keys(kn, kr, g_mk_ref, rope_s)
        mk_ref[cols, :] = mk.T.astype(BF16)
        mv_ref[:, cols] = mv.astype(BF16)
        if is_ctx:
            ckvf_ref[c0 // w] = ckvn.T
            krf_ref[c0 // w] = jnp.concatenate([kr, jnp.zeros((LANE - MLA_ROPE, w), F32)], axis=0).T
        yield

    groups = [token_group(i * (tm // PREMIX_GROUPS), tm // PREMIX_GROUPS) for i in range(PREMIX_GROUPS)]
    for _ in range(4):
        for g in groups:
            next(g)


def _premix(x_t, shift, scale, lw, rope, *, is_ctx, tm, seq, layer=0, depth=1, stacked=None):
    d, n = x_t.shape
    nt = n // tm
    nch = tm // LANE
    tiles_per_batch = max(1, seq // tm)
    if is_ctx:
        bidx = lambda j: (0, 0, 0)
    else:
        bidx = lambda j: (j // tiles_per_batch, 0, 0)
    in_arrays = [x_t, shift, scale, lw["n1"], lw["w_in"], lw["g_dq"], lw["g_dk"], lw["g_qn"], lw["g_kvn"],
                 lw["w_uq"], lw["g_mq"], lw["w_uk"], lw["w_uv"], lw["g_mk"]]
    in_specs = [pl.BlockSpec((d, tm), lambda j: (0, j)),
                pl.BlockSpec((None, d, 1), bidx), pl.BlockSpec((None, d, 1), bidx)]
    in_specs += [_full(a.shape) for a in in_arrays[3:]]
    if not is_ctx:
        for tab in rope:
            in_arrays.append(tab)
            in_specs.append(pl.BlockSpec((tab.shape[0], tm), lambda j: (0, j % tiles_per_batch)))

    chunk_spec = pl.BlockSpec((nch, RET_W, LANE), lambda j: (j, 0, 0))
    fm = lambda rows: pl.BlockSpec((rows, tm), lambda j: (0, j))
    tok = lambda cols: pl.BlockSpec((tm, cols), lambda j: (j, 0))
    nck = n // LANE
    out_specs = [chunk_spec] * 4 + [fm(DIFF_QK_W), tok(DIFF_QK_W), fm(DIFF_W),
                                    fm(MLA_QK_PAD_W), tok(MLA_QK_PAD_W), fm(MLA_W)]
    out_shape = [jax.ShapeDtypeStruct((nck, RET_W, LANE), BF16), jax.ShapeDtypeStruct((nck, RET_W, LANE), F32),
                 jax.ShapeDtypeStruct((nck, RET_W, LANE), BF16), jax.ShapeDtypeStruct((nck, RET_W, LANE), F32),
                 jax.ShapeDtypeStruct((DIFF_QK_W, n), BF16), jax.ShapeDtypeStruct((n, DIFF_QK_W), BF16),
                 jax.ShapeDtypeStruct((DIFF_W, n), BF16),
                 jax.ShapeDtypeStruct((MLA_QK_PAD_W, n), BF16), jax.ShapeDtypeStruct((n, MLA_QK_PAD_W), BF16),
                 jax.ShapeDtypeStruct((MLA_W, n), BF16)]
    aliases = {}
    if is_ctx:
        assert tm // PREMIX_GROUPS == seq, "each token group of a context tile must be one request"
        for cols in (DIFF_QK_W, DIFF_W, MLA_KV_RANK, LANE):
            out_specs.append(pl.BlockSpec((tm // seq, None, seq, cols), lambda j: (j, layer, 0, 0)))
            out_shape.append(jax.ShapeDtypeStruct((n // seq, depth, seq, cols), F32))
        if stacked is not None:
            n_out = len(out_shape)
            for k, arr in enumerate(stacked):
                aliases[len(in_arrays)] = n_out - len(stacked) + k
                in_arrays.append(arr)
                in_specs.append(pl.BlockSpec(memory_space=pl.ANY))
    return pl.pallas_call(
        functools.partial(_premix_kernel, is_ctx=is_ctx, tm=tm, n_aliased=len(aliases)),
        grid=(nt,),
        in_specs=in_specs, out_specs=out_specs, out_shape=out_shape,
        input_output_aliases=aliases,
        compiler_params=_cparams(1),
        name="premix_ctx" if is_ctx else "premix_smp",
    )(*in_arrays)


def _retention_kernel(rq_ref, rk_ref, rv_ref, rg_ref, s0_ref, dec_ref, gn_ref, out_ref, sfin_ref,
                      o_acc, s_acc, intra_ref, vec_ref, *, nc):
    C = RET_CHUNK

    @pl.when(pl.program_id(0) == 0)
    def _():
        n_idx = lax.broadcasted_iota(jnp.int32, (C, C), 1).astype(F32)
        m_idx = lax.broadcasted_iota(jnp.int32, (C, C), 0).astype(F32)
        lane = lax.broadcasted_iota(jnp.int32, (8, C), 1).astype(F32)
        for d in range(2):
            for hd in range(RET_HEADS):
                i = d * RET_HEADS + hd
                z = dec_ref[i]
                lg8 = jnp.minimum(z, 0.0) - jnp.log1p(jnp.exp(-jnp.abs(z)))
                lg = jnp.broadcast_to(lg8[0:1, :], (C, C))
                dist = (n_idx - m_idx) if d == 0 else (m_idx - n_idx)
                ok = dist >= 0.0
                intra_ref[i] = jnp.where(ok, jnp.exp(lg * jnp.where(ok, dist, 0.0)), 0.0)
                if d == 0:
                    qdec = jnp.exp(lg8 * (lane + 1.0))
                    kdec = jnp.exp(lg8 * (C - 1.0 - lane))
                else:
                    qdec = jnp.exp(lg8 * (C - lane))
                    kdec = jnp.exp(lg8 * lane)
                vec_ref[i, 0] = qdec
                vec_ref[i, 1] = kdec
                vec_ref[i, 2] = jnp.exp(lg8 * float(C))

    s_acc[...] = s0_ref[...]
    o_acc[...] = jnp.zeros_like(o_acc)

    row_head = lax.broadcasted_iota(jnp.int32, (RET_HEADS * RET_DK, C), 0) // RET_DK

    def step(c, carry):
        first, vs = [], []
        for d in range(2):
            cc = c if d == 0 else nc - 1 - c
            q_all = rq_ref[cc]
            k_all = rk_ref[cc]
            k_tok = k_all.T.astype(BF16)
            q_bd = jnp.concatenate([jnp.where(row_head == hd, q_all, jnp.zeros_like(q_all))
                                    for hd in range(RET_HEADS)], axis=1)
            att_all = jnp.dot(k_tok, q_bd, preferred_element_type=F32)
            for hd in range(RET_HEADS):
                i = d * RET_HEADS + hd
                r0 = hd * RET_DK
                q = q_all[r0:r0 + RET_DK]
                k = k_all[r0:r0 + RET_DK]
                v = rv_ref[cc, r0:r0 + RET_DV, :]
                att_t = att_all[:, hd * C:(hd + 1) * C]
                s_t = s_acc[i]
                cross = jnp.dot(s_t.astype(BF16), q, preferred_element_type=F32)
                kd = (k * vec_ref[i, 1][0:1, :]).astype(BF16)
                kv = lax.dot_general(v, kd, (((1,), (1,)), ((), ())), preferred_element_type=F32)
                s_acc[i] = s_t * vec_ref[i, 2][0:1, 0:RET_DK] + kv
                first.append((att_t, cross))
                vs.append((i, cc, r0, v))
        masked = [(att_t * intra_ref[i]).astype(BF16) for (att_t, _), (i, _, _, _) in zip(first, vs)]
        for (_, cross), att_m, (i, cc, r0, v) in zip(first, masked, vs):
            o = jnp.dot(v, att_m, preferred_element_type=F32) + cross * vec_ref[i, 0][0:1, :]
            o_acc[cc, r0:r0 + RET_DV, :] += o
        return carry

    lax.fori_loop(0, nc, step, 0, unroll=min(nc, 8))
    sfin_ref[...] = s_acc[...]

    def finish(c, carry):
        o = o_acc[c]
        g = rg_ref[c]
        parts = []
        for hd in range(RET_HEADS):
            r0 = hd * RET_DV
            oh = o[r0:r0 + RET_DV]
            mu = jnp.mean(oh, axis=0, keepdims=True)
            var = jnp.mean(jnp.square(oh - mu), axis=0, keepdims=True)
            y = (oh - mu) * lax.rsqrt(var + EPS) * gn_ref[r0:r0 + RET_DV, :]
            parts.append(y * _silu(g[r0:r0 + RET_DV]))
        out_ref[c] = jnp.concatenate(parts, axis=0).astype(BF16)
        return carry

    lax.fori_loop(0, nc, finish, 0, unroll=min(nc, 4))


def _retention(rq, rk, rv, rg, s0_t, dec_b, gn_col, *, batch):
    nck = rq.shape[0]
    nc = nck // batch
    blk = pl.BlockSpec((nc, RET_W, LANE), lambda b: (b, 0, 0))
    nst = 2 * RET_HEADS
    st_spec = pl.BlockSpec((None, nst, RET_DV, RET_DK), lambda b: (b, 0, 0, 0))
    return pl.pallas_call(
        functools.partial(_retention_kernel, nc=nc),
        grid=(batch,),
        in_specs=[blk, blk, blk, blk, st_spec, _full(dec_b.shape), _full(gn_col.shape)],
        out_specs=[blk, st_spec],
        out_shape=[jax.ShapeDtypeStruct((nck, RET_W, LANE), BF16),
                   jax.ShapeDtypeStruct((batch, nst, RET_DV, RET_DK), F32)],
        scratch_shapes=[pltpu.VMEM((nc, RET_W, LANE), F32),
                        pltpu.VMEM((nst, RET_DV, RET_DK), F32),
                        pltpu.VMEM((nst, RET_CHUNK, RET_CHUNK), F32),
                        pltpu.VMEM((nst, 3, 8, LANE), F32)],
        compiler_params=_cparams(1),
        name="retention",
    )(rq, rk, rv, rg, s0_t, dec_b, gn_col)


def _attend(problems, tk, shifts=None):
    flat = [(pi, part, r0, min(tk, part[4])) for pi, (_, parts) in enumerate(problems) for part in parts
            for r0 in range(0, part[4], min(tk, part[4]))]

    def scores(item):
        pi, (k_ref, c0, _, _, _), r0, nk = item
        q_t = problems[pi][0]
        return jnp.dot(k_ref[r0:r0 + nk, c0:c0 + q_t.shape[0]], q_t, preferred_element_type=F32)

    state = [(jnp.full((1, q_t.shape[1]), -jnp.inf, F32), jnp.zeros((HEAD_DV, q_t.shape[1]), F32),
              jnp.zeros((1, q_t.shape[1]), F32)) for q_t, _ in problems]
    pending = [scores(it) for it in flat[:SCORE_LOOKAHEAD]]
    for i, (pi, (_, _, v_ref, v0, _), r0, nk) in enumerate(flat):
        if i + SCORE_LOOKAHEAD < len(flat):
            pending.append(scores(flat[i + SCORE_LOOKAHEAD]))
        s = pending.pop(0)
        m, acc, l = state[pi]
        if shifts is None:
            m_new = jnp.maximum(m, jnp.max(s, axis=0, keepdims=True))
            alpha = jnp.exp2(m - m_new)
            acc, l = acc * alpha, l * alpha
        else:
            m_new = shifts[pi]
        p = jnp.exp2(s - m_new)
        l = l + jnp.sum(p, axis=0, keepdims=True)
        pv = jnp.dot(v_ref[v0:v0 + HEAD_DV, r0:r0 + nk], p.astype(BF16), preferred_element_type=F32)
        state[pi] = (m_new, acc + pv, l)
    return [acc / l for _, acc, l in state]


def _attend_guarded(problems, key_sq_max, safe, tk):
    shifts = []
    for (q_t, _), ksq in zip(problems, key_sq_max):
        q32 = q_t.astype(F32)
        qsq = jnp.sum(q32 * q32, axis=0, keepdims=True)
        shifts.append(jnp.sqrt(qsq * ksq) * BOUND_SLACK)
    outs = lax.cond(safe,
                    lambda: tuple(_attend(problems, tk, shifts)),
                    lambda: tuple(_attend(problems, tk, None)))
    return list(outs)


def _row_bcast(a, row, tq):
    rid = lax.broadcasted_iota(jnp.int32, a.shape, 0)
    r = jnp.max(jnp.where(rid == row, a, 0.0), axis=0, keepdims=True)
    return jnp.concatenate([r] * (tq // LANE), axis=1)


def _diff_attn_kernel(*refs, has_ctx, rows_main, rows_ctx, tk, hps, lam_init):
    it = iter(refs)
    safe_ref, q_ref, k_ref, v_ref, ksq_ref = (next(it) for _ in range(5))
    if has_ctx:
        kc_ref, vc_ref = (next(it) for _ in range(2))
    dl_ref, g_ref, out_ref = (next(it) for _ in range(3))
    grp = pl.program_id(1)
    tq = q_ref.shape[1]
    segs_per_tile = LANE // DIFF_DK
    seg = lax.broadcasted_iota(jnp.int32, (LANE, tq), 0) // DIFF_DK
    problems, key_sq_max = [], []
    for j in range(hps):
        tile = (2 * j) // segs_per_tile
        q_tile = q_ref[tile * LANE:(tile + 1) * LANE, :]
        parts = [(k_ref, tile * LANE, v_ref, j * HEAD_DV, rows_main)]
        if has_ctx:
            parts.append((kc_ref, tile * LANE, vc_ref, j * HEAD_DV, rows_ctx))
        for mm in range(2):
            q_m = jnp.where(seg == (2 * j + mm) % segs_per_tile, q_tile, jnp.zeros_like(q_tile))
            problems.append((q_m, parts))
            key_sq_max.append(_row_bcast(ksq_ref[...], 2 * (grp * hps + j) + mm, tq))
    dl = dl_ref[...]
    lam = (jnp.exp(jnp.sum(dl[0:1] * dl[1:2], axis=1, keepdims=True))
           - jnp.exp(jnp.sum(dl[2:3] * dl[3:4], axis=1, keepdims=True)) + lam_init)
    outs = _attend_guarded(problems, key_sq_max, safe_ref[pl.program_id(0), grp] != 0, tk)
    for j in range(hps):
        o = outs[2 * j] - lam * outs[2 * j + 1]
        y = o * _rms_scale(o, DIFF_DV) * g_ref[...]
        out_ref[j * DIFF_DV:(j + 1) * DIFF_DV, :] = (y * (1.0 - lam_init)).astype(BF16)


def _safe_flags(q_sq, k_sq, per_group):
    bound = jnp.sqrt(q_sq * k_sq) * (BOUND_SLACK * BOUND_SLACK)
    ok = (2.0 * bound <= SAFE_EXP_RANGE).reshape(q_sq.shape[0], -1, per_group)
    return jnp.all(ok, axis=-1).astype(jnp.int32)


def _diff_attention(dq_t, dk_tok, dv_t, q_sq, k_sq, ctx, dl, g_col, *, batch, tq, tk, hps, lam_init):
    n_all = dq_t.shape[1]
    n = n_all // batch
    qt = n // tq
    nseg = 2 * DIFF_HEADS
    has_ctx = ctx is not None
    rows_ctx = ctx[0].shape[1] if has_ctx else 0
    assert (hps * 2 * DIFF_DK) % LANE == 0, "a grid step must cover whole 128-lane key tiles"
    qk_w = hps * 2 * DIFF_DK
    safe = _safe_flags(q_sq, k_sq, 2 * hps)
    k_sq_b = jnp.broadcast_to(k_sq[:, :, None], (batch, nseg, LANE))
    in_arrays = [safe, dq_t, dk_tok, dv_t, k_sq_b]
    in_specs = [pl.BlockSpec(memory_space=pltpu.SMEM),
                pl.BlockSpec((qk_w, tq), lambda b, g, i: (g, b * qt + i)),
                pl.BlockSpec((n, qk_w), lambda b, g, i: (b, g)),
                pl.BlockSpec((hps * HEAD_DV, n), lambda b, g, i: (g, b)),
                pl.BlockSpec((None, nseg, LANE), lambda b, g, i: (b, 0, 0))]
    if has_ctx:
        in_arrays += list(ctx)
        in_specs += [pl.BlockSpec((None, rows_ctx, qk_w), lambda b, g, i: (b, 0, g)),
                     pl.BlockSpec((None, hps * HEAD_DV, rows_ctx), lambda b, g, i: (b, g, 0))]
    in_arrays += [dl, g_col]
    in_specs += [_full(dl.shape), _full(g_col.shape)]
    return pl.pallas_call(
        functools.partial(_diff_attn_kernel, has_ctx=has_ctx, rows_main=n, rows_ctx=rows_ctx, tk=tk, hps=hps,
                          lam_init=lam_init),
        grid=(batch, DIFF_HEADS // hps, qt),
        in_specs=in_specs,
        out_specs=pl.BlockSpec((hps * DIFF_DV, tq), lambda b, g, i: (g, b * qt + i)),
        out_shape=jax.ShapeDtypeStruct((DIFF_W, n_all), BF16),
        compiler_params=_cparams(3),
        name="diff_attn_smp" if has_ctx else "diff_attn_ctx",
    )(*in_arrays)


def _mla_attn_kernel(*refs, has_ctx, rows_main, rows_ctx, tk, hps):
    it = iter(refs)
    safe_ref, q_ref, k_ref, v_ref, ksq_ref = (next(it) for _ in range(5))
    if has_ctx:
        kc_ref, vc_ref = (next(it) for _ in range(2))
    out_ref = next(it)
    grp = pl.program_id(1)
    tq = q_ref.shape[1]
    problems, key_sq_max = [], []
    for j in range(hps):
        parts = [(k_ref, j * MLA_PAD, v_ref, j * HEAD_DV, rows_main)]
        if has_ctx:
            parts.append((kc_ref, j * MLA_PAD, vc_ref, j * HEAD_DV, rows_ctx))
        problems.append((q_ref[j * MLA_PAD:(j + 1) * MLA_PAD, :], parts))
        key_sq_max.append(_row_bcast(ksq_ref[...], grp * hps + j, tq))
    outs = _attend_guarded(problems, key_sq_max, safe_ref[pl.program_id(0), grp] != 0, tk)
    for j, o in enumerate(outs):
        out_ref[j * MLA_DV:(j + 1) * MLA_DV, :] = o.astype(BF16)


def _mla_attention(mq_t, mk_tok, mv_t, q_sq, k_sq, ctx, *, batch, tq, tk, hps):
    n_all = mq_t.shape[1]
    n = n_all // batch
    qt = n // tq
    has_ctx = ctx is not None
    rows_ctx = ctx[0].shape[1] if has_ctx else 0
    safe = _safe_flags(q_sq, k_sq, hps)
    k_sq_b = jnp.broadcast_to(k_sq[:, :, None], (batch, MLA_HEADS, LANE))
    in_arrays = [safe, mq_t, mk_tok, mv_t, k_sq_b]
    in_specs = [pl.BlockSpec(memory_space=pltpu.SMEM),
                pl.BlockSpec((hps * MLA_PAD, tq), lambda b, g, i: (g, b * qt + i)),
                pl.BlockSpec((n, hps * MLA_PAD), lambda b, g, i: (b, g)),
                pl.BlockSpec((hps * HEAD_DV, n), lambda b, g, i: (g, b)),
                pl.BlockSpec((None, MLA_HEADS, LANE), lambda b, g, i: (b, 0, 0))]
    if has_ctx:
        in_arrays += list(ctx)
        in_specs += [pl.BlockSpec((None, rows_ctx, hps * MLA_PAD), lambda b, g, i: (b, 0, g)),
                     pl.BlockSpec((None, hps * HEAD_DV, rows_ctx), lambda b, g, i: (b, g, 0))]
    return pl.pallas_call(
        functools.partial(_mla_attn_kernel, has_ctx=has_ctx, rows_main=n, rows_ctx=rows_ctx, tk=tk, hps=hps),
        grid=(batch, MLA_HEADS // hps, qt),
        in_specs=in_specs,
        out_specs=pl.BlockSpec((hps * MLA_DV, tq), lambda b, g, i: (g, b * qt + i)),
        out_shape=jax.ShapeDtypeStruct((MLA_W, n_all), BF16),
        compiler_params=_cparams(3),
        name="mla_attn_smp" if has_ctx else "mla_attn_ctx",
    )(*in_arrays)


def _post_kernel(x_ref, ret_ref, diff_ref, mla_ref, g1_ref, sh2_ref, sc2_ref, g2_ref, n2_ref,
                 w_out_ref, w_gu_ref, w_down_ref, out_ref, *, tm, d_ff, ff_cuts, token_major_out):
    nch = tm // LANE
    ret = jnp.concatenate([ret_ref[j] for j in range(nch)], axis=1)
    mixed = jnp.concatenate([ret, diff_ref[...], mla_ref[...]], axis=0)
    d = x_ref.shape[0]
    halves = [slice(i * (tm // 2), (i + 1) * (tm // 2)) for i in range(2)]
    attn = [jnp.dot(w_out_ref[...], mixed[:, c], preferred_element_type=F32) for c in halves]
    x1 = [x_ref[:, c] + g1_ref[...] * a for c, a in zip(halves, attn)]
    gain2 = n2_ref[...] * (1.0 + sc2_ref[...])
    h = [(v * _rms_scale(v, d) * gain2 + sh2_ref[...]).astype(BF16) for v in x1]
    ffn = None
    for lo, hi in zip((0,) + ff_cuts, ff_cuts + (d_ff,)):
        acts = []
        for i in range(2):
            g = jnp.dot(w_gu_ref[lo:hi, :], h[i], preferred_element_type=F32)
            u = jnp.dot(w_gu_ref[d_ff + lo:d_ff + hi, :], h[i], preferred_element_type=F32)
            acts.append((_silu(g) * u).astype(BF16))
        part = jnp.dot(w_down_ref[:, lo:hi], jnp.concatenate(acts, axis=1), preferred_element_type=F32)
        ffn = part if ffn is None else ffn + part
    res = jnp.concatenate(x1, axis=1) + g2_ref[...] * ffn
    out_ref[...] = res.T if token_major_out else res


def _post(x_t, ret, diff_t, mla_t, mods, lw, *, is_ctx, tm, tiles_per_batch, token_major_out=False):
    d, n = x_t.shape
    nt = n // tm
    nch = tm // LANE
    d_ff = lw["w_down"].shape[1]
    ff_cuts = ((d_ff // 2) // MXU_DEPTH * MXU_DEPTH,)
    if is_ctx:
        bidx = lambda j: (0, 0, 0)
    else:
        bidx = lambda j: (j // tiles_per_batch, 0, 0)
    col = pl.BlockSpec((None, d, 1), bidx)
    const = lambda a: pl.BlockSpec(a.shape, lambda j: (0,) * a.ndim, pipeline_mode=pl.Buffered(1))
    g1, sh2, sc2, g2 = mods
    return pl.pallas_call(
        functools.partial(_post_kernel, tm=tm, d_ff=d_ff, ff_cuts=ff_cuts, token_major_out=token_major_out),
        grid=(nt,),
        in_specs=[pl.BlockSpec((d, tm), lambda j: (0, j)),
                  pl.BlockSpec((nch, RET_W, LANE), lambda j: (j, 0, 0)),
                  pl.BlockSpec((DIFF_W, tm), lambda j: (0, j)),
                  pl.BlockSpec((MLA_W, tm), lambda j: (0, j)),
                  col, col, col, col, const(lw["n2"]),
                  const(lw["w_out"]), const(lw["w_gu"]), const(lw["w_down"])],
        out_specs=(pl.BlockSpec((tm, d), lambda j: (j, 0)) if token_major_out
                   else pl.BlockSpec((d, tm), lambda j: (0, j))),
        out_shape=jax.ShapeDtypeStruct((n, d) if token_major_out else (d, n), F32),
        compiler_params=_cparams(1),
        name="post_ctx" if is_ctx else "post_smp",
    )(x_t, ret, diff_t, mla_t, g1, sh2, sc2, g2, lw["n2"], lw["w_out"], lw["w_gu"], lw["w_down"])


def _rope_tables_t(n, rot_dim):
    rows = n // GRID_W
    row = jnp.repeat(jnp.arange(rows, dtype=F32), GRID_W)
    col = jnp.tile(jnp.arange(GRID_W, dtype=F32), rows)
    n_freq = rot_dim // 4
    inv = 1.0 / (ROPE_BASE ** (jnp.arange(n_freq, dtype=F32) / n_freq))
    ang = jnp.concatenate([inv[:, None] * row[None, :], inv[:, None] * col[None, :]], axis=0)
    return jnp.cos(ang), jnp.sin(ang)


def _col(v):
    return v.astype(F32).reshape(-1, 1)


def _pad_heads_rows(w_t, used, pad):
    hk = w_t.shape[0] // used
    w3 = w_t.reshape(hk, used, w_t.shape[1])
    w3 = jnp.pad(w3, ((0, 0), (0, pad - used), (0, 0)))
    return w3.reshape(hk * pad, w_t.shape[1])


def _layer_weights(l, w_in, norm1, norm2, diff_qk_gain, mla_q_norm, mla_kv_norm, w_uq, w_ukv, mla_qk_gain,
                   w_out, w_gu, w_down):
    w_ukv_t = w_ukv[l].T.reshape(MLA_HEADS, MLA_NOPE + MLA_DV, MLA_KV_RANK)
    w_uk = jnp.pad(w_ukv_t[:, :MLA_NOPE], ((0, 0), (0, MLA_PAD - MLA_NOPE), (0, 0)))
    g_pad = lambda g: jnp.tile(jnp.pad(g.astype(F32), (0, MLA_PAD - MLA_DQK)), MLA_HEADS).reshape(-1, 1)
    return {
        "n1": _col(norm1[l]), "n2": _col(norm2[l]),
        "w_in": jnp.concatenate([w_in[l][:, W_IN_MLA0:], w_in[l][:, :W_IN_MLA0]], axis=1).T.astype(BF16),
        "g_dq": _col(jnp.tile(diff_qk_gain[l, 0], 2 * DIFF_HEADS)),
        "g_dk": _col(jnp.tile(diff_qk_gain[l, 1], 2 * DIFF_HEADS)),
        "g_qn": _col(mla_q_norm[l]), "g_kvn": _col(mla_kv_norm[l]),
        "w_uq": _pad_heads_rows(w_uq[l].T, MLA_DQK, MLA_PAD).astype(BF16),
        "g_mq": g_pad(mla_qk_gain[l, 0]), "g_mk": g_pad(mla_qk_gain[l, 1]),
        "w_uk": w_uk.reshape(MLA_QK_PAD_W, MLA_KV_RANK).astype(BF16),
        "w_uv": w_ukv_t[:, MLA_NOPE:].reshape(MLA_W, MLA_KV_RANK).astype(BF16),
        "w_out": w_out[l].T.astype(BF16), "w_gu": w_gu[l].T.astype(BF16), "w_down": w_down[l].T.astype(BF16),
    }


def kernel(x_prompt, x_sample, c, state_ret, cache_diff_k, cache_diff_v, cache_mla_ckv, cache_mla_kr, c_ctx,
           w_mod, b_mod, norm1, norm2, w_in, ret_decay, ret_gn_gain, diff_qk_gain, diff_lambda, diff_subln_gain,
           mla_q_norm, mla_kv_norm, w_uq, w_ukv, mla_qk_gain, w_out, w_gu, w_down):
    depth = w_in.shape[0]
    bp, sp, d = x_prompt.shape
    bs, ss, _ = x_sample.shape
    past = cache_diff_k.shape[2]

    tm = 512
    tq_s, tk_s = 512, 256
    tq_p = tk_p = sp

    n_cond = 1 + bs
    r_pad = -(-n_cond // 16) * 16
    cond = jnp.concatenate([c_ctx[None, :], c, jnp.zeros((r_pad - n_cond, d), F32)], axis=0)
    mod = _modulation(cond, w_mod, b_mod)
    mod = mod.reshape(depth, r_pad, 6, d, 1)

    xp = x_prompt.reshape(bp * sp, d).T
    xs = x_sample.reshape(bs * ss, d).T
    rope = _rope_tables_t(ss, RET_DK) + _rope_tables_t(ss, DIFF_DK)
    dec_b = jnp.broadcast_to(ret_decay.astype(F32).reshape(depth, 2 * RET_HEADS, 1, 1),
                             (depth, 2 * RET_HEADS, 8, LANE))
    s0_zero = jnp.zeros((bp, 2 * RET_HEADS, RET_DV, RET_DK), F32)

    new_ret = []
    ctx_out = None
    for l in range(depth):
        lw = _layer_weights(l, w_in, norm1, norm2, diff_qk_gain, mla_q_norm, mla_kv_norm, w_uq, w_ukv,
                            mla_qk_gain, w_out, w_gu, w_down)
        lam_init = 0.8 - 0.6 * math.exp(-0.3 * l)
        gn_col = _col(ret_gn_gain[l])
        subln_col = _col(diff_subln_gain[l])
        dl = diff_lambda[l].astype(F32)

        mc = [mod[l, 0:1, i] for i in range(6)]
        def norm_bound(gain, n, out_scale, batch):
            return jnp.full((batch, 8), n * out_scale * out_scale, F32) * jnp.max(jnp.square(gain.astype(F32)))

        dq_sq = functools.partial(norm_bound, diff_qk_gain[l, 0], DIFF_DK, DIFF_DK ** -0.5 * LOG2E)
        dk_sq = functools.partial(norm_bound, diff_qk_gain[l, 1], DIFF_DK, 1.0)
        mq_sq = functools.partial(norm_bound, mla_qk_gain[l, 0], MLA_DQK, MLA_DQK ** -0.5 * LOG2E)
        mk_sq = functools.partial(norm_bound, mla_qk_gain[l, 1], MLA_DQK, 1.0)

        (rq, rk, rv, rg, dq_t, dk_tok, dv_t, mq_t, mk_tok, mv_t, *ctx_out) = _premix(
            xp, mc[0], mc[1], lw, None, is_ctx=True, tm=tm, seq=sp, layer=l, depth=depth, stacked=ctx_out)
        ret, s_fin = _retention(rq, rk, rv, rg, s0_zero, dec_b[l], gn_col, batch=bp)
        diff_t = _diff_attention(dq_t, dk_tok, dv_t, dq_sq(bp), dk_sq(bp), None, dl, subln_col, batch=bp, tq=tq_p,
                                 tk=tk_p, hps=DIFF_HEADS, lam_init=lam_init)
        mla_t = _mla_attention(mq_t, mk_tok, mv_t, mq_sq(bp), mk_sq(bp), None, batch=bp, tq=tq_p, tk=tk_p,
                               hps=MLA_HEADS)
        xp = _post(xp, ret, diff_t, mla_t, (mc[2], mc[3], mc[4], mc[5]), lw, is_ctx=True, tm=tm,
                   tiles_per_batch=1, token_major_out=(l == depth - 1))
        new_ret.append(jnp.swapaxes(s_fin.reshape(bp, 2, RET_HEADS, RET_DV, RET_DK), -1, -2))

        ms = [mod[l, 1:1 + bs, i] for i in range(6)]
        s0_t = jnp.swapaxes(state_ret[:, l].astype(F32), -1, -2).reshape(bs, 2 * RET_HEADS, RET_DV, RET_DK)
        ctx_dk_f = cache_diff_k[:, l].reshape(bs, past, DIFF_QK_W).astype(F32)
        ctx_dk = ctx_dk_f.astype(BF16)
        ctx_dv_t = jnp.swapaxes(cache_diff_v[:, l].reshape(bs, past, DIFF_W), 1, 2).astype(BF16)
        ctx_mk, ctx_mv_t, ctx_sq = _ctx_mla_kv(
            jnp.swapaxes(cache_mla_ckv[:, l].astype(F32), 1, 2), jnp.swapaxes(cache_mla_kr[:, l].astype(F32), 1, 2),
            jnp.swapaxes(ctx_dk_f, 1, 2), lw["w_uk"], lw["w_uv"], lw["g_mk"])
        ctx_sq = jnp.max(ctx_sq, axis=-1)
        (rq, rk, rv, rg, dq_t, dk_tok, dv_t, mq_t, mk_tok, mv_t) = _premix(
            xs, ms[0], ms[1], lw, rope, is_ctx=False, tm=tm, seq=ss)
        ret, _ = _retention(rq, rk, rv, rg, s0_t, dec_b[l], gn_col, batch=bs)
        diff_t = _diff_attention(dq_t, dk_tok, dv_t, dq_sq(bs), jnp.maximum(dk_sq(bs), ctx_sq[:, 1]),
                                 (ctx_dk, ctx_dv_t), dl, subln_col, batch=bs, tq=tq_s, tk=tk_s, hps=2,
                                 lam_init=lam_init)
        mla_t = _mla_attention(mq_t, mk_tok, mv_t, mq_sq(bs), jnp.maximum(mk_sq(bs), ctx_sq[:, 0]),
                               (ctx_mk, ctx_mv_t), batch=bs, tq=tq_s, tk=tk_s, hps=4)
        xs = _post(xs, ret, diff_t, mla_t, (ms[2], ms[3], ms[4], ms[5]), lw, is_ctx=False, tm=tm,
                   tiles_per_batch=ss // tm, token_major_out=(l == depth - 1))

    y_p = xp.reshape(bp, sp, d)
    y_s = xs.reshape(bs, ss, d)
    dk_all, dv_all, ckv_all, kr_all = ctx_out
    return (y_p, y_s, jnp.stack(new_ret, axis=1),
            dk_all.reshape(bp, depth, sp, DIFF_HEADS, 2, DIFF_DK), dv_all.reshape(bp, depth, sp, DIFF_HEADS, DIFF_DV),
            ckv_all, kr_all[..., :MLA_ROPE])
```

```python
import functools
import math

import jax
import jax.numpy as jnp
from jax import lax
from jax.experimental import pallas as pl
from jax.experimental.pallas import tpu as pltpu

F32 = jnp.float32
BF16 = jnp.bfloat16
EPS = 1e-6

GRID_W = 64
RET_HEADS = 4
RET_DK = 64
RET_DV = 64
RET_CHUNK = 128
DIFF_HEADS = 4
DIFF_DK = 32
DIFF_DV = 64
MLA_HEADS = 8
MLA_NOPE = 64
MLA_ROPE = 32
MLA_DQK = MLA_NOPE + MLA_ROPE
MLA_DV = 64
MLA_Q_RANK = 768
MLA_KV_RANK = 256
ROPE_BASE = 10000.0

RET_W = RET_HEADS * RET_DV
DIFF_W = DIFF_HEADS * DIFF_DV
MLA_W = MLA_HEADS * MLA_DV
DIFF_QK_W = DIFF_HEADS * 2 * DIFF_DK
MLA_PAD = 128
MLA_QK_PAD_W = MLA_HEADS * MLA_PAD
assert DIFF_DV == MLA_DV
HEAD_DV = MLA_DV

_SPLITS = (RET_HEADS * RET_DK, RET_HEADS * RET_DK, RET_W, RET_W,
           DIFF_QK_W, DIFF_QK_W, DIFF_W, MLA_Q_RANK, MLA_KV_RANK, MLA_ROPE)
W_IN_MLA0 = int(sum(_SPLITS[:7]))
_PERM_SPLITS = _SPLITS[7:] + _SPLITS[:7]
_OFFS = tuple(int(sum(_PERM_SPLITS[:i])) for i in range(len(_PERM_SPLITS) + 1))
O_CQ, O_CKV, O_KR, O_RQ, O_RK, O_RV, O_RG, O_DQ, O_DK, O_DV, O_END = _OFFS

LOG2E = math.log2(math.e)
SCORE_LOOKAHEAD = 2
SAFE_EXP_RANGE = 100.0
BOUND_SLACK = 1.0 + 2.0 ** -6
LANE = 128
MXU_DEPTH = 256
PREMIX_GROUPS = 2
VMEM_LIMIT = 56 * 1024 * 1024


def _cparams(n_grid):
    return pltpu.CompilerParams(dimension_semantics=("arbitrary",) * n_grid,
                                vmem_limit_bytes=VMEM_LIMIT)


def _full(shape):
    nd = len(shape)
    return pl.BlockSpec(shape, lambda *_: (0,) * nd)


def _silu(x):
    return x / (1.0 + jnp.exp(-x))


def _rms_scale(x, n):
    return lax.rsqrt(jnp.sum(x * x, axis=0, keepdims=True) * (1.0 / n) + EPS)


def _rope_pair(x1, x2, c, s):
    return x1 * c - x2 * s, x2 * c + x1 * s


def _mod_kernel(c_ref, w_ref, b_ref, o_ref):
    a = _silu(c_ref[...]).astype(BF16)
    o_ref[...] = jnp.dot(a, w_ref[...].astype(BF16), preferred_element_type=F32) + b_ref[...]


def _modulation(cond, w_mod, b_mod):
    depth, d, d6 = w_mod.shape
    r = cond.shape[0]
    tn = 1536
    return pl.pallas_call(
        _mod_kernel,
        grid=(depth, d6 // tn),
        in_specs=[pl.BlockSpec((r, d), lambda l, j: (0, 0)),
                  pl.BlockSpec((None, d, tn), lambda l, j: (l, 0, j)),
                  pl.BlockSpec((None, 1, tn), lambda l, j: (l, 0, j))],
        out_specs=pl.BlockSpec((None, r, tn), lambda l, j: (l, 0, j)),
        out_shape=jax.ShapeDtypeStruct((depth, r, d6), F32),
        compiler_params=_cparams(2),
        name="adaln_mod",
    )(cond, w_mod, b_mod.reshape(depth, 1, d6))


def _mla_kv_matmuls(ckvn_bf, w_uk_ref, w_uv_ref):
    return (jnp.dot(w_uk_ref[...], ckvn_bf, preferred_element_type=F32),
            jnp.dot(w_uv_ref[...], ckvn_bf, preferred_element_type=F32))


def _mla_keys(kn, kr, g_mk_ref, rope):
    t = kr.shape[1]
    kr_ss = jnp.sum(kr * kr, axis=0, keepdims=True)
    zpad = jnp.zeros((MLA_PAD - MLA_DQK, t), F32)
    half = MLA_ROPE // 2
    heads = []
    for hd in range(MLA_HEADS):
        r0 = hd * MLA_PAD
        kh = kn[r0:r0 + MLA_NOPE]
        ss = jnp.sum(kh * kh, axis=0, keepdims=True) + kr_ss
        r = lax.rsqrt(ss * (1.0 / MLA_DQK) + EPS)
        g = g_mk_ref[r0:r0 + MLA_PAD, :]
        y_nope = kh * r * g[:MLA_NOPE]
        y_r = kr * r * g[MLA_NOPE:MLA_DQK]
        if rope is not None:
            c, s = rope
            y1, y2 = _rope_pair(y_r[:half], y_r[half:], c, s)
            heads += [y_nope, y1, y2, zpad]
        else:
            heads += [y_nope, y_r, zpad]
    return jnp.concatenate(heads, axis=0)


def _group_sq_norms(a, rows_per_group):
    groups = a.shape[0] // rows_per_group
    return jnp.concatenate([jnp.sum(jnp.square(a[g * rows_per_group:(g + 1) * rows_per_group]), axis=0, keepdims=True)
                            for g in range(groups)], axis=0)


def _lane_tile_max(sq):
    return functools.reduce(jnp.maximum, [sq[:, t * LANE:(t + 1) * LANE] for t in range(sq.shape[1] // LANE)])


def _ctx_kv_kernel(ckv_ref, kr_ref, dk_ref, w_uk_ref, w_uv_ref, g_mk_ref, mk_ref, mv_ref, sq_ref):
    kn, v = _mla_kv_matmuls(ckv_ref[...].astype(BF16), w_uk_ref, w_uv_ref)
    k = _mla_keys(kn, kr_ref[...], g_mk_ref, None)
    mk_ref[...] = k.T.astype(BF16)
    mv_ref[...] = v.astype(BF16)
    sq_ref[0] = _lane_tile_max(_group_sq_norms(k, MLA_PAD))
    sq_ref[1] = _lane_tile_max(_group_sq_norms(dk_ref[...], DIFF_DK))


def _ctx_mla_kv(ckv_t, kr_t, dk_t, w_uk, w_uv, g_mk):
    b, _, l = ckv_t.shape
    return pl.pallas_call(
        _ctx_kv_kernel,
        grid=(b,),
        in_specs=[pl.BlockSpec((None, MLA_KV_RANK, l), lambda i: (i, 0, 0)),
                  pl.BlockSpec((None, MLA_ROPE, l), lambda i: (i, 0, 0)),
                  pl.BlockSpec((None, DIFF_QK_W, l), lambda i: (i, 0, 0)),
                  _full(w_uk.shape), _full(w_uv.shape), _full(g_mk.shape)],
        out_specs=[pl.BlockSpec((None, l, MLA_QK_PAD_W), lambda i: (i, 0, 0)),
                   pl.BlockSpec((None, MLA_W, l), lambda i: (i, 0, 0)),
                   pl.BlockSpec((None, 2, 8, LANE), lambda i: (i, 0, 0, 0))],
        out_shape=[jax.ShapeDtypeStruct((b, l, MLA_QK_PAD_W), BF16),
                   jax.ShapeDtypeStruct((b, MLA_W, l), BF16),
                   jax.ShapeDtypeStruct((b, 2, 8, LANE), F32)],
        compiler_params=_cparams(1),
        name="ctx_mla_kv",
    )(ckv_t, kr_t, dk_t, w_uk, w_uv, g_mk)


def _premix_kernel(*refs, is_ctx, tm, n_aliased, token_major_in):
    it = iter(refs)
    x_ref, shift_ref, scale_ref, n1_ref, w_in_ref = (next(it) for _ in range(5))
    g_dq_ref, g_dk_ref, g_qn_ref, g_kvn_ref = (next(it) for _ in range(4))
    w_uq_ref, g_mq_ref, w_uk_ref, w_uv_ref, g_mk_ref = (next(it) for _ in range(5))
    if not is_ctx:
        cr_ref, sr_ref, cs_ref, ss_ref = (next(it) for _ in range(4))
    for _ in range(n_aliased):
        next(it)
    rq_ref, rk_ref, rv_ref, rg_ref = (next(it) for _ in range(4))
    dq_ref, dk_ref, dv_ref, mq_ref, mk_ref, mv_ref = (next(it) for _ in range(6))
    if is_ctx:
        dkf_ref, dvf_ref, ckvf_ref, krf_ref = (next(it) for _ in range(4))
    d = n1_ref.shape[0]

    def token_group(c0, w):
        cols = slice(c0, c0 + w)
        if is_ctx:
            rope_r = rope_s = None
        else:
            rope_r = (cr_ref[:, cols], sr_ref[:, cols])
            rope_s = (cs_ref[:, cols], ss_ref[:, cols])
        x = x_ref[cols, :].T if token_major_in else x_ref[:, cols]
        h = (x * _rms_scale(x, d) * (n1_ref[...] * (1.0 + scale_ref[...])) + shift_ref[...]).astype(BF16)
        yield

        def put_chunks(ref, val):
            for j in range(w // LANE):
                ref[c0 // LANE + j] = val[:, j * LANE:(j + 1) * LANE].astype(ref.dtype)

        def proj(lo, hi):
            return jnp.dot(w_in_ref[lo:hi, :], h, preferred_element_type=F32)

        p_mla = proj(O_CQ, O_RQ)
        cq, ckv, kr = p_mla[O_CQ:O_CKV], p_mla[O_CKV:O_KR], p_mla[O_KR:O_RQ]
        p_ret = proj(O_RQ, O_DQ)
        rq, rk, rv, rg = (p_ret[o - O_RQ:o - O_RQ + RET_W] for o in (O_RQ, O_RK, O_RV, O_RG))
        yield
        cqn = (cq * _rms_scale(cq, MLA_Q_RANK) * g_qn_ref[...]).astype(BF16)
        ckvn = ckv * _rms_scale(ckv, MLA_KV_RANK) * g_kvn_ref[...]
        mq = jnp.dot(w_uq_ref[...], cqn, preferred_element_type=F32)
        kn, mv = _mla_kv_matmuls(ckvn.astype(BF16), w_uk_ref, w_uv_ref)
        p_diff = proj(O_DQ, O_END)
        dq_raw, dk_raw, dv = (p_diff[o - O_DQ:o - O_DQ + DIFF_QK_W] for o in (O_DQ, O_DK, O_DV))
        yield

        if rope_r is not None:
            c, s = rope_r
            hk = RET_DK // 2

            def rope_heads(a):
                parts = []
                for hd in range(RET_HEADS):
                    r0 = hd * RET_DK
                    parts += list(_rope_pair(a[r0:r0 + hk], a[r0 + hk:r0 + RET_DK], c, s))
                return jnp.concatenate(parts, axis=0)

            rq = rope_heads(rq)
            rk = rope_heads(rk)
        put_chunks(rq_ref, rq)
        put_chunks(rk_ref, rk * (RET_DK ** -0.5))
        put_chunks(rv_ref, rv)
        put_chunks(rg_ref, rg)

        def diff_qk(a, g_ref, out_scale):
            parts = []
            hs = DIFF_DK // 2
            for seg in range(2 * DIFF_HEADS):
                r0 = seg * DIFF_DK
                xs = a[r0:r0 + DIFF_DK]
                y = xs * _rms_scale(xs, DIFF_DK) * g_ref[r0:r0 + DIFF_DK, :]
                if out_scale != 1.0:
                    y = y * out_scale
                if rope_s is not None:
                    parts += list(_rope_pair(y[:hs], y[hs:], rope_s[0], rope_s[1]))
                else:
                    parts.append(y)
            return jnp.concatenate(parts, axis=0)

        dq = diff_qk(dq_raw, g_dq_ref, DIFF_DK ** -0.5 * LOG2E)
        dq_ref[:, cols] = dq.astype(BF16)
        dk = diff_qk(dk_raw, g_dk_ref, 1.0)
        dk_t = dk.T
        dk_ref[cols, :] = dk_t.astype(BF16)
        dv_ref[:, cols] = dv.astype(BF16)
        if is_ctx:
            dkf_ref[c0 // w] = dk_t
            dvf_ref[c0 // w] = dv.T

        half = MLA_ROPE // 2
        parts = []
        for hd in range(MLA_HEADS):
            r0 = hd * MLA_PAD
            xs = mq[r0:r0 + MLA_PAD]
            y = xs * _rms_scale(xs, MLA_DQK) * (g_mq_ref[r0:r0 + MLA_PAD, :] * (MLA_DQK ** -0.5 * LOG2E))
            if rope_s is not None:
                y1, y2 = _rope_pair(y[MLA_NOPE:MLA_NOPE + half], y[MLA_NOPE + half:MLA_DQK], rope_s[0], rope_s[1])
                parts += [y[:MLA_NOPE], y1, y2, y[MLA_DQK:]]
            else:
                parts.append(y)
        mq_ref[:, cols] = jnp.concatenate(parts, axis=0).astype(BF16)

        mk = _mla_keys(kn, kr, g_mk_ref, rope_s)
        mk_ref[cols, :] = mk.T.astype(BF16)
        mv_ref[:, cols] = mv.astype(BF16)
        if is_ctx:
            ckvf_ref[c0 // w] = ckvn.T
            krf_ref[c0 // w] = jnp.concatenate([kr, jnp.zeros((LANE - MLA_ROPE, w), F32)], axis=0).T
        yield

    groups = [token_group(i * (tm // PREMIX_GROUPS), tm // PREMIX_GROUPS) for i in range(PREMIX_GROUPS)]
    for _ in range(4):
        for g in groups:
            next(g)


def _premix(x_t, shift, scale, lw, rope, *, is_ctx, tm, seq, layer=0, depth=1, stacked=None,
            token_major_in=False):
    n, d = x_t.shape if token_major_in else x_t.shape[::-1]
    nt = n // tm
    nch = tm // LANE
    tiles_per_batch = max(1, seq // tm)
    if is_ctx:
        bidx = lambda j: (0, 0, 0)
    else:
        bidx = lambda j: (j // tiles_per_batch, 0, 0)
    in_arrays = [x_t, shift, scale, lw["n1"], lw["w_in"], lw["g_dq"], lw["g_dk"], lw["g_qn"], lw["g_kvn"],
                 lw["w_uq"], lw["g_mq"], lw["w_uk"], lw["w_uv"], lw["g_mk"]]
    in_specs = [pl.BlockSpec((tm, d), lambda j: (j, 0)) if token_major_in else pl.BlockSpec((d, tm), lambda j: (0, j)),
                pl.BlockSpec((None, d, 1), bidx), pl.BlockSpec((None, d, 1), bidx)]
    in_specs += [_full(a.shape) for a in in_arrays[3:]]
    if not is_ctx:
        for tab in rope:
            in_arrays.append(tab)
            in_specs.append(pl.BlockSpec((tab.shape[0], tm), lambda j: (0, j % tiles_per_batch)))

    chunk_spec = pl.BlockSpec((nch, RET_W, LANE), lambda j: (j, 0, 0))
    fm = lambda rows: pl.BlockSpec((rows, tm), lambda j: (0, j))
    tok = lambda cols: pl.BlockSpec((tm, cols), lambda j: (j, 0))
    nck = n // LANE
    out_specs = [chunk_spec] * 4 + [fm(DIFF_QK_W), tok(DIFF_QK_W), fm(DIFF_W),
                                    fm(MLA_QK_PAD_W), tok(MLA_QK_PAD_W), fm(MLA_W)]
    out_shape = [jax.ShapeDtypeStruct((nck, RET_W, LANE), BF16), jax.ShapeDtypeStruct((nck, RET_W, LANE), F32),
                 jax.ShapeDtypeStruct((nck, RET_W, LANE), BF16), jax.ShapeDtypeStruct((nck, RET_W, LANE), F32),
                 jax.ShapeDtypeStruct((DIFF_QK_W, n), BF16), jax.ShapeDtypeStruct((n, DIFF_QK_W), BF16),
                 jax.ShapeDtypeStruct((DIFF_W, n), BF16),
                 jax.ShapeDtypeStruct((MLA_QK_PAD_W, n), BF16), jax.ShapeDtypeStruct((n, MLA_QK_PAD_W), BF16),
                 jax.ShapeDtypeStruct((MLA_W, n), BF16)]
    aliases = {}
    if is_ctx:
        assert tm // PREMIX_GROUPS == seq, "each token group of a context tile must be one request"
        for cols in (DIFF_QK_W, DIFF_W, MLA_KV_RANK, LANE):
            out_specs.append(pl.BlockSpec((tm // seq, None, seq, cols), lambda j: (j, layer, 0, 0)))
            out_shape.append(jax.ShapeDtypeStruct((n // seq, depth, seq, cols), F32))
        if stacked is not None:
            n_out = len(out_shape)
            for k, arr in enumerate(stacked):
                aliases[len(in_arrays)] = n_out - len(stacked) + k
                in_arrays.append(arr)
                in_specs.append(pl.BlockSpec(memory_space=pl.ANY))
    return pl.pallas_call(
        functools.partial(_premix_kernel, is_ctx=is_ctx, tm=tm, n_aliased=len(aliases),
                          token_major_in=token_major_in),
        grid=(nt,),
        in_specs=in_specs, out_specs=out_specs, out_shape=out_shape,
        input_output_aliases=aliases,
        compiler_params=_cparams(1),
        name="premix_ctx" if is_ctx else "premix_smp",
    )(*in_arrays)


def _retention_kernel(rq_ref, rk_ref, rv_ref, rg_ref, s0_ref, dec_ref, gn_ref, out_ref, sfin_ref,
                      o_acc, s_acc, intra_ref, vec_ref, *, nc):
    C = RET_CHUNK

    @pl.when(pl.program_id(0) == 0)
    def _():
        n_idx = lax.broadcasted_iota(jnp.int32, (C, C), 1).astype(F32)
        m_idx = lax.broadcasted_iota(jnp.int32, (C, C), 0).astype(F32)
        lane = lax.broadcasted_iota(jnp.int32, (8, C), 1).astype(F32)
        for d in range(2):
            for hd in range(RET_HEADS):
                i = d * RET_HEADS + hd
                z = dec_ref[i]
                lg8 = jnp.minimum(z, 0.0) - jnp.log1p(jnp.exp(-jnp.abs(z)))
                lg = jnp.broadcast_to(lg8[0:1, :], (C, C))
                dist = (n_idx - m_idx) if d == 0 else (m_idx - n_idx)
                ok = dist >= 0.0
                intra_ref[i] = jnp.where(ok, jnp.exp(lg * jnp.where(ok, dist, 0.0)), 0.0)
                if d == 0:
                    qdec = jnp.exp(lg8 * (lane + 1.0))
                    kdec = jnp.exp(lg8 * (C - 1.0 - lane))
                else:
                    qdec = jnp.exp(lg8 * (C - lane))
                    kdec = jnp.exp(lg8 * lane)
                vec_ref[i, 0] = qdec
                vec_ref[i, 1] = kdec
                vec_ref[i, 2] = jnp.exp(lg8 * float(C))

    s_acc[...] = s0_ref[...]
    o_acc[...] = jnp.zeros_like(o_acc)

    row_head = lax.broadcasted_iota(jnp.int32, (RET_HEADS * RET_DK, C), 0) // RET_DK

    def step(c, carry):
        first, vs = [], []
        for d in range(2):
            cc = c if d == 0 else nc - 1 - c
            q_all = rq_ref[cc]
            k_all = rk_ref[cc]
            k_tok = k_all.T.astype(BF16)
            q_bd = jnp.concatenate([jnp.where(row_head == hd, q_all, jnp.zeros_like(q_all))
                                    for hd in range(RET_HEADS)], axis=1)
            att_all = jnp.dot(k_tok, q_bd, preferred_element_type=F32)
            for hd in range(RET_HEADS):
                i = d * RET_HEADS + hd
                r0 = hd * RET_DK
                q = q_all[r0:r0 + RET_DK]
                k = k_all[r0:r0 + RET_DK]
                v = rv_ref[cc, r0:r0 + RET_DV, :]
                att_t = att_all[:, hd * C:(hd + 1) * C]
                s_t = s_acc[i]
                cross = jnp.dot(s_t.astype(BF16), q, preferred_element_type=F32)
                kd = (k * vec_ref[i, 1][0:1, :]).astype(BF16)
                kv = lax.dot_general(v, kd, (((1,), (1,)), ((), ())), preferred_element_type=F32)
                s_acc[i] = s_t * vec_ref[i, 2][0:1, 0:RET_DK] + kv
                first.append((att_t, cross))
                vs.append((i, cc, r0, v))
        masked = [(att_t * intra_ref[i]).astype(BF16) for (att_t, _), (i, _, _, _) in zip(first, vs)]
        for (_, cross), att_m, (i, cc, r0, v) in zip(first, masked, vs):
            o = jnp.dot(v, att_m, preferred_element_type=F32) + cross * vec_ref[i, 0][0:1, :]
            o_acc[cc, r0:r0 + RET_DV, :] += o
        return carry

    lax.fori_loop(0, nc, step, 0, unroll=min(nc, 8))
    sfin_ref[...] = s_acc[...]

    def finish(c, carry):
        o = o_acc[c]
        g = rg_ref[c]
        parts = []
        for hd in range(RET_HEADS):
            r0 = hd * RET_DV
            oh = o[r0:r0 + RET_DV]
            mu = jnp.mean(oh, axis=0, keepdims=True)
            var = jnp.mean(jnp.square(oh - mu), axis=0, keepdims=True)
            y = (oh - mu) * lax.rsqrt(var + EPS) * gn_ref[r0:r0 + RET_DV, :]
            parts.append(y * _silu(g[r0:r0 + RET_DV]))
        out_ref[c] = jnp.concatenate(parts, axis=0).astype(BF16)
        return carry

    lax.fori_loop(0, nc, finish, 0, unroll=min(nc, 4))


def _retention(rq, rk, rv, rg, s0_t, dec_b, gn_col, *, batch):
    nck = rq.shape[0]
    nc = nck // batch
    blk = pl.BlockSpec((nc, RET_W, LANE), lambda b: (b, 0, 0))
    nst = 2 * RET_HEADS
    st_spec = pl.BlockSpec((None, nst, RET_DV, RET_DK), lambda b: (b, 0, 0, 0))
    return pl.pallas_call(
        functools.partial(_retention_kernel, nc=nc),
        grid=(batch,),
        in_specs=[blk, blk, blk, blk, st_spec, _full(dec_b.shape), _full(gn_col.shape)],
        out_specs=[blk, st_spec],
        out_shape=[jax.ShapeDtypeStruct((nck, RET_W, LANE), BF16),
                   jax.ShapeDtypeStruct((batch, nst, RET_DV, RET_DK), F32)],
        scratch_shapes=[pltpu.VMEM((nc, RET_W, LANE), F32),
                        pltpu.VMEM((nst, RET_DV, RET_DK), F32),
                        pltpu.VMEM((nst, RET_CHUNK, RET_CHUNK), F32),
                        pltpu.VMEM((nst, 3, 8, LANE), F32)],
        compiler_params=_cparams(1),
        name="retention",
    )(rq, rk, rv, rg, s0_t, dec_b, gn_col)


def _attend(problems, tk, shifts=None):
    flat = [(pi, part, r0, min(tk, part[4])) for pi, (_, parts) in enumerate(problems) for part in parts
            for r0 in range(0, part[4], min(tk, part[4]))]

    def scores(item):
        pi, (k_ref, c0, _, _, _), r0, nk = item
        q_t = problems[pi][0]
        return jnp.dot(k_ref[r0:r0 + nk, c0:c0 + q_t.shape[0]], q_t, preferred_element_type=F32)

    state = [(jnp.full((1, q_t.shape[1]), -jnp.inf, F32), jnp.zeros((HEAD_DV, q_t.shape[1]), F32),
              jnp.zeros((1, q_t.shape[1]), F32)) for q_t, _ in problems]
    pending = [scores(it) for it in flat[:SCORE_LOOKAHEAD]]
    for i, (pi, (_, _, v_ref, v0, _), r0, nk) in enumerate(flat):
        if i + SCORE_LOOKAHEAD < len(flat):
            pending.append(scores(flat[i + SCORE_LOOKAHEAD]))
        s = pending.pop(0)
        m, acc, l = state[pi]
        if shifts is None:
            m_new = jnp.maximum(m, jnp.max(s, axis=0, keepdims=True))
            alpha = jnp.exp2(m - m_new)
            acc, l = acc * alpha, l * alpha
        else:
            m_new = shifts[pi]
        p = jnp.exp2(s - m_new)
        l = l + jnp.sum(p, axis=0, keepdims=True)
        pv = jnp.dot(v_ref[v0:v0 + HEAD_DV, r0:r0 + nk], p.astype(BF16), preferred_element_type=F32)
        state[pi] = (m_new, acc + pv, l)
    return [acc / l for _, acc, l in state]


def _attend_guarded(problems, key_sq_max, safe, tk):
    shifts = []
    for (q_t, _), ksq in zip(problems, key_sq_max):
        q32 = q_t.astype(F32)
        qsq = jnp.sum(q32 * q32, axis=0, keepdims=True)
        shifts.append(jnp.sqrt(qsq * ksq) * BOUND_SLACK)
    outs = lax.cond(safe,
                    lambda: tuple(_attend(problems, tk, shifts)),
                    lambda: tuple(_attend(problems, tk, None)))
    return list(outs)


def _row_bcast(a, row, tq):
    rid = lax.broadcasted_iota(jnp.int32, a.shape, 0)
    r = jnp.max(jnp.where(rid == row, a, 0.0), axis=0, keepdims=True)
    return jnp.concatenate([r] * (tq // LANE), axis=1)


def _diff_attn_kernel(*refs, has_ctx, rows_main, rows_ctx, tk, hps, lam_init):
    it = iter(refs)
    safe_ref, q_ref, k_ref, v_ref, ksq_ref = (next(it) for _ in range(5))
    if has_ctx:
        kc_ref, vc_ref = (next(it) for _ in range(2))
    dl_ref, g_ref, out_ref = (next(it) for _ in range(3))
    grp = pl.program_id(1)
    tq = q_ref.shape[1]
    segs_per_tile = LANE // DIFF_DK
    seg = lax.broadcasted_iota(jnp.int32, (LANE, tq), 0) // DIFF_DK
    problems, key_sq_max = [], []
    for j in range(hps):
        tile = (2 * j) // segs_per_tile
        q_tile = q_ref[tile * LANE:(tile + 1) * LANE, :]
        parts = [(k_ref, tile * LANE, v_ref, j * HEAD_DV, rows_main)]
        if has_ctx:
            parts.append((kc_ref, tile * LANE, vc_ref, j * HEAD_DV, rows_ctx))
        for mm in range(2):
            q_m = jnp.where(seg == (2 * j + mm) % segs_per_tile, q_tile, jnp.zeros_like(q_tile))
            problems.append((q_m, parts))
            key_sq_max.append(_row_bcast(ksq_ref[...], 2 * (grp * hps + j) + mm, tq))
    dl = dl_ref[...]
    lam = (jnp.exp(jnp.sum(dl[0:1] * dl[1:2], axis=1, keepdims=True))
           - jnp.exp(jnp.sum(dl[2:3] * dl[3:4], axis=1, keepdims=True)) + lam_init)
    outs = _attend_guarded(problems, key_sq_max, safe_ref[pl.program_id(0), grp] != 0, tk)
    for j in range(hps):
        o = outs[2 * j] - lam * outs[2 * j + 1]
        y = o * _rms_scale(o, DIFF_DV) * g_ref[...]
        out_ref[j * DIFF_DV:(j + 1) * DIFF_DV, :] = (y * (1.0 - lam_init)).astype(BF16)


def _safe_flags(q_sq, k_sq, per_group):
    bound = jnp.sqrt(q_sq * k_sq) * (BOUND_SLACK * BOUND_SLACK)
    ok = (2.0 * bound <= SAFE_EXP_RANGE).reshape(q_sq.shape[0], -1, per_group)
    return jnp.all(ok, axis=-1).astype(jnp.int32)


def _diff_attention(dq_t, dk_tok, dv_t, q_sq, k_sq, ctx, dl, g_col, *, batch, tq, tk, hps, lam_init):
    n_all = dq_t.shape[1]
    n = n_all // batch
    qt = n // tq
    nseg = 2 * DIFF_HEADS
    has_ctx = ctx is not None
    rows_ctx = ctx[0].shape[1] if has_ctx else 0
    assert (hps * 2 * DIFF_DK) % LANE == 0, "a grid step must cover whole 128-lane key tiles"
    qk_w = hps * 2 * DIFF_DK
    safe = _safe_flags(q_sq, k_sq, 2 * hps)
    k_sq_b = jnp.broadcast_to(k_sq[:, :, None], (batch, nseg, LANE))
    in_arrays = [safe, dq_t, dk_tok, dv_t, k_sq_b]
    in_specs = [pl.BlockSpec(memory_space=pltpu.SMEM),
                pl.BlockSpec((qk_w, tq), lambda b, g, i: (g, b * qt + i)),
                pl.BlockSpec((n, qk_w), lambda b, g, i: (b, g)),
                pl.BlockSpec((hps * HEAD_DV, n), lambda b, g, i: (g, b)),
                pl.BlockSpec((None, nseg, LANE), lambda b, g, i: (b, 0, 0))]
    if has_ctx:
        in_arrays += list(ctx)
        in_specs += [pl.BlockSpec((None, rows_ctx, qk_w), lambda b, g, i: (b, 0, g)),
                     pl.BlockSpec((None, hps * HEAD_DV, rows_ctx), lambda b, g, i: (b, g, 0))]
    in_arrays += [dl, g_col]
    in_specs += [_full(dl.shape), _full(g_col.shape)]
    return pl.pallas_call(
        functools.partial(_diff_attn_kernel, has_ctx=has_ctx, rows_main=n, rows_ctx=rows_ctx, tk=tk, hps=hps,
                          lam_init=lam_init),
        grid=(batch, DIFF_HEADS // hps, qt),
        in_specs=in_specs,
        out_specs=pl.BlockSpec((hps * DIFF_DV, tq), lambda b, g, i: (g, b * qt + i)),
        out_shape=jax.ShapeDtypeStruct((DIFF_W, n_all), BF16),
        compiler_params=_cparams(3),
        name="diff_attn_smp" if has_ctx else "diff_attn_ctx",
    )(*in_arrays)


def _mla_attn_kernel(*refs, has_ctx, rows_main, rows_ctx, tk, hps):
    it = iter(refs)
    safe_ref, q_ref, k_ref, v_ref, ksq_ref = (next(it) for _ in range(5))
    if has_ctx:
        kc_ref, vc_ref = (next(it) for _ in range(2))
    out_ref = next(it)
    grp = pl.program_id(1)
    tq = q_ref.shape[1]
    problems, key_sq_max = [], []
    for j in range(hps):
        parts = [(k_ref, j * MLA_PAD, v_ref, j * HEAD_DV, rows_main)]
        if has_ctx:
            parts.append((kc_ref, j * MLA_PAD, vc_ref, j * HEAD_DV, rows_ctx))
        problems.append((q_ref[j * MLA_PAD:(j + 1) * MLA_PAD, :], parts))
        key_sq_max.append(_row_bcast(ksq_ref[...], grp * hps + j, tq))
    outs = _attend_guarded(problems, key_sq_max, safe_ref[pl.program_id(0), grp] != 0, tk)
    for j, o in enumerate(outs):
        out_ref[j * MLA_DV:(j + 1) * MLA_DV, :] = o.astype(BF16)


def _mla_attention(mq_t, mk_tok, mv_t, q_sq, k_sq, ctx, *, batch, tq, tk, hps):
    n_all = mq_t.shape[1]
    n = n_all // batch
    qt = n // tq
    has_ctx = ctx is not None
    rows_ctx = ctx[0].shape[1] if has_ctx else 0
    safe = _safe_flags(q_sq, k_sq, hps)
    k_sq_b = jnp.broadcast_to(k_sq[:, :, None], (batch, MLA_HEADS, LANE))
    in_arrays = [safe, mq_t, mk_tok, mv_t, k_sq_b]
    in_specs = [pl.BlockSpec(memory_space=pltpu.SMEM),
                pl.BlockSpec((hps * MLA_PAD, tq), lambda b, g, i: (g, b * qt + i)),
                pl.BlockSpec((n, hps * MLA_PAD), lambda b, g, i: (b, g)),
                pl.BlockSpec((hps * HEAD_DV, n), lambda b, g, i: (g, b)),
                pl.BlockSpec((None, MLA_HEADS, LANE), lambda b, g, i: (b, 0, 0))]
    if has_ctx:
        in_arrays += list(ctx)
        in_specs += [pl.BlockSpec((None, rows_ctx, hps * MLA_PAD), lambda b, g, i: (b, 0, g)),
                     pl.BlockSpec((None, hps * HEAD_DV, rows_ctx), lambda b, g, i: (b, g, 0))]
    return pl.pallas_call(
        functools.partial(_mla_attn_kernel, has_ctx=has_ctx, rows_main=n, rows_ctx=rows_ctx, tk=tk, hps=hps),
        grid=(batch, MLA_HEADS // hps, qt),
        in_specs=in_specs,
        out_specs=pl.BlockSpec((hps * MLA_DV, tq), lambda b, g, i: (g, b * qt + i)),
        out_shape=jax.ShapeDtypeStruct((MLA_W, n_all), BF16),
        compiler_params=_cparams(3),
        name="mla_attn_smp" if has_ctx else "mla_attn_ctx",
    )(*in_arrays)


def _post_kernel(x_ref, ret_ref, diff_ref, mla_ref, g1_ref, sh2_ref, sc2_ref, g2_ref, n2_ref,
                 w_out_ref, w_gu_ref, w_down_ref, out_ref, *, tm, d_ff, ff_cuts, token_major_in, token_major_out):
    nch = tm // LANE
    ret = jnp.concatenate([ret_ref[j] for j in range(nch)], axis=1)
    mixed = jnp.concatenate([ret, diff_ref[...], mla_ref[...]], axis=0)
    d = n2_ref.shape[0]
    halves = [slice(i * (tm // 2), (i + 1) * (tm // 2)) for i in range(2)]
    attn = [jnp.dot(w_out_ref[...], mixed[:, c], preferred_element_type=F32) for c in halves]
    x1 = [(x_ref[c, :].T if token_major_in else x_ref[:, c]) + g1_ref[...] * a for c, a in zip(halves, attn)]
    gain2 = n2_ref[...] * (1.0 + sc2_ref[...])
    h = [(v * _rms_scale(v, d) * gain2 + sh2_ref[...]).astype(BF16) for v in x1]
    ffn = None
    for lo, hi in zip((0,) + ff_cuts, ff_cuts + (d_ff,)):
        acts = []
        for i in range(2):
            g = jnp.dot(w_gu_ref[lo:hi, :], h[i], preferred_element_type=F32)
            u = jnp.dot(w_gu_ref[d_ff + lo:d_ff + hi, :], h[i], preferred_element_type=F32)
            acts.append((_silu(g) * u).astype(BF16))
        part = jnp.dot(w_down_ref[:, lo:hi], jnp.concatenate(acts, axis=1), preferred_element_type=F32)
        ffn = part if ffn is None else ffn + part
    res = jnp.concatenate(x1, axis=1) + g2_ref[...] * ffn
    out_ref[...] = res.T if token_major_out else res


def _post(x_t, ret, diff_t, mla_t, mods, lw, *, is_ctx, tm, tiles_per_batch, token_major_in=False,
          token_major_out=False):
    n, d = x_t.shape if token_major_in else x_t.shape[::-1]
    nt = n // tm
    nch = tm // LANE
    d_ff = lw["w_down"].shape[1]
    ff_cuts = ((d_ff // 2) // MXU_DEPTH * MXU_DEPTH,)
    if is_ctx:
        bidx = lambda j: (0, 0, 0)
    else:
        bidx = lambda j: (j // tiles_per_batch, 0, 0)
    col = pl.BlockSpec((None, d, 1), bidx)
    const = lambda a: pl.BlockSpec(a.shape, lambda j: (0,) * a.ndim, pipeline_mode=pl.Buffered(1))
    g1, sh2, sc2, g2 = mods
    return pl.pallas_call(
        functools.partial(_post_kernel, tm=tm, d_ff=d_ff, ff_cuts=ff_cuts, token_major_in=token_major_in,
                          token_major_out=token_major_out),
        grid=(nt,),
        in_specs=[pl.BlockSpec((tm, d), lambda j: (j, 0)) if token_major_in else pl.BlockSpec((d, tm), lambda j: (0, j)),
                  pl.BlockSpec((nch, RET_W, LANE), lambda j: (j, 0, 0)),
                  pl.BlockSpec((DIFF_W, tm), lambda j: (0, j)),
                  pl.BlockSpec((MLA_W, tm), lambda j: (0, j)),
                  col, col, col, col, const(lw["n2"]),
                  const(lw["w_out"]), const(lw["w_gu"]), const(lw["w_down"])],
        out_specs=(pl.BlockSpec((tm, d), lambda j: (j, 0)) if token_major_out
                   else pl.BlockSpec((d, tm), lambda j: (0, j))),
        out_shape=jax.ShapeDtypeStruct((n, d) if token_major_out else (d, n), F32),
        compiler_params=_cparams(1),
        name="post_ctx" if is_ctx else "post_smp",
    )(x_t, ret, diff_t, mla_t, g1, sh2, sc2, g2, lw["n2"], lw["w_out"], lw["w_gu"], lw["w_down"])


def _rope_tables_t(n, rot_dim):
    rows = n // GRID_W
    row = jnp.repeat(jnp.arange(rows, dtype=F32), GRID_W)
    col = jnp.tile(jnp.arange(GRID_W, dtype=F32), rows)
    n_freq = rot_dim // 4
    inv = 1.0 / (ROPE_BASE ** (jnp.arange(n_freq, dtype=F32) / n_freq))
    ang = jnp.concatenate([inv[:, None] * row[None, :], inv[:, None] * col[None, :]], axis=0)
    return jnp.cos(ang), jnp.sin(ang)


def _col(v):
    return v.astype(F32).reshape(-1, 1)


def _pad_heads_rows(w_t, used, pad):
    hk = w_t.shape[0] // used
    w3 = w_t.reshape(hk, used, w_t.shape[1])
    w3 = jnp.pad(w3, ((0, 0), (0, pad - used), (0, 0)))
    return w3.reshape(hk * pad, w_t.shape[1])


def _layer_weights(l, w_in, norm1, norm2, diff_qk_gain, mla_q_norm, mla_kv_norm, w_uq, w_ukv, mla_qk_gain,
                   w_out, w_gu, w_down):
    w_ukv_t = w_ukv[l].T.reshape(MLA_HEADS, MLA_NOPE + MLA_DV, MLA_KV_RANK)
    w_uk = jnp.pad(w_ukv_t[:, :MLA_NOPE], ((0, 0), (0, MLA_PAD - MLA_NOPE), (0, 0)))
    g_pad = lambda g: jnp.tile(jnp.pad(g.astype(F32), (0, MLA_PAD - MLA_DQK)), MLA_HEADS).reshape(-1, 1)
    return {
        "n1": _col(norm1[l]), "n2": _col(norm2[l]),
        "w_in": jnp.concatenate([w_in[l][:, W_IN_MLA0:], w_in[l][:, :W_IN_MLA0]], axis=1).T.astype(BF16),
        "g_dq": _col(jnp.tile(diff_qk_gain[l, 0], 2 * DIFF_HEADS)),
        "g_dk": _col(jnp.tile(diff_qk_gain[l, 1], 2 * DIFF_HEADS)),
        "g_qn": _col(mla_q_norm[l]), "g_kvn": _col(mla_kv_norm[l]),
        "w_uq": _pad_heads_rows(w_uq[l].T, MLA_DQK, MLA_PAD).astype(BF16),
        "g_mq": g_pad(mla_qk_gain[l, 0]), "g_mk": g_pad(mla_qk_gain[l, 1]),
        "w_uk": w_uk.reshape(MLA_QK_PAD_W, MLA_KV_RANK).astype(BF16),
        "w_uv": w_ukv_t[:, MLA_NOPE:].reshape(MLA_W, MLA_KV_RANK).astype(BF16),
        "w_out": w_out[l].T.astype(BF16), "w_gu": w_gu[l].T.astype(BF16), "w_down": w_down[l].T.astype(BF16),
    }


def kernel(x_prompt, x_sample, c, state_ret, cache_diff_k, cache_diff_v, cache_mla_ckv, cache_mla_kr, c_ctx,
           w_mod, b_mod, norm1, norm2, w_in, ret_decay, ret_gn_gain, diff_qk_gain, diff_lambda, diff_subln_gain,
           mla_q_norm, mla_kv_norm, w_uq, w_ukv, mla_qk_gain, w_out, w_gu, w_down):
    depth = w_in.shape[0]
    bp, sp, d = x_prompt.shape
    bs, ss, _ = x_sample.shape
    past = cache_diff_k.shape[2]

    tm = 512
    tq_s, tk_s = 512, 256
    tq_p = tk_p = sp

    n_cond = 1 + bs
    r_pad = -(-n_cond // 16) * 16
    cond = jnp.concatenate([c_ctx[None, :], c, jnp.zeros((r_pad - n_cond, d), F32)], axis=0)
    mod = _modulation(cond, w_mod, b_mod)
    mod = mod.reshape(depth, r_pad, 6, d, 1)

    xp = x_prompt.reshape(bp * sp, d)
    xs = x_sample.reshape(bs * ss, d)
    rope = _rope_tables_t(ss, RET_DK) + _rope_tables_t(ss, DIFF_DK)
    dec_b = jnp.broadcast_to(ret_decay.astype(F32).reshape(depth, 2 * RET_HEADS, 1, 1),
                             (depth, 2 * RET_HEADS, 8, LANE))
    s0_zero = jnp.zeros((bp, 2 * RET_HEADS, RET_DV, RET_DK), F32)

    new_ret = []
    ctx_out = None
    for l in range(depth):
        lw = _layer_weights(l, w_in, norm1, norm2, diff_qk_gain, mla_q_norm, mla_kv_norm, w_uq, w_ukv,
                            mla_qk_gain, w_out, w_gu, w_down)
        lam_init = 0.8 - 0.6 * math.exp(-0.3 * l)
        gn_col = _col(ret_gn_gain[l])
        subln_col = _col(diff_subln_gain[l])
        dl = diff_lambda[l].astype(F32)

        mc = [mod[l, 0:1, i] for i in range(6)]
        def norm_bound(gain, n, out_scale, batch):
            return jnp.full((batch, 8), n * out_scale * out_scale, F32) * jnp.max(jnp.square(gain.astype(F32)))

        dq_sq = functools.partial(norm_bound, diff_qk_gain[l, 0], DIFF_DK, DIFF_DK ** -0.5 * LOG2E)
        dk_sq = functools.partial(norm_bound, diff_qk_gain[l, 1], DIFF_DK, 1.0)
        mq_sq = functools.partial(norm_bound, mla_qk_gain[l, 0], MLA_DQK, MLA_DQK ** -0.5 * LOG2E)
        mk_sq = functools.partial(norm_bound, mla_qk_gain[l, 1], MLA_DQK, 1.0)

        (rq, rk, rv, rg, dq_t, dk_tok, dv_t, mq_t, mk_tok, mv_t, *ctx_out) = _premix(
            xp, mc[0], mc[1], lw, None, is_ctx=True, tm=tm, seq=sp, layer=l, depth=depth, stacked=ctx_out,
            token_major_in=(l == 0))
        ret, s_fin = _retention(rq, rk, rv, rg, s0_zero, dec_b[l], gn_col, batch=bp)
        diff_t = _diff_attention(dq_t, dk_tok, dv_t, dq_sq(bp), dk_sq(bp), None, dl, subln_col, batch=bp, tq=tq_p,
                                 tk=tk_p, hps=DIFF_HEADS, lam_init=lam_init)
        mla_t = _mla_attention(mq_t, mk_tok, mv_t, mq_sq(bp), mk_sq(bp), None, batch=bp, tq=tq_p, tk=tk_p,
                               hps=MLA_HEADS)
        xp = _post(xp, ret, diff_t, mla_t, (mc[2], mc[3], mc[4], mc[5]), lw, is_ctx=True, tm=tm,
                   tiles_per_batch=1, token_major_in=(l == 0), token_major_out=(l == depth - 1))
        new_ret.append(jnp.swapaxes(s_fin.reshape(bp, 2, RET_HEADS, RET_DV, RET_DK), -1, -2))

        ms = [mod[l, 1:1 + bs, i] for i in range(6)]
        s0_t = jnp.swapaxes(state_ret[:, l].astype(F32), -1, -2).reshape(bs, 2 * RET_HEADS, RET_DV, RET_DK)
        ctx_dk_f = cache_diff_k[:, l].reshape(bs, past, DIFF_QK_W).astype(F32)
        ctx_dk = ctx_dk_f.astype(BF16)
        ctx_dv_t = jnp.swapaxes(cache_diff_v[:, l].reshape(bs, past, DIFF_W), 1, 2).astype(BF16)
        ctx_mk, ctx_mv_t, ctx_sq = _ctx_mla_kv(
            jnp.swapaxes(cache_mla_ckv[:, l].astype(F32), 1, 2), jnp.swapaxes(cache_mla_kr[:, l].astype(F32), 1, 2),
            jnp.swapaxes(ctx_dk_f, 1, 2), lw["w_uk"], lw["w_uv"], lw["g_mk"])
        ctx_sq = jnp.max(ctx_sq, axis=-1)
        (rq, rk, rv, rg, dq_t, dk_tok, dv_t, mq_t, mk_tok, mv_t) = _premix(
            xs, ms[0], ms[1], lw, rope, is_ctx=False, tm=tm, seq=ss, token_major_in=(l == 0))
        ret, _ = _retention(rq, rk, rv, rg, s0_t, dec_b[l], gn_col, batch=bs)
        diff_t = _diff_attention(dq_t, dk_tok, dv_t, dq_sq(bs), jnp.maximum(dk_sq(bs), ctx_sq[:, 1]),
                                 (ctx_dk, ctx_dv_t), dl, subln_col, batch=bs, tq=tq_s, tk=tk_s, hps=2,
                                 lam_init=lam_init)
        mla_t = _mla_attention(mq_t, mk_tok, mv_t, mq_sq(bs), jnp.maximum(mk_sq(bs), ctx_sq[:, 0]),
                               (ctx_mk, ctx_mv_t), batch=bs, tq=tq_s, tk=tk_s, hps=4)
        xs = _post(xs, ret, diff_t, mla_t, (ms[2], ms[3], ms[4], ms[5]), lw, is_ctx=False, tm=tm,
                   tiles_per_batch=ss // tm, token_major_in=(l == 0), token_major_out=(l == depth - 1))

    y_p = xp.reshape(bp, sp, d)
    y_s = xs.reshape(bs, ss, d)
    dk_all, dv_all, ckv_all, kr_all = ctx_out
    return (y_p, y_s, jnp.stack(new_ret, axis=1),
            dk_all.reshape(bp, depth, sp, DIFF_HEADS, 2, DIFF_DK), dv_all.reshape(bp, depth, sp, DIFF_HEADS, DIFF_DV),
            ckv_all, kr_all[..., :MLA_ROPE])
```
